```python
import jax, jax.numpy as jnp
from jax import lax
import numpy as np

D_MODEL = 1024
BATCH = 8
SEQ = 2048
DEPTH = 1

MIX_WIDTH = D_MODEL
CONV_WIDTH = MIX_WIDTH // 2
CONV_HEADS = 8
CONV_KERNEL = 31
POOL_WIDTH = MIX_WIDTH - CONV_WIDTH
POOL_WINDOWS = (2, 4, 8, 16)
POOL_GROUPS = len(POOL_WINDOWS)
POOL_GROUP_DIM = POOL_WIDTH // POOL_GROUPS
IN_WIDTH = 2 * CONV_WIDTH + POOL_WIDTH
D_FF = ((8 * D_MODEL // 3 + 255) // 256) * 256
N_MOD = 6
EPS = 1e-6

kernel_name = "hybrid_conv_pool_adaln_block"


def rmsnorm(x, g):
    xf = x.astype(jnp.float32)
    y = xf * lax.rsqrt(jnp.mean(xf * xf, axis=-1, keepdims=True) + EPS)
    return (y * g.astype(jnp.float32)).astype(x.dtype)


def layernorm(x, g, b):
    xf = x.astype(jnp.float32)
    mu = jnp.mean(xf, axis=-1, keepdims=True)
    var = jnp.mean(jnp.square(xf - mu), axis=-1, keepdims=True)
    y = (xf - mu) * lax.rsqrt(var + EPS)
    return (y * g.astype(jnp.float32) + b.astype(jnp.float32)).astype(x.dtype)


def conv_mixer(u, dw_w, dw_b, ln_g, ln_b, w_pw):
    a, g = jnp.split(u, 2, axis=-1)
    h = a * jax.nn.sigmoid(g)
    h = lax.conv_general_dilated(
        h, dw_w[:, None, :].astype(h.dtype), window_strides=(1,),
        padding=[(CONV_KERNEL - 1, 0)],
        dimension_numbers=('NWC', 'WIO', 'NWC'),
        feature_group_count=CONV_WIDTH) + dw_b
    h = layernorm(h, ln_g, ln_b)
    h = jax.nn.silu(h)
    return h @ w_pw


def pool_mixer(v, w_group, scale):
    B, S, _ = v.shape
    vg = v.astype(jnp.float32).reshape(B, S, POOL_GROUPS, POOL_GROUP_DIM)
    cs = jnp.cumsum(vg, axis=1)
    t = jnp.arange(S)
    pooled = []
    for gi, w in enumerate(POOL_WINDOWS):
        c_g = cs[:, :, gi]
        shifted = jnp.pad(c_g, ((0, 0), (w, 0), (0, 0)))[:, :S]
        cnt = jnp.minimum(t + 1, w).astype(jnp.float32)
        pooled.append((c_g - shifted) / cnt[None, :, None])
    p = (jnp.stack(pooled, axis=2) - vg).astype(v.dtype)
    y = jnp.einsum('bsgc,gcd->bsgd', p, w_group).reshape(B, S, POOL_WIDTH)
    return y * scale


def _fwd_setup_inputs(seed: int = 0) -> dict:
    key = jax.random.key(seed)
    ks = jax.random.split(key, 20)
    L, D = DEPTH, D_MODEL
    n = lambda k, shp, s: jax.random.normal(k, shp, jnp.float32) * s
    return {
        "x": n(ks[0], (BATCH, SEQ, D), 1.0),
        "c": n(ks[1], (BATCH, D), 1.0),
        "w_ada": n(ks[2], (L, D, N_MOD * D), 0.3 * D ** -0.5),
        "b_ada": n(ks[3], (L, N_MOD * D), 0.01),
        "g_norm1": 1.0 + n(ks[4], (L, D), 0.05),
        "w_in": n(ks[5], (L, D, IN_WIDTH), D ** -0.5),
        "dw_w": n(ks[6], (L, CONV_KERNEL, CONV_WIDTH), CONV_KERNEL ** -0.5),
        "dw_b": n(ks[7], (L, CONV_WIDTH), 0.01),
        "conv_ln_g": 1.0 + n(ks[8], (L, CONV_WIDTH), 0.05),
        "conv_ln_b": n(ks[9], (L, CONV_WIDTH), 0.01),
        "w_conv_pw": n(ks[10], (L, CONV_WIDTH, CONV_WIDTH), CONV_WIDTH ** -0.5),
        "w_pool_group": n(ks[11], (L, POOL_GROUPS, POOL_GROUP_DIM, POOL_GROUP_DIM), POOL_GROUP_DIM ** -0.5),
        "pool_scale": 1.0 + n(ks[12], (L, POOL_WIDTH), 0.1),
        "w_out": n(ks[13], (L, MIX_WIDTH, D), MIX_WIDTH ** -0.5),
        "g_norm2": 1.0 + n(ks[14], (L, D), 0.05),
        "w_ffn_gate": n(ks[15], (L, D, D_FF), D ** -0.5),
        "w_ffn_up": n(ks[16], (L, D, D_FF), D ** -0.5),
        "w_ffn_down": n(ks[17], (L, D_FF, D), D_FF ** -0.5),
        "g_final": 1.0 + n(ks[18], (D,), 0.05),
    }


def _fwd_reference(x, c, w_ada, b_ada, g_norm1, w_in, dw_w, dw_b, conv_ln_g, conv_ln_b,
              w_conv_pw, w_pool_group, pool_scale, w_out, g_norm2,
              w_ffn_gate, w_ffn_up, w_ffn_down, g_final):
    c_act = jax.nn.silu(c)
    for l in range(DEPTH):
        mod = c_act @ w_ada[l] + b_ada[l]
        sh1, sc1, gt1, sh2, sc2, gt2 = [m[:, None, :] for m in jnp.split(mod, N_MOD, axis=-1)]

        h = rmsnorm(x, g_norm1[l]) * (1 + sc1) + sh1
        u = h @ w_in[l]
        u_conv = u[..., :2 * CONV_WIDTH]
        u_pool = u[..., 2 * CONV_WIDTH:]
        y_conv = conv_mixer(u_conv, dw_w[l], dw_b[l], conv_ln_g[l], conv_ln_b[l], w_conv_pw[l])
        y_pool = pool_mixer(u_pool, w_pool_group[l], pool_scale[l])
        y = jnp.concatenate([y_conv, y_pool], axis=-1) @ w_out[l]
        x = x + gt1 * y

        h = rmsnorm(x, g_norm2[l]) * (1 + sc2) + sh2
        f = (jax.nn.silu(h @ w_ffn_gate[l]) * (h @ w_ffn_up[l])) @ w_ffn_down[l]
        x = x + gt2 * f
    return rmsnorm(x, g_final)


import jax as _jax
import jax.numpy as _jnp

TWIN_FORMAT = 'train_step'
FWD_PARAMS = ['x', 'c', 'w_ada', 'b_ada', 'g_norm1', 'w_in', 'dw_w', 'dw_b', 'conv_ln_g', 'conv_ln_b', 'w_conv_pw', 'w_pool_group', 'pool_scale', 'w_out', 'g_norm2', 'w_ffn_gate', 'w_ffn_up', 'w_ffn_down', 'g_final']
TWIN_WEIGHTS = ['w_ada', 'b_ada', 'g_norm1', 'w_in', 'dw_w', 'dw_b', 'conv_ln_g', 'conv_ln_b', 'w_conv_pw', 'w_pool_group', 'pool_scale', 'w_out', 'g_norm2', 'w_ffn_gate', 'w_ffn_up', 'w_ffn_down', 'g_final']
TWIN_DIFF_INPUT = 'x'
TWIN_INPUTS = ['x', 'c', 'w_ada', 'b_ada', 'g_norm1', 'w_in', 'dw_w', 'dw_b', 'conv_ln_g', 'conv_ln_b', 'w_conv_pw', 'w_pool_group', 'pool_scale', 'w_out', 'g_norm2', 'w_ffn_gate', 'w_ffn_up', 'w_ffn_down', 'g_final', 'loss_target', 'm_w_ada', 'm_b_ada', 'm_g_norm1', 'm_w_in', 'm_dw_w', 'm_dw_b', 'm_conv_ln_g', 'm_conv_ln_b', 'm_w_conv_pw', 'm_w_pool_group', 'm_pool_scale', 'm_w_out', 'm_g_norm2', 'm_w_ffn_gate', 'm_w_ffn_up', 'm_w_ffn_down', 'm_g_final', 'v_w_ada', 'v_b_ada', 'v_g_norm1', 'v_w_in', 'v_dw_w', 'v_dw_b', 'v_conv_ln_g', 'v_conv_ln_b', 'v_w_conv_pw', 'v_w_pool_group', 'v_pool_scale', 'v_w_out', 'v_g_norm2', 'v_w_ffn_gate', 'v_w_ffn_up', 'v_w_ffn_down', 'v_g_final']
TWIN_OUTPUTS = ['loss', 'grad_x', 'grad_w_ada', 'grad_b_ada', 'grad_g_norm1', 'grad_w_in', 'grad_dw_w', 'grad_dw_b', 'grad_conv_ln_g', 'grad_conv_ln_b', 'grad_w_conv_pw', 'grad_w_pool_group', 'grad_pool_scale', 'grad_w_out', 'grad_g_norm2', 'grad_w_ffn_gate', 'grad_w_ffn_up', 'grad_w_ffn_down', 'grad_g_final', 'delta_w_ada', 'delta_b_ada', 'delta_g_norm1', 'delta_w_in', 'delta_dw_w', 'delta_dw_b', 'delta_conv_ln_g', 'delta_conv_ln_b', 'delta_w_conv_pw', 'delta_w_pool_group', 'delta_pool_scale', 'delta_w_out', 'delta_g_norm2', 'delta_w_ffn_gate', 'delta_w_ffn_up', 'delta_w_ffn_down', 'delta_g_final', 'new_m_w_ada', 'new_m_b_ada', 'new_m_g_norm1', 'new_m_w_in', 'new_m_dw_w', 'new_m_dw_b', 'new_m_conv_ln_g', 'new_m_conv_ln_b', 'new_m_w_conv_pw', 'new_m_w_pool_group', 'new_m_pool_scale', 'new_m_w_out', 'new_m_g_norm2', 'new_m_w_ffn_gate', 'new_m_w_ffn_up', 'new_m_w_ffn_down', 'new_m_g_final', 'new_v_w_ada', 'new_v_b_ada', 'new_v_g_norm1', 'new_v_w_in', 'new_v_dw_w', 'new_v_dw_b', 'new_v_conv_ln_g', 'new_v_conv_ln_b', 'new_v_w_conv_pw', 'new_v_w_pool_group', 'new_v_pool_scale', 'new_v_w_out', 'new_v_g_norm2', 'new_v_w_ffn_gate', 'new_v_w_ffn_up', 'new_v_w_ffn_down', 'new_v_g_final']
TWIN_LEAF_KINDS = {'loss': 'loss', 'grad_x': 'grad_x', 'grad_w_ada': 'grad_w', 'grad_b_ada': 'grad_w', 'grad_g_norm1': 'grad_w', 'grad_w_in': 'grad_w', 'grad_dw_w': 'grad_w', 'grad_dw_b': 'grad_w', 'grad_conv_ln_g': 'grad_w', 'grad_conv_ln_b': 'grad_w', 'grad_w_conv_pw': 'grad_w', 'grad_w_pool_group': 'grad_w', 'grad_pool_scale': 'grad_w', 'grad_w_out': 'grad_w', 'grad_g_norm2': 'grad_w', 'grad_w_ffn_gate': 'grad_w', 'grad_w_ffn_up': 'grad_w', 'grad_w_ffn_down': 'grad_w', 'grad_g_final': 'grad_w', 'delta_w_ada': 'delta_w', 'delta_b_ada': 'delta_w', 'delta_g_norm1': 'delta_w', 'delta_w_in': 'delta_w', 'delta_dw_w': 'delta_w', 'delta_dw_b': 'delta_w', 'delta_conv_ln_g': 'delta_w', 'delta_conv_ln_b': 'delta_w', 'delta_w_conv_pw': 'delta_w', 'delta_w_pool_group': 'delta_w', 'delta_pool_scale': 'delta_w', 'delta_w_out': 'delta_w', 'delta_g_norm2': 'delta_w', 'delta_w_ffn_gate': 'delta_w', 'delta_w_ffn_up': 'delta_w', 'delta_w_ffn_down': 'delta_w', 'delta_g_final': 'delta_w', 'new_m_w_ada': 'new_m', 'new_m_b_ada': 'new_m', 'new_m_g_norm1': 'new_m', 'new_m_w_in': 'new_m', 'new_m_dw_w': 'new_m', 'new_m_dw_b': 'new_m', 'new_m_conv_ln_g': 'new_m', 'new_m_conv_ln_b': 'new_m', 'new_m_w_conv_pw': 'new_m', 'new_m_w_pool_group': 'new_m', 'new_m_pool_scale': 'new_m', 'new_m_w_out': 'new_m', 'new_m_g_norm2': 'new_m', 'new_m_w_ffn_gate': 'new_m', 'new_m_w_ffn_up': 'new_m', 'new_m_w_ffn_down': 'new_m', 'new_m_g_final': 'new_m', 'new_v_w_ada': 'new_v', 'new_v_b_ada': 'new_v', 'new_v_g_norm1': 'new_v', 'new_v_w_in': 'new_v', 'new_v_dw_w': 'new_v', 'new_v_dw_b': 'new_v', 'new_v_conv_ln_g': 'new_v', 'new_v_conv_ln_b': 'new_v', 'new_v_w_conv_pw': 'new_v', 'new_v_w_pool_group': 'new_v', 'new_v_pool_scale': 'new_v', 'new_v_w_out': 'new_v', 'new_v_g_norm2': 'new_v', 'new_v_w_ffn_gate': 'new_v', 'new_v_w_ffn_up': 'new_v', 'new_v_w_ffn_down': 'new_v', 'new_v_g_final': 'new_v'}


def _forward(args):
    return _fwd_reference(*[args[k] for k in FWD_PARAMS])


def _output_shape():
    out = _jax.eval_shape(lambda: _forward(_fwd_setup_inputs(0)))
    return out.shape, out.dtype

N_MICROBATCH = 1
ADAM_LR = 0.001
ADAM_B1 = 0.9
ADAM_B2 = 0.999
ADAM_EPS = 1e-08
ADAM_WD = 0.01
ADAM_STEP = 10
PER_EXAMPLE_BATCH_AXIS = {'x': 0, 'c': 0, 'loss_target': 0}
SHARED_INPUTS = []
_WEIGHT_DTYPES = {'w_ada': _jnp.float32, 'b_ada': _jnp.float32, 'g_norm1': _jnp.float32, 'w_in': _jnp.float32, 'dw_w': _jnp.float32, 'dw_b': _jnp.float32, 'conv_ln_g': _jnp.float32, 'conv_ln_b': _jnp.float32, 'w_conv_pw': _jnp.float32, 'w_pool_group': _jnp.float32, 'pool_scale': _jnp.float32, 'w_out': _jnp.float32, 'g_norm2': _jnp.float32, 'w_ffn_gate': _jnp.float32, 'w_ffn_up': _jnp.float32, 'w_ffn_down': _jnp.float32, 'g_final': _jnp.float32}
MOMENT_SCALE = {'w_ada': 3.736054e-02, 'b_ada': 6.394355e-02, 'g_norm1': 1.830084e-02, 'w_in': 1.512462e-02, 'dw_w': 1.449891e-02, 'dw_b': 2.901904e-02, 'conv_ln_g': 1.703243e-02, 'conv_ln_b': 1.651105e-02, 'w_conv_pw': 1.416781e-02, 'w_pool_group': 2.121337e-02, 'pool_scale': 2.029817e-02, 'w_out': 1.802178e-02, 'g_norm2': 2.185492e-02, 'w_ffn_gate': 9.177627e-03, 'w_ffn_up': 8.900568e-03, 'w_ffn_down': 1.475872e-02, 'g_final': 1.603212e+01}


def _to_microbatches(a, axis):
    t = _jnp.moveaxis(a, axis, 0)
    t = t.reshape((N_MICROBATCH, t.shape[0] // N_MICROBATCH) + t.shape[1:])
    return _jnp.moveaxis(t, 1, axis + 1)


def setup_inputs(seed: int = 0) -> dict:
    inp = _fwd_setup_inputs(seed)
    key = _jax.random.fold_in(_jax.random.key(seed), 7919)
    shape, _ = _output_shape()
    out = dict(inp)
    out["loss_target"] = _jax.random.normal(_jax.random.fold_in(key, 0), shape, _jnp.float32)
    for i, name in enumerate(TWIN_WEIGHTS):
        w = inp[name].astype(_jnp.float32)
        if MOMENT_SCALE is None:
            s = _jnp.sqrt(_jnp.mean(_jnp.square(w)) + 1e-30)
        else:
            s = MOMENT_SCALE[name]
        km, kv = _jax.random.split(_jax.random.fold_in(key, i + 1))
        out[name] = w
        out["m_" + name] = s * _jax.random.normal(km, w.shape, _jnp.float32)
        out["v_" + name] = (s * s) * _jax.random.uniform(kv, w.shape, _jnp.float32, 0.5, 1.5)
    if N_MICROBATCH > 1:
        for name, axis in PER_EXAMPLE_BATCH_AXIS.items():
            out[name] = _to_microbatches(out[name], axis)
    return {'x': out['x'], 'c': out['c'], 'w_ada': out['w_ada'], 'b_ada': out['b_ada'], 'g_norm1': out['g_norm1'], 'w_in': out['w_in'], 'dw_w': out['dw_w'], 'dw_b': out['dw_b'], 'conv_ln_g': out['conv_ln_g'], 'conv_ln_b': out['conv_ln_b'], 'w_conv_pw': out['w_conv_pw'], 'w_pool_group': out['w_pool_group'], 'pool_scale': out['pool_scale'], 'w_out': out['w_out'], 'g_norm2': out['g_norm2'], 'w_ffn_gate': out['w_ffn_gate'], 'w_ffn_up': out['w_ffn_up'], 'w_ffn_down': out['w_ffn_down'], 'g_final': out['g_final'], 'loss_target': out['loss_target'], 'm_w_ada': out['m_w_ada'], 'm_b_ada': out['m_b_ada'], 'm_g_norm1': out['m_g_norm1'], 'm_w_in': out['m_w_in'], 'm_dw_w': out['m_dw_w'], 'm_dw_b': out['m_dw_b'], 'm_conv_ln_g': out['m_conv_ln_g'], 'm_conv_ln_b': out['m_conv_ln_b'], 'm_w_conv_pw': out['m_w_conv_pw'], 'm_w_pool_group': out['m_w_pool_group'], 'm_pool_scale': out['m_pool_scale'], 'm_w_out': out['m_w_out'], 'm_g_norm2': out['m_g_norm2'], 'm_w_ffn_gate': out['m_w_ffn_gate'], 'm_w_ffn_up': out['m_w_ffn_up'], 'm_w_ffn_down': out['m_w_ffn_down'], 'm_g_final': out['m_g_final'], 'v_w_ada': out['v_w_ada'], 'v_b_ada': out['v_b_ada'], 'v_g_norm1': out['v_g_norm1'], 'v_w_in': out['v_w_in'], 'v_dw_w': out['v_dw_w'], 'v_dw_b': out['v_dw_b'], 'v_conv_ln_g': out['v_conv_ln_g'], 'v_conv_ln_b': out['v_conv_ln_b'], 'v_w_conv_pw': out['v_w_conv_pw'], 'v_w_pool_group': out['v_w_pool_group'], 'v_pool_scale': out['v_pool_scale'], 'v_w_out': out['v_w_out'], 'v_g_norm2': out['v_g_norm2'], 'v_w_ffn_gate': out['v_w_ffn_gate'], 'v_w_ffn_up': out['v_w_ffn_up'], 'v_w_ffn_down': out['v_w_ffn_down'], 'v_g_final': out['v_g_final']}


def _loss(weights, diff, rest, loss_target):
    with _jax.named_scope("forward"):
        args = {**rest, TWIN_DIFF_INPUT: diff, **{k: w.astype(_WEIGHT_DTYPES[k]) for k, w in weights.items()}}
        y = _forward(args)
    with _jax.named_scope("loss_head"):
        err = _jnp.square(y.astype(_jnp.float32) - loss_target)
        return 0.5 * _jnp.sum(_jnp.mean(err, axis=-1)) if err.ndim else 0.5 * err


def _adamw(w, g, m, v):
    m = ADAM_B1 * m + (1.0 - ADAM_B1) * g
    v = ADAM_B2 * v + (1.0 - ADAM_B2) * _jnp.square(g)
    m_hat = m / (1.0 - ADAM_B1 ** ADAM_STEP)
    v_hat = v / (1.0 - ADAM_B2 ** ADAM_STEP)
    delta = -ADAM_LR * (m_hat / (_jnp.sqrt(v_hat) + ADAM_EPS) + ADAM_WD * w)
    return delta, m, v


def reference(x, c, w_ada, b_ada, g_norm1, w_in, dw_w, dw_b, conv_ln_g, conv_ln_b, w_conv_pw, w_pool_group, pool_scale, w_out, g_norm2, w_ffn_gate, w_ffn_up, w_ffn_down, g_final, loss_target, m_w_ada, m_b_ada, m_g_norm1, m_w_in, m_dw_w, m_dw_b, m_conv_ln_g, m_conv_ln_b, m_w_conv_pw, m_w_pool_group, m_pool_scale, m_w_out, m_g_norm2, m_w_ffn_gate, m_w_ffn_up, m_w_ffn_down, m_g_final, v_w_ada, v_b_ada, v_g_norm1, v_w_in, v_dw_w, v_dw_b, v_conv_ln_g, v_conv_ln_b, v_w_conv_pw, v_w_pool_group, v_pool_scale, v_w_out, v_g_norm2, v_w_ffn_gate, v_w_ffn_up, v_w_ffn_down, v_g_final):
    given = dict(x=x, c=c, w_ada=w_ada, b_ada=b_ada, g_norm1=g_norm1, w_in=w_in, dw_w=dw_w, dw_b=dw_b, conv_ln_g=conv_ln_g, conv_ln_b=conv_ln_b, w_conv_pw=w_conv_pw, w_pool_group=w_pool_group, pool_scale=pool_scale, w_out=w_out, g_norm2=g_norm2, w_ffn_gate=w_ffn_gate, w_ffn_up=w_ffn_up, w_ffn_down=w_ffn_down, g_final=g_final, loss_target=loss_target, m_w_ada=m_w_ada, m_b_ada=m_b_ada, m_g_norm1=m_g_norm1, m_w_in=m_w_in, m_dw_w=m_dw_w, m_dw_b=m_dw_b, m_conv_ln_g=m_conv_ln_g, m_conv_ln_b=m_conv_ln_b, m_w_conv_pw=m_w_conv_pw, m_w_pool_group=m_w_pool_group, m_pool_scale=m_pool_scale, m_w_out=m_w_out, m_g_norm2=m_g_norm2, m_w_ffn_gate=m_w_ffn_gate, m_w_ffn_up=m_w_ffn_up, m_w_ffn_down=m_w_ffn_down, m_g_final=m_g_final, v_w_ada=v_w_ada, v_b_ada=v_b_ada, v_g_norm1=v_g_norm1, v_w_in=v_w_in, v_dw_w=v_dw_w, v_dw_b=v_dw_b, v_conv_ln_g=v_conv_ln_g, v_conv_ln_b=v_conv_ln_b, v_w_conv_pw=v_w_conv_pw, v_w_pool_group=v_w_pool_group, v_pool_scale=v_pool_scale, v_w_out=v_w_out, v_g_norm2=v_g_norm2, v_w_ffn_gate=v_w_ffn_gate, v_w_ffn_up=v_w_ffn_up, v_w_ffn_down=v_w_ffn_down, v_g_final=v_g_final)
    weights = {n: given[n] for n in TWIN_WEIGHTS}
    shared = {n: given[n] for n in SHARED_INPUTS}
    per_example = {n: given[n] for n in ['x', 'c']}
    grad_fn = _jax.value_and_grad(_loss, argnums=(0, 1))

    def one_microbatch(ex, loss_target):
        ex = dict(ex)
        diff = ex.pop(TWIN_DIFF_INPUT)
        return grad_fn(weights, diff, {**shared, **ex}, loss_target)

    if N_MICROBATCH == 1:
        loss, (grad_w, grad_x) = one_microbatch(per_example, given["loss_target"])
    else:
        def body(carry, xs):
            loss_sum, grad_sum = carry
            l_k, (gw_k, gx_k) = one_microbatch(xs[0], xs[1])
            with _jax.named_scope("update"):
                return (loss_sum + l_k, _jax.tree.map(_jnp.add, grad_sum, gw_k)), gx_k

        init = (_jnp.zeros((), _jnp.float32), _jax.tree.map(_jnp.zeros_like, weights))
        (loss, grad_w), grad_x = _jax.lax.scan(body, init, (per_example, given["loss_target"]))
    with _jax.named_scope("update"):
        delta_w, new_m, new_v = {}, {}, {}
        for n in TWIN_WEIGHTS:
            delta_w[n], new_m[n], new_v[n] = _adamw(weights[n], grad_w[n], given["m_" + n], given["v_" + n])
    return (loss, grad_x, *[grad_w[n] for n in TWIN_WEIGHTS], *[delta_w[n] for n in TWIN_WEIGHTS],
            *[new_m[n] for n in TWIN_WEIGHTS], *[new_v[n] for n in TWIN_WEIGHTS])
```

```python
import functools

import jax
import jax.numpy as jnp
from jax import lax
from jax.experimental import pallas as pl
from jax.experimental.pallas import tpu as pltpu

F32, BF16 = jnp.float32, jnp.bfloat16
SEQ, DM = 2048, 1024
CONVW, POOLW = 512, 512
KCONV = 31
WINS = (2, 4, 8, 16)
PGD = 128
DFF = 2816
NDEV = 8
MODW = 6 * DM // NDEV
EPS = 1e-6
TOK = 256
NTILE = SEQ // TOK
CH = 256
NCH = DFF // CH
HALO_C, HALO_P = 32, 16
MESH = pl.DeviceIdType.MESH
VMEM_LIMIT = 56 * 1024 * 1024
ADAM_LR, ADAM_B1, ADAM_B2, ADAM_EPS, ADAM_WD, ADAM_STEP = 0.001, 0.9, 0.999, 1e-08, 0.01, 10
HI = lax.Precision.HIGHEST

_VM = pl.BlockSpec(memory_space=pltpu.VMEM)
_HBM = pl.BlockSpec(memory_space=pltpu.HBM)


def _place():
    x, y, c = lax.axis_index("x"), lax.axis_index("y"), lax.axis_index("c")
    return x, y, c, 4 * x + 2 * y + c


def _flip(x, y, c, r):
    px = 1 - x if r & 4 else x
    py = 1 - y if r & 2 else y
    pc = 1 - c if r & 1 else c
    return (px, py, pc), 4 * px + 2 * py + pc


def _rows(ref, blk, n):
    return ref.at[pl.ds(pl.multiple_of(blk * n, 16), n), :]


def _sig(z):
    return jax.nn.sigmoid(z)


def _dot_nt(a, b):
    return lax.dot_general(a, b, (((1,), (1,)), ((), ())), preferred_element_type=F32)


def _dot_nn(a, b):
    return lax.dot_general(a, b, (((1,), (0,)), ((), ())), preferred_element_type=F32)


def _dot_tn(a, b):
    return lax.dot_general(a, b, (((0,), (0,)), ((), ())), preferred_element_type=F32)


def _gather_call(c, w_ada, b_my, dww, shards):
    nw = len(shards)
    rws = [s.shape[0] for s in shards]

    def body(*refs):
        c_ref, wada_ref, bmy_ref, dww_ref = refs[:4]
        s_refs = refs[4 : 4 + nw]
        g_refs = refs[4 + nw : 4 + 2 * nw]
        mod8_ref, cact_ref, dww8_ref = refs[4 + 2 * nw : 7 + 2 * nw]
        crecv, msend, mrecv = refs[7 + 2 * nw : 10 + 2 * nw]
        sbufs = refs[10 + 2 * nw : 10 + 3 * nw]
        csem_s, csem_r, dsem_s, dsem_r, msem_s, msem_r, wsem_s, wsem_r, fsem_s, fsem_r, lsem = refs[10 + 3 * nw :]
        x, y, c_, me = _place()

        def rcopy(src, dst, ss, rs, dev):
            return pltpu.make_async_remote_copy(src_ref=src, dst_ref=dst, send_sem=ss, recv_sem=rs, device_id=dev, device_id_type=MESH)

        crecv[me] = jnp.broadcast_to(c_ref[...], (8, DM))
        dww8_ref[me] = dww_ref[...]
        small = []
        for r in range(1, NDEV):
            dev, _ = _flip(x, y, c_, r)
            small.append(rcopy(crecv.at[me], crecv.at[me], csem_s.at[r], csem_r.at[r], dev))
            small.append(rcopy(dww8_ref.at[me], dww8_ref.at[me], dsem_s.at[r], dsem_r.at[r], dev))
        for cp in small:
            cp.start()

        first, local = [], []
        ways = (1, 4, 2, 6)
        for k in range(nw):
            sbufs[k][...] = s_refs[k][...].astype(BF16)
            lc = pltpu.make_async_copy(sbufs[k], _rows(g_refs[k], me, rws[k]), lsem.at[k])
            lc.start()
            local.append(lc)
            for j, r in enumerate(ways):
                dev, _ = _flip(x, y, c_, r)
                cp = rcopy(sbufs[k], _rows(g_refs[k], me, rws[k]), wsem_s.at[k, j], wsem_r.at[k, j], dev)
                cp.start()
                first.append(cp)

        rowid = lax.broadcasted_iota(jnp.int32, (8, DM), 0)
        for r in range(1, NDEV):
            _, pb = _flip(x, y, c_, r)
            rcopy(crecv.at[me], crecv.at[pb], csem_s.at[r], csem_r.at[r], (x, y, c_)).wait_recv()
        call = jnp.zeros((8, DM), F32)
        for s in range(NDEV):
            call = jnp.where(rowid == s, crecv[s], call)
        cact = call * _sig(call)
        cact_ref[...] = cact
        modp = jnp.dot(cact, wada_ref[...], precision=HI, preferred_element_type=F32) + bmy_ref[...]
        rowm = lax.broadcasted_iota(jnp.int32, (8, MODW), 0)
        for b in range(NDEV):
            row = jnp.sum(jnp.where(rowm == b, modp, 0.0), axis=0, keepdims=True)
            msend[b] = jnp.broadcast_to(row, (8, MODW))
        mrecv[me] = msend[me]
        msends = []
        for r in range(1, NDEV):
            dev, pb = _flip(x, y, c_, r)
            cp = rcopy(msend.at[pb], mrecv.at[me], msem_s.at[r], msem_r.at[r], dev)
            cp.start()
            msends.append(cp)

        sib, _ = _flip(x, y, c_, 1)
        passed = []
        for k in range(nw):
            for j, r in enumerate(ways[1:]):
                _, pb = _flip(x, y, c_, r)
                blk = _rows(g_refs[k], pb, rws[k])
                rcopy(sbufs[k], blk, wsem_s.at[k, j + 1], wsem_r.at[k, j + 1], (x, y, c_)).wait_recv()
                cp = rcopy(blk, blk, fsem_s.at[k, j], fsem_r.at[k, j], sib)
                cp.start()
                passed.append(cp)

        for r in range(1, NDEV):
            _, pb = _flip(x, y, c_, r)
            rcopy(msend.at[pb], mrecv.at[pb], msem_s.at[r], msem_r.at[r], (x, y, c_)).wait_recv()
        for s in range(NDEV):
            mod8_ref[:, s * MODW : (s + 1) * MODW] = mrecv[s]

        for k in range(nw):
            _, sb = _flip(x, y, c_, 1)
            rcopy(sbufs[k], _rows(g_refs[k], sb, rws[k]), wsem_s.at[k, 0], wsem_r.at[k, 0], (x, y, c_)).wait_recv()
            for j, r in enumerate(ways[1:]):
                _, pb = _flip(x, y, c_, r ^ 1)
                blk = _rows(g_refs[k], pb, rws[k])
                rcopy(blk, blk, fsem_s.at[k, j], fsem_r.at[k, j], (x, y, c_)).wait_recv()
        for r in range(1, NDEV):
            _, pb = _flip(x, y, c_, r)
            rcopy(dww8_ref.at[me], dww8_ref.at[pb], dsem_s.at[r], dsem_r.at[r], (x, y, c_)).wait_recv()
        for cp in small + first + msends + passed:
            cp.wait_send()
        for lc in local:
            lc.wait()

    out_shape = [jax.ShapeDtypeStruct((NDEV * s.shape[0], s.shape[1]), BF16) for s in shards]
    out_shape += [
        jax.ShapeDtypeStruct((8, 6 * DM), F32),
        jax.ShapeDtypeStruct((8, DM), F32),
        jax.ShapeDtypeStruct((NDEV,) + dww.shape, F32),
    ]
    scratch = [pltpu.VMEM((NDEV, 8, DM), F32), pltpu.VMEM((NDEV, 8, MODW), F32), pltpu.VMEM((NDEV, 8, MODW), F32)]
    scratch += [pltpu.VMEM(s.shape, BF16) for s in shards]
    scratch += [pltpu.SemaphoreType.DMA((NDEV,))] * 6
    scratch += [pltpu.SemaphoreType.DMA((nw, 4))] * 2 + [pltpu.SemaphoreType.DMA((nw, 3))] * 2 + [pltpu.SemaphoreType.DMA((nw,))]
    return pl.pallas_call(
        body,
        name="gather",
        out_shape=out_shape,
        in_specs=[_VM] * (4 + nw),
        out_specs=[_HBM] * nw + [_VM] * 3,
        scratch_shapes=scratch,
        compiler_params=pltpu.CompilerParams(vmem_limit_bytes=VMEM_LIMIT),
    )(c, w_ada, b_my, dww, *shards)


def _const(shape):
    return pl.BlockSpec(shape, lambda i: (0,) * len(shape))


def _tile(width, rev=False):
    if rev:
        return pl.BlockSpec((TOK, width), lambda i: (NTILE - 1 - i, 0))
    return pl.BlockSpec((TOK, width), lambda i: (i, 0))


def _norm_mod(x, g, sc, sh):
    r = lax.rsqrt(jnp.mean(x * x, axis=-1, keepdims=True) + EPS)
    xr = x * r
    return r, xr, xr * g * (1.0 + sc) + sh


def _fwd_mix_call(x, mod8, g1, win_t, dww, dwb, lng, lnb, wpw, wg, psc, wout):
    def body(x_ref, mod_ref, g1_ref, win_ref, dww_ref, dwb_ref, lng_ref, lnb_ref, wpw_ref, wg_ref, psc_ref, wout_ref,
             h1b_ref, uag_ref, hc_ref, hd_ref, hsb_ref, pb_ref, ycb_ref, y_ref, x1_ref, hc_ext, up_ext):
        i = pl.program_id(0)

        @pl.when(i == 0)
        def _():
            hc_ext[0:HALO_C, :] = jnp.zeros((HALO_C, CONVW), F32)
            up_ext[0:HALO_P, :] = jnp.zeros((HALO_P, POOLW), F32)

        x = x_ref[...]
        sh1, sc1, gt1 = mod_ref[0:1, 0:DM], mod_ref[0:1, DM : 2 * DM], mod_ref[0:1, 2 * DM : 3 * DM]
        _, _, h1 = _norm_mod(x, g1_ref[...], sc1, sh1)
        h1b = h1.astype(BF16)
        h1b_ref[...] = h1b
        u = _dot_nt(h1b, win_ref[...])
        uag_ref[...] = u[:, : 2 * CONVW]
        hc = u[:, :CONVW] * _sig(u[:, CONVW : 2 * CONVW])
        hc_ref[...] = hc
        hc_ext[HALO_C : HALO_C + TOK, :] = hc
        up_ext[HALO_P : HALO_P + TOK, :] = u[:, 2 * CONVW :]

        acc = jnp.zeros((TOK, CONVW), F32)
        for k in range(KCONV):
            acc = acc + dww_ref[k : k + 1, :] * hc_ext[pl.ds(HALO_C - (KCONV - 1) + k, TOK), :]
        hd = acc + dwb_ref[...]
        hd_ref[...] = hd
        hc_ext[0:HALO_C, :] = hc_ext[TOK : TOK + HALO_C, :]
        mu = jnp.mean(hd, axis=-1, keepdims=True)
        dlt = hd - mu
        rstd = lax.rsqrt(jnp.mean(dlt * dlt, axis=-1, keepdims=True) + EPS)
        hl = dlt * rstd * lng_ref[...] + lnb_ref[...]
        hsb = (hl * _sig(hl)).astype(BF16)
        hsb_ref[...] = hsb
        ycb_ref[:, 0:CONVW] = _dot_nn(hsb, wpw_ref[...]).astype(BF16)

        tg = i * TOK + lax.broadcasted_iota(jnp.int32, (TOK, 1), 0)
        for g, w in enumerate(WINS):
            ln = slice(PGD * g, PGD * (g + 1))
            v = up_ext[pl.ds(HALO_P, TOK), ln]
            ssum = v
            for d in range(1, w):
                ssum = ssum + up_ext[pl.ds(HALO_P - d, TOK), ln]
            cnt = jnp.minimum(tg + 1, w).astype(F32)
            pb = (ssum / cnt - v).astype(BF16)
            pb_ref[:, ln] = pb
            z = _dot_nn(pb, wg_ref[g].astype(BF16))
            ycb_ref[:, CONVW + PGD * g : CONVW + PGD * (g + 1)] = (z * psc_ref[:, ln]).astype(BF16)
        up_ext[0:HALO_P, :] = up_ext[TOK : TOK + HALO_P, :]

        yv = _dot_nn(ycb_ref[...], wout_ref[...])
        y_ref[...] = yv
        x1_ref[...] = x + gt1 * yv

    outs = [(DM, BF16), (2 * CONVW, F32), (CONVW, F32), (CONVW, F32), (CONVW, BF16), (POOLW, BF16), (DM, BF16), (DM, F32), (DM, F32)]
    return pl.pallas_call(
        body,
        name="fwd_mix",
        grid=(NTILE,),
        out_shape=[jax.ShapeDtypeStruct((SEQ, w), d) for w, d in outs],
        in_specs=[_tile(DM), _const((8, 6 * DM)), _const((1, DM)), _const(win_t.shape), _const(dww.shape), _const((1, CONVW)),
                  _const((1, CONVW)), _const((1, CONVW)), _const(wpw.shape), _const(wg.shape), _const((1, POOLW)), _const(wout.shape)],
        out_specs=[_tile(w) for w, _ in outs],
        scratch_shapes=[pltpu.VMEM((TOK + HALO_C, CONVW), F32), pltpu.VMEM((TOK + HALO_P, POOLW), F32)],
        compiler_params=pltpu.CompilerParams(dimension_semantics=("arbitrary",), vmem_limit_bytes=VMEM_LIMIT),
    )(x, mod8, g1, win_t, dww, dwb, lng, lnb, wpw, wg, psc, wout)


def _ffn_call(x1, tgt, mod8, g2, gf, wg_t, wu_t, wd):
    def body(x1_ref, tgt_ref, mod_ref, g2_ref, gf_ref, wg_hbm, wu_hbm, wd_hbm,
             h2b_ref, a3_ref, dfb_ref, dg3_ref, du3_ref, dx1_ref, acc_ref,
             wg_ref, wu_ref, wd_ref, gate_s, up_s, f_s, dh_s, wsem):
        i = pl.program_id(0)

        @pl.when(i == 0)
        def _():
            cps = [pltpu.make_async_copy(s, d, wsem.at[n]) for n, (s, d) in enumerate(((wg_hbm, wg_ref), (wu_hbm, wu_ref), (wd_hbm, wd_ref)))]
            for cp in cps:
                cp.start()
            acc_ref[...] = jnp.zeros((8, DM), F32)
            for cp in cps:
                cp.wait()

        x1 = x1_ref[...]
        sh2, sc2, gt2 = mod_ref[0:1, 3 * DM : 4 * DM], mod_ref[0:1, 4 * DM : 5 * DM], mod_ref[0:1, 5 * DM : 6 * DM]
        g2 = g2_ref[...]
        r2, xr, h2 = _norm_mod(x1, g2, sc2, sh2)
        h2b = h2.astype(BF16)
        h2b_ref[...] = h2b
        f_s[...] = jnp.zeros((TOK, DM), F32)

        def fwd_chunk(cix, carry):
            rows = pl.ds(pl.multiple_of(cix * CH, CH), CH)
            gate = _dot_nt(h2b, wg_ref[rows, :])
            up = _dot_nt(h2b, wu_ref[rows, :])
            gate_s[cix] = gate
            up_s[cix] = up
            ab = (gate * _sig(gate) * up).astype(BF16)
            a3_ref[cix] = ab
            f_s[...] += _dot_nn(ab, wd_ref[rows, :])
            return carry

        lax.fori_loop(0, NCH, fwd_chunk, 0)

        f = f_s[...]
        x2 = x1 + gt2 * f
        rf = lax.rsqrt(jnp.mean(x2 * x2, axis=-1, keepdims=True) + EPS)
        nf = x2 * rf
        gf_ = gf_ref[...]
        err = nf * gf_ - tgt_ref[...]
        loss = 0.5 * jnp.sum(jnp.sum(err * err, axis=-1, keepdims=True), axis=0, keepdims=True) * (1.0 / DM)
        dout = err * (1.0 / DM)
        dnf = dout * gf_
        dx2 = rf * (dnf - nf * jnp.mean(dnf * nf, axis=-1, keepdims=True))
        dfb = (gt2 * dx2).astype(BF16)
        dfb_ref[...] = dfb
        dh_s[...] = jnp.zeros((TOK, DM), F32)

        def bwd_chunk(cix, carry):
            rows = pl.ds(pl.multiple_of(cix * CH, CH), CH)
            da = _dot_nt(dfb, wd_ref[rows, :])
            gate, up = gate_s[cix], up_s[cix]
            sg = _sig(gate)
            sl = gate * sg
            dgb = (da * up * (sg * (1.0 + gate * (1.0 - sg)))).astype(BF16)
            dub = (da * sl).astype(BF16)
            dg3_ref[cix] = dgb
            du3_ref[cix] = dub
            dh_s[...] += _dot_nn(dgb, wg_ref[rows, :]) + _dot_nn(dub, wu_ref[rows, :])
            return carry

        lax.fori_loop(0, NCH, bwd_chunk, 0)

        dh2 = dh_s[...]
        dn2 = dh2 * (1.0 + sc2)
        dxr = dn2 * g2
        dx1_ref[...] = dx2 + r2 * (dxr - xr * jnp.mean(dxr * xr, axis=-1, keepdims=True))

        def colsum(v):
            return jnp.sum(v, axis=0, keepdims=True)

        acc_ref[0:1, :] += colsum(dh2)
        acc_ref[1:2, :] += colsum(dh2 * (xr * g2))
        acc_ref[2:3, :] += colsum(dx2 * f)
        acc_ref[3:4, :] += colsum(dn2 * xr)
        acc_ref[4:5, :] += colsum(dout * nf)
        acc_ref[5:6, :] += jnp.broadcast_to(loss, (1, DM))

    c3 = pl.BlockSpec((NCH, TOK, CH), lambda i: (0, i, 0))
    return pl.pallas_call(
        body,
        name="ffn",
        grid=(NTILE,),
        out_shape=[jax.ShapeDtypeStruct((SEQ, DM), BF16), jax.ShapeDtypeStruct((NCH, SEQ, CH), BF16), jax.ShapeDtypeStruct((SEQ, DM), BF16),
                   jax.ShapeDtypeStruct((NCH, SEQ, CH), BF16), jax.ShapeDtypeStruct((NCH, SEQ, CH), BF16),
                   jax.ShapeDtypeStruct((SEQ, DM), F32), jax.ShapeDtypeStruct((8, DM), F32)],
        in_specs=[_tile(DM), _tile(DM), _const((8, 6 * DM)), _const((1, DM)), _const((1, DM)), _HBM, _HBM, _HBM],
        out_specs=[_tile(DM), c3, _tile(DM), c3, c3, _tile(DM), _const((8, DM))],
        scratch_shapes=[pltpu.VMEM((DFF, DM), BF16)] * 3 + [pltpu.VMEM((NCH, TOK, CH), F32)] * 2 + [pltpu.VMEM((TOK, DM), F32)] * 2
        + [pltpu.SemaphoreType.DMA((3,))],
        compiler_params=pltpu.CompilerParams(dimension_semantics=("arbitrary",), vmem_limit_bytes=VMEM_LIMIT),
    )(x1, tgt, mod8, g2, gf, wg_t, wu_t, wd)


def _bwd_mix_call(dx1, x, y, uag, hc, hd, pbv, mod8, g1, win_t, dww, lng, lnb, wpw, wg, psc, wout):
    hpt = TOK // HALO_C

    def body(dx1_ref, x_ref, y_ref, uag_ref, hc_ref, halo_ref, hd_ref, pb_ref, mod_ref, g1_ref, win_ref, dww_ref, lng_ref, lnb_ref,
             wpw_ref, wg_ref, psc_ref, wout_ref,
             gx_ref, dyb_ref, dycb_ref, dub_ref, acc_ref, ddw_ref, dwg_ref, d_ext, q_ext, hcx):
        i = pl.program_id(0)
        it = NTILE - 1 - i

        @pl.when(i == 0)
        def _():
            d_ext[TOK : TOK + HALO_C, :] = jnp.zeros((HALO_C, CONVW), F32)
            q_ext[TOK : TOK + HALO_P, :] = jnp.zeros((HALO_P, POOLW), F32)
            acc_ref[...] = jnp.zeros((8, DM), F32)
            ddw_ref[...] = jnp.zeros((HALO_C, CONVW), F32)
            dwg_ref[...] = jnp.zeros((len(WINS) * PGD, PGD), F32)

        def colsum(v):
            return jnp.sum(v, axis=0, keepdims=True)

        dx1 = dx1_ref[...]
        x = x_ref[...]
        sh1, sc1, gt1 = mod_ref[0:1, 0:DM], mod_ref[0:1, DM : 2 * DM], mod_ref[0:1, 2 * DM : 3 * DM]
        acc_ref[2:3, :] += colsum(dx1 * y_ref[...])
        dyb = (gt1 * dx1).astype(BF16)
        dyb_ref[...] = dyb
        dycat = _dot_nt(dyb, wout_ref[...])

        hd = hd_ref[...]
        mu = jnp.mean(hd, axis=-1, keepdims=True)
        dlt = hd - mu
        rstd = lax.rsqrt(jnp.mean(dlt * dlt, axis=-1, keepdims=True) + EPS)
        xhat = dlt * rstd
        lng = lng_ref[...]
        hl = xhat * lng + lnb_ref[...]
        sgl = _sig(hl)
        dycb = dycat[:, :CONVW].astype(BF16)
        dycb_ref[...] = dycb
        dhl = _dot_nt(dycb, wpw_ref[...]) * (sgl * (1.0 + hl * (1.0 - sgl)))
        acc_ref[5:6, 0:CONVW] += colsum(dhl)
        acc_ref[4:5, CONVW:DM] += colsum(dhl * xhat)
        dxh = dhl * lng
        dhd = rstd * (dxh - jnp.mean(dxh, axis=-1, keepdims=True) - xhat * jnp.mean(dxh * xhat, axis=-1, keepdims=True))
        acc_ref[4:5, 0:CONVW] += colsum(dhd)

        hcx[0:HALO_C, :] = jnp.where(it == 0, 0.0, halo_ref[...])
        hcx[HALO_C : HALO_C + TOK, :] = hc_ref[...]
        d_ext[0:TOK, :] = dhd
        dhc = jnp.zeros((TOK, CONVW), F32)
        for k in range(KCONV):
            ddw_ref[k : k + 1, :] += colsum(dhd * hcx[pl.ds(HALO_C - (KCONV - 1) + k, TOK), :])
            dhc = dhc + dww_ref[k : k + 1, :] * d_ext[pl.ds(KCONV - 1 - k, TOK), :]
        d_ext[TOK : TOK + HALO_C, :] = d_ext[0:HALO_C, :]
        ua, ug = uag_ref[:, 0:CONVW], uag_ref[:, CONVW : 2 * CONVW]
        sgg = _sig(ug)
        dub_ref[:, 0:CONVW] = (dhc * sgg).astype(BF16)
        dub_ref[:, CONVW : 2 * CONVW] = (dhc * ua * sgg * (1.0 - sgg)).astype(BF16)

        tg = it * TOK + lax.broadcasted_iota(jnp.int32, (TOK, 1), 0)
        for g, w in enumerate(WINS):
            ln = slice(PGD * g, PGD * (g + 1))
            wgb = wg_ref[g].astype(BF16)
            pb = pb_ref[:, ln]
            dyp = dycat[:, CONVW + PGD * g : CONVW + PGD * (g + 1)]
            acc_ref[5:6, CONVW + PGD * g : CONVW + PGD * (g + 1)] += colsum(dyp * _dot_nn(pb, wgb))
            dzb = (dyp * psc_ref[:, ln]).astype(BF16)
            dwg_ref[PGD * g : PGD * (g + 1), :] += _dot_tn(pb, dzb)
            dp = _dot_nt(dzb, wgb)
            cnt = jnp.minimum(tg + 1, w).astype(F32)
            q_ext[0:TOK, ln] = dp / cnt
            dv = -dp
            for d in range(w):
                dv = dv + q_ext[pl.ds(d, TOK), ln]
            dub_ref[:, 2 * CONVW + PGD * g : 2 * CONVW + PGD * (g + 1)] = dv.astype(BF16)
        q_ext[TOK : TOK + HALO_P, :] = q_ext[0:HALO_P, :]

        dh1 = _dot_nn(dub_ref[...], win_ref[...])
        g1 = g1_ref[...]
        r1 = lax.rsqrt(jnp.mean(x * x, axis=-1, keepdims=True) + EPS)
        xr = x * r1
        acc_ref[0:1, :] += colsum(dh1)
        acc_ref[1:2, :] += colsum(dh1 * (xr * g1))
        dn1 = dh1 * (1.0 + sc1)
        acc_ref[3:4, :] += colsum(dn1 * xr)
        dxr = dn1 * g1
        gx_ref[...] = dx1 + r1 * (dxr - xr * jnp.mean(dxr * xr, axis=-1, keepdims=True))

    halo = pl.BlockSpec((HALO_C, CONVW), lambda i: (jnp.maximum((NTILE - 1 - i) * hpt - 1, 0), 0))
    return pl.pallas_call(
        body,
        name="bwd_mix",
        grid=(NTILE,),
        out_shape=[jax.ShapeDtypeStruct((SEQ, DM), F32), jax.ShapeDtypeStruct((SEQ, DM), BF16), jax.ShapeDtypeStruct((SEQ, CONVW), BF16),
                   jax.ShapeDtypeStruct((SEQ, 3 * CONVW), BF16), jax.ShapeDtypeStruct((8, DM), F32),
                   jax.ShapeDtypeStruct((HALO_C, CONVW), F32), jax.ShapeDtypeStruct((len(WINS) * PGD, PGD), F32)],
        in_specs=[_tile(DM, True), _tile(DM, True), _tile(DM, True), _tile(2 * CONVW, True), _tile(CONVW, True), halo, _tile(CONVW, True),
                  _tile(POOLW, True), _const((8, 6 * DM)), _const((1, DM)), _const(win_t.shape), _const(dww.shape), _const((1, CONVW)),
                  _const((1, CONVW)), _const(wpw.shape), _const(wg.shape), _const((1, POOLW)), _const(wout.shape)],
        out_specs=[_tile(DM, True), _tile(DM, True), _tile(CONVW, True), _tile(3 * CONVW, True), _const((8, DM)),
                   _const((HALO_C, CONVW)), _const((len(WINS) * PGD, PGD))],
        scratch_shapes=[pltpu.VMEM((TOK + HALO_C, CONVW), F32), pltpu.VMEM((TOK + HALO_P, POOLW), F32), pltpu.VMEM((TOK + HALO_C, CONVW), F32)],
        compiler_params=pltpu.CompilerParams(dimension_semantics=("arbitrary",), vmem_limit_bytes=VMEM_LIMIT),
    )(dx1, x, y, uag, hc, hc, hd, pbv, mod8, g1, win_t, dww, lng, lnb, wpw, wg, psc, wout)


def _wgrad_call(name, a, b):
    n = b.shape[1]
    if a.ndim == 3:
        m = a.shape[0] * CH
        a_spec = pl.BlockSpec((None, SEQ, CH), lambda j: (j, 0, 0))
    else:
        m = a.shape[1]
        a_spec = pl.BlockSpec((SEQ, CH), lambda j: (0, j))

    def body(a_ref, b_ref, o_ref):
        o_ref[...] = _dot_tn(a_ref[...], b_ref[...]).astype(BF16)

    return pl.pallas_call(
        body,
        name=name,
        grid=(m // CH,),
        out_shape=jax.ShapeDtypeStruct((m, n), BF16),
        in_specs=[a_spec, pl.BlockSpec((SEQ, n), lambda j: (0, 0))],
        out_specs=pl.BlockSpec((CH, n), lambda j: (j, 0)),
        compiler_params=pltpu.CompilerParams(dimension_semantics=("arbitrary",), vmem_limit_bytes=VMEM_LIMIT),
    )(a, b)


def _reduce_call(parts, spack, dmodp, cact_t):
    nw = len(parts)
    rws = [p.shape[0] // NDEV for p in parts]
    cols = [p.shape[1] for p in parts]
    srows = spack.shape[0]
    sub = 32

    def body(*refs):
        p_refs = refs[:nw]
        spack_ref, dmodp_ref, cact_t_ref = refs[nw : nw + 3]
        o_refs = refs[nw + 3 : 2 * nw + 3]
        ssum_ref, gbada_ref, gwada_ref = refs[2 * nw + 3 : 2 * nw + 6]
        recvs = refs[2 * nw + 6 : 3 * nw + 6]
        srecv, mrecv = refs[3 * nw + 6 : 3 * nw + 8]
        psem_s, psem_r, ssem_s, ssem_r, msem_s, msem_r, lsem = refs[3 * nw + 8 :]
        x, y, c_, me = _place()

        def rcopy(src, dst, ss, rs, dev):
            return pltpu.make_async_remote_copy(src_ref=src, dst_ref=dst, send_sem=ss, recv_sem=rs, device_id=dev, device_id_type=MESH)

        srecv[me] = spack_ref[...]
        mrecv[me] = dmodp_ref[...]
        sends, local = [], []
        for r in range(1, NDEV):
            dev, _ = _flip(x, y, c_, r)
            sends.append(rcopy(srecv.at[me], srecv.at[me], ssem_s.at[r], ssem_r.at[r], dev))
            sends.append(rcopy(mrecv.at[me], mrecv.at[me], msem_s.at[r], msem_r.at[r], dev))
        for k in range(nw):
            lc = pltpu.make_async_copy(_rows(p_refs[k], me, rws[k]), recvs[k].at[me], lsem.at[k])
            local.append(lc)
            for r in range(1, NDEV):
                dev, pb = _flip(x, y, c_, r)
                sends.append(rcopy(_rows(p_refs[k], pb, rws[k]), recvs[k].at[me], psem_s.at[k, r], psem_r.at[k, r], dev))
        for cp in local + sends:
            cp.start()

        for r in range(1, NDEV):
            _, pb = _flip(x, y, c_, r)
            rcopy(srecv.at[me], srecv.at[pb], ssem_s.at[r], ssem_r.at[r], (x, y, c_)).wait_recv()
            rcopy(mrecv.at[me], mrecv.at[pb], msem_s.at[r], msem_r.at[r], (x, y, c_)).wait_recv()
        tot = srecv[0]
        for s in range(1, NDEV):
            tot = tot + srecv[s]
        ssum_ref[...] = tot
        btot = mrecv[0]
        for s in range(1, NDEV):
            btot = btot + mrecv[s]
        gbada_ref[...] = btot
        rowm = lax.broadcasted_iota(jnp.int32, (8, MODW), 0)
        gw = jnp.zeros((DM, MODW), F32)
        for s in range(NDEV):
            drow = jnp.sum(jnp.where(rowm == me, mrecv[s], 0.0), axis=0, keepdims=True)
            gw = gw + cact_t_ref[:, s : s + 1] * drow
        gwada_ref[...] = gw

        for k in range(nw):
            local[k].wait()
            for r in range(1, NDEV):
                _, pb = _flip(x, y, c_, r)
                rcopy(_rows(p_refs[k], pb, rws[k]), recvs[k].at[pb], psem_s.at[k, r], psem_r.at[k, r], (x, y, c_)).wait_recv()

            def add_rows(j, carry, k=k):
                rr = pl.ds(pl.multiple_of(j * sub, sub), sub)
                t = recvs[k][0, rr, :].astype(F32)
                for s in range(1, NDEV):
                    t = t + recvs[k][s, rr, :].astype(F32)
                o_refs[k][rr, :] = t
                return carry

            lax.fori_loop(0, rws[k] // sub, add_rows, 0)
        for cp in sends:
            cp.wait_send()

    out_shape = [jax.ShapeDtypeStruct((rws[k], cols[k]), F32) for k in range(nw)]
    out_shape += [jax.ShapeDtypeStruct((srows, DM), F32), jax.ShapeDtypeStruct((8, MODW), F32), jax.ShapeDtypeStruct((DM, MODW), F32)]
    scratch = [pltpu.VMEM((NDEV, rws[k], cols[k]), BF16) for k in range(nw)]
    scratch += [pltpu.VMEM((NDEV, srows, DM), F32), pltpu.VMEM((NDEV, 8, MODW), F32)]
    scratch += [pltpu.SemaphoreType.DMA((nw, NDEV))] * 2 + [pltpu.SemaphoreType.DMA((NDEV,))] * 4 + [pltpu.SemaphoreType.DMA((nw,))]
    return pl.pallas_call(
        body,
        name="reduce",
        out_shape=out_shape,
        in_specs=[_HBM] * nw + [_VM] * 3,
        out_specs=[_VM] * (nw + 3),
        scratch_shapes=scratch,
        compiler_params=pltpu.CompilerParams(vmem_limit_bytes=VMEM_LIMIT),
    )(*parts, spack, dmodp, cact_t)


def _adam_call(name, ws, gs, ms, vs):
    n = len(ws)
    bc1 = 1.0 - ADAM_B1 ** ADAM_STEP
    bc2 = 1.0 - ADAM_B2 ** ADAM_STEP

    def body(*refs):
        for i in range(n):
            w, g, m, v = (refs[j * n + i][...] for j in range(4))
            m = ADAM_B1 * m + (1.0 - ADAM_B1) * g
            v = ADAM_B2 * v + (1.0 - ADAM_B2) * (g * g)
            m_hat = m / bc1
            v_hat = v / bc2
            refs[4 * n + i][...] = -ADAM_LR * (m_hat / (jnp.sqrt(v_hat) + ADAM_EPS) + ADAM_WD * w)
            refs[5 * n + i][...] = m
            refs[6 * n + i][...] = v

    shapes = [jax.ShapeDtypeStruct(w.shape, F32) for w in ws]
    outs = pl.pallas_call(
        body,
        name=name,
        out_shape=shapes * 3,
        in_specs=[_VM] * (4 * n),
        out_specs=[_VM] * (3 * n),
        compiler_params=pltpu.CompilerParams(vmem_limit_bytes=VMEM_LIMIT),
    )(*ws, *gs, *ms, *vs)
    return outs[:n], outs[n : 2 * n], outs[2 * n :]


def kernel(x, c, w_ada, b_ada, g_norm1, w_in, dw_w, dw_b, conv_ln_g, conv_ln_b, w_conv_pw, w_pool_group, pool_scale, w_out, g_norm2, w_ffn_gate, w_ffn_up, w_ffn_down, g_final, loss_target, m_w_ada, m_b_ada, m_g_norm1, m_w_in, m_dw_w, m_dw_b, m_conv_ln_g, m_conv_ln_b, m_w_conv_pw, m_w_pool_group, m_pool_scale, m_w_out, m_g_norm2, m_w_ffn_gate, m_w_ffn_up, m_w_ffn_down, m_g_final, v_w_ada, v_b_ada, v_g_norm1, v_w_in, v_dw_w, v_dw_b, v_conv_ln_g, v_conv_ln_b, v_w_conv_pw, v_w_pool_group, v_pool_scale, v_w_out, v_g_norm2, v_w_ffn_gate, v_w_ffn_up, v_w_ffn_down, v_g_final):
    me = 4 * lax.axis_index("x") + 2 * lax.axis_index("y") + lax.axis_index("c")
    xs, tgt = x[0], loss_target[0]
    b_my = lax.dynamic_slice(b_ada, (0, me * MODW), (1, MODW))
    shards = [w_in[0].T, w_ffn_gate[0].T, w_ffn_up[0].T, w_ffn_down[0], w_out[0], w_conv_pw[0]]
    win_t, wg_t, wu_t, wd, wout, wpw, mod8, cact, dww8 = _gather_call(c, w_ada[0], b_my, dw_w[0], shards)
    dww = jnp.pad(jnp.transpose(dww8, (1, 0, 2)).reshape(KCONV, CONVW), ((0, HALO_C - KCONV), (0, 0)))
    wgp = w_pool_group[0]

    h1b, uag, hc, hd, hsb, pbv, ycb, y, x1 = _fwd_mix_call(
        xs, mod8, g_norm1, win_t, dww, dw_b, conv_ln_g, conv_ln_b, wpw, wgp, pool_scale, wout)
    h2b, a3, dfb, dg3, du3, dx1, facc = _ffn_call(x1, tgt, mod8, g_norm2, g_final.reshape(1, DM), wg_t, wu_t, wd)
    gx, dyb, dycb, dub, macc, ddw, dwg = _bwd_mix_call(
        dx1, xs, y, uag, hc, hd, pbv, mod8, g_norm1, win_t, dww, conv_ln_g, conv_ln_b, wpw, wgp, pool_scale, wout)

    parts = [
        _wgrad_call("wgrad_in", dub, h1b),
        _wgrad_call("wgrad_gate", dg3, h2b),
        _wgrad_call("wgrad_up", du3, h2b),
        _wgrad_call("wgrad_down", a3, dfb),
        _wgrad_call("wgrad_out", ycb, dyb),
        _wgrad_call("wgrad_pw", hsb, dycb),
    ]
    spack = jnp.concatenate(
        [macc[3:4], facc[3:4], facc[4:5], macc[4:6], facc[5:6], jnp.zeros((2, DM), F32), ddw.reshape(HALO_C // 2, DM), dwg.reshape(-1, DM)], axis=0)
    dmodp = jnp.concatenate([macc[0:3], facc[0:3]], axis=0).reshape(8, MODW)
    g_in_t, g_gate_t, g_up_t, g_down, g_out, g_pw, ssum, gbada, g_wada = _reduce_call(parts, spack, dmodp, cact.T)

    loss = ssum[5, 0]
    ddw_all = ssum[8 : 8 + HALO_C // 2].reshape(HALO_C, CONVW)[:KCONV]
    grads = {
        "w_ada": g_wada,
        "b_ada": gbada.reshape(1, 6 * DM),
        "g_norm1": ssum[0:1],
        "w_in": g_in_t.T,
        "dw_w": lax.dynamic_slice(ddw_all, (0, me * (CONVW // NDEV)), (KCONV, CONVW // NDEV)),
        "dw_b": ssum[3:4, 0:CONVW],
        "conv_ln_g": ssum[3:4, CONVW:DM],
        "conv_ln_b": ssum[4:5, 0:CONVW],
        "w_conv_pw": g_pw,
        "w_pool_group": ssum[8 + HALO_C // 2 :].reshape(len(WINS) * PGD, PGD),
        "pool_scale": ssum[4:5, CONVW:DM],
        "w_out": g_out,
        "g_norm2": ssum[1:2],
        "w_ffn_gate": g_gate_t.T,
        "w_ffn_up": g_up_t.T,
        "w_ffn_down": g_down,
        "g_final": ssum[2:3],
    }
    given = dict(w_ada=(w_ada, m_w_ada, v_w_ada), b_ada=(b_ada, m_b_ada, v_b_ada), g_norm1=(g_norm1, m_g_norm1, v_g_norm1),
                 w_in=(w_in, m_w_in, v_w_in), dw_w=(dw_w, m_dw_w, v_dw_w), dw_b=(dw_b, m_dw_b, v_dw_b),
                 conv_ln_g=(conv_ln_g, m_conv_ln_g, v_conv_ln_g), conv_ln_b=(conv_ln_b, m_conv_ln_b, v_conv_ln_b),
                 w_conv_pw=(w_conv_pw, m_w_conv_pw, v_w_conv_pw), w_pool_group=(w_pool_group, m_w_pool_group, v_w_pool_group),
                 pool_scale=(pool_scale, m_pool_scale, v_pool_scale), w_out=(w_out, m_w_out, v_w_out), g_norm2=(g_norm2, m_g_norm2, v_g_norm2),
                 w_ffn_gate=(w_ffn_gate, m_w_ffn_gate, v_w_ffn_gate), w_ffn_up=(w_ffn_up, m_w_ffn_up, v_w_ffn_up),
                 w_ffn_down=(w_ffn_down, m_w_ffn_down, v_w_ffn_down), g_final=(g_final, m_g_final, v_g_final))
    names = list(given)
    groups = [["w_ada"], ["w_ffn_gate", "w_ffn_up"], ["w_ffn_down", "w_in", "w_out", "w_conv_pw"],
              ["b_ada", "g_norm1", "dw_w", "dw_b", "conv_ln_g", "conv_ln_b", "w_pool_group", "pool_scale", "g_norm2", "g_final"]]
    delta, new_m, new_v = {}, {}, {}
    for gi, grp in enumerate(groups):
        shp = [grads[n].shape for n in grp]
        ds, ms, vs = _adam_call(
            f"adam{gi}", [given[n][0].reshape(s) for n, s in zip(grp, shp)], [grads[n] for n in grp],
            [given[n][1].reshape(s) for n, s in zip(grp, shp)], [given[n][2].reshape(s) for n, s in zip(grp, shp)])
        for n, d_, m_, v_ in zip(grp, ds, ms, vs):
            delta[n], new_m[n], new_v[n] = d_, m_, v_

    def full(n, a):
        return a.reshape(given[n][0].shape)

    return (loss, gx.reshape(x.shape), *[full(n, grads[n]) for n in names], *[full(n, delta[n]) for n in names],
            *[full(n, new_m[n]) for n in names], *[full(n, new_v[n]) for n in names])
```

```python
import functools

import jax
import jax.numpy as jnp
from jax import lax
from jax.experimental import pallas as pl
from jax.experimental.pallas import tpu as pltpu

F32, BF16 = jnp.float32, jnp.bfloat16
SEQ, DM = 2048, 1024
CONVW, POOLW = 512, 512
KCONV = 31
WINS = (2, 4, 8, 16)
PGD = 128
DFF = 2816
NDEV = 8
MODW = 6 * DM // NDEV
EPS = 1e-6
TOK = 256
NTILE = SEQ // TOK
CH = 256
NCH = DFF // CH
HALO_C, HALO_P = 32, 16
MESH = pl.DeviceIdType.MESH
VMEM_LIMIT = 56 * 1024 * 1024
ADAM_LR, ADAM_B1, ADAM_B2, ADAM_EPS, ADAM_WD, ADAM_STEP = 0.001, 0.9, 0.999, 1e-08, 0.01, 10
HI = lax.Precision.HIGHEST

_VM = pl.BlockSpec(memory_space=pltpu.VMEM)
_HBM = pl.BlockSpec(memory_space=pltpu.HBM)


def _place():
    x, y, c = lax.axis_index("x"), lax.axis_index("y"), lax.axis_index("c")
    return x, y, c, 4 * x + 2 * y + c


def _flip(x, y, c, r):
    px = 1 - x if r & 4 else x
    py = 1 - y if r & 2 else y
    pc = 1 - c if r & 1 else c
    return (px, py, pc), 4 * px + 2 * py + pc


def _rows(ref, blk, n):
    return ref.at[pl.ds(pl.multiple_of(blk * n, 16), n), :]


def _sig(z):
    return jax.nn.sigmoid(z)


def _dot_nt(a, b):
    return lax.dot_general(a, b, (((1,), (1,)), ((), ())), preferred_element_type=F32)


def _dot_nn(a, b):
    return lax.dot_general(a, b, (((1,), (0,)), ((), ())), preferred_element_type=F32)


def _dot_tn(a, b):
    return lax.dot_general(a, b, (((0,), (0,)), ((), ())), preferred_element_type=F32)


def _gather_call(c, w_ada, b_my, dww, shards):
    nw = len(shards)
    rws = [s.shape[0] for s in shards]

    def body(*refs):
        c_ref, wada_ref, bmy_ref, dww_ref = refs[:4]
        s_refs = refs[4 : 4 + nw]
        g_refs = refs[4 + nw : 4 + 2 * nw]
        mod8_ref, cact_ref, dww8_ref = refs[4 + 2 * nw : 7 + 2 * nw]
        crecv, msend, mrecv = refs[7 + 2 * nw : 10 + 2 * nw]
        sbufs = refs[10 + 2 * nw : 10 + 3 * nw]
        csem_s, csem_r, dsem_s, dsem_r, msem_s, msem_r, wsem_s, wsem_r, fsem_s, fsem_r, lsem = refs[10 + 3 * nw :]
        x, y, c_, me = _place()

        def rcopy(src, dst, ss, rs, dev):
            return pltpu.make_async_remote_copy(src_ref=src, dst_ref=dst, send_sem=ss, recv_sem=rs, device_id=dev, device_id_type=MESH)

        crecv[me] = jnp.broadcast_to(c_ref[...], (8, DM))
        dww8_ref[me] = dww_ref[...]
        small = []
        for r in range(1, NDEV):
            dev, _ = _flip(x, y, c_, r)
            small.append(rcopy(crecv.at[me], crecv.at[me], csem_s.at[r], csem_r.at[r], dev))
            small.append(rcopy(dww8_ref.at[me], dww8_ref.at[me], dsem_s.at[r], dsem_r.at[r], dev))
        for cp in small:
            cp.start()

        first, local = [], []
        ways = (1, 4, 2, 6)
        for k in range(nw):
            sbufs[k][...] = s_refs[k][...].astype(BF16)
            lc = pltpu.make_async_copy(sbufs[k], _rows(g_refs[k], me, rws[k]), lsem.at[k])
            lc.start()
            local.append(lc)
            for j, r in enumerate(ways):
                dev, _ = _flip(x, y, c_, r)
                cp = rcopy(sbufs[k], _rows(g_refs[k], me, rws[k]), wsem_s.at[k, j], wsem_r.at[k, j], dev)
                cp.start()
                first.append(cp)

        rowid = lax.broadcasted_iota(jnp.int32, (8, DM), 0)
        for r in range(1, NDEV):
            _, pb = _flip(x, y, c_, r)
            rcopy(crecv.at[me], crecv.at[pb], csem_s.at[r], csem_r.at[r], (x, y, c_)).wait_recv()
        call = jnp.zeros((8, DM), F32)
        for s in range(NDEV):
            call = jnp.where(rowid == s, crecv[s], call)
        cact = call * _sig(call)
        cact_ref[...] = cact
        modp = jnp.dot(cact, wada_ref[...], precision=HI, preferred_element_type=F32) + bmy_ref[...]
        rowm = lax.broadcasted_iota(jnp.int32, (8, MODW), 0)
        for b in range(NDEV):
            row = jnp.sum(jnp.where(rowm == b, modp, 0.0), axis=0, keepdims=True)
            msend[b] = jnp.broadcast_to(row, (8, MODW))
        mrecv[me] = msend[me]
        msends = []
        for r in range(1, NDEV):
            dev, pb = _flip(x, y, c_, r)
            cp = rcopy(msend.at[pb], mrecv.at[me], msem_s.at[r], msem_r.at[r], dev)
            cp.start()
            msends.append(cp)

        sib, _ = _flip(x, y, c_, 1)
        passed = []
        for k in range(nw):
            for j, r in enumerate(ways[1:]):
                _, pb = _flip(x, y, c_, r)
                blk = _rows(g_refs[k], pb, rws[k])
                rcopy(sbufs[k], blk, wsem_s.at[k, j + 1], wsem_r.at[k, j + 1], (x, y, c_)).wait_recv()
                cp = rcopy(blk, blk, fsem_s.at[k, j], fsem_r.at[k, j], sib)
                cp.start()
                passed.append(cp)

        for r in range(1, NDEV):
            _, pb = _flip(x, y, c_, r)
            rcopy(msend.at[pb], mrecv.at[pb], msem_s.at[r], msem_r.at[r], (x, y, c_)).wait_recv()
        for s in range(NDEV):
            mod8_ref[:, s * MODW : (s + 1) * MODW] = mrecv[s]

        for k in range(nw):
            _, sb = _flip(x, y, c_, 1)
            rcopy(sbufs[k], _rows(g_refs[k], sb, rws[k]), wsem_s.at[k, 0], wsem_r.at[k, 0], (x, y, c_)).wait_recv()
            for j, r in enumerate(ways[1:]):
                _, pb = _flip(x, y, c_, r ^ 1)
                blk = _rows(g_refs[k], pb, rws[k])
                rcopy(blk, blk, fsem_s.at[k, j], fsem_r.at[k, j], (x, y, c_)).wait_recv()
        for r in range(1, NDEV):
            _, pb = _flip(x, y, c_, r)
            rcopy(dww8_ref.at[me], dww8_ref.at[pb], dsem_s.at[r], dsem_r.at[r], (x, y, c_)).wait_recv()
        for cp in small + first + msends + passed:
            cp.wait_send()
        for lc in local:
            lc.wait()

    out_shape = [jax.ShapeDtypeStruct((NDEV * s.shape[0], s.shape[1]), BF16) for s in shards]
    out_shape += [
        jax.ShapeDtypeStruct((8, 6 * DM), F32),
        jax.ShapeDtypeStruct((8, DM), F32),
        jax.ShapeDtypeStruct((NDEV,) + dww.shape, F32),
    ]
    scratch = [pltpu.VMEM((NDEV, 8, DM), F32), pltpu.VMEM((NDEV, 8, MODW), F32), pltpu.VMEM((NDEV, 8, MODW), F32)]
    scratch += [pltpu.VMEM(s.shape, BF16) for s in shards]
    scratch += [pltpu.SemaphoreType.DMA((NDEV,))] * 6
    scratch += [pltpu.SemaphoreType.DMA((nw, 4))] * 2 + [pltpu.SemaphoreType.DMA((nw, 3))] * 2 + [pltpu.SemaphoreType.DMA((nw,))]
    return pl.pallas_call(
        body,
        name="gather",
        out_shape=out_shape,
        in_specs=[_VM] * (4 + nw),
        out_specs=[_HBM] * nw + [_VM] * 3,
        scratch_shapes=scratch,
        compiler_params=pltpu.CompilerParams(vmem_limit_bytes=VMEM_LIMIT),
    )(c, w_ada, b_my, dww, *shards)


def _const(shape):
    return pl.BlockSpec(shape, lambda i: (0,) * len(shape))


def _tile(width, rev=False):
    if rev:
        return pl.BlockSpec((TOK, width), lambda i: (NTILE - 1 - i, 0))
    return pl.BlockSpec((TOK, width), lambda i: (i, 0))


def _tile3(nch, rev=False):
    if rev:
        return pl.BlockSpec((nch, TOK, CH), lambda i: (0, NTILE - 1 - i, 0))
    return pl.BlockSpec((nch, TOK, CH), lambda i: (0, i, 0))


def _norm_mod(x, g, sc, sh):
    r = lax.rsqrt(jnp.mean(x * x, axis=-1, keepdims=True) + EPS)
    xr = x * r
    return r, xr, xr * g * (1.0 + sc) + sh


def _fwd_mix_call(x, mod8, g1, win_t, dww, dwb, lng, lnb, wpw, wg, psc, wout):
    def body(x_ref, mod_ref, g1_ref, win_ref, dww_ref, dwb_ref, lng_ref, lnb_ref, wpw_ref, wg_ref, psc_ref, wout_ref,
             h1b_ref, uag_ref, hc_ref, hd_ref, hsb3_ref, pb_ref, ycb3_ref, y_ref, x1_ref, hc_ext, up_ext, ycb_ref):
        i = pl.program_id(0)

        @pl.when(i == 0)
        def _():
            hc_ext[0:HALO_C, :] = jnp.zeros((HALO_C, CONVW), F32)
            up_ext[0:HALO_P, :] = jnp.zeros((HALO_P, POOLW), F32)

        x = x_ref[...]
        sh1, sc1, gt1 = mod_ref[0:1, 0:DM], mod_ref[0:1, DM : 2 * DM], mod_ref[0:1, 2 * DM : 3 * DM]
        _, _, h1 = _norm_mod(x, g1_ref[...], sc1, sh1)
        h1b = h1.astype(BF16)
        h1b_ref[...] = h1b
        u = _dot_nt(h1b, win_ref[...])
        uag_ref[...] = u[:, : 2 * CONVW]
        hc = u[:, :CONVW] * _sig(u[:, CONVW : 2 * CONVW])
        hc_ref[...] = hc
        hc_ext[HALO_C : HALO_C + TOK, :] = hc
        up_ext[HALO_P : HALO_P + TOK, :] = u[:, 2 * CONVW :]

        acc = jnp.zeros((TOK, CONVW), F32)
        for k in range(KCONV):
            acc = acc + dww_ref[k : k + 1, :] * hc_ext[pl.ds(HALO_C - (KCONV - 1) + k, TOK), :]
        hd = acc + dwb_ref[...]
        hd_ref[...] = hd
        hc_ext[0:HALO_C, :] = hc_ext[TOK : TOK + HALO_C, :]
        mu = jnp.mean(hd, axis=-1, keepdims=True)
        dlt = hd - mu
        rstd = lax.rsqrt(jnp.mean(dlt * dlt, axis=-1, keepdims=True) + EPS)
        hl = dlt * rstd * lng_ref[...] + lnb_ref[...]
        hsb = (hl * _sig(hl)).astype(BF16)
        for j in range(CONVW // CH):
            hsb3_ref[j] = hsb[:, j * CH : (j + 1) * CH]
        ycb_ref[:, 0:CONVW] = _dot_nn(hsb, wpw_ref[...]).astype(BF16)

        tg = i * TOK + lax.broadcasted_iota(jnp.int32, (TOK, 1), 0)
        for g, w in enumerate(WINS):
            ln = slice(PGD * g, PGD * (g + 1))
            v = up_ext[pl.ds(HALO_P, TOK), ln]
            ssum = v
            for d in range(1, w):
                ssum = ssum + up_ext[pl.ds(HALO_P - d, TOK), ln]
            cnt = jnp.minimum(tg + 1, w).astype(F32)
            pb = (ssum / cnt - v).astype(BF16)
            pb_ref[:, ln] = pb
            z = _dot_nn(pb, wg_ref[g].astype(BF16))
            ycb_ref[:, CONVW + PGD * g : CONVW + PGD * (g + 1)] = (z * psc_ref[:, ln]).astype(BF16)
        up_ext[0:HALO_P, :] = up_ext[TOK : TOK + HALO_P, :]

        for j in range(DM // CH):
            ycb3_ref[j] = ycb_ref[:, j * CH : (j + 1) * CH]
        yv = _dot_nn(ycb_ref[...], wout_ref[...])
        y_ref[...] = yv
        x1_ref[...] = x + gt1 * yv

    outs = [(DM, BF16), (2 * CONVW, F32), (CONVW, F32), (CONVW, F32), (-CONVW, BF16), (POOLW, BF16), (-DM, BF16), (DM, F32), (DM, F32)]
    return pl.pallas_call(
        body,
        name="fwd_mix",
        grid=(NTILE,),
        out_shape=[jax.ShapeDtypeStruct((SEQ, w) if w > 0 else (-w // CH, SEQ, CH), d) for w, d in outs],
        in_specs=[_tile(DM), _const((8, 6 * DM)), _const((1, DM)), _const(win_t.shape), _const(dww.shape), _const((1, CONVW)),
                  _const((1, CONVW)), _const((1, CONVW)), _const(wpw.shape), _const(wg.shape), _const((1, POOLW)), _const(wout.shape)],
        out_specs=[_tile(w) if w > 0 else _tile3(-w // CH) for w, _ in outs],
        scratch_shapes=[pltpu.VMEM((TOK + HALO_C, CONVW), F32), pltpu.VMEM((TOK + HALO_P, POOLW), F32), pltpu.VMEM((TOK, DM), BF16)],
        compiler_params=pltpu.CompilerParams(dimension_semantics=("arbitrary",), vmem_limit_bytes=VMEM_LIMIT),
    )(x, mod8, g1, win_t, dww, dwb, lng, lnb, wpw, wg, psc, wout)


def _ffn_call(x1, tgt, mod8, g2, gf, wg_t, wu_t, wd):
    def body(x1_ref, tgt_ref, mod_ref, g2_ref, gf_ref, wg_hbm, wu_hbm, wd_hbm,
             h2b_ref, a3_ref, dfb_ref, dg3_ref, du3_ref, dx1_ref, acc_ref,
             wg_ref, wu_ref, wd_ref, gate_s, up_s, f_s, dh_s, wsem):
        i = pl.program_id(0)

        @pl.when(i == 0)
        def _():
            cps = [pltpu.make_async_copy(s, d, wsem.at[n]) for n, (s, d) in enumerate(((wg_hbm, wg_ref), (wu_hbm, wu_ref), (wd_hbm, wd_ref)))]
            for cp in cps:
                cp.start()
            acc_ref[...] = jnp.zeros((8, DM), F32)
            for cp in cps:
                cp.wait()

        x1 = x1_ref[...]
        sh2, sc2, gt2 = mod_ref[0:1, 3 * DM : 4 * DM], mod_ref[0:1, 4 * DM : 5 * DM], mod_ref[0:1, 5 * DM : 6 * DM]
        g2 = g2_ref[...]
        r2, xr, h2 = _norm_mod(x1, g2, sc2, sh2)
        h2b = h2.astype(BF16)
        h2b_ref[...] = h2b
        f_s[...] = jnp.zeros((TOK, DM), F32)

        def fwd_chunk(cix, carry):
            rows = pl.ds(pl.multiple_of(cix * CH, CH), CH)
            gate = _dot_nt(h2b, wg_ref[rows, :])
            up = _dot_nt(h2b, wu_ref[rows, :])
            gate_s[cix] = gate
            up_s[cix] = up
            ab = (gate * _sig(gate) * up).astype(BF16)
            a3_ref[cix] = ab
            f_s[...] += _dot_nn(ab, wd_ref[rows, :])
            return carry

        lax.fori_loop(0, NCH, fwd_chunk, 0)

        f = f_s[...]
        x2 = x1 + gt2 * f
        rf = lax.rsqrt(jnp.mean(x2 * x2, axis=-1, keepdims=True) + EPS)
        nf = x2 * rf
        gf_ = gf_ref[...]
        err = nf * gf_ - tgt_ref[...]
        loss = 0.5 * jnp.sum(jnp.sum(err * err, axis=-1, keepdims=True), axis=0, keepdims=True) * (1.0 / DM)
        dout = err * (1.0 / DM)
        dnf = dout * gf_
        dx2 = rf * (dnf - nf * jnp.mean(dnf * nf, axis=-1, keepdims=True))
        dfb = (gt2 * dx2).astype(BF16)
        dfb_ref[...] = dfb
        dh_s[...] = jnp.zeros((TOK, DM), F32)

        def bwd_chunk(cix, carry):
            rows = pl.ds(pl.multiple_of(cix * CH, CH), CH)
            da = _dot_nt(dfb, wd_ref[rows, :])
            gate, up = gate_s[cix], up_s[cix]
            sg = _sig(gate)
            sl = gate * sg
            dgb = (da * up * (sg * (1.0 + gate * (1.0 - sg)))).astype(BF16)
            dub = (da * sl).astype(BF16)
            dg3_ref[cix] = dgb
            du3_ref[cix] = dub
            dh_s[...] += _dot_nn(dgb, wg_ref[rows, :]) + _dot_nn(dub, wu_ref[rows, :])
            return carry

        lax.fori_loop(0, NCH, bwd_chunk, 0)

        dh2 = dh_s[...]
        dn2 = dh2 * (1.0 + sc2)
        dxr = dn2 * g2
        dx1_ref[...] = dx2 + r2 * (dxr - xr * jnp.mean(dxr * xr, axis=-1, keepdims=True))

        def colsum(v):
            return jnp.sum(v, axis=0, keepdims=True)

        acc_ref[0:1, :] += colsum(dh2)
        acc_ref[1:2, :] += colsum(dh2 * (xr * g2))
        acc_ref[2:3, :] += colsum(dx2 * f)
        acc_ref[3:4, :] += colsum(dn2 * xr)
        acc_ref[4:5, :] += colsum(dout * nf)
        acc_ref[5:6, :] += jnp.broadcast_to(loss, (1, DM))

    c3 = pl.BlockSpec((NCH, TOK, CH), lambda i: (0, i, 0))
    return pl.pallas_call(
        body,
        name="ffn",
        grid=(NTILE,),
        out_shape=[jax.ShapeDtypeStruct((SEQ, DM), BF16), jax.ShapeDtypeStruct((NCH, SEQ, CH), BF16), jax.ShapeDtypeStruct((SEQ, DM), BF16),
                   jax.ShapeDtypeStruct((NCH, SEQ, CH), BF16), jax.ShapeDtypeStruct((NCH, SEQ, CH), BF16),
                   jax.ShapeDtypeStruct((SEQ, DM), F32), jax.ShapeDtypeStruct((8, DM), F32)],
        in_specs=[_tile(DM), _tile(DM), _const((8, 6 * DM)), _const((1, DM)), _const((1, DM)), _HBM, _HBM, _HBM],
        out_specs=[_tile(DM), c3, _tile(DM), c3, c3, _tile(DM), _const((8, DM))],
        scratch_shapes=[pltpu.VMEM((DFF, DM), BF16)] * 3 + [pltpu.VMEM((NCH, TOK, CH), F32)] * 2 + [pltpu.VMEM((TOK, DM), F32)] * 2
        + [pltpu.SemaphoreType.DMA((3,))],
        compiler_params=pltpu.CompilerParams(dimension_semantics=("arbitrary",), vmem_limit_bytes=VMEM_LIMIT),
    )(x1, tgt, mod8, g2, gf, wg_t, wu_t, wd)


def _bwd_mix_call(dx1, x, y, uag, hc, hd, pbv, mod8, g1, win_t, dww, lng, lnb, wpw, wg, psc, wout):
    hpt = TOK // HALO_C

    def body(dx1_ref, x_ref, y_ref, uag_ref, hc_ref, halo_ref, hd_ref, pb_ref, mod_ref, g1_ref, win_ref, dww_ref, lng_ref, lnb_ref,
             wpw_ref, wg_ref, psc_ref, wout_ref,
             gx_ref, dyb_ref, dycb_ref, dub3_ref, acc_ref, ddw_ref, dwg_ref, d_ext, q_ext, hcx, dub_ref):
        i = pl.program_id(0)
        it = NTILE - 1 - i

        @pl.when(i == 0)
        def _():
            d_ext[TOK : TOK + HALO_C, :] = jnp.zeros((HALO_C, CONVW), F32)
            q_ext[TOK : TOK + HALO_P, :] = jnp.zeros((HALO_P, POOLW), F32)
            acc_ref[...] = jnp.zeros((8, DM), F32)
            ddw_ref[...] = jnp.zeros((HALO_C, CONVW), F32)
            dwg_ref[...] = jnp.zeros((len(WINS) * PGD, PGD), F32)

        def colsum(v):
            return jnp.sum(v, axis=0, keepdims=True)

        dx1 = dx1_ref[...]
        x = x_ref[...]
        sh1, sc1, gt1 = mod_ref[0:1, 0:DM], mod_ref[0:1, DM : 2 * DM], mod_ref[0:1, 2 * DM : 3 * DM]
        acc_ref[2:3, :] += colsum(dx1 * y_ref[...])
        dyb = (gt1 * dx1).astype(BF16)
        dyb_ref[...] = dyb
        dycat = _dot_nt(dyb, wout_ref[...])

        hd = hd_ref[...]
        mu = jnp.mean(hd, axis=-1, keepdims=True)
        dlt = hd - mu
        rstd = lax.rsqrt(jnp.mean(dlt * dlt, axis=-1, keepdims=True) + EPS)
        xhat = dlt * rstd
        lng = lng_ref[...]
        hl = xhat * lng + lnb_ref[...]
        sgl = _sig(hl)
        dycb = dycat[:, :CONVW].astype(BF16)
        dycb_ref[...] = dycb
        dhl = _dot_nt(dycb, wpw_ref[...]) * (sgl * (1.0 + hl * (1.0 - sgl)))
        acc_ref[5:6, 0:CONVW] += colsum(dhl)
        acc_ref[4:5, CONVW:DM] += colsum(dhl * xhat)
        dxh = dhl * lng
        dhd = rstd * (dxh - jnp.mean(dxh, axis=-1, keepdims=True) - xhat * jnp.mean(dxh * xhat, axis=-1, keepdims=True))
        acc_ref[4:5, 0:CONVW] += colsum(dhd)

        hcx[0:HALO_C, :] = jnp.where(it == 0, 0.0, halo_ref[...])
        hcx[HALO_C : HALO_C + TOK, :] = hc_ref[...]
        d_ext[0:TOK, :] = dhd
        dhc = jnp.zeros((TOK, CONVW), F32)
        for k in range(KCONV):
            ddw_ref[k : k + 1, :] += colsum(dhd * hcx[pl.ds(HALO_C - (KCONV - 1) + k, TOK), :])
            dhc = dhc + dww_ref[k : k + 1, :] * d_ext[pl.ds(KCONV - 1 - k, TOK), :]
        d_ext[TOK : TOK + HALO_C, :] = d_ext[0:HALO_C, :]
        ua, ug = uag_ref[:, 0:CONVW], uag_ref[:, CONVW : 2 * CONVW]
        sgg = _sig(ug)
        dub_ref[:, 0:CONVW] = (dhc * sgg).astype(BF16)
        dub_ref[:, CONVW : 2 * CONVW] = (dhc * ua * sgg * (1.0 - sgg)).astype(BF16)

        tg = it * TOK + lax.broadcasted_iota(jnp.int32, (TOK, 1), 0)
        for g, w in enumerate(WINS):
            ln = slice(PGD * g, PGD * (g + 1))
            wgb = wg_ref[g].astype(BF16)
            pb = pb_ref[:, ln]
            dyp = dycat[:, CONVW + PGD * g : CONVW + PGD * (g + 1)]
            acc_ref[5:6, CONVW + PGD * g : CONVW + PGD * (g + 1)] += colsum(dyp * _dot_nn(pb, wgb))
            dzb = (dyp * psc_ref[:, ln]).astype(BF16)
            dwg_ref[PGD * g : PGD * (g + 1), :] += _dot_tn(pb, dzb)
            dp = _dot_nt(dzb, wgb)
            cnt = jnp.minimum(tg + 1, w).astype(F32)
            q_ext[0:TOK, ln] = dp / cnt
            dv = -dp
            for d in range(w):
                dv = dv + q_ext[pl.ds(d, TOK), ln]
            dub_ref[:, 2 * CONVW + PGD * g : 2 * CONVW + PGD * (g + 1)] = dv.astype(BF16)
        q_ext[TOK : TOK + HALO_P, :] = q_ext[0:HALO_P, :]

        for j in range(3 * CONVW // CH):
            dub3_ref[j] = dub_ref[:, j * CH : (j + 1) * CH]
        dh1 = _dot_nn(dub_ref[...], win_ref[...])
        g1 = g1_ref[...]
        r1 = lax.rsqrt(jnp.mean(x * x, axis=-1, keepdims=True) + EPS)
        xr = x * r1
        acc_ref[0:1, :] += colsum(dh1)
        acc_ref[1:2, :] += colsum(dh1 * (xr * g1))
        dn1 = dh1 * (1.0 + sc1)
        acc_ref[3:4, :] += colsum(dn1 * xr)
        dxr = dn1 * g1
        gx_ref[...] = dx1 + r1 * (dxr - xr * jnp.mean(dxr * xr, axis=-1, keepdims=True))

    halo = pl.BlockSpec((HALO_C, CONVW), lambda i: (jnp.maximum((NTILE - 1 - i) * hpt - 1, 0), 0))
    return pl.pallas_call(
        body,
        name="bwd_mix",
        grid=(NTILE,),
        out_shape=[jax.ShapeDtypeStruct((SEQ, DM), F32), jax.ShapeDtypeStruct((SEQ, DM), BF16), jax.ShapeDtypeStruct((SEQ, CONVW), BF16),
                   jax.ShapeDtypeStruct((3 * CONVW // CH, SEQ, CH), BF16), jax.ShapeDtypeStruct((8, DM), F32),
                   jax.ShapeDtypeStruct((HALO_C, CONVW), F32), jax.ShapeDtypeStruct((len(WINS) * PGD, PGD), F32)],
        in_specs=[_tile(DM, True), _tile(DM, True), _tile(DM, True), _tile(2 * CONVW, True), _tile(CONVW, True), halo, _tile(CONVW, True),
                  _tile(POOLW, True), _const((8, 6 * DM)), _const((1, DM)), _const(win_t.shape), _const(dww.shape), _const((1, CONVW)),
                  _const((1, CONVW)), _const(wpw.shape), _const(wg.shape), _const((1, POOLW)), _const(wout.shape)],
        out_specs=[_tile(DM, True), _tile(DM, True), _tile(CONVW, True), _tile3(3 * CONVW // CH, True), _const((8, DM)),
                   _const((HALO_C, CONVW)), _const((len(WINS) * PGD, PGD))],
        scratch_shapes=[pltpu.VMEM((TOK + HALO_C, CONVW), F32), pltpu.VMEM((TOK + HALO_P, POOLW), F32), pltpu.VMEM((TOK + HALO_C, CONVW), F32),
                        pltpu.VMEM((TOK, 3 * CONVW), BF16)],
        compiler_params=pltpu.CompilerParams(dimension_semantics=("arbitrary",), vmem_limit_bytes=VMEM_LIMIT),
    )(dx1, x, y, uag, hc, hc, hd, pbv, mod8, g1, win_t, dww, lng, lnb, wpw, wg, psc, wout)


CHIPS = (0, 4, 2, 6)


def _wgrad_rs_call(a3s, bmap, bs, order, spack, dmodp, cact_t):
    nw = len(a3s)
    nchs = [a.shape[0] for a in a3s]
    rws = [n * CH // NDEV for n in nchs]
    cols = [bs[bmap[k]].shape[1] for k in range(nw)]
    tshapes = sorted({(rws[k], cols[k]) for k in range(nw)})
    srows = spack.shape[0]
    sub = 32
    nb = len(bs)

    def body(*refs):
        pos = 0

        def take(n):
            nonlocal pos
            pos += n
            return refs[pos - n : pos]

        a_refs, b_refs = take(nw), take(nb)
        spack_ref, dmodp_ref, cact_t_ref = take(3)
        o_refs = take(nw)
        ssum_ref, gbada_ref, gwada_ref = take(3)
        p_refs = take(nw)
        r1, cb, r2 = take(nw), take(nw), take(nw)
        tmps = dict(zip(tshapes, take(len(tshapes))))
        abuf, bbuf, bbuf_n, obuf, obuf_n, srecv, mrecv = take(7)
        asem, osem, bsem, tsem, d_s, d_r, i_s, i_r, ssem_s, ssem_r, msem_s, msem_r = take(12)
        x, y, c_, me = _place()
        sib, _ = _flip(x, y, c_, 1)

        def rcopy(src, dst, ss, rs, dev):
            return pltpu.make_async_remote_copy(src_ref=src, dst_ref=dst, send_sem=ss, recv_sem=rs, device_id=dev, device_id_type=MESH)

        srecv[me] = spack_ref[...]
        mrecv[me] = dmodp_ref[...]
        sends = []
        for r in range(1, NDEV):
            dev, _ = _flip(x, y, c_, r)
            sends.append(rcopy(srecv.at[me], srecv.at[me], ssem_s.at[r], ssem_r.at[r], dev))
            sends.append(rcopy(mrecv.at[me], mrecv.at[me], msem_s.at[r], msem_r.at[r], dev))
        for cp in sends:
            cp.start()

        loaded = None
        for k in order:
            wide = cols[k] == DM
            bb, ob = (bbuf, obuf) if wide else (bbuf_n, obuf_n)
            a_ref, p_ref, rw, tmp = a_refs[k], p_refs[k], rws[k], tmps[(rws[k], cols[k])]
            if bmap[k] != loaded:
                cp = pltpu.make_async_copy(b_refs[bmap[k]], bb, bsem)
                cp.start()
                cp.wait()
                loaded = bmap[k]

            def a_copy(m, slot, a_ref=a_ref):
                return pltpu.make_async_copy(a_ref.at[m], abuf.at[slot], asem.at[slot])

            def o_copy(m, slot, ob=ob, p_ref=p_ref):
                return pltpu.make_async_copy(ob.at[slot], p_ref.at[pl.ds(pl.multiple_of(m * CH, CH), CH), :], osem.at[slot])

            a_copy(0, 0).start()

            def step(m, carry, k=k, a_copy=a_copy, o_copy=o_copy, bb=bb, ob=ob):
                slot = lax.rem(m, 2)
                a_copy(m, slot).wait()

                @pl.when(m + 1 < nchs[k])
                def _():
                    a_copy(m + 1, 1 - slot).start()

                @pl.when(m >= 2)
                def _():
                    o_copy(m - 2, slot).wait()

                ob[slot] = _dot_tn(abuf[slot], bb[...]).astype(BF16)
                o_copy(m, slot).start()
                return carry

            lax.fori_loop(0, nchs[k], step, 0)
            for m in (nchs[k] - 2, nchs[k] - 1):
                o_copy(m, m % 2).wait()

            for q, r in enumerate(CHIPS):
                _, owner = _flip(x, y, c_, r | 1)
                cp = rcopy(_rows(p_ref, owner, rw), r1[k].at[q], d_s.at[k, q], d_r.at[k, q], sib)
                cp.start()
                sends.append(cp)
            for q, r in enumerate(CHIPS):
                dev, owner = _flip(x, y, c_, r)
                rcopy(_rows(p_ref, owner, rw), r1[k].at[q], d_s.at[k, q], d_r.at[k, q], (x, y, c_)).wait_recv()
                cp = pltpu.make_async_copy(_rows(p_ref, owner, rw), tmp, tsem)
                cp.start()
                cp.wait()

                def add_sib(j, carry, k=k, q=q, tmp=tmp):
                    rr = pl.ds(pl.multiple_of(j * sub, sub), sub)
                    t = tmp[rr, :].astype(F32) + r1[k][q, rr, :].astype(F32)
                    if q == 0:
                        o_refs[k][rr, :] = t
                    else:
                        cb[k][q - 1, rr, :] = t.astype(BF16)
                    return carry

                lax.fori_loop(0, rw // sub, add_sib, 0)
                if q:
                    cp = rcopy(cb[k].at[q - 1], r2[k].at[q - 1], i_s.at[k, q - 1], i_r.at[k, q - 1], dev)
                    cp.start()
                    sends.append(cp)

        for r in range(1, NDEV):
            _, pb = _flip(x, y, c_, r)
            rcopy(srecv.at[me], srecv.at[pb], ssem_s.at[r], ssem_r.at[r], (x, y, c_)).wait_recv()
            rcopy(mrecv.at[me], mrecv.at[pb], msem_s.at[r], msem_r.at[r], (x, y, c_)).wait_recv()
        tot = srecv[0]
        for s in range(1, NDEV):
            tot = tot + srecv[s]
        ssum_ref[...] = tot
        btot = mrecv[0]
        for s in range(1, NDEV):
            btot = btot + mrecv[s]
        gbada_ref[...] = btot
        rowm = lax.broadcasted_iota(jnp.int32, (8, MODW), 0)
        gw = jnp.zeros((DM, MODW), F32)
        for s in range(NDEV):
            drow = jnp.sum(jnp.where(rowm == me, mrecv[s], 0.0), axis=0, keepdims=True)
            gw = gw + cact_t_ref[:, s : s + 1] * drow
        gwada_ref[...] = gw

        for k in order:
            for q in range(3):
                rcopy(cb[k].at[q], r2[k].at[q], i_s.at[k, q], i_r.at[k, q], (x, y, c_)).wait_recv()

            def add_far(j, carry, k=k):
                rr = pl.ds(pl.multiple_of(j * sub, sub), sub)
                t = o_refs[k][rr, :]
                for q in range(3):
                    t = t + r2[k][q, rr, :].astype(F32)
                o_refs[k][rr, :] = t
                return carry

            lax.fori_loop(0, rws[k] // sub, add_far, 0)
        for cp in sends:
            cp.wait_send()

    out_shape = [jax.ShapeDtypeStruct((rws[k], cols[k]), F32) for k in range(nw)]
    out_shape += [jax.ShapeDtypeStruct((srows, DM), F32), jax.ShapeDtypeStruct((8, MODW), F32), jax.ShapeDtypeStruct((DM, MODW), F32)]
    out_shape += [jax.ShapeDtypeStruct((nchs[k] * CH, cols[k]), BF16) for k in range(nw)]
    scratch = [pltpu.VMEM((4, rws[k], cols[k]), BF16) for k in range(nw)]
    scratch += [pltpu.VMEM((3, rws[k], cols[k]), BF16) for k in range(nw)] * 2
    scratch += [pltpu.VMEM(s, BF16) for s in tshapes]
    scratch += [pltpu.VMEM((2, SEQ, CH), BF16), pltpu.VMEM((SEQ, DM), BF16), pltpu.VMEM((SEQ, CONVW), BF16),
                pltpu.VMEM((2, CH, DM), BF16), pltpu.VMEM((2, CH, CONVW), BF16),
                pltpu.VMEM((NDEV, srows, DM), F32), pltpu.VMEM((NDEV, 8, MODW), F32)]
    scratch += [pltpu.SemaphoreType.DMA((2,))] * 2 + [pltpu.SemaphoreType.DMA] * 2
    scratch += [pltpu.SemaphoreType.DMA((nw, 4))] * 2 + [pltpu.SemaphoreType.DMA((nw, 3))] * 2 + [pltpu.SemaphoreType.DMA((NDEV,))] * 4
    outs = pl.pallas_call(
        body,
        name="wgrad_rs",
        out_shape=out_shape,
        in_specs=[_HBM] * (nw + nb) + [_VM] * 3,
        out_specs=[_VM] * (nw + 3) + [_HBM] * nw,
        scratch_shapes=scratch,
        compiler_params=pltpu.CompilerParams(vmem_limit_bytes=60 * 1024 * 1024),
    )(*a3s, *bs, spack, dmodp, cact_t)
    return outs[: nw + 3]


def _adam_call(name, ws, gs, ms, vs):
    n = len(ws)
    bc1 = 1.0 - ADAM_B1 ** ADAM_STEP
    bc2 = 1.0 - ADAM_B2 ** ADAM_STEP

    def body(*refs):
        for i in range(n):
            w, g, m, v = (refs[j * n + i][...] for j in range(4))
            m = ADAM_B1 * m + (1.0 - ADAM_B1) * g
            v = ADAM_B2 * v + (1.0 - ADAM_B2) * (g * g)
            m_hat = m / bc1
            v_hat = v / bc2
            refs[4 * n + i][...] = -ADAM_LR * (m_hat / (jnp.sqrt(v_hat) + ADAM_EPS) + ADAM_WD * w)
            refs[5 * n + i][...] = m
            refs[6 * n + i][...] = v

    shapes = [jax.ShapeDtypeStruct(w.shape, F32) for w in ws]
    outs = pl.pallas_call(
        body,
        name=name,
        out_shape=shapes * 3,
        in_specs=[_VM] * (4 * n),
        out_specs=[_VM] * (3 * n),
        compiler_params=pltpu.CompilerParams(vmem_limit_bytes=VMEM_LIMIT),
    )(*ws, *gs, *ms, *vs)
    return outs[:n], outs[n : 2 * n], outs[2 * n :]


def kernel(x, c, w_ada, b_ada, g_norm1, w_in, dw_w, dw_b, conv_ln_g, conv_ln_b, w_conv_pw, w_pool_group, pool_scale, w_out, g_norm2, w_ffn_gate, w_ffn_up, w_ffn_down, g_final, loss_target, m_w_ada, m_b_ada, m_g_norm1, m_w_in, m_dw_w, m_dw_b, m_conv_ln_g, m_conv_ln_b, m_w_conv_pw, m_w_pool_group, m_pool_scale, m_w_out, m_g_norm2, m_w_ffn_gate, m_w_ffn_up, m_w_ffn_down, m_g_final, v_w_ada, v_b_ada, v_g_norm1, v_w_in, v_dw_w, v_dw_b, v_conv_ln_g, v_conv_ln_b, v_w_conv_pw, v_w_pool_group, v_pool_scale, v_w_out, v_g_norm2, v_w_ffn_gate, v_w_ffn_up, v_w_ffn_down, v_g_final):
    me = 4 * lax.axis_index("x") + 2 * lax.axis_index("y") + lax.axis_index("c")
    xs, tgt = x[0], loss_target[0]
    b_my = lax.dynamic_slice(b_ada, (0, me * MODW), (1, MODW))
    shards = [w_in[0].T, w_ffn_gate[0].T, w_ffn_up[0].T, w_ffn_down[0], w_out[0], w_conv_pw[0]]
    win_t, wg_t, wu_t, wd, wout, wpw, mod8, cact, dww8 = _gather_call(c, w_ada[0], b_my, dw_w[0], shards)
    dww = jnp.pad(jnp.transpose(dww8, (1, 0, 2)).reshape(KCONV, CONVW), ((0, HALO_C - KCONV), (0, 0)))
    wgp = w_pool_group[0]

    h1b, uag, hc, hd, hsb3, pbv, ycb3, y, x1 = _fwd_mix_call(
        xs, mod8, g_norm1, win_t, dww, dw_b, conv_ln_g, conv_ln_b, wpw, wgp, pool_scale, wout)
    h2b, a3, dfb, dg3, du3, dx1, facc = _ffn_call(x1, tgt, mod8, g_norm2, g_final.reshape(1, DM), wg_t, wu_t, wd)
    gx, dyb, dycb, dub3, macc, ddw, dwg = _bwd_mix_call(
        dx1, xs, y, uag, hc, hd, pbv, mod8, g_norm1, win_t, dww, conv_ln_g, conv_ln_b, wpw, wgp, pool_scale, wout)
    spack = jnp.concatenate(
        [macc[3:4], facc[3:4], facc[4:5], macc[4:6], facc[5:6], jnp.zeros((2, DM), F32), ddw.reshape(HALO_C // 2, DM), dwg.reshape(-1, DM)], axis=0)
    dmodp = jnp.concatenate([macc[0:3], facc[0:3]], axis=0).reshape(8, MODW)
    g_in_t, g_gate_t, g_up_t, g_down, g_out, g_pw, ssum, gbada, g_wada = _wgrad_rs_call(
        [dub3, dg3, du3, a3, ycb3, hsb3], [0, 1, 1, 2, 3, 4], [h1b, h2b, dfb, dyb, dycb], (4, 3, 1, 2, 0, 5), spack, dmodp, cact.T)

    loss = ssum[5, 0]
    ddw_all = ssum[8 : 8 + HALO_C // 2].reshape(HALO_C, CONVW)[:KCONV]
    grads = {
        "w_ada": g_wada,
        "b_ada": gbada.reshape(1, 6 * DM),
        "g_norm1": ssum[0:1],
        "w_in": g_in_t.T,
        "dw_w": lax.dynamic_slice(ddw_all, (0, me * (CONVW // NDEV)), (KCONV, CONVW // NDEV)),
        "dw_b": ssum[3:4, 0:CONVW],
        "conv_ln_g": ssum[3:4, CONVW:DM],
        "conv_ln_b": ssum[4:5, 0:CONVW],
        "w_conv_pw": g_pw,
        "w_pool_group": ssum[8 + HALO_C // 2 :].reshape(len(WINS) * PGD, PGD),
        "pool_scale": ssum[4:5, CONVW:DM],
        "w_out": g_out,
        "g_norm2": ssum[1:2],
        "w_ffn_gate": g_gate_t.T,
        "w_ffn_up": g_up_t.T,
        "w_ffn_down": g_down,
        "g_final": ssum[2:3],
    }
    given = dict(w_ada=(w_ada, m_w_ada, v_w_ada), b_ada=(b_ada, m_b_ada, v_b_ada), g_norm1=(g_norm1, m_g_norm1, v_g_norm1),
                 w_in=(w_in, m_w_in, v_w_in), dw_w=(dw_w, m_dw_w, v_dw_w), dw_b=(dw_b, m_dw_b, v_dw_b),
                 conv_ln_g=(conv_ln_g, m_conv_ln_g, v_conv_ln_g), conv_ln_b=(conv_ln_b, m_conv_ln_b, v_conv_ln_b),
                 w_conv_pw=(w_conv_pw, m_w_conv_pw, v_w_conv_pw), w_pool_group=(w_pool_group, m_w_pool_group, v_w_pool_group),
                 pool_scale=(pool_scale, m_pool_scale, v_pool_scale), w_out=(w_out, m_w_out, v_w_out), g_norm2=(g_norm2, m_g_norm2, v_g_norm2),
                 w_ffn_gate=(w_ffn_gate, m_w_ffn_gate, v_w_ffn_gate), w_ffn_up=(w_ffn_up, m_w_ffn_up, v_w_ffn_up),
                 w_ffn_down=(w_ffn_down, m_w_ffn_down, v_w_ffn_down), g_final=(g_final, m_g_final, v_g_final))
    names = list(given)
    groups = [["w_ada"], ["w_ffn_gate", "w_ffn_up"], ["w_ffn_down", "w_in", "w_out", "w_conv_pw"],
              ["b_ada", "g_norm1", "dw_w", "dw_b", "conv_ln_g", "conv_ln_b", "w_pool_group", "pool_scale", "g_norm2", "g_final"]]
    delta, new_m, new_v = {}, {}, {}
    for gi, grp in enumerate(groups):
        shp = [grads[n].shape for n in grp]
        ds, ms, vs = _adam_call(
            f"adam{gi}", [given[n][0].reshape(s) for n, s in zip(grp, shp)], [grads[n] for n in grp],
            [given[n][1].reshape(s) for n, s in zip(grp, shp)], [given[n][2].reshape(s) for n, s in zip(grp, shp)])
        for n, d_, m_, v_ in zip(grp, ds, ms, vs):
            delta[n], new_m[n], new_v[n] = d_, m_, v_

    def full(n, a):
        return a.reshape(given[n][0].shape)

    return (loss, gx.reshape(x.shape), *[full(n, grads[n]) for n in names], *[full(n, delta[n]) for n in names],
            *[full(n, new_m[n]) for n in names], *[full(n, new_v[n]) for n in names])
```

```python
import functools

import jax
import jax.numpy as jnp
from jax import lax
from jax.experimental import pallas as pl
from jax.experimental.pallas import tpu as pltpu

F32, BF16 = jnp.float32, jnp.bfloat16
SEQ, DM = 2048, 1024
CONVW, POOLW = 512, 512
KCONV = 31
WINS = (2, 4, 8, 16)
PGD = 128
DFF = 2816
NDEV = 8
MODW = 6 * DM // NDEV
EPS = 1e-6
TOK = 256
NTILE = SEQ // TOK
CH = 256
NCH = DFF // CH
HALO_C, HALO_P = 32, 16
MESH = pl.DeviceIdType.MESH
VMEM_LIMIT = 56 * 1024 * 1024
ADAM_LR, ADAM_B1, ADAM_B2, ADAM_EPS, ADAM_WD, ADAM_STEP = 0.001, 0.9, 0.999, 1e-08, 0.01, 10
HI = lax.Precision.HIGHEST

_VM = pl.BlockSpec(memory_space=pltpu.VMEM)
_HBM = pl.BlockSpec(memory_space=pltpu.HBM)


def _place():
    x, y, c = lax.axis_index("x"), lax.axis_index("y"), lax.axis_index("c")
    return x, y, c, 4 * x + 2 * y + c


def _flip(x, y, c, r):
    px = 1 - x if r & 4 else x
    py = 1 - y if r & 2 else y
    pc = 1 - c if r & 1 else c
    return (px, py, pc), 4 * px + 2 * py + pc


def _rows(ref, blk, n):
    return ref.at[pl.ds(pl.multiple_of(blk * n, 16), n), :]


def _sig(z):
    return jax.nn.sigmoid(z)


def _dot_nt(a, b):
    return lax.dot_general(a, b, (((1,), (1,)), ((), ())), preferred_element_type=F32)


def _dot_nn(a, b):
    return lax.dot_general(a, b, (((1,), (0,)), ((), ())), preferred_element_type=F32)


def _dot_tn(a, b):
    return lax.dot_general(a, b, (((0,), (0,)), ((), ())), preferred_element_type=F32)


WAYS = (1, 4, 2, 6)


def _rcopy(src, dst, ss, rs, dev):
    return pltpu.make_async_remote_copy(src_ref=src, dst_ref=dst, send_sem=ss, recv_sem=rs, device_id=dev, device_id_type=MESH)


def _ag_copies(sbufs, g_refs, rws, wsem_s, wsem_r, lsem):
    x, y, c_, me = _place()
    local = [pltpu.make_async_copy(sbufs[k], _rows(g_refs[k], me, rws[k]), lsem.at[k]) for k in range(len(sbufs))]
    first = [_rcopy(sbufs[k], _rows(g_refs[k], me, rws[k]), wsem_s.at[k, j], wsem_r.at[k, j], _flip(x, y, c_, r)[0])
             for k in range(len(sbufs)) for j, r in enumerate(WAYS)]
    return local, first


def _ag_passes(g_refs, rws, fsem_s, fsem_r):
    x, y, c_, me = _place()
    sib, _ = _flip(x, y, c_, 1)
    out = []
    for k in range(len(g_refs)):
        for j, r in enumerate(WAYS[1:]):
            blk = _rows(g_refs[k], _flip(x, y, c_, r)[1], rws[k])
            out.append(_rcopy(blk, blk, fsem_s.at[k, j], fsem_r.at[k, j], sib))
    return out


def _ag_start(sbufs, g_refs, rws, wsem_s, wsem_r, lsem):
    local, first = _ag_copies(sbufs, g_refs, rws, wsem_s, wsem_r, lsem)
    for cp in local + first:
        cp.start()


def _ag_pass_on(sbufs, g_refs, rws, wsem_s, wsem_r, fsem_s, fsem_r):
    x, y, c_, me = _place()
    passes = _ag_passes(g_refs, rws, fsem_s, fsem_r)
    for k in range(len(g_refs)):
        for j, r in enumerate(WAYS[1:]):
            blk = _rows(g_refs[k], _flip(x, y, c_, r)[1], rws[k])
            _rcopy(sbufs[k], blk, wsem_s.at[k, j + 1], wsem_r.at[k, j + 1], (x, y, c_)).wait_recv()
            passes[3 * k + j].start()


def _ag_finish(sbufs, g_refs, rws, wsem_s, wsem_r, fsem_s, fsem_r, lsem):
    x, y, c_, me = _place()
    for k in range(len(g_refs)):
        blk = _rows(g_refs[k], _flip(x, y, c_, 1)[1], rws[k])
        _rcopy(sbufs[k], blk, wsem_s.at[k, 0], wsem_r.at[k, 0], (x, y, c_)).wait_recv()
        for j, r in enumerate(WAYS[1:]):
            blk = _rows(g_refs[k], _flip(x, y, c_, r ^ 1)[1], rws[k])
            _rcopy(blk, blk, fsem_s.at[k, j], fsem_r.at[k, j], (x, y, c_)).wait_recv()
    local, first = _ag_copies(sbufs, g_refs, rws, wsem_s, wsem_r, lsem)
    for cp in first + _ag_passes(g_refs, rws, fsem_s, fsem_r):
        cp.wait_send()
    for cp in local:
        cp.wait()


def _gather_call(c, w_ada, b_my, dww, shards):
    nw = len(shards)
    rws = [s.shape[0] for s in shards]

    def body(*refs):
        c_ref, wada_ref, bmy_ref, dww_ref = refs[:4]
        s_refs = refs[4 : 4 + nw]
        g_refs = refs[4 + nw : 4 + 2 * nw]
        mod8_ref, cact_ref, dww8_ref = refs[4 + 2 * nw : 7 + 2 * nw]
        crecv, msend, mrecv = refs[7 + 2 * nw : 10 + 2 * nw]
        sbufs = refs[10 + 2 * nw : 10 + 3 * nw]
        csem_s, csem_r, dsem_s, dsem_r, msem_s, msem_r, wsem_s, wsem_r, fsem_s, fsem_r, lsem = refs[10 + 3 * nw :]
        x, y, c_, me = _place()

        def rcopy(src, dst, ss, rs, dev):
            return pltpu.make_async_remote_copy(src_ref=src, dst_ref=dst, send_sem=ss, recv_sem=rs, device_id=dev, device_id_type=MESH)

        crecv[me] = jnp.broadcast_to(c_ref[...], (8, DM))
        dww8_ref[me] = dww_ref[...]
        small = []
        for r in range(1, NDEV):
            dev, _ = _flip(x, y, c_, r)
            small.append(rcopy(crecv.at[me], crecv.at[me], csem_s.at[r], csem_r.at[r], dev))
            small.append(rcopy(dww8_ref.at[me], dww8_ref.at[me], dsem_s.at[r], dsem_r.at[r], dev))
        for cp in small:
            cp.start()

        for k in range(nw):
            sbufs[k][...] = s_refs[k][...].astype(BF16)
        _ag_start(sbufs, g_refs, rws, wsem_s, wsem_r, lsem)

        rowid = lax.broadcasted_iota(jnp.int32, (8, DM), 0)
        for r in range(1, NDEV):
            _, pb = _flip(x, y, c_, r)
            rcopy(crecv.at[me], crecv.at[pb], csem_s.at[r], csem_r.at[r], (x, y, c_)).wait_recv()
        call = jnp.zeros((8, DM), F32)
        for s in range(NDEV):
            call = jnp.where(rowid == s, crecv[s], call)
        cact = call * _sig(call)
        cact_ref[...] = cact
        modp = jnp.dot(cact, wada_ref[...], precision=HI, preferred_element_type=F32) + bmy_ref[...]
        rowm = lax.broadcasted_iota(jnp.int32, (8, MODW), 0)
        for b in range(NDEV):
            row = jnp.sum(jnp.where(rowm == b, modp, 0.0), axis=0, keepdims=True)
            msend[b] = jnp.broadcast_to(row, (8, MODW))
        mrecv[me] = msend[me]
        msends = []
        for r in range(1, NDEV):
            dev, pb = _flip(x, y, c_, r)
            cp = rcopy(msend.at[pb], mrecv.at[me], msem_s.at[r], msem_r.at[r], dev)
            cp.start()
            msends.append(cp)

        _ag_pass_on(sbufs, g_refs, rws, wsem_s, wsem_r, fsem_s, fsem_r)

        for r in range(1, NDEV):
            _, pb = _flip(x, y, c_, r)
            rcopy(msend.at[pb], mrecv.at[pb], msem_s.at[r], msem_r.at[r], (x, y, c_)).wait_recv()
        for s in range(NDEV):
            mod8_ref[:, s * MODW : (s + 1) * MODW] = mrecv[s]

        _ag_finish(sbufs, g_refs, rws, wsem_s, wsem_r, fsem_s, fsem_r, lsem)
        for r in range(1, NDEV):
            _, pb = _flip(x, y, c_, r)
            rcopy(dww8_ref.at[me], dww8_ref.at[pb], dsem_s.at[r], dsem_r.at[r], (x, y, c_)).wait_recv()
        for cp in small + msends:
            cp.wait_send()

    out_shape = [jax.ShapeDtypeStruct((NDEV * s.shape[0], s.shape[1]), BF16) for s in shards]
    out_shape += [
        jax.ShapeDtypeStruct((8, 6 * DM), F32),
        jax.ShapeDtypeStruct((8, DM), F32),
        jax.ShapeDtypeStruct((NDEV,) + dww.shape, F32),
    ]
    scratch = [pltpu.VMEM((NDEV, 8, DM), F32), pltpu.VMEM((NDEV, 8, MODW), F32), pltpu.VMEM((NDEV, 8, MODW), F32)]
    scratch += [pltpu.VMEM(s.shape, BF16) for s in shards]
    scratch += [pltpu.SemaphoreType.DMA((NDEV,))] * 6
    scratch += [pltpu.SemaphoreType.DMA((nw, 4))] * 2 + [pltpu.SemaphoreType.DMA((nw, 3))] * 2 + [pltpu.SemaphoreType.DMA((nw,))]
    return pl.pallas_call(
        body,
        name="gather",
        out_shape=out_shape,
        in_specs=[_VM] * (4 + nw),
        out_specs=[_HBM] * nw + [_VM] * 3,
        scratch_shapes=scratch,
        compiler_params=pltpu.CompilerParams(vmem_limit_bytes=VMEM_LIMIT),
    )(c, w_ada, b_my, dww, *shards)


def _const(shape):
    return pl.BlockSpec(shape, lambda i: (0,) * len(shape))


def _tile(width, rev=False):
    if rev:
        return pl.BlockSpec((TOK, width), lambda i: (NTILE - 1 - i, 0))
    return pl.BlockSpec((TOK, width), lambda i: (i, 0))


def _tile3(nch, rev=False):
    if rev:
        return pl.BlockSpec((nch, TOK, CH), lambda i: (0, NTILE - 1 - i, 0))
    return pl.BlockSpec((nch, TOK, CH), lambda i: (0, i, 0))


def _norm_mod(x, g, sc, sh):
    r = lax.rsqrt(jnp.mean(x * x, axis=-1, keepdims=True) + EPS)
    xr = x * r
    return r, xr, xr * g * (1.0 + sc) + sh


def _fwd_mix_call(x, mod8, g1, win_t, dww, dwb, lng, lnb, wpw, wg, psc, wout, shards):
    ns = len(shards)
    rws = [s.shape[0] for s in shards]

    def body(x_ref, mod_ref, g1_ref, win_ref, dww_ref, dwb_ref, lng_ref, lnb_ref, wpw_ref, wg_ref, psc_ref, wout_ref, *rest):
        s_refs = rest[:ns]
        h1b_ref, uag_ref, hc_ref, hd_ref, hsb3_ref, pb_ref, ycb3_ref, y_ref, x1_ref = rest[ns : ns + 9]
        g_refs = rest[ns + 9 : 2 * ns + 9]
        hc_ext, up_ext, ycb_ref, stage = rest[2 * ns + 9 : 2 * ns + 13]
        sbufs = rest[2 * ns + 13 : 3 * ns + 13]
        wsem_s, wsem_r, fsem_s, fsem_r, lsem, ssem = rest[3 * ns + 13 :]
        i = pl.program_id(0)

        @pl.when(i == 0)
        def _():
            for k in range(ns):
                cp = pltpu.make_async_copy(s_refs[k], stage, ssem)
                cp.start()
                cp.wait()
                sbufs[k][...] = stage[...].astype(BF16)
            _ag_start(sbufs, g_refs, rws, wsem_s, wsem_r, lsem)
            hc_ext[0:HALO_C, :] = jnp.zeros((HALO_C, CONVW), F32)
            up_ext[0:HALO_P, :] = jnp.zeros((HALO_P, POOLW), F32)

        x = x_ref[...]
        sh1, sc1, gt1 = mod_ref[0:1, 0:DM], mod_ref[0:1, DM : 2 * DM], mod_ref[0:1, 2 * DM : 3 * DM]
        _, _, h1 = _norm_mod(x, g1_ref[...], sc1, sh1)
        h1b = h1.astype(BF16)
        h1b_ref[...] = h1b
        u = _dot_nt(h1b, win_ref[...])
        uag_ref[...] = u[:, : 2 * CONVW]
        hc = u[:, :CONVW] * _sig(u[:, CONVW : 2 * CONVW])
        hc_ref[...] = hc
        hc_ext[HALO_C : HALO_C + TOK, :] = hc
        up_ext[HALO_P : HALO_P + TOK, :] = u[:, 2 * CONVW :]

        acc = jnp.zeros((TOK, CONVW), F32)
        for k in range(KCONV):
            acc = acc + dww_ref[k : k + 1, :] * hc_ext[pl.ds(HALO_C - (KCONV - 1) + k, TOK), :]
        hd = acc + dwb_ref[...]
        hd_ref[...] = hd
        hc_ext[0:HALO_C, :] = hc_ext[TOK : TOK + HALO_C, :]
        mu = jnp.mean(hd, axis=-1, keepdims=True)
        dlt = hd - mu
        rstd = lax.rsqrt(jnp.mean(dlt * dlt, axis=-1, keepdims=True) + EPS)
        hl = dlt * rstd * lng_ref[...] + lnb_ref[...]
        hsb = (hl * _sig(hl)).astype(BF16)
        for j in range(CONVW // CH):
            hsb3_ref[j] = hsb[:, j * CH : (j + 1) * CH]
        ycb_ref[:, 0:CONVW] = _dot_nn(hsb, wpw_ref[...]).astype(BF16)

        tg = i * TOK + lax.broadcasted_iota(jnp.int32, (TOK, 1), 0)
        for g, w in enumerate(WINS):
            ln = slice(PGD * g, PGD * (g + 1))
            v = up_ext[pl.ds(HALO_P, TOK), ln]
            ssum = v
            for d in range(1, w):
                ssum = ssum + up_ext[pl.ds(HALO_P - d, TOK), ln]
            cnt = jnp.minimum(tg + 1, w).astype(F32)
            pb = (ssum / cnt - v).astype(BF16)
            pb_ref[:, ln] = pb
            z = _dot_nn(pb, wg_ref[g].astype(BF16))
            ycb_ref[:, CONVW + PGD * g : CONVW + PGD * (g + 1)] = (z * psc_ref[:, ln]).astype(BF16)
        up_ext[0:HALO_P, :] = up_ext[TOK : TOK + HALO_P, :]

        for j in range(DM // CH):
            ycb3_ref[j] = ycb_ref[:, j * CH : (j + 1) * CH]
        yv = _dot_nn(ycb_ref[...], wout_ref[...])
        y_ref[...] = yv
        x1_ref[...] = x + gt1 * yv

        @pl.when(i == NTILE - 1)
        def _():
            _ag_pass_on(sbufs, g_refs, rws, wsem_s, wsem_r, fsem_s, fsem_r)
            _ag_finish(sbufs, g_refs, rws, wsem_s, wsem_r, fsem_s, fsem_r, lsem)

    outs = [(DM, BF16), (2 * CONVW, F32), (CONVW, F32), (CONVW, F32), (-CONVW, BF16), (POOLW, BF16), (-DM, BF16), (DM, F32), (DM, F32)]
    return pl.pallas_call(
        body,
        name="fwd_mix",
        grid=(NTILE,),
        out_shape=[jax.ShapeDtypeStruct((SEQ, w) if w > 0 else (-w // CH, SEQ, CH), d) for w, d in outs]
        + [jax.ShapeDtypeStruct((NDEV * s.shape[0], s.shape[1]), BF16) for s in shards],
        in_specs=[_tile(DM), _const((8, 6 * DM)), _const((1, DM)), _const(win_t.shape), _const(dww.shape), _const((1, CONVW)),
                  _const((1, CONVW)), _const((1, CONVW)), _const(wpw.shape), _const(wg.shape), _const((1, POOLW)), _const(wout.shape)]
        + [_HBM] * ns,
        out_specs=[_tile(w) if w > 0 else _tile3(-w // CH) for w, _ in outs] + [_HBM] * ns,
        scratch_shapes=[pltpu.VMEM((TOK + HALO_C, CONVW), F32), pltpu.VMEM((TOK + HALO_P, POOLW), F32), pltpu.VMEM((TOK, DM), BF16),
                        pltpu.VMEM(shards[0].shape, F32)] + [pltpu.VMEM(s.shape, BF16) for s in shards]
        + [pltpu.SemaphoreType.DMA((ns, 4))] * 2 + [pltpu.SemaphoreType.DMA((ns, 3))] * 2 + [pltpu.SemaphoreType.DMA((ns,)), pltpu.SemaphoreType.DMA],
        compiler_params=pltpu.CompilerParams(dimension_semantics=("arbitrary",), vmem_limit_bytes=VMEM_LIMIT),
    )(x, mod8, g1, win_t, dww, dwb, lng, lnb, wpw, wg, psc, wout, *shards)


def _ffn_call(x1, tgt, mod8, g2, gf, wg_t, wu_t, wd):
    def body(x1_ref, tgt_ref, mod_ref, g2_ref, gf_ref, wg_hbm, wu_hbm, wd_hbm,
             h2b_ref, a3_ref, dfb_ref, dg3_ref, du3_ref, dx1_ref, acc_ref,
             wg_ref, wu_ref, wd_ref, wsem):
        i = pl.program_id(0)

        @pl.when(i == 0)
        def _():
            cps = [pltpu.make_async_copy(s, d, wsem.at[n]) for n, (s, d) in enumerate(((wg_hbm, wg_ref), (wu_hbm, wu_ref), (wd_hbm, wd_ref)))]
            for cp in cps:
                cp.start()
            acc_ref[...] = jnp.zeros((8, DM), F32)
            for cp in cps:
                cp.wait()

        x1 = x1_ref[...]
        sh2, sc2, gt2 = mod_ref[0:1, 3 * DM : 4 * DM], mod_ref[0:1, 4 * DM : 5 * DM], mod_ref[0:1, 5 * DM : 6 * DM]
        g2 = g2_ref[...]
        r2, xr, h2 = _norm_mod(x1, g2, sc2, sh2)
        h2b = h2.astype(BF16)
        h2b_ref[...] = h2b
        gate = _dot_nt(h2b, wg_ref[...])
        up = _dot_nt(h2b, wu_ref[...])
        ab = (gate * _sig(gate) * up).astype(BF16)
        for j in range(NCH):
            a3_ref[j] = ab[:, j * CH : (j + 1) * CH]
        f = _dot_nn(ab, wd_ref[...])
        x2 = x1 + gt2 * f
        rf = lax.rsqrt(jnp.mean(x2 * x2, axis=-1, keepdims=True) + EPS)
        nf = x2 * rf
        gf_ = gf_ref[...]
        err = nf * gf_ - tgt_ref[...]
        loss = 0.5 * jnp.sum(jnp.sum(err * err, axis=-1, keepdims=True), axis=0, keepdims=True) * (1.0 / DM)
        dout = err * (1.0 / DM)
        dnf = dout * gf_
        dx2 = rf * (dnf - nf * jnp.mean(dnf * nf, axis=-1, keepdims=True))
        dfb = (gt2 * dx2).astype(BF16)
        dfb_ref[...] = dfb
        da = _dot_nt(dfb, wd_ref[...])
        sg = _sig(gate)
        dgb = (da * up * (sg * (1.0 + gate * (1.0 - sg)))).astype(BF16)
        dub = (da * (gate * sg)).astype(BF16)
        for j in range(NCH):
            dg3_ref[j] = dgb[:, j * CH : (j + 1) * CH]
            du3_ref[j] = dub[:, j * CH : (j + 1) * CH]
        dh2 = _dot_nn(dgb, wg_ref[...]) + _dot_nn(dub, wu_ref[...])
        dn2 = dh2 * (1.0 + sc2)
        dxr = dn2 * g2
        dx1_ref[...] = dx2 + r2 * (dxr - xr * jnp.mean(dxr * xr, axis=-1, keepdims=True))

        def colsum(v):
            return jnp.sum(v, axis=0, keepdims=True)

        acc_ref[0:1, :] += colsum(dh2)
        acc_ref[1:2, :] += colsum(dh2 * (xr * g2))
        acc_ref[2:3, :] += colsum(dx2 * f)
        acc_ref[3:4, :] += colsum(dn2 * xr)
        acc_ref[4:5, :] += colsum(dout * nf)
        acc_ref[5:6, :] += jnp.broadcast_to(loss, (1, DM))

    c3 = pl.BlockSpec((NCH, TOK, CH), lambda i: (0, i, 0))
    return pl.pallas_call(
        body,
        name="ffn",
        grid=(NTILE,),
        out_shape=[jax.ShapeDtypeStruct((SEQ, DM), BF16), jax.ShapeDtypeStruct((NCH, SEQ, CH), BF16), jax.ShapeDtypeStruct((SEQ, DM), BF16),
                   jax.ShapeDtypeStruct((NCH, SEQ, CH), BF16), jax.ShapeDtypeStruct((NCH, SEQ, CH), BF16),
                   jax.ShapeDtypeStruct((SEQ, DM), F32), jax.ShapeDtypeStruct((8, DM), F32)],
        in_specs=[_tile(DM), _tile(DM), _const((8, 6 * DM)), _const((1, DM)), _const((1, DM)), _HBM, _HBM, _HBM],
        out_specs=[_tile(DM), c3, _tile(DM), c3, c3, _tile(DM), _const((8, DM))],
        scratch_shapes=[pltpu.VMEM((DFF, DM), BF16)] * 3 + [pltpu.SemaphoreType.DMA((3,))],
        compiler_params=pltpu.CompilerParams(dimension_semantics=("arbitrary",), vmem_limit_bytes=VMEM_LIMIT),
    )(x1, tgt, mod8, g2, gf, wg_t, wu_t, wd)


def _bwd_mix_call(dx1, x, y, uag, hc, hd, pbv, mod8, g1, win_t, dww, lng, lnb, wpw, wg, psc, wout):
    hpt = TOK // HALO_C

    def body(dx1_ref, x_ref, y_ref, uag_ref, hc_ref, halo_ref, hd_ref, pb_ref, mod_ref, g1_ref, win_ref, dww_ref, lng_ref, lnb_ref,
             wpw_ref, wg_ref, psc_ref, wout_ref,
             gx_ref, dyb_ref, dycb_ref, dub3_ref, acc_ref, ddw_ref, dwg_ref, d_ext, q_ext, hcx, dub_ref):
        i = pl.program_id(0)
        it = NTILE - 1 - i

        @pl.when(i == 0)
        def _():
            d_ext[TOK : TOK + HALO_C, :] = jnp.zeros((HALO_C, CONVW), F32)
            q_ext[TOK : TOK + HALO_P, :] = jnp.zeros((HALO_P, POOLW), F32)
            acc_ref[...] = jnp.zeros((8, DM), F32)
            ddw_ref[...] = jnp.zeros((HALO_C, CONVW), F32)
            dwg_ref[...] = jnp.zeros((len(WINS) * PGD, PGD), F32)

        def colsum(v):
            return jnp.sum(v, axis=0, keepdims=True)

        dx1 = dx1_ref[...]
        x = x_ref[...]
        sh1, sc1, gt1 = mod_ref[0:1, 0:DM], mod_ref[0:1, DM : 2 * DM], mod_ref[0:1, 2 * DM : 3 * DM]
        acc_ref[2:3, :] += colsum(dx1 * y_ref[...])
        dyb = (gt1 * dx1).astype(BF16)
        dyb_ref[...] = dyb
        dycat = _dot_nt(dyb, wout_ref[...])

        hd = hd_ref[...]
        mu = jnp.mean(hd, axis=-1, keepdims=True)
        dlt = hd - mu
        rstd = lax.rsqrt(jnp.mean(dlt * dlt, axis=-1, keepdims=True) + EPS)
        xhat = dlt * rstd
        lng = lng_ref[...]
        hl = xhat * lng + lnb_ref[...]
        sgl = _sig(hl)
        dycb = dycat[:, :CONVW].astype(BF16)
        dycb_ref[...] = dycb
        dhl = _dot_nt(dycb, wpw_ref[...]) * (sgl * (1.0 + hl * (1.0 - sgl)))
        acc_ref[5:6, 0:CONVW] += colsum(dhl)
        acc_ref[4:5, CONVW:DM] += colsum(dhl * xhat)
        dxh = dhl * lng
        dhd = rstd * (dxh - jnp.mean(dxh, axis=-1, keepdims=True) - xhat * jnp.mean(dxh * xhat, axis=-1, keepdims=True))
        acc_ref[4:5, 0:CONVW] += colsum(dhd)

        hcx[0:HALO_C, :] = jnp.where(it == 0, 0.0, halo_ref[...])
        hcx[HALO_C : HALO_C + TOK, :] = hc_ref[...]
        d_ext[0:TOK, :] = dhd
        dhc = jnp.zeros((TOK, CONVW), F32)
        for k in range(KCONV):
            ddw_ref[k : k + 1, :] += colsum(dhd * hcx[pl.ds(HALO_C - (KCONV - 1) + k, TOK), :])
            dhc = dhc + dww_ref[k : k + 1, :] * d_ext[pl.ds(KCONV - 1 - k, TOK), :]
        d_ext[TOK : TOK + HALO_C, :] = d_ext[0:HALO_C, :]
        ua, ug = uag_ref[:, 0:CONVW], uag_ref[:, CONVW : 2 * CONVW]
        sgg = _sig(ug)
        dub_ref[:, 0:CONVW] = (dhc * sgg).astype(BF16)
        dub_ref[:, CONVW : 2 * CONVW] = (dhc * ua * sgg * (1.0 - sgg)).astype(BF16)

        tg = it * TOK + lax.broadcasted_iota(jnp.int32, (TOK, 1), 0)
        for g, w in enumerate(WINS):
            ln = slice(PGD * g, PGD * (g + 1))
            wgb = wg_ref[g].astype(BF16)
            pb = pb_ref[:, ln]
            dyp = dycat[:, CONVW + PGD * g : CONVW + PGD * (g + 1)]
            acc_ref[5:6, CONVW + PGD * g : CONVW + PGD * (g + 1)] += colsum(dyp * _dot_nn(pb, wgb))
            dzb = (dyp * psc_ref[:, ln]).astype(BF16)
            dwg_ref[PGD * g : PGD * (g + 1), :] += _dot_tn(pb, dzb)
            dp = _dot_nt(dzb, wgb)
            cnt = jnp.minimum(tg + 1, w).astype(F32)
            q_ext[0:TOK, ln] = dp / cnt
            dv = -dp
            for d in range(w):
                dv = dv + q_ext[pl.ds(d, TOK), ln]
            dub_ref[:, 2 * CONVW + PGD * g : 2 * CONVW + PGD * (g + 1)] = dv.astype(BF16)
        q_ext[TOK : TOK + HALO_P, :] = q_ext[0:HALO_P, :]

        for j in range(3 * CONVW // CH):
            dub3_ref[j] = dub_ref[:, j * CH : (j + 1) * CH]
        dh1 = _dot_nn(dub_ref[...], win_ref[...])
        g1 = g1_ref[...]
        r1 = lax.rsqrt(jnp.mean(x * x, axis=-1, keepdims=True) + EPS)
        xr = x * r1
        acc_ref[0:1, :] += colsum(dh1)
        acc_ref[1:2, :] += colsum(dh1 * (xr * g1))
        dn1 = dh1 * (1.0 + sc1)
        acc_ref[3:4, :] += colsum(dn1 * xr)
        dxr = dn1 * g1
        gx_ref[...] = dx1 + r1 * (dxr - xr * jnp.mean(dxr * xr, axis=-1, keepdims=True))

    halo = pl.BlockSpec((HALO_C, CONVW), lambda i: (jnp.maximum((NTILE - 1 - i) * hpt - 1, 0), 0))
    return pl.pallas_call(
        body,
        name="bwd_mix",
        grid=(NTILE,),
        out_shape=[jax.ShapeDtypeStruct((SEQ, DM), F32), jax.ShapeDtypeStruct((SEQ, DM), BF16), jax.ShapeDtypeStruct((SEQ, CONVW), BF16),
                   jax.ShapeDtypeStruct((3 * CONVW // CH, SEQ, CH), BF16), jax.ShapeDtypeStruct((8, DM), F32),
                   jax.ShapeDtypeStruct((HALO_C, CONVW), F32), jax.ShapeDtypeStruct((len(WINS) * PGD, PGD), F32)],
        in_specs=[_tile(DM, True), _tile(DM, True), _tile(DM, True), _tile(2 * CONVW, True), _tile(CONVW, True), halo, _tile(CONVW, True),
                  _tile(POOLW, True), _const((8, 6 * DM)), _const((1, DM)), _const(win_t.shape), _const(dww.shape), _const((1, CONVW)),
                  _const((1, CONVW)), _const(wpw.shape), _const(wg.shape), _const((1, POOLW)), _const(wout.shape)],
        out_specs=[_tile(DM, True), _tile(DM, True), _tile(CONVW, True), _tile3(3 * CONVW // CH, True), _const((8, DM)),
                   _const((HALO_C, CONVW)), _const((len(WINS) * PGD, PGD))],
        scratch_shapes=[pltpu.VMEM((TOK + HALO_C, CONVW), F32), pltpu.VMEM((TOK + HALO_P, POOLW), F32), pltpu.VMEM((TOK + HALO_C, CONVW), F32),
                        pltpu.VMEM((TOK, 3 * CONVW), BF16)],
        compiler_params=pltpu.CompilerParams(dimension_semantics=("arbitrary",), vmem_limit_bytes=VMEM_LIMIT),
    )(dx1, x, y, uag, hc, hc, hd, pbv, mod8, g1, win_t, dww, lng, lnb, wpw, wg, psc, wout)


CHIPS = (0, 4, 2, 6)


def _wgrad_rs_call(a3s, bmap, bs, order, spack, dmodp, cact_t):
    nw = len(a3s)
    nchs = [a.shape[0] for a in a3s]
    rws = [n * CH // NDEV for n in nchs]
    cols = [bs[bmap[k]].shape[1] for k in range(nw)]
    tshapes = sorted({(rws[k], cols[k]) for k in range(nw)})
    srows = spack.shape[0]
    sub = 32
    nb = len(bs)

    def body(*refs):
        pos = 0

        def take(n):
            nonlocal pos
            pos += n
            return refs[pos - n : pos]

        a_refs, b_refs = take(nw), take(nb)
        spack_ref, dmodp_ref, cact_t_ref = take(3)
        o_refs = take(nw)
        ssum_ref, gbada_ref, gwada_ref = take(3)
        p_refs = take(nw)
        r1, cb, r2 = take(nw), take(nw), take(nw)
        tmps = dict(zip(tshapes, take(len(tshapes))))
        abuf, bbuf, bbuf_n, obuf, obuf_n, srecv, mrecv = take(7)
        asem, osem, bsem, tsem, d_s, d_r, i_s, i_r, ssem_s, ssem_r, msem_s, msem_r = take(12)
        x, y, c_, me = _place()
        sib, _ = _flip(x, y, c_, 1)

        def rcopy(src, dst, ss, rs, dev):
            return pltpu.make_async_remote_copy(src_ref=src, dst_ref=dst, send_sem=ss, recv_sem=rs, device_id=dev, device_id_type=MESH)

        srecv[me] = spack_ref[...]
        mrecv[me] = dmodp_ref[...]
        sends = []
        for r in range(1, NDEV):
            dev, _ = _flip(x, y, c_, r)
            sends.append(rcopy(srecv.at[me], srecv.at[me], ssem_s.at[r], ssem_r.at[r], dev))
            sends.append(rcopy(mrecv.at[me], mrecv.at[me], msem_s.at[r], msem_r.at[r], dev))
        for cp in sends:
            cp.start()

        loaded = None
        for k in order:
            wide = cols[k] == DM
            bb, ob = (bbuf, obuf) if wide else (bbuf_n, obuf_n)
            a_ref, p_ref, rw, tmp = a_refs[k], p_refs[k], rws[k], tmps[(rws[k], cols[k])]
            if bmap[k] != loaded:
                cp = pltpu.make_async_copy(b_refs[bmap[k]], bb, bsem)
                cp.start()
                cp.wait()
                loaded = bmap[k]

            def a_copy(m, slot, a_ref=a_ref):
                return pltpu.make_async_copy(a_ref.at[m], abuf.at[slot], asem.at[slot])

            def o_copy(m, slot, ob=ob, p_ref=p_ref):
                return pltpu.make_async_copy(ob.at[slot], p_ref.at[pl.ds(pl.multiple_of(m * CH, CH), CH), :], osem.at[slot])

            a_copy(0, 0).start()

            def step(m, carry, k=k, a_copy=a_copy, o_copy=o_copy, bb=bb, ob=ob):
                slot = lax.rem(m, 2)
                a_copy(m, slot).wait()

                @pl.when(m + 1 < nchs[k])
                def _():
                    a_copy(m + 1, 1 - slot).start()

                @pl.when(m >= 2)
                def _():
                    o_copy(m - 2, slot).wait()

                ob[slot] = _dot_tn(abuf[slot], bb[...]).astype(BF16)
                o_copy(m, slot).start()
                return carry

            lax.fori_loop(0, nchs[k], step, 0)
            for m in (nchs[k] - 2, nchs[k] - 1):
                o_copy(m, m % 2).wait()

            for q, r in enumerate(CHIPS):
                _, owner = _flip(x, y, c_, r | 1)
                cp = rcopy(_rows(p_ref, owner, rw), r1[k].at[q], d_s.at[k, q], d_r.at[k, q], sib)
                cp.start()
                sends.append(cp)
            for q, r in enumerate(CHIPS):
                dev, owner = _flip(x, y, c_, r)
                rcopy(_rows(p_ref, owner, rw), r1[k].at[q], d_s.at[k, q], d_r.at[k, q], (x, y, c_)).wait_recv()
                cp = pltpu.make_async_copy(_rows(p_ref, owner, rw), tmp, tsem)
                cp.start()
                cp.wait()

                def add_sib(j, carry, k=k, q=q, tmp=tmp):
                    rr = pl.ds(pl.multiple_of(j * sub, sub), sub)
                    t = tmp[rr, :].astype(F32) + r1[k][q, rr, :].astype(F32)
                    if q == 0:
                        o_refs[k][rr, :] = t
                    else:
                        cb[k][q - 1, rr, :] = t.astype(BF16)
                    return carry

                lax.fori_loop(0, rw // sub, add_sib, 0)
                if q:
                    cp = rcopy(cb[k].at[q - 1], r2[k].at[q - 1], i_s.at[k, q - 1], i_r.at[k, q - 1], dev)
                    cp.start()
                    sends.append(cp)

        for r in range(1, NDEV):
            _, pb = _flip(x, y, c_, r)
            rcopy(srecv.at[me], srecv.at[pb], ssem_s.at[r], ssem_r.at[r], (x, y, c_)).wait_recv()
            rcopy(mrecv.at[me], mrecv.at[pb], msem_s.at[r], msem_r.at[r], (x, y, c_)).wait_recv()
        tot = srecv[0]
        for s in range(1, NDEV):
            tot = tot + srecv[s]
        ssum_ref[...] = tot
        btot = mrecv[0]
        for s in range(1, NDEV):
            btot = btot + mrecv[s]
        gbada_ref[...] = btot
        rowm = lax.broadcasted_iota(jnp.int32, (8, MODW), 0)
        gw = jnp.zeros((DM, MODW), F32)
        for s in range(NDEV):
            drow = jnp.sum(jnp.where(rowm == me, mrecv[s], 0.0), axis=0, keepdims=True)
            gw = gw + cact_t_ref[:, s : s + 1] * drow
        gwada_ref[...] = gw

        for k in order:
            for q in range(3):
                rcopy(cb[k].at[q], r2[k].at[q], i_s.at[k, q], i_r.at[k, q], (x, y, c_)).wait_recv()

            def add_far(j, carry, k=k):
                rr = pl.ds(pl.multiple_of(j * sub, sub), sub)
                t = o_refs[k][rr, :]
                for q in range(3):
                    t = t + r2[k][q, rr, :].astype(F32)
                o_refs[k][rr, :] = t
                return carry

            lax.fori_loop(0, rws[k] // sub, add_far, 0)
        for cp in sends:
            cp.wait_send()

    out_shape = [jax.ShapeDtypeStruct((rws[k], cols[k]), F32) for k in range(nw)]
    out_shape += [jax.ShapeDtypeStruct((srows, DM), F32), jax.ShapeDtypeStruct((8, MODW), F32), jax.ShapeDtypeStruct((DM, MODW), F32)]
    out_shape += [jax.ShapeDtypeStruct((nchs[k] * CH, cols[k]), BF16) for k in range(nw)]
    scratch = [pltpu.VMEM((4, rws[k], cols[k]), BF16) for k in range(nw)]
    scratch += [pltpu.VMEM((3, rws[k], cols[k]), BF16) for k in range(nw)] * 2
    scratch += [pltpu.VMEM(s, BF16) for s in tshapes]
    scratch += [pltpu.VMEM((2, SEQ, CH), BF16), pltpu.VMEM((SEQ, DM), BF16), pltpu.VMEM((SEQ, CONVW), BF16),
                pltpu.VMEM((2, CH, DM), BF16), pltpu.VMEM((2, CH, CONVW), BF16),
                pltpu.VMEM((NDEV, srows, DM), F32), pltpu.VMEM((NDEV, 8, MODW), F32)]
    scratch += [pltpu.SemaphoreType.DMA((2,))] * 2 + [pltpu.SemaphoreType.DMA] * 2
    scratch += [pltpu.SemaphoreType.DMA((nw, 4))] * 2 + [pltpu.SemaphoreType.DMA((nw, 3))] * 2 + [pltpu.SemaphoreType.DMA((NDEV,))] * 4
    outs = pl.pallas_call(
        body,
        name="wgrad_rs",
        out_shape=out_shape,
        in_specs=[_HBM] * (nw + nb) + [_VM] * 3,
        out_specs=[_VM] * (nw + 3) + [_HBM] * nw,
        scratch_shapes=scratch,
        compiler_params=pltpu.CompilerParams(vmem_limit_bytes=60 * 1024 * 1024),
    )(*a3s, *bs, spack, dmodp, cact_t)
    return outs[: nw + 3]


def _adam_call(name, ws, gs, ms, vs):
    n = len(ws)
    bc1 = 1.0 - ADAM_B1 ** ADAM_STEP
    bc2 = 1.0 - ADAM_B2 ** ADAM_STEP

    def body(*refs):
        for i in range(n):
            w, g, m, v = (refs[j * n + i][...] for j in range(4))
            m = ADAM_B1 * m + (1.0 - ADAM_B1) * g
            v = ADAM_B2 * v + (1.0 - ADAM_B2) * (g * g)
            m_hat = m / bc1
            v_hat = v / bc2
            refs[4 * n + i][...] = -ADAM_LR * (m_hat / (jnp.sqrt(v_hat) + ADAM_EPS) + ADAM_WD * w)
            refs[5 * n + i][...] = m
            refs[6 * n + i][...] = v

    shapes = [jax.ShapeDtypeStruct(w.shape, F32) for w in ws]
    outs = pl.pallas_call(
        body,
        name=name,
        out_shape=shapes * 3,
        in_specs=[_VM] * (4 * n),
        out_specs=[_VM] * (3 * n),
        compiler_params=pltpu.CompilerParams(vmem_limit_bytes=VMEM_LIMIT),
    )(*ws, *gs, *ms, *vs)
    return outs[:n], outs[n : 2 * n], outs[2 * n :]


def kernel(x, c, w_ada, b_ada, g_norm1, w_in, dw_w, dw_b, conv_ln_g, conv_ln_b, w_conv_pw, w_pool_group, pool_scale, w_out, g_norm2, w_ffn_gate, w_ffn_up, w_ffn_down, g_final, loss_target, m_w_ada, m_b_ada, m_g_norm1, m_w_in, m_dw_w, m_dw_b, m_conv_ln_g, m_conv_ln_b, m_w_conv_pw, m_w_pool_group, m_pool_scale, m_w_out, m_g_norm2, m_w_ffn_gate, m_w_ffn_up, m_w_ffn_down, m_g_final, v_w_ada, v_b_ada, v_g_norm1, v_w_in, v_dw_w, v_dw_b, v_conv_ln_g, v_conv_ln_b, v_w_conv_pw, v_w_pool_group, v_pool_scale, v_w_out, v_g_norm2, v_w_ffn_gate, v_w_ffn_up, v_w_ffn_down, v_g_final):
    me = 4 * lax.axis_index("x") + 2 * lax.axis_index("y") + lax.axis_index("c")
    xs, tgt = x[0], loss_target[0]
    b_my = lax.dynamic_slice(b_ada, (0, me * MODW), (1, MODW))
    win_t, wout, wpw, mod8, cact, dww8 = _gather_call(c, w_ada[0], b_my, dw_w[0], [w_in[0].T, w_out[0], w_conv_pw[0]])
    dww = jnp.pad(jnp.transpose(dww8, (1, 0, 2)).reshape(KCONV, CONVW), ((0, HALO_C - KCONV), (0, 0)))
    wgp = w_pool_group[0]

    h1b, uag, hc, hd, hsb3, pbv, ycb3, y, x1, wg_t, wu_t, wd = _fwd_mix_call(
        xs, mod8, g_norm1, win_t, dww, dw_b, conv_ln_g, conv_ln_b, wpw, wgp, pool_scale, wout,
        [w_ffn_gate[0].T, w_ffn_up[0].T, w_ffn_down[0]])
    h2b, a3, dfb, dg3, du3, dx1, facc = _ffn_call(x1, tgt, mod8, g_norm2, g_final.reshape(1, DM), wg_t, wu_t, wd)
    gx, dyb, dycb, dub3, macc, ddw, dwg = _bwd_mix_call(
        dx1, xs, y, uag, hc, hd, pbv, mod8, g_norm1, win_t, dww, conv_ln_g, conv_ln_b, wpw, wgp, pool_scale, wout)
    spack = jnp.concatenate(
        [macc[3:4], facc[3:4], facc[4:5], macc[4:6], facc[5:6], jnp.zeros((2, DM), F32), ddw.reshape(HALO_C // 2, DM), dwg.reshape(-1, DM)], axis=0)
    dmodp = jnp.concatenate([macc[0:3], facc[0:3]], axis=0).reshape(8, MODW)
    g_in_t, g_gate_t, g_up_t, g_down, g_out, g_pw, ssum, gbada, g_wada = _wgrad_rs_call(
        [dub3, dg3, du3, a3, ycb3, hsb3], [0, 1, 1, 2, 3, 4], [h1b, h2b, dfb, dyb, dycb], (4, 3, 1, 2, 0, 5), spack, dmodp, cact.T)

    loss = ssum[5, 0]
    ddw_all = ssum[8 : 8 + HALO_C // 2].reshape(HALO_C, CONVW)[:KCONV]
    grads = {
        "w_ada": g_wada,
        "b_ada": gbada.reshape(1, 6 * DM),
        "g_norm1": ssum[0:1],
        "w_in": g_in_t.T,
        "dw_w": lax.dynamic_slice(ddw_all, (0, me * (CONVW // NDEV)), (KCONV, CONVW // NDEV)),
        "dw_b": ssum[3:4, 0:CONVW],
        "conv_ln_g": ssum[3:4, CONVW:DM],
        "conv_ln_b": ssum[4:5, 0:CONVW],
        "w_conv_pw": g_pw,
        "w_pool_group": ssum[8 + HALO_C // 2 :].reshape(len(WINS) * PGD, PGD),
        "pool_scale": ssum[4:5, CONVW:DM],
        "w_out": g_out,
        "g_norm2": ssum[1:2],
        "w_ffn_gate": g_gate_t.T,
        "w_ffn_up": g_up_t.T,
        "w_ffn_down": g_down,
        "g_final": ssum[2:3],
    }
    given = dict(w_ada=(w_ada, m_w_ada, v_w_ada), b_ada=(b_ada, m_b_ada, v_b_ada), g_norm1=(g_norm1, m_g_norm1, v_g_norm1),
                 w_in=(w_in, m_w_in, v_w_in), dw_w=(dw_w, m_dw_w, v_dw_w), dw_b=(dw_b, m_dw_b, v_dw_b),
                 conv_ln_g=(conv_ln_g, m_conv_ln_g, v_conv_ln_g), conv_ln_b=(conv_ln_b, m_conv_ln_b, v_conv_ln_b),
                 w_conv_pw=(w_conv_pw, m_w_conv_pw, v_w_conv_pw), w_pool_group=(w_pool_group, m_w_pool_group, v_w_pool_group),
                 pool_scale=(pool_scale, m_pool_scale, v_pool_scale), w_out=(w_out, m_w_out, v_w_out), g_norm2=(g_norm2, m_g_norm2, v_g_norm2),
                 w_ffn_gate=(w_ffn_gate, m_w_ffn_gate, v_w_ffn_gate), w_ffn_up=(w_ffn_up, m_w_ffn_up, v_w_ffn_up),
                 w_ffn_down=(w_ffn_down, m_w_ffn_down, v_w_ffn_down), g_final=(g_final, m_g_final, v_g_final))
    names = list(given)
    groups = [["w_ada"], ["w_ffn_gate", "w_ffn_up"], ["w_ffn_down", "w_in", "w_out", "w_conv_pw"],
              ["b_ada", "g_norm1", "dw_w", "dw_b", "conv_ln_g", "conv_ln_b", "w_pool_group", "pool_scale", "g_norm2", "g_final"]]
    delta, new_m, new_v = {}, {}, {}
    for gi, grp in enumerate(groups):
        shp = [grads[n].shape for n in grp]
        ds, ms, vs = _adam_call(
            f"adam{gi}", [given[n][0].reshape(s) for n, s in zip(grp, shp)], [grads[n] for n in grp],
            [given[n][1].reshape(s) for n, s in zip(grp, shp)], [given[n][2].reshape(s) for n, s in zip(grp, shp)])
        for n, d_, m_, v_ in zip(grp, ds, ms, vs):
            delta[n], new_m[n], new_v[n] = d_, m_, v_

    def full(n, a):
        return a.reshape(given[n][0].shape)

    return (loss, gx.reshape(x.shape), *[full(n, grads[n]) for n in names], *[full(n, delta[n]) for n in names],
            *[full(n, new_m[n]) for n in names], *[full(n, new_v[n]) for n in names])
```

```python
import functools

import jax
import jax.numpy as jnp
from jax import lax
from jax.experimental import pallas as pl
from jax.experimental.pallas import tpu as pltpu

F32, BF16 = jnp.float32, jnp.bfloat16
SEQ, DM = 2048, 1024
CONVW, POOLW = 512, 512
KCONV = 31
WINS = (2, 4, 8, 16)
PGD = 128
DFF = 2816
NDEV = 8
MODW = 6 * DM // NDEV
EPS = 1e-6
TOK = 256
NTILE = SEQ // TOK
CH = 256
NCH = DFF // CH
HALO_C, HALO_P = 32, 16
MESH = pl.DeviceIdType.MESH
VMEM_LIMIT = 56 * 1024 * 1024
ADAM_LR, ADAM_B1, ADAM_B2, ADAM_EPS, ADAM_WD, ADAM_STEP = 0.001, 0.9, 0.999, 1e-08, 0.01, 10
HI = lax.Precision.HIGHEST

_VM = pl.BlockSpec(memory_space=pltpu.VMEM)
_HBM = pl.BlockSpec(memory_space=pltpu.HBM)


def _place():
    x, y, c = lax.axis_index("x"), lax.axis_index("y"), lax.axis_index("c")
    return x, y, c, 4 * x + 2 * y + c


def _flip(x, y, c, r):
    px = 1 - x if r & 4 else x
    py = 1 - y if r & 2 else y
    pc = 1 - c if r & 1 else c
    return (px, py, pc), 4 * px + 2 * py + pc


def _rows(ref, blk, n):
    return ref.at[pl.ds(pl.multiple_of(blk * n, 16), n), :]


def _sig(z):
    return jax.nn.sigmoid(z)


def _dot_nt(a, b):
    return lax.dot_general(a, b, (((1,), (1,)), ((), ())), preferred_element_type=F32)


def _dot_nn(a, b):
    return lax.dot_general(a, b, (((1,), (0,)), ((), ())), preferred_element_type=F32)


def _dot_tn(a, b):
    return lax.dot_general(a, b, (((0,), (0,)), ((), ())), preferred_element_type=F32)


WAYS = (1, 4, 2, 6)


def _rcopy(src, dst, ss, rs, dev):
    return pltpu.make_async_remote_copy(src_ref=src, dst_ref=dst, send_sem=ss, recv_sem=rs, device_id=dev, device_id_type=MESH)


def _ag_copies(sbufs, g_refs, rws, wsem_s, wsem_r, lsem):
    x, y, c_, me = _place()
    local = [pltpu.make_async_copy(sbufs[k], _rows(g_refs[k], me, rws[k]), lsem.at[k]) for k in range(len(sbufs))]
    first = [_rcopy(sbufs[k], _rows(g_refs[k], me, rws[k]), wsem_s.at[k, j], wsem_r.at[k, j], _flip(x, y, c_, r)[0])
             for k in range(len(sbufs)) for j, r in enumerate(WAYS)]
    return local, first


def _ag_passes(g_refs, rws, fsem_s, fsem_r):
    x, y, c_, me = _place()
    sib, _ = _flip(x, y, c_, 1)
    out = []
    for k in range(len(g_refs)):
        for j, r in enumerate(WAYS[1:]):
            blk = _rows(g_refs[k], _flip(x, y, c_, r)[1], rws[k])
            out.append(_rcopy(blk, blk, fsem_s.at[k, j], fsem_r.at[k, j], sib))
    return out


def _ag_start(sbufs, g_refs, rws, wsem_s, wsem_r, lsem):
    local, first = _ag_copies(sbufs, g_refs, rws, wsem_s, wsem_r, lsem)
    for cp in local + first:
        cp.start()


def _ag_pass_on(sbufs, g_refs, rws, wsem_s, wsem_r, fsem_s, fsem_r):
    x, y, c_, me = _place()
    passes = _ag_passes(g_refs, rws, fsem_s, fsem_r)
    for k in range(len(g_refs)):
        for j, r in enumerate(WAYS[1:]):
            blk = _rows(g_refs[k], _flip(x, y, c_, r)[1], rws[k])
            _rcopy(sbufs[k], blk, wsem_s.at[k, j + 1], wsem_r.at[k, j + 1], (x, y, c_)).wait_recv()
            passes[3 * k + j].start()


def _ag_finish(sbufs, g_refs, rws, wsem_s, wsem_r, fsem_s, fsem_r, lsem):
    x, y, c_, me = _place()
    for k in range(len(g_refs)):
        blk = _rows(g_refs[k], _flip(x, y, c_, 1)[1], rws[k])
        _rcopy(sbufs[k], blk, wsem_s.at[k, 0], wsem_r.at[k, 0], (x, y, c_)).wait_recv()
        for j, r in enumerate(WAYS[1:]):
            blk = _rows(g_refs[k], _flip(x, y, c_, r ^ 1)[1], rws[k])
            _rcopy(blk, blk, fsem_s.at[k, j], fsem_r.at[k, j], (x, y, c_)).wait_recv()
    local, first = _ag_copies(sbufs, g_refs, rws, wsem_s, wsem_r, lsem)
    for cp in first + _ag_passes(g_refs, rws, fsem_s, fsem_r):
        cp.wait_send()
    for cp in local:
        cp.wait()


def _gather_call(c, w_ada, b_my, dww, shards):
    nw = len(shards)
    rws = [s.shape[0] for s in shards]

    def body(*refs):
        c_ref, wada_ref, bmy_ref, dww_ref = refs[:4]
        s_refs = refs[4 : 4 + nw]
        g_refs = refs[4 + nw : 4 + 2 * nw]
        mod8_ref, cact_ref, dww8_ref = refs[4 + 2 * nw : 7 + 2 * nw]
        crecv, msend, mrecv = refs[7 + 2 * nw : 10 + 2 * nw]
        sbufs = refs[10 + 2 * nw : 10 + 3 * nw]
        csem_s, csem_r, dsem_s, dsem_r, msem_s, msem_r, wsem_s, wsem_r, fsem_s, fsem_r, lsem = refs[10 + 3 * nw :]
        x, y, c_, me = _place()

        def rcopy(src, dst, ss, rs, dev):
            return pltpu.make_async_remote_copy(src_ref=src, dst_ref=dst, send_sem=ss, recv_sem=rs, device_id=dev, device_id_type=MESH)

        crecv[me] = jnp.broadcast_to(c_ref[...], (8, DM))
        dww8_ref[me] = dww_ref[...]
        small = []
        for r in range(1, NDEV):
            dev, _ = _flip(x, y, c_, r)
            small.append(rcopy(crecv.at[me], crecv.at[me], csem_s.at[r], csem_r.at[r], dev))
            small.append(rcopy(dww8_ref.at[me], dww8_ref.at[me], dsem_s.at[r], dsem_r.at[r], dev))
        for cp in small:
            cp.start()

        for k in range(nw):
            sbufs[k][...] = s_refs[k][...].astype(BF16)
        _ag_start(sbufs, g_refs, rws, wsem_s, wsem_r, lsem)

        rowid = lax.broadcasted_iota(jnp.int32, (8, DM), 0)
        for r in range(1, NDEV):
            _, pb = _flip(x, y, c_, r)
            rcopy(crecv.at[me], crecv.at[pb], csem_s.at[r], csem_r.at[r], (x, y, c_)).wait_recv()
        call = jnp.zeros((8, DM), F32)
        for s in range(NDEV):
            call = jnp.where(rowid == s, crecv[s], call)
        cact = call * _sig(call)
        cact_ref[...] = cact
        modp = jnp.dot(cact, wada_ref[...], precision=HI, preferred_element_type=F32) + bmy_ref[...]
        rowm = lax.broadcasted_iota(jnp.int32, (8, MODW), 0)
        for b in range(NDEV):
            row = jnp.sum(jnp.where(rowm == b, modp, 0.0), axis=0, keepdims=True)
            msend[b] = jnp.broadcast_to(row, (8, MODW))
        mrecv[me] = msend[me]
        msends = []
        for r in range(1, NDEV):
            dev, pb = _flip(x, y, c_, r)
            cp = rcopy(msend.at[pb], mrecv.at[me], msem_s.at[r], msem_r.at[r], dev)
            cp.start()
            msends.append(cp)

        _ag_pass_on(sbufs, g_refs, rws, wsem_s, wsem_r, fsem_s, fsem_r)

        for r in range(1, NDEV):
            _, pb = _flip(x, y, c_, r)
            rcopy(msend.at[pb], mrecv.at[pb], msem_s.at[r], msem_r.at[r], (x, y, c_)).wait_recv()
        for s in range(NDEV):
            mod8_ref[:, s * MODW : (s + 1) * MODW] = mrecv[s]

        _ag_finish(sbufs, g_refs, rws, wsem_s, wsem_r, fsem_s, fsem_r, lsem)
        for r in range(1, NDEV):
            _, pb = _flip(x, y, c_, r)
            rcopy(dww8_ref.at[me], dww8_ref.at[pb], dsem_s.at[r], dsem_r.at[r], (x, y, c_)).wait_recv()
        for cp in small + msends:
            cp.wait_send()

    out_shape = [jax.ShapeDtypeStruct((NDEV * s.shape[0], s.shape[1]), BF16) for s in shards]
    out_shape += [
        jax.ShapeDtypeStruct((8, 6 * DM), F32),
        jax.ShapeDtypeStruct((8, DM), F32),
        jax.ShapeDtypeStruct((NDEV,) + dww.shape, F32),
    ]
    scratch = [pltpu.VMEM((NDEV, 8, DM), F32), pltpu.VMEM((NDEV, 8, MODW), F32), pltpu.VMEM((NDEV, 8, MODW), F32)]
    scratch += [pltpu.VMEM(s.shape, BF16) for s in shards]
    scratch += [pltpu.SemaphoreType.DMA((NDEV,))] * 6
    scratch += [pltpu.SemaphoreType.DMA((nw, 4))] * 2 + [pltpu.SemaphoreType.DMA((nw, 3))] * 2 + [pltpu.SemaphoreType.DMA((nw,))]
    return pl.pallas_call(
        body,
        name="gather",
        out_shape=out_shape,
        in_specs=[_VM] * (4 + nw),
        out_specs=[_HBM] * nw + [_VM] * 3,
        scratch_shapes=scratch,
        compiler_params=pltpu.CompilerParams(vmem_limit_bytes=VMEM_LIMIT),
    )(c, w_ada, b_my, dww, *shards)


def _const(shape):
    return pl.BlockSpec(shape, lambda i: (0,) * len(shape))


def _tile(width, rev=False):
    if rev:
        return pl.BlockSpec((TOK, width), lambda i: (NTILE - 1 - i, 0))
    return pl.BlockSpec((TOK, width), lambda i: (i, 0))


def _tile3(nch, rev=False):
    if rev:
        return pl.BlockSpec((nch, TOK, CH), lambda i: (0, NTILE - 1 - i, 0))
    return pl.BlockSpec((nch, TOK, CH), lambda i: (0, i, 0))


def _norm_mod(x, g, sc, sh):
    r = lax.rsqrt(jnp.mean(x * x, axis=-1, keepdims=True) + EPS)
    xr = x * r
    return r, xr, xr * g * (1.0 + sc) + sh


def _fwd_mix_call(x, mod8, g1, win_t, dww, dwb, lng, lnb, wpw, wg, psc, wout, shards):
    ns = len(shards)
    rws = [s.shape[0] for s in shards]

    def body(x_ref, mod_ref, g1_ref, win_ref, dww_ref, dwb_ref, lng_ref, lnb_ref, wpw_ref, wg_ref, psc_ref, wout_ref, *rest):
        s_refs = rest[:ns]
        h1b_ref, uag_ref, hc_ref, hd_ref, hsb3_ref, pb_ref, ycb3_ref, y_ref, x1_ref = rest[ns : ns + 9]
        g_refs = rest[ns + 9 : 2 * ns + 9]
        hc_ext, up_ext, ycb_ref, stage = rest[2 * ns + 9 : 2 * ns + 13]
        sbufs = rest[2 * ns + 13 : 3 * ns + 13]
        wsem_s, wsem_r, fsem_s, fsem_r, lsem, ssem = rest[3 * ns + 13 :]
        i = pl.program_id(0)

        @pl.when(i == 0)
        def _():
            for k in range(ns):
                cp = pltpu.make_async_copy(s_refs[k], stage, ssem)
                cp.start()
                cp.wait()
                sbufs[k][...] = stage[...].astype(BF16)
            _ag_start(sbufs, g_refs, rws, wsem_s, wsem_r, lsem)
            hc_ext[0:HALO_C, :] = jnp.zeros((HALO_C, CONVW), F32)
            up_ext[0:HALO_P, :] = jnp.zeros((HALO_P, POOLW), F32)

        x = x_ref[...]
        sh1, sc1, gt1 = mod_ref[0:1, 0:DM], mod_ref[0:1, DM : 2 * DM], mod_ref[0:1, 2 * DM : 3 * DM]
        _, _, h1 = _norm_mod(x, g1_ref[...], sc1, sh1)
        h1b = h1.astype(BF16)
        h1b_ref[...] = h1b
        u = _dot_nt(h1b, win_ref[...])
        uag_ref[...] = u[:, : 2 * CONVW]
        hc = u[:, :CONVW] * _sig(u[:, CONVW : 2 * CONVW])
        hc_ref[...] = hc
        hc_ext[HALO_C : HALO_C + TOK, :] = hc
        up_ext[HALO_P : HALO_P + TOK, :] = u[:, 2 * CONVW :]

        acc = jnp.zeros((TOK, CONVW), F32)
        for k in range(KCONV):
            acc = acc + dww_ref[k : k + 1, :] * hc_ext[pl.ds(HALO_C - (KCONV - 1) + k, TOK), :]
        hd = acc + dwb_ref[...]
        hd_ref[...] = hd
        hc_ext[0:HALO_C, :] = hc_ext[TOK : TOK + HALO_C, :]
        mu = jnp.mean(hd, axis=-1, keepdims=True)
        dlt = hd - mu
        rstd = lax.rsqrt(jnp.mean(dlt * dlt, axis=-1, keepdims=True) + EPS)
        hl = dlt * rstd * lng_ref[...] + lnb_ref[...]
        hsb = (hl * _sig(hl)).astype(BF16)
        for j in range(CONVW // CH):
            hsb3_ref[j] = hsb[:, j * CH : (j + 1) * CH]
        ycb_ref[:, 0:CONVW] = _dot_nn(hsb, wpw_ref[...]).astype(BF16)

        tg = i * TOK + lax.broadcasted_iota(jnp.int32, (TOK, 1), 0)
        for g, w in enumerate(WINS):
            ln = slice(PGD * g, PGD * (g + 1))
            v = up_ext[pl.ds(HALO_P, TOK), ln]
            ssum = v
            for d in range(1, w):
                ssum = ssum + up_ext[pl.ds(HALO_P - d, TOK), ln]
            cnt = jnp.minimum(tg + 1, w).astype(F32)
            pb = (ssum / cnt - v).astype(BF16)
            pb_ref[:, ln] = pb
            z = _dot_nn(pb, wg_ref[g].astype(BF16))
            ycb_ref[:, CONVW + PGD * g : CONVW + PGD * (g + 1)] = (z * psc_ref[:, ln]).astype(BF16)
        up_ext[0:HALO_P, :] = up_ext[TOK : TOK + HALO_P, :]

        for j in range(DM // CH):
            ycb3_ref[j] = ycb_ref[:, j * CH : (j + 1) * CH]
        yv = _dot_nn(ycb_ref[...], wout_ref[...])
        y_ref[...] = yv
        x1_ref[...] = x + gt1 * yv

        @pl.when(i == NTILE - 1)
        def _():
            _ag_pass_on(sbufs, g_refs, rws, wsem_s, wsem_r, fsem_s, fsem_r)
            _ag_finish(sbufs, g_refs, rws, wsem_s, wsem_r, fsem_s, fsem_r, lsem)

    outs = [(DM, BF16), (2 * CONVW, F32), (CONVW, F32), (CONVW, F32), (-CONVW, BF16), (POOLW, BF16), (-DM, BF16), (DM, F32), (DM, F32)]
    return pl.pallas_call(
        body,
        name="fwd_mix",
        grid=(NTILE,),
        out_shape=[jax.ShapeDtypeStruct((SEQ, w) if w > 0 else (-w // CH, SEQ, CH), d) for w, d in outs]
        + [jax.ShapeDtypeStruct((NDEV * s.shape[0], s.shape[1]), BF16) for s in shards],
        in_specs=[_tile(DM), _const((8, 6 * DM)), _const((1, DM)), _const(win_t.shape), _const(dww.shape), _const((1, CONVW)),
                  _const((1, CONVW)), _const((1, CONVW)), _const(wpw.shape), _const(wg.shape), _const((1, POOLW)), _const(wout.shape)]
        + [_HBM] * ns,
        out_specs=[_tile(w) if w > 0 else _tile3(-w // CH) for w, _ in outs] + [_HBM] * ns,
        scratch_shapes=[pltpu.VMEM((TOK + HALO_C, CONVW), F32), pltpu.VMEM((TOK + HALO_P, POOLW), F32), pltpu.VMEM((TOK, DM), BF16),
                        pltpu.VMEM(shards[0].shape, F32)] + [pltpu.VMEM(s.shape, BF16) for s in shards]
        + [pltpu.SemaphoreType.DMA((ns, 4))] * 2 + [pltpu.SemaphoreType.DMA((ns, 3))] * 2 + [pltpu.SemaphoreType.DMA((ns,)), pltpu.SemaphoreType.DMA],
        compiler_params=pltpu.CompilerParams(dimension_semantics=("arbitrary",), vmem_limit_bytes=VMEM_LIMIT),
    )(x, mod8, g1, win_t, dww, dwb, lng, lnb, wpw, wg, psc, wout, *shards)


def _ffn_call(x1, tgt, mod8, g2, gf, wg_t, wu_t, wd):
    def body(x1_ref, tgt_ref, mod_ref, g2_ref, gf_ref, wg_hbm, wu_hbm, wd_hbm,
             h2b_ref, a3_ref, dfb_ref, dg3_ref, du3_ref, dx1_ref, acc_ref,
             wg_ref, wu_ref, wd_ref, wsem):
        i = pl.program_id(0)

        @pl.when(i == 0)
        def _():
            cps = [pltpu.make_async_copy(s, d, wsem.at[n]) for n, (s, d) in enumerate(((wg_hbm, wg_ref), (wu_hbm, wu_ref), (wd_hbm, wd_ref)))]
            for cp in cps:
                cp.start()
            acc_ref[...] = jnp.zeros((8, DM), F32)
            for cp in cps:
                cp.wait()

        x1 = x1_ref[...]
        sh2, sc2, gt2 = mod_ref[0:1, 3 * DM : 4 * DM], mod_ref[0:1, 4 * DM : 5 * DM], mod_ref[0:1, 5 * DM : 6 * DM]
        g2 = g2_ref[...]
        r2, xr, h2 = _norm_mod(x1, g2, sc2, sh2)
        h2b = h2.astype(BF16)
        h2b_ref[...] = h2b
        gate = _dot_nt(h2b, wg_ref[...])
        up = _dot_nt(h2b, wu_ref[...])
        ab = (gate * _sig(gate) * up).astype(BF16)
        for j in range(NCH):
            a3_ref[j] = ab[:, j * CH : (j + 1) * CH]
        f = _dot_nn(ab, wd_ref[...])
        x2 = x1 + gt2 * f
        rf = lax.rsqrt(jnp.mean(x2 * x2, axis=-1, keepdims=True) + EPS)
        nf = x2 * rf
        gf_ = gf_ref[...]
        err = nf * gf_ - tgt_ref[...]
        loss = 0.5 * jnp.sum(jnp.sum(err * err, axis=-1, keepdims=True), axis=0, keepdims=True) * (1.0 / DM)
        dout = err * (1.0 / DM)
        dnf = dout * gf_
        dx2 = rf * (dnf - nf * jnp.mean(dnf * nf, axis=-1, keepdims=True))
        dfb = (gt2 * dx2).astype(BF16)
        dfb_ref[...] = dfb
        da = _dot_nt(dfb, wd_ref[...])
        sg = _sig(gate)
        dgb = (da * up * (sg * (1.0 + gate * (1.0 - sg)))).astype(BF16)
        dub = (da * (gate * sg)).astype(BF16)
        for j in range(NCH):
            dg3_ref[j] = dgb[:, j * CH : (j + 1) * CH]
            du3_ref[j] = dub[:, j * CH : (j + 1) * CH]
        dh2 = _dot_nn(dgb, wg_ref[...]) + _dot_nn(dub, wu_ref[...])
        dn2 = dh2 * (1.0 + sc2)
        dxr = dn2 * g2
        dx1_ref[...] = dx2 + r2 * (dxr - xr * jnp.mean(dxr * xr, axis=-1, keepdims=True))

        def colsum(v):
            return jnp.sum(v, axis=0, keepdims=True)

        acc_ref[0:1, :] += colsum(dh2)
        acc_ref[1:2, :] += colsum(dh2 * (xr * g2))
        acc_ref[2:3, :] += colsum(dx2 * f)
        acc_ref[3:4, :] += colsum(dn2 * xr)
        acc_ref[4:5, :] += colsum(dout * nf)
        acc_ref[5:6, :] += jnp.broadcast_to(loss, (1, DM))

    c3 = pl.BlockSpec((NCH, TOK, CH), lambda i: (0, i, 0))
    return pl.pallas_call(
        body,
        name="ffn",
        grid=(NTILE,),
        out_shape=[jax.ShapeDtypeStruct((SEQ, DM), BF16), jax.ShapeDtypeStruct((NCH, SEQ, CH), BF16), jax.ShapeDtypeStruct((SEQ, DM), BF16),
                   jax.ShapeDtypeStruct((NCH, SEQ, CH), BF16), jax.ShapeDtypeStruct((NCH, SEQ, CH), BF16),
                   jax.ShapeDtypeStruct((SEQ, DM), F32), jax.ShapeDtypeStruct((8, DM), F32)],
        in_specs=[_tile(DM), _tile(DM), _const((8, 6 * DM)), _const((1, DM)), _const((1, DM)), _HBM, _HBM, _HBM],
        out_specs=[_tile(DM), c3, _tile(DM), c3, c3, _tile(DM), _const((8, DM))],
        scratch_shapes=[pltpu.VMEM((DFF, DM), BF16)] * 3 + [pltpu.SemaphoreType.DMA((3,))],
        compiler_params=pltpu.CompilerParams(dimension_semantics=("arbitrary",), vmem_limit_bytes=VMEM_LIMIT),
    )(x1, tgt, mod8, g2, gf, wg_t, wu_t, wd)


def _bwd_mix_call(dx1, x, y, uag, hc, hd, pbv, mod8, g1, win_t, dww, lng, lnb, wpw, wg, psc, wout):
    hpt = TOK // HALO_C

    def body(dx1_ref, x_ref, y_ref, uag_ref, hc_ref, halo_ref, hd_ref, pb_ref, mod_ref, g1_ref, win_ref, dww_ref, lng_ref, lnb_ref,
             wpw_ref, wg_ref, psc_ref, wout_ref,
             gx_ref, dyb_ref, dycb_ref, dub3_ref, acc_ref, ddw_ref, dwg_ref, d_ext, q_ext, hcx, dub_ref, hrot, drot):
        i = pl.program_id(0)
        it = NTILE - 1 - i

        @pl.when(i == 0)
        def _():
            d_ext[TOK : TOK + HALO_C, :] = jnp.zeros((HALO_C, CONVW), F32)
            q_ext[TOK : TOK + HALO_P, :] = jnp.zeros((HALO_P, POOLW), F32)
            acc_ref[...] = jnp.zeros((8, DM), F32)
            ddw_ref[...] = jnp.zeros((HALO_C, CONVW), F32)
            dwg_ref[...] = jnp.zeros((len(WINS) * PGD, PGD), F32)

        def colsum(v):
            return jnp.sum(v, axis=0, keepdims=True)

        dx1 = dx1_ref[...]
        x = x_ref[...]
        sh1, sc1, gt1 = mod_ref[0:1, 0:DM], mod_ref[0:1, DM : 2 * DM], mod_ref[0:1, 2 * DM : 3 * DM]
        acc_ref[2:3, :] += colsum(dx1 * y_ref[...])
        dyb = (gt1 * dx1).astype(BF16)
        dyb_ref[...] = dyb
        dycat = _dot_nt(dyb, wout_ref[...])

        hd = hd_ref[...]
        mu = jnp.mean(hd, axis=-1, keepdims=True)
        dlt = hd - mu
        rstd = lax.rsqrt(jnp.mean(dlt * dlt, axis=-1, keepdims=True) + EPS)
        xhat = dlt * rstd
        lng = lng_ref[...]
        hl = xhat * lng + lnb_ref[...]
        sgl = _sig(hl)
        dycb = dycat[:, :CONVW].astype(BF16)
        dycb_ref[...] = dycb
        dhl = _dot_nt(dycb, wpw_ref[...]) * (sgl * (1.0 + hl * (1.0 - sgl)))
        acc_ref[5:6, 0:CONVW] += colsum(dhl)
        acc_ref[4:5, CONVW:DM] += colsum(dhl * xhat)
        dxh = dhl * lng
        dhd = rstd * (dxh - jnp.mean(dxh, axis=-1, keepdims=True) - xhat * jnp.mean(dxh * xhat, axis=-1, keepdims=True))
        acc_ref[4:5, 0:CONVW] += colsum(dhd)

        hcx[0:HALO_C, :] = jnp.where(it == 0, 0.0, halo_ref[...])
        hcx[HALO_C : HALO_C + TOK, :] = hc_ref[...]
        d_ext[0:TOK, :] = dhd
        for b in range(1, 8):
            hrot[b - 1] = hcx[pl.ds(b, TOK + HALO_C - 8), :]
            drot[b - 1] = d_ext[pl.ds(b, TOK + HALO_C - 8), :]

        def tap(base, rot, off):
            a, b = divmod(off, 8)
            return base[pl.ds(8 * a, TOK), :] if b == 0 else rot[b - 1, pl.ds(8 * a, TOK), :]

        dhc = jnp.zeros((TOK, CONVW), F32)
        for k in range(KCONV):
            ddw_ref[k : k + 1, :] += colsum(dhd * tap(hcx, hrot, HALO_C - (KCONV - 1) + k))
            dhc = dhc + dww_ref[k : k + 1, :] * tap(d_ext, drot, KCONV - 1 - k)
        d_ext[TOK : TOK + HALO_C, :] = d_ext[0:HALO_C, :]
        ua, ug = uag_ref[:, 0:CONVW], uag_ref[:, CONVW : 2 * CONVW]
        sgg = _sig(ug)
        dub_ref[:, 0:CONVW] = (dhc * sgg).astype(BF16)
        dub_ref[:, CONVW : 2 * CONVW] = (dhc * ua * sgg * (1.0 - sgg)).astype(BF16)

        tg = it * TOK + lax.broadcasted_iota(jnp.int32, (TOK, 1), 0)
        for g, w in enumerate(WINS):
            ln = slice(PGD * g, PGD * (g + 1))
            wgb = wg_ref[g].astype(BF16)
            pb = pb_ref[:, ln]
            dyp = dycat[:, CONVW + PGD * g : CONVW + PGD * (g + 1)]
            acc_ref[5:6, CONVW + PGD * g : CONVW + PGD * (g + 1)] += colsum(dyp * _dot_nn(pb, wgb))
            dzb = (dyp * psc_ref[:, ln]).astype(BF16)
            dwg_ref[PGD * g : PGD * (g + 1), :] += _dot_tn(pb, dzb)
            dp = _dot_nt(dzb, wgb)
            cnt = jnp.minimum(tg + 1, w).astype(F32)
            q_ext[0:TOK, ln] = dp / cnt
            dv = -dp
            for d in range(w):
                dv = dv + q_ext[pl.ds(d, TOK), ln]
            dub_ref[:, 2 * CONVW + PGD * g : 2 * CONVW + PGD * (g + 1)] = dv.astype(BF16)
        q_ext[TOK : TOK + HALO_P, :] = q_ext[0:HALO_P, :]

        for j in range(3 * CONVW // CH):
            dub3_ref[j] = dub_ref[:, j * CH : (j + 1) * CH]
        dh1 = _dot_nn(dub_ref[...], win_ref[...])
        g1 = g1_ref[...]
        r1 = lax.rsqrt(jnp.mean(x * x, axis=-1, keepdims=True) + EPS)
        xr = x * r1
        acc_ref[0:1, :] += colsum(dh1)
        acc_ref[1:2, :] += colsum(dh1 * (xr * g1))
        dn1 = dh1 * (1.0 + sc1)
        acc_ref[3:4, :] += colsum(dn1 * xr)
        dxr = dn1 * g1
        gx_ref[...] = dx1 + r1 * (dxr - xr * jnp.mean(dxr * xr, axis=-1, keepdims=True))

    halo = pl.BlockSpec((HALO_C, CONVW), lambda i: (jnp.maximum((NTILE - 1 - i) * hpt - 1, 0), 0))
    return pl.pallas_call(
        body,
        name="bwd_mix",
        grid=(NTILE,),
        out_shape=[jax.ShapeDtypeStruct((SEQ, DM), F32), jax.ShapeDtypeStruct((SEQ, DM), BF16), jax.ShapeDtypeStruct((SEQ, CONVW), BF16),
                   jax.ShapeDtypeStruct((3 * CONVW // CH, SEQ, CH), BF16), jax.ShapeDtypeStruct((8, DM), F32),
                   jax.ShapeDtypeStruct((HALO_C, CONVW), F32), jax.ShapeDtypeStruct((len(WINS) * PGD, PGD), F32)],
        in_specs=[_tile(DM, True), _tile(DM, True), _tile(DM, True), _tile(2 * CONVW, True), _tile(CONVW, True), halo, _tile(CONVW, True),
                  _tile(POOLW, True), _const((8, 6 * DM)), _const((1, DM)), _const(win_t.shape), _const(dww.shape), _const((1, CONVW)),
                  _const((1, CONVW)), _const(wpw.shape), _const(wg.shape), _const((1, POOLW)), _const(wout.shape)],
        out_specs=[_tile(DM, True), _tile(DM, True), _tile(CONVW, True), _tile3(3 * CONVW // CH, True), _const((8, DM)),
                   _const((HALO_C, CONVW)), _const((len(WINS) * PGD, PGD))],
        scratch_shapes=[pltpu.VMEM((TOK + HALO_C, CONVW), F32), pltpu.VMEM((TOK + HALO_P, POOLW), F32), pltpu.VMEM((TOK + HALO_C, CONVW), F32),
                        pltpu.VMEM((TOK, 3 * CONVW), BF16)] + [pltpu.VMEM((7, TOK + HALO_C - 8, CONVW), F32)] * 2,
        compiler_params=pltpu.CompilerParams(dimension_semantics=("arbitrary",), vmem_limit_bytes=VMEM_LIMIT),
    )(dx1, x, y, uag, hc, hc, hd, pbv, mod8, g1, win_t, dww, lng, lnb, wpw, wg, psc, wout)


CHIPS = (0, 4, 2, 6)


def _wgrad_rs_call(a3s, bmap, bs, order, spack, dmodp, cact_t):
    nw = len(a3s)
    nchs = [a.shape[0] for a in a3s]
    rws = [n * CH // NDEV for n in nchs]
    cols = [bs[bmap[k]].shape[1] for k in range(nw)]
    tshapes = sorted({(rws[k], cols[k]) for k in range(nw)})
    srows = spack.shape[0]
    sub = 32
    nb = len(bs)

    def body(*refs):
        pos = 0

        def take(n):
            nonlocal pos
            pos += n
            return refs[pos - n : pos]

        a_refs, b_refs = take(nw), take(nb)
        spack_ref, dmodp_ref, cact_t_ref = take(3)
        o_refs = take(nw)
        ssum_ref, gbada_ref, gwada_ref = take(3)
        p_refs = take(nw)
        r1, cb, r2 = take(nw), take(nw), take(nw)
        tmps = dict(zip(tshapes, take(len(tshapes))))
        abuf, bbuf, bbuf_n, obuf, obuf_n, srecv, mrecv = take(7)
        asem, osem, bsem, tsem, d_s, d_r, i_s, i_r, ssem_s, ssem_r, msem_s, msem_r = take(12)
        x, y, c_, me = _place()
        sib, _ = _flip(x, y, c_, 1)

        def rcopy(src, dst, ss, rs, dev):
            return pltpu.make_async_remote_copy(src_ref=src, dst_ref=dst, send_sem=ss, recv_sem=rs, device_id=dev, device_id_type=MESH)

        srecv[me] = spack_ref[...]
        mrecv[me] = dmodp_ref[...]
        sends = []
        for r in range(1, NDEV):
            dev, _ = _flip(x, y, c_, r)
            sends.append(rcopy(srecv.at[me], srecv.at[me], ssem_s.at[r], ssem_r.at[r], dev))
            sends.append(rcopy(mrecv.at[me], mrecv.at[me], msem_s.at[r], msem_r.at[r], dev))
        for cp in sends:
            cp.start()

        loaded = None
        for k in order:
            wide = cols[k] == DM
            bb, ob = (bbuf, obuf) if wide else (bbuf_n, obuf_n)
            a_ref, p_ref, rw, tmp = a_refs[k], p_refs[k], rws[k], tmps[(rws[k], cols[k])]
            if bmap[k] != loaded:
                cp = pltpu.make_async_copy(b_refs[bmap[k]], bb, bsem)
                cp.start()
                cp.wait()
                loaded = bmap[k]

            def a_copy(m, slot, a_ref=a_ref):
                return pltpu.make_async_copy(a_ref.at[m], abuf.at[slot], asem.at[slot])

            def o_copy(m, slot, ob=ob, p_ref=p_ref):
                return pltpu.make_async_copy(ob.at[slot], p_ref.at[pl.ds(pl.multiple_of(m * CH, CH), CH), :], osem.at[slot])

            a_copy(0, 0).start()

            def step(m, carry, k=k, a_copy=a_copy, o_copy=o_copy, bb=bb, ob=ob):
                slot = lax.rem(m, 2)
                a_copy(m, slot).wait()

                @pl.when(m + 1 < nchs[k])
                def _():
                    a_copy(m + 1, 1 - slot).start()

                @pl.when(m >= 2)
                def _():
                    o_copy(m - 2, slot).wait()

                ob[slot] = _dot_tn(abuf[slot], bb[...]).astype(BF16)
                o_copy(m, slot).start()
                return carry

            lax.fori_loop(0, nchs[k], step, 0)
            for m in (nchs[k] - 2, nchs[k] - 1):
                o_copy(m, m % 2).wait()

            for q, r in enumerate(CHIPS):
                _, owner = _flip(x, y, c_, r | 1)
                cp = rcopy(_rows(p_ref, owner, rw), r1[k].at[q], d_s.at[k, q], d_r.at[k, q], sib)
                cp.start()
                sends.append(cp)
            for q, r in enumerate(CHIPS):
                dev, owner = _flip(x, y, c_, r)
                rcopy(_rows(p_ref, owner, rw), r1[k].at[q], d_s.at[k, q], d_r.at[k, q], (x, y, c_)).wait_recv()
                cp = pltpu.make_async_copy(_rows(p_ref, owner, rw), tmp, tsem)
                cp.start()
                cp.wait()

                def add_sib(j, carry, k=k, q=q, tmp=tmp):
                    rr = pl.ds(pl.multiple_of(j * sub, sub), sub)
                    t = tmp[rr, :].astype(F32) + r1[k][q, rr, :].astype(F32)
                    if q == 0:
                        o_refs[k][rr, :] = t
                    else:
                        cb[k][q - 1, rr, :] = t.astype(BF16)
                    return carry

                lax.fori_loop(0, rw // sub, add_sib, 0)
                if q:
                    cp = rcopy(cb[k].at[q - 1], r2[k].at[q - 1], i_s.at[k, q - 1], i_r.at[k, q - 1], dev)
                    cp.start()
                    sends.append(cp)

        for r in range(1, NDEV):
            _, pb = _flip(x, y, c_, r)
            rcopy(srecv.at[me], srecv.at[pb], ssem_s.at[r], ssem_r.at[r], (x, y, c_)).wait_recv()
            rcopy(mrecv.at[me], mrecv.at[pb], msem_s.at[r], msem_r.at[r], (x, y, c_)).wait_recv()
        tot = srecv[0]
        for s in range(1, NDEV):
            tot = tot + srecv[s]
        ssum_ref[...] = tot
        btot = mrecv[0]
        for s in range(1, NDEV):
            btot = btot + mrecv[s]
        gbada_ref[...] = btot
        rowm = lax.broadcasted_iota(jnp.int32, (8, MODW), 0)
        gw = jnp.zeros((DM, MODW), F32)
        for s in range(NDEV):
            drow = jnp.sum(jnp.where(rowm == me, mrecv[s], 0.0), axis=0, keepdims=True)
            gw = gw + cact_t_ref[:, s : s + 1] * drow
        gwada_ref[...] = gw

        for k in order:
            for q in range(3):
                rcopy(cb[k].at[q], r2[k].at[q], i_s.at[k, q], i_r.at[k, q], (x, y, c_)).wait_recv()

            def add_far(j, carry, k=k):
                rr = pl.ds(pl.multiple_of(j * sub, sub), sub)
                t = o_refs[k][rr, :]
                for q in range(3):
                    t = t + r2[k][q, rr, :].astype(F32)
                o_refs[k][rr, :] = t
                return carry

            lax.fori_loop(0, rws[k] // sub, add_far, 0)
        for cp in sends:
            cp.wait_send()

    out_shape = [jax.ShapeDtypeStruct((rws[k], cols[k]), F32) for k in range(nw)]
    out_shape += [jax.ShapeDtypeStruct((srows, DM), F32), jax.ShapeDtypeStruct((8, MODW), F32), jax.ShapeDtypeStruct((DM, MODW), F32)]
    out_shape += [jax.ShapeDtypeStruct((nchs[k] * CH, cols[k]), BF16) for k in range(nw)]
    scratch = [pltpu.VMEM((4, rws[k], cols[k]), BF16) for k in range(nw)]
    scratch += [pltpu.VMEM((3, rws[k], cols[k]), BF16) for k in range(nw)] * 2
    scratch += [pltpu.VMEM(s, BF16) for s in tshapes]
    scratch += [pltpu.VMEM((2, SEQ, CH), BF16), pltpu.VMEM((SEQ, DM), BF16), pltpu.VMEM((SEQ, CONVW), BF16),
                pltpu.VMEM((2, CH, DM), BF16), pltpu.VMEM((2, CH, CONVW), BF16),
                pltpu.VMEM((NDEV, srows, DM), F32), pltpu.VMEM((NDEV, 8, MODW), F32)]
    scratch += [pltpu.SemaphoreType.DMA((2,))] * 2 + [pltpu.SemaphoreType.DMA] * 2
    scratch += [pltpu.SemaphoreType.DMA((nw, 4))] * 2 + [pltpu.SemaphoreType.DMA((nw, 3))] * 2 + [pltpu.SemaphoreType.DMA((NDEV,))] * 4
    outs = pl.pallas_call(
        body,
        name="wgrad_rs",
        out_shape=out_shape,
        in_specs=[_HBM] * (nw + nb) + [_VM] * 3,
        out_specs=[_VM] * (nw + 3) + [_HBM] * nw,
        scratch_shapes=scratch,
        compiler_params=pltpu.CompilerParams(vmem_limit_bytes=60 * 1024 * 1024),
    )(*a3s, *bs, spack, dmodp, cact_t)
    return outs[: nw + 3]


def _adam_call(name, ws, gs, ms, vs):
    n = len(ws)
    bc1 = 1.0 - ADAM_B1 ** ADAM_STEP
    bc2 = 1.0 - ADAM_B2 ** ADAM_STEP

    def body(*refs):
        for i in range(n):
            w, g, m, v = (refs[j * n + i][...] for j in range(4))
            m = ADAM_B1 * m + (1.0 - ADAM_B1) * g
            v = ADAM_B2 * v + (1.0 - ADAM_B2) * (g * g)
            m_hat = m / bc1
            v_hat = v / bc2
            refs[4 * n + i][...] = -ADAM_LR * (m_hat / (jnp.sqrt(v_hat) + ADAM_EPS) + ADAM_WD * w)
            refs[5 * n + i][...] = m
            refs[6 * n + i][...] = v

    shapes = [jax.ShapeDtypeStruct(w.shape, F32) for w in ws]
    outs = pl.pallas_call(
        body,
        name=name,
        out_shape=shapes * 3,
        in_specs=[_VM] * (4 * n),
        out_specs=[_VM] * (3 * n),
        compiler_params=pltpu.CompilerParams(vmem_limit_bytes=VMEM_LIMIT),
    )(*ws, *gs, *ms, *vs)
    return outs[:n], outs[n : 2 * n], outs[2 * n :]


def kernel(x, c, w_ada, b_ada, g_norm1, w_in, dw_w, dw_b, conv_ln_g, conv_ln_b, w_conv_pw, w_pool_group, pool_scale, w_out, g_norm2, w_ffn_gate, w_ffn_up, w_ffn_down, g_final, loss_target, m_w_ada, m_b_ada, m_g_norm1, m_w_in, m_dw_w, m_dw_b, m_conv_ln_g, m_conv_ln_b, m_w_conv_pw, m_w_pool_group, m_pool_scale, m_w_out, m_g_norm2, m_w_ffn_gate, m_w_ffn_up, m_w_ffn_down, m_g_final, v_w_ada, v_b_ada, v_g_norm1, v_w_in, v_dw_w, v_dw_b, v_conv_ln_g, v_conv_ln_b, v_w_conv_pw, v_w_pool_group, v_pool_scale, v_w_out, v_g_norm2, v_w_ffn_gate, v_w_ffn_up, v_w_ffn_down, v_g_final):
    me = 4 * lax.axis_index("x") + 2 * lax.axis_index("y") + lax.axis_index("c")
    xs, tgt = x[0], loss_target[0]
    b_my = lax.dynamic_slice(b_ada, (0, me * MODW), (1, MODW))
    win_t, wout, wpw, mod8, cact, dww8 = _gather_call(c, w_ada[0], b_my, dw_w[0], [w_in[0].T, w_out[0], w_conv_pw[0]])
    dww = jnp.pad(jnp.transpose(dww8, (1, 0, 2)).reshape(KCONV, CONVW), ((0, HALO_C - KCONV), (0, 0)))
    wgp = w_pool_group[0]

    h1b, uag, hc, hd, hsb3, pbv, ycb3, y, x1, wg_t, wu_t, wd = _fwd_mix_call(
        xs, mod8, g_norm1, win_t, dww, dw_b, conv_ln_g, conv_ln_b, wpw, wgp, pool_scale, wout,
        [w_ffn_gate[0].T, w_ffn_up[0].T, w_ffn_down[0]])
    h2b, a3, dfb, dg3, du3, dx1, facc = _ffn_call(x1, tgt, mod8, g_norm2, g_final.reshape(1, DM), wg_t, wu_t, wd)
    gx, dyb, dycb, dub3, macc, ddw, dwg = _bwd_mix_call(
        dx1, xs, y, uag, hc, hd, pbv, mod8, g_norm1, win_t, dww, conv_ln_g, conv_ln_b, wpw, wgp, pool_scale, wout)
    spack = jnp.concatenate(
        [macc[3:4], facc[3:4], facc[4:5], macc[4:6], facc[5:6], jnp.zeros((2, DM), F32), ddw.reshape(HALO_C // 2, DM), dwg.reshape(-1, DM)], axis=0)
    dmodp = jnp.concatenate([macc[0:3], facc[0:3]], axis=0).reshape(8, MODW)
    g_in_t, g_gate_t, g_up_t, g_down, g_out, g_pw, ssum, gbada, g_wada = _wgrad_rs_call(
        [dub3, dg3, du3, a3, ycb3, hsb3], [0, 1, 1, 2, 3, 4], [h1b, h2b, dfb, dyb, dycb], (4, 3, 1, 2, 0, 5), spack, dmodp, cact.T)

    loss = ssum[5, 0]
    ddw_all = ssum[8 : 8 + HALO_C // 2].reshape(HALO_C, CONVW)[:KCONV]
    grads = {
        "w_ada": g_wada,
        "b_ada": gbada.reshape(1, 6 * DM),
        "g_norm1": ssum[0:1],
        "w_in": g_in_t,
        "dw_w": lax.dynamic_slice(ddw_all, (0, me * (CONVW // NDEV)), (KCONV, CONVW // NDEV)),
        "dw_b": ssum[3:4, 0:CONVW],
        "conv_ln_g": ssum[3:4, CONVW:DM],
        "conv_ln_b": ssum[4:5, 0:CONVW],
        "w_conv_pw": g_pw,
        "w_pool_group": ssum[8 + HALO_C // 2 :].reshape(len(WINS) * PGD, PGD),
        "pool_scale": ssum[4:5, CONVW:DM],
        "w_out": g_out,
        "g_norm2": ssum[1:2],
        "w_ffn_gate": g_gate_t,
        "w_ffn_up": g_up_t,
        "w_ffn_down": g_down,
        "g_final": ssum[2:3],
    }
    given = dict(w_ada=(w_ada, m_w_ada, v_w_ada), b_ada=(b_ada, m_b_ada, v_b_ada), g_norm1=(g_norm1, m_g_norm1, v_g_norm1),
                 w_in=(w_in, m_w_in, v_w_in), dw_w=(dw_w, m_dw_w, v_dw_w), dw_b=(dw_b, m_dw_b, v_dw_b),
                 conv_ln_g=(conv_ln_g, m_conv_ln_g, v_conv_ln_g), conv_ln_b=(conv_ln_b, m_conv_ln_b, v_conv_ln_b),
                 w_conv_pw=(w_conv_pw, m_w_conv_pw, v_w_conv_pw), w_pool_group=(w_pool_group, m_w_pool_group, v_w_pool_group),
                 pool_scale=(pool_scale, m_pool_scale, v_pool_scale), w_out=(w_out, m_w_out, v_w_out), g_norm2=(g_norm2, m_g_norm2, v_g_norm2),
                 w_ffn_gate=(w_ffn_gate, m_w_ffn_gate, v_w_ffn_gate), w_ffn_up=(w_ffn_up, m_w_ffn_up, v_w_ffn_up),
                 w_ffn_down=(w_ffn_down, m_w_ffn_down, v_w_ffn_down), g_final=(g_final, m_g_final, v_g_final))
    names = list(given)
    groups = [["w_ada"], ["w_ffn_gate", "w_ffn_up"], ["w_ffn_down", "w_in", "w_out", "w_conv_pw"],
              ["b_ada", "g_norm1", "dw_w", "dw_b", "conv_ln_g", "conv_ln_b", "w_pool_group", "pool_scale", "g_norm2", "g_final"]]
    turned = ("w_in", "w_ffn_gate", "w_ffn_up")

    def work(n, a):
        return a[0].T if n in turned else a.reshape(grads[n].shape)

    def full(n, a):
        return a.T[None] if n in turned else a.reshape(given[n][0].shape)

    delta, new_m, new_v = {}, {}, {}
    for gi, grp in enumerate(groups):
        ds, ms, vs = _adam_call(f"adam{gi}", [work(n, given[n][0]) for n in grp], [grads[n] for n in grp],
                                [work(n, given[n][1]) for n in grp], [work(n, given[n][2]) for n in grp])
        for n, d_, m_, v_ in zip(grp, ds, ms, vs):
            delta[n], new_m[n], new_v[n] = d_, m_, v_

    return (loss, gx.reshape(x.shape), *[full(n, grads[n]) for n in names], *[full(n, delta[n]) for n in names],
            *[full(n, new_m[n]) for n in names], *[full(n, new_v[n]) for n in names])
```

```python
import functools

import jax
import jax.numpy as jnp
from jax import lax
from jax.experimental import pallas as pl
from jax.experimental.pallas import tpu as pltpu

F32, BF16 = jnp.float32, jnp.bfloat16
SEQ, DM = 2048, 1024
CONVW, POOLW = 512, 512
KCONV = 31
WINS = (2, 4, 8, 16)
PGD = 128
DFF = 2816
NDEV = 8
MODW = 6 * DM // NDEV
EPS = 1e-6
TOK = 256
NTILE = SEQ // TOK
CH = 256
NCH = DFF // CH
HALO_C, HALO_P = 32, 16
MESH = pl.DeviceIdType.MESH
VMEM_LIMIT = 56 * 1024 * 1024
ADAM_LR, ADAM_B1, ADAM_B2, ADAM_EPS, ADAM_WD, ADAM_STEP = 0.001, 0.9, 0.999, 1e-08, 0.01, 10
HI = lax.Precision.HIGHEST

_VM = pl.BlockSpec(memory_space=pltpu.VMEM)
_HBM = pl.BlockSpec(memory_space=pltpu.HBM)


def _place():
    x, y, c = lax.axis_index("x"), lax.axis_index("y"), lax.axis_index("c")
    return x, y, c, 4 * x + 2 * y + c


def _flip(x, y, c, r):
    px = 1 - x if r & 4 else x
    py = 1 - y if r & 2 else y
    pc = 1 - c if r & 1 else c
    return (px, py, pc), 4 * px + 2 * py + pc


def _rows(ref, blk, n):
    return ref.at[pl.ds(pl.multiple_of(blk * n, 16), n), :]


def _sig(z):
    return jax.nn.sigmoid(z)


def _dot_nt(a, b):
    return lax.dot_general(a, b, (((1,), (1,)), ((), ())), preferred_element_type=F32)


def _dot_nn(a, b):
    return lax.dot_general(a, b, (((1,), (0,)), ((), ())), preferred_element_type=F32)


def _dot_tn(a, b):
    return lax.dot_general(a, b, (((0,), (0,)), ((), ())), preferred_element_type=F32)


def _rcopy(src, dst, ss, rs, dev):
    return pltpu.make_async_remote_copy(src_ref=src, dst_ref=dst, send_sem=ss, recv_sem=rs, device_id=dev, device_id_type=MESH)


def _ag_sems(nw):
    return ([pltpu.SemaphoreType.DMA((nw, 3))] * 2 + [pltpu.SemaphoreType.DMA((nw, 4))] * 2 + [pltpu.SemaphoreType.DMA((nw, 2))] * 2
            + [pltpu.SemaphoreType.DMA((nw,))])


def _ag_plan(sbufs, g_refs, rws, sems):
    wsem_s, wsem_r, fsem_s, fsem_r, hsem_s, hsem_r, lsem = sems
    x, y, c_, me = _place()
    here = (x, y, c_)
    plans = []
    for k, (sb, g, n) in enumerate(zip(sbufs, g_refs, rws)):

        def blk(r, half=None, g=g, n=n):
            b = _flip(x, y, c_, r)[1]
            if half is None:
                return _rows(g, b, n)
            return g.at[pl.ds(pl.multiple_of(b * n + half * (n // 2), 16), n // 2), :]

        def same(ref, ss, rs, j, dev, k=k):
            return _rcopy(ref, ref, ss.at[k, j], rs.at[k, j], dev)

        sib, xn, yn = (_flip(x, y, c_, r)[0] for r in (1, 4, 2))
        plans.append(dict(
            local=pltpu.make_async_copy(sb, blk(0), lsem.at[k]),
            first=[_rcopy(sb, blk(0), wsem_s.at[k, j], wsem_r.at[k, j], dev) for j, dev in enumerate((sib, xn, yn))],
            got=[_rcopy(sb, blk(r), wsem_s.at[k, j], wsem_r.at[k, j], here) for j, r in enumerate((1, 4, 2))],
            passes=[same(blk(4), fsem_s, fsem_r, 0, sib), same(blk(2), fsem_s, fsem_r, 1, sib),
                    same(blk(6, 0), fsem_s, fsem_r, 2, sib), same(blk(6, 1), fsem_s, fsem_r, 3, sib)],
            passed=[same(blk(5), fsem_s, fsem_r, 0, here), same(blk(3), fsem_s, fsem_r, 1, here),
                    same(blk(7, 0), fsem_s, fsem_r, 2, here), same(blk(7, 1), fsem_s, fsem_r, 3, here)],
            halves=[same(blk(4, 0), hsem_s, hsem_r, 0, yn), same(blk(2, 1), hsem_s, hsem_r, 1, xn)],
            halved=[same(blk(6, 0), hsem_s, hsem_r, 0, here), same(blk(6, 1), hsem_s, hsem_r, 1, here)],
        ))
    return plans


def _ag_start(sbufs, g_refs, rws, sems):
    for p in _ag_plan(sbufs, g_refs, rws, sems):
        p["local"].start()
        for cp in p["first"]:
            cp.start()


def _ag_pass_on(sbufs, g_refs, rws, sems):
    plans = _ag_plan(sbufs, g_refs, rws, sems)
    for p in plans:
        for j in (0, 1):
            p["got"][j + 1].wait_recv()
            p["halves"][j].start()
            p["passes"][j].start()
    for p in plans:
        for j in (0, 1):
            p["halved"][j].wait_recv()
            p["passes"][j + 2].start()


def _ag_finish(sbufs, g_refs, rws, sems):
    plans = _ag_plan(sbufs, g_refs, rws, sems)
    for p in plans:
        p["got"][0].wait_recv()
        for cp in p["passed"]:
            cp.wait_recv()
    for p in plans:
        for cp in p["first"] + p["passes"] + p["halves"]:
            cp.wait_send()
        p["local"].wait()


def _gather_call(c, w_ada, b_my, dww, shards):
    nw = len(shards)
    rws = [s.shape[0] for s in shards]

    def body(*refs):
        c_ref, wada_ref, bmy_ref, dww_ref = refs[:4]
        s_refs = refs[4 : 4 + nw]
        g_refs = refs[4 + nw : 4 + 2 * nw]
        mod8_ref, cact_ref, dww8_ref = refs[4 + 2 * nw : 7 + 2 * nw]
        crecv, msend, mrecv = refs[7 + 2 * nw : 10 + 2 * nw]
        sbufs = refs[10 + 2 * nw : 10 + 3 * nw]
        csem_s, csem_r, dsem_s, dsem_r, msem_s, msem_r = refs[10 + 3 * nw : 16 + 3 * nw]
        ag_sems = refs[16 + 3 * nw :]
        x, y, c_, me = _place()

        def rcopy(src, dst, ss, rs, dev):
            return pltpu.make_async_remote_copy(src_ref=src, dst_ref=dst, send_sem=ss, recv_sem=rs, device_id=dev, device_id_type=MESH)

        crecv[me] = jnp.broadcast_to(c_ref[...], (8, DM))
        dww8_ref[me] = dww_ref[...]
        small = []
        for r in range(1, NDEV):
            dev, _ = _flip(x, y, c_, r)
            small.append(rcopy(crecv.at[me], crecv.at[me], csem_s.at[r], csem_r.at[r], dev))
            small.append(rcopy(dww8_ref.at[me], dww8_ref.at[me], dsem_s.at[r], dsem_r.at[r], dev))
        for cp in small:
            cp.start()

        for k in range(nw):
            sbufs[k][...] = s_refs[k][...].astype(BF16)
        _ag_start(sbufs, g_refs, rws, ag_sems)

        rowid = lax.broadcasted_iota(jnp.int32, (8, DM), 0)
        for r in range(1, NDEV):
            _, pb = _flip(x, y, c_, r)
            rcopy(crecv.at[me], crecv.at[pb], csem_s.at[r], csem_r.at[r], (x, y, c_)).wait_recv()
        call = jnp.zeros((8, DM), F32)
        for s in range(NDEV):
            call = jnp.where(rowid == s, crecv[s], call)
        cact = call * _sig(call)
        cact_ref[...] = cact
        modp = jnp.dot(cact, wada_ref[...], precision=HI, preferred_element_type=F32) + bmy_ref[...]
        rowm = lax.broadcasted_iota(jnp.int32, (8, MODW), 0)
        for b in range(NDEV):
            row = jnp.sum(jnp.where(rowm == b, modp, 0.0), axis=0, keepdims=True)
            msend[b] = jnp.broadcast_to(row, (8, MODW))
        mrecv[me] = msend[me]
        msends = []
        for r in range(1, NDEV):
            dev, pb = _flip(x, y, c_, r)
            cp = rcopy(msend.at[pb], mrecv.at[me], msem_s.at[r], msem_r.at[r], dev)
            cp.start()
            msends.append(cp)

        _ag_pass_on(sbufs, g_refs, rws, ag_sems)

        for r in range(1, NDEV):
            _, pb = _flip(x, y, c_, r)
            rcopy(msend.at[pb], mrecv.at[pb], msem_s.at[r], msem_r.at[r], (x, y, c_)).wait_recv()
        for s in range(NDEV):
            mod8_ref[:, s * MODW : (s + 1) * MODW] = mrecv[s]

        _ag_finish(sbufs, g_refs, rws, ag_sems)
        for r in range(1, NDEV):
            _, pb = _flip(x, y, c_, r)
            rcopy(dww8_ref.at[me], dww8_ref.at[pb], dsem_s.at[r], dsem_r.at[r], (x, y, c_)).wait_recv()
        for cp in small + msends:
            cp.wait_send()

    out_shape = [jax.ShapeDtypeStruct((NDEV * s.shape[0], s.shape[1]), BF16) for s in shards]
    out_shape += [
        jax.ShapeDtypeStruct((8, 6 * DM), F32),
        jax.ShapeDtypeStruct((8, DM), F32),
        jax.ShapeDtypeStruct((NDEV,) + dww.shape, F32),
    ]
    scratch = [pltpu.VMEM((NDEV, 8, DM), F32), pltpu.VMEM((NDEV, 8, MODW), F32), pltpu.VMEM((NDEV, 8, MODW), F32)]
    scratch += [pltpu.VMEM(s.shape, BF16) for s in shards]
    scratch += [pltpu.SemaphoreType.DMA((NDEV,))] * 6 + _ag_sems(nw)
    return pl.pallas_call(
        body,
        name="gather",
        out_shape=out_shape,
        in_specs=[_VM] * (4 + nw),
        out_specs=[_HBM] * nw + [_VM] * 3,
        scratch_shapes=scratch,
        compiler_params=pltpu.CompilerParams(vmem_limit_bytes=VMEM_LIMIT),
    )(c, w_ada, b_my, dww, *shards)


def _const(shape):
    return pl.BlockSpec(shape, lambda i: (0,) * len(shape))


def _tile(width, rev=False):
    if rev:
        return pl.BlockSpec((TOK, width), lambda i: (NTILE - 1 - i, 0))
    return pl.BlockSpec((TOK, width), lambda i: (i, 0))


def _tile3(nch, rev=False):
    if rev:
        return pl.BlockSpec((nch, TOK, CH), lambda i: (0, NTILE - 1 - i, 0))
    return pl.BlockSpec((nch, TOK, CH), lambda i: (0, i, 0))


def _norm_mod(x, g, sc, sh):
    r = lax.rsqrt(jnp.mean(x * x, axis=-1, keepdims=True) + EPS)
    xr = x * r
    return r, xr, xr * g * (1.0 + sc) + sh


def _fwd_mix_call(x, mod8, g1, win_t, dww, dwb, lng, lnb, wpw, wg, psc, wout, shards):
    ns = len(shards)
    rws = [s.shape[0] for s in shards]

    def body(x_ref, mod_ref, g1_ref, win_ref, dww_ref, dwb_ref, lng_ref, lnb_ref, wpw_ref, wg_ref, psc_ref, wout_ref, *rest):
        s_refs = rest[:ns]
        h1b_ref, uag_ref, hc_ref, hd_ref, hsb3_ref, pb_ref, ycb3_ref, y_ref, x1_ref = rest[ns : ns + 9]
        g_refs = rest[ns + 9 : 2 * ns + 9]
        hc_ext, up_ext, ycb_ref, stage = rest[2 * ns + 9 : 2 * ns + 13]
        sbufs = rest[2 * ns + 13 : 3 * ns + 13]
        ssem = rest[3 * ns + 13]
        ag_sems = rest[3 * ns + 14 :]
        i = pl.program_id(0)

        @pl.when(i == 0)
        def _():
            for k in range(ns):
                cp = pltpu.make_async_copy(s_refs[k], stage, ssem)
                cp.start()
                cp.wait()
                sbufs[k][...] = stage[...].astype(BF16)
            _ag_start(sbufs, g_refs, rws, ag_sems)
            hc_ext[0:HALO_C, :] = jnp.zeros((HALO_C, CONVW), F32)
            up_ext[0:HALO_P, :] = jnp.zeros((HALO_P, POOLW), F32)

        x = x_ref[...]
        sh1, sc1, gt1 = mod_ref[0:1, 0:DM], mod_ref[0:1, DM : 2 * DM], mod_ref[0:1, 2 * DM : 3 * DM]
        _, _, h1 = _norm_mod(x, g1_ref[...], sc1, sh1)
        h1b = h1.astype(BF16)
        h1b_ref[...] = h1b
        u = _dot_nt(h1b, win_ref[...])
        uag_ref[...] = u[:, : 2 * CONVW]
        hc = u[:, :CONVW] * _sig(u[:, CONVW : 2 * CONVW])
        hc_ref[...] = hc
        hc_ext[HALO_C : HALO_C + TOK, :] = hc
        up_ext[HALO_P : HALO_P + TOK, :] = u[:, 2 * CONVW :]

        acc = jnp.zeros((TOK, CONVW), F32)
        for k in range(KCONV):
            acc = acc + dww_ref[k : k + 1, :] * hc_ext[pl.ds(HALO_C - (KCONV - 1) + k, TOK), :]
        hd = acc + dwb_ref[...]
        hd_ref[...] = hd
        hc_ext[0:HALO_C, :] = hc_ext[TOK : TOK + HALO_C, :]
        mu = jnp.mean(hd, axis=-1, keepdims=True)
        dlt = hd - mu
        rstd = lax.rsqrt(jnp.mean(dlt * dlt, axis=-1, keepdims=True) + EPS)
        hl = dlt * rstd * lng_ref[...] + lnb_ref[...]
        hsb = (hl * _sig(hl)).astype(BF16)
        for j in range(CONVW // CH):
            hsb3_ref[j] = hsb[:, j * CH : (j + 1) * CH]
        ycb_ref[:, 0:CONVW] = _dot_nn(hsb, wpw_ref[...]).astype(BF16)

        tg = i * TOK + lax.broadcasted_iota(jnp.int32, (TOK, 1), 0)
        for g, w in enumerate(WINS):
            ln = slice(PGD * g, PGD * (g + 1))
            v = up_ext[pl.ds(HALO_P, TOK), ln]
            ssum = v
            for d in range(1, w):
                ssum = ssum + up_ext[pl.ds(HALO_P - d, TOK), ln]
            cnt = jnp.minimum(tg + 1, w).astype(F32)
            pb = (ssum / cnt - v).astype(BF16)
            pb_ref[:, ln] = pb
            z = _dot_nn(pb, wg_ref[g].astype(BF16))
            ycb_ref[:, CONVW + PGD * g : CONVW + PGD * (g + 1)] = (z * psc_ref[:, ln]).astype(BF16)
        up_ext[0:HALO_P, :] = up_ext[TOK : TOK + HALO_P, :]

        for j in range(DM // CH):
            ycb3_ref[j] = ycb_ref[:, j * CH : (j + 1) * CH]
        yv = _dot_nn(ycb_ref[...], wout_ref[...])
        y_ref[...] = yv
        x1_ref[...] = x + gt1 * yv

        @pl.when(i == NTILE - 1)
        def _():
            _ag_pass_on(sbufs, g_refs, rws, ag_sems)
            _ag_finish(sbufs, g_refs, rws, ag_sems)

    outs = [(DM, BF16), (2 * CONVW, F32), (CONVW, F32), (CONVW, F32), (-CONVW, BF16), (POOLW, BF16), (-DM, BF16), (DM, F32), (DM, F32)]
    return pl.pallas_call(
        body,
        name="fwd_mix",
        grid=(NTILE,),
        out_shape=[jax.ShapeDtypeStruct((SEQ, w) if w > 0 else (-w // CH, SEQ, CH), d) for w, d in outs]
        + [jax.ShapeDtypeStruct((NDEV * s.shape[0], s.shape[1]), BF16) for s in shards],
        in_specs=[_tile(DM), _const((8, 6 * DM)), _const((1, DM)), _const(win_t.shape), _const(dww.shape), _const((1, CONVW)),
                  _const((1, CONVW)), _const((1, CONVW)), _const(wpw.shape), _const(wg.shape), _const((1, POOLW)), _const(wout.shape)]
        + [_HBM] * ns,
        out_specs=[_tile(w) if w > 0 else _tile3(-w // CH) for w, _ in outs] + [_HBM] * ns,
        scratch_shapes=[pltpu.VMEM((TOK + HALO_C, CONVW), F32), pltpu.VMEM((TOK + HALO_P, POOLW), F32), pltpu.VMEM((TOK, DM), BF16),
                        pltpu.VMEM(shards[0].shape, F32)] + [pltpu.VMEM(s.shape, BF16) for s in shards]
        + [pltpu.SemaphoreType.DMA] + _ag_sems(ns),
        compiler_params=pltpu.CompilerParams(dimension_semantics=("arbitrary",), vmem_limit_bytes=VMEM_LIMIT),
    )(x, mod8, g1, win_t, dww, dwb, lng, lnb, wpw, wg, psc, wout, *shards)


def _ffn_call(x1, tgt, mod8, g2, gf, wg_t, wu_t, wd):
    def body(x1_ref, tgt_ref, mod_ref, g2_ref, gf_ref, wg_hbm, wu_hbm, wd_hbm,
             h2b_ref, a3_ref, dfb_ref, dg3_ref, du3_ref, dx1_ref, acc_ref,
             wg_ref, wu_ref, wd_ref, wsem):
        i = pl.program_id(0)

        @pl.when(i == 0)
        def _():
            cps = [pltpu.make_async_copy(s, d, wsem.at[n]) for n, (s, d) in enumerate(((wg_hbm, wg_ref), (wu_hbm, wu_ref), (wd_hbm, wd_ref)))]
            for cp in cps:
                cp.start()
            acc_ref[...] = jnp.zeros((8, DM), F32)
            for cp in cps:
                cp.wait()

        x1 = x1_ref[...]
        sh2, sc2, gt2 = mod_ref[0:1, 3 * DM : 4 * DM], mod_ref[0:1, 4 * DM : 5 * DM], mod_ref[0:1, 5 * DM : 6 * DM]
        g2 = g2_ref[...]
        r2, xr, h2 = _norm_mod(x1, g2, sc2, sh2)
        h2b = h2.astype(BF16)
        h2b_ref[...] = h2b
        gate = _dot_nt(h2b, wg_ref[...])
        up = _dot_nt(h2b, wu_ref[...])
        ab = (gate * _sig(gate) * up).astype(BF16)
        for j in range(NCH):
            a3_ref[j] = ab[:, j * CH : (j + 1) * CH]
        f = _dot_nn(ab, wd_ref[...])
        x2 = x1 + gt2 * f
        rf = lax.rsqrt(jnp.mean(x2 * x2, axis=-1, keepdims=True) + EPS)
        nf = x2 * rf
        gf_ = gf_ref[...]
        err = nf * gf_ - tgt_ref[...]
        loss = 0.5 * jnp.sum(jnp.sum(err * err, axis=-1, keepdims=True), axis=0, keepdims=True) * (1.0 / DM)
        dout = err * (1.0 / DM)
        dnf = dout * gf_
        dx2 = rf * (dnf - nf * jnp.mean(dnf * nf, axis=-1, keepdims=True))
        dfb = (gt2 * dx2).astype(BF16)
        dfb_ref[...] = dfb
        da = _dot_nt(dfb, wd_ref[...])
        sg = _sig(gate)
        dgb = (da * up * (sg * (1.0 + gate * (1.0 - sg)))).astype(BF16)
        dub = (da * (gate * sg)).astype(BF16)
        for j in range(NCH):
            dg3_ref[j] = dgb[:, j * CH : (j + 1) * CH]
            du3_ref[j] = dub[:, j * CH : (j + 1) * CH]
        dh2 = _dot_nn(dgb, wg_ref[...]) + _dot_nn(dub, wu_ref[...])
        dn2 = dh2 * (1.0 + sc2)
        dxr = dn2 * g2
        dx1_ref[...] = dx2 + r2 * (dxr - xr * jnp.mean(dxr * xr, axis=-1, keepdims=True))

        def colsum(v):
            return jnp.sum(v, axis=0, keepdims=True)

        acc_ref[0:1, :] += colsum(dh2)
        acc_ref[1:2, :] += colsum(dh2 * (xr * g2))
        acc_ref[2:3, :] += colsum(dx2 * f)
        acc_ref[3:4, :] += colsum(dn2 * xr)
        acc_ref[4:5, :] += colsum(dout * nf)
        acc_ref[5:6, :] += jnp.broadcast_to(loss, (1, DM))

    c3 = pl.BlockSpec((NCH, TOK, CH), lambda i: (0, i, 0))
    return pl.pallas_call(
        body,
        name="ffn",
        grid=(NTILE,),
        out_shape=[jax.ShapeDtypeStruct((SEQ, DM), BF16), jax.ShapeDtypeStruct((NCH, SEQ, CH), BF16), jax.ShapeDtypeStruct((SEQ, DM), BF16),
                   jax.ShapeDtypeStruct((NCH, SEQ, CH), BF16), jax.ShapeDtypeStruct((NCH, SEQ, CH), BF16),
                   jax.ShapeDtypeStruct((SEQ, DM), F32), jax.ShapeDtypeStruct((8, DM), F32)],
        in_specs=[_tile(DM), _tile(DM), _const((8, 6 * DM)), _const((1, DM)), _const((1, DM)), _HBM, _HBM, _HBM],
        out_specs=[_tile(DM), c3, _tile(DM), c3, c3, _tile(DM), _const((8, DM))],
        scratch_shapes=[pltpu.VMEM((DFF, DM), BF16)] * 3 + [pltpu.SemaphoreType.DMA((3,))],
        compiler_params=pltpu.CompilerParams(dimension_semantics=("arbitrary",), vmem_limit_bytes=VMEM_LIMIT),
    )(x1, tgt, mod8, g2, gf, wg_t, wu_t, wd)


def _bwd_mix_call(dx1, x, y, uag, hc, hd, pbv, mod8, g1, win_t, dww, lng, lnb, wpw, wg, psc, wout):
    hpt = TOK // HALO_C

    def body(dx1_ref, x_ref, y_ref, uag_ref, hc_ref, halo_ref, hd_ref, pb_ref, mod_ref, g1_ref, win_ref, dww_ref, lng_ref, lnb_ref,
             wpw_ref, wg_ref, psc_ref, wout_ref,
             gx_ref, dyb_ref, dycb_ref, dub3_ref, acc_ref, ddw_ref, dwg_ref, d_ext, q_ext, hcx, dub_ref, hrot, drot):
        i = pl.program_id(0)
        it = NTILE - 1 - i

        @pl.when(i == 0)
        def _():
            d_ext[TOK : TOK + HALO_C, :] = jnp.zeros((HALO_C, CONVW), F32)
            q_ext[TOK : TOK + HALO_P, :] = jnp.zeros((HALO_P, POOLW), F32)
            acc_ref[...] = jnp.zeros((8, DM), F32)
            ddw_ref[...] = jnp.zeros((HALO_C, CONVW), F32)
            dwg_ref[...] = jnp.zeros((len(WINS) * PGD, PGD), F32)

        def colsum(v):
            return jnp.sum(v, axis=0, keepdims=True)

        dx1 = dx1_ref[...]
        x = x_ref[...]
        sh1, sc1, gt1 = mod_ref[0:1, 0:DM], mod_ref[0:1, DM : 2 * DM], mod_ref[0:1, 2 * DM : 3 * DM]
        acc_ref[2:3, :] += colsum(dx1 * y_ref[...])
        dyb = (gt1 * dx1).astype(BF16)
        dyb_ref[...] = dyb
        dycat = _dot_nt(dyb, wout_ref[...])

        hd = hd_ref[...]
        mu = jnp.mean(hd, axis=-1, keepdims=True)
        dlt = hd - mu
        rstd = lax.rsqrt(jnp.mean(dlt * dlt, axis=-1, keepdims=True) + EPS)
        xhat = dlt * rstd
        lng = lng_ref[...]
        hl = xhat * lng + lnb_ref[...]
        sgl = _sig(hl)
        dycb = dycat[:, :CONVW].astype(BF16)
        dycb_ref[...] = dycb
        dhl = _dot_nt(dycb, wpw_ref[...]) * (sgl * (1.0 + hl * (1.0 - sgl)))
        acc_ref[5:6, 0:CONVW] += colsum(dhl)
        acc_ref[4:5, CONVW:DM] += colsum(dhl * xhat)
        dxh = dhl * lng
        dhd = rstd * (dxh - jnp.mean(dxh, axis=-1, keepdims=True) - xhat * jnp.mean(dxh * xhat, axis=-1, keepdims=True))
        acc_ref[4:5, 0:CONVW] += colsum(dhd)

        hcx[0:HALO_C, :] = jnp.where(it == 0, 0.0, halo_ref[...])
        hcx[HALO_C : HALO_C + TOK, :] = hc_ref[...]
        d_ext[0:TOK, :] = dhd
        for b in range(1, 8):
            hrot[b - 1] = hcx[pl.ds(b, TOK + HALO_C - 8), :]
            drot[b - 1] = d_ext[pl.ds(b, TOK + HALO_C - 8), :]

        def tap(base, rot, off):
            a, b = divmod(off, 8)
            return base[pl.ds(8 * a, TOK), :] if b == 0 else rot[b - 1, pl.ds(8 * a, TOK), :]

        dhc = jnp.zeros((TOK, CONVW), F32)
        for k in range(KCONV):
            ddw_ref[k : k + 1, :] += colsum(dhd * tap(hcx, hrot, HALO_C - (KCONV - 1) + k))
            dhc = dhc + dww_ref[k : k + 1, :] * tap(d_ext, drot, KCONV - 1 - k)
        d_ext[TOK : TOK + HALO_C, :] = d_ext[0:HALO_C, :]
        ua, ug = uag_ref[:, 0:CONVW], uag_ref[:, CONVW : 2 * CONVW]
        sgg = _sig(ug)
        dub_ref[:, 0:CONVW] = (dhc * sgg).astype(BF16)
        dub_ref[:, CONVW : 2 * CONVW] = (dhc * ua * sgg * (1.0 - sgg)).astype(BF16)

        tg = it * TOK + lax.broadcasted_iota(jnp.int32, (TOK, 1), 0)
        for g, w in enumerate(WINS):
            ln = slice(PGD * g, PGD * (g + 1))
            wgb = wg_ref[g].astype(BF16)
            pb = pb_ref[:, ln]
            dyp = dycat[:, CONVW + PGD * g : CONVW + PGD * (g + 1)]
            acc_ref[5:6, CONVW + PGD * g : CONVW + PGD * (g + 1)] += colsum(dyp * _dot_nn(pb, wgb))
            dzb = (dyp * psc_ref[:, ln]).astype(BF16)
            dwg_ref[PGD * g : PGD * (g + 1), :] += _dot_tn(pb, dzb)
            dp = _dot_nt(dzb, wgb)
            cnt = jnp.minimum(tg + 1, w).astype(F32)
            q_ext[0:TOK, ln] = dp / cnt
            dv = -dp
            for d in range(w):
                dv = dv + q_ext[pl.ds(d, TOK), ln]
            dub_ref[:, 2 * CONVW + PGD * g : 2 * CONVW + PGD * (g + 1)] = dv.astype(BF16)
        q_ext[TOK : TOK + HALO_P, :] = q_ext[0:HALO_P, :]

        for j in range(3 * CONVW // CH):
            dub3_ref[j] = dub_ref[:, j * CH : (j + 1) * CH]
        dh1 = _dot_nn(dub_ref[...], win_ref[...])
        g1 = g1_ref[...]
        r1 = lax.rsqrt(jnp.mean(x * x, axis=-1, keepdims=True) + EPS)
        xr = x * r1
        acc_ref[0:1, :] += colsum(dh1)
        acc_ref[1:2, :] += colsum(dh1 * (xr * g1))
        dn1 = dh1 * (1.0 + sc1)
        acc_ref[3:4, :] += colsum(dn1 * xr)
        dxr = dn1 * g1
        gx_ref[...] = dx1 + r1 * (dxr - xr * jnp.mean(dxr * xr, axis=-1, keepdims=True))

    halo = pl.BlockSpec((HALO_C, CONVW), lambda i: (jnp.maximum((NTILE - 1 - i) * hpt - 1, 0), 0))
    return pl.pallas_call(
        body,
        name="bwd_mix",
        grid=(NTILE,),
        out_shape=[jax.ShapeDtypeStruct((SEQ, DM), F32), jax.ShapeDtypeStruct((SEQ, DM), BF16), jax.ShapeDtypeStruct((SEQ, CONVW), BF16),
                   jax.ShapeDtypeStruct((3 * CONVW // CH, SEQ, CH), BF16), jax.ShapeDtypeStruct((8, DM), F32),
                   jax.ShapeDtypeStruct((HALO_C, CONVW), F32), jax.ShapeDtypeStruct((len(WINS) * PGD, PGD), F32)],
        in_specs=[_tile(DM, True), _tile(DM, True), _tile(DM, True), _tile(2 * CONVW, True), _tile(CONVW, True), halo, _tile(CONVW, True),
                  _tile(POOLW, True), _const((8, 6 * DM)), _const((1, DM)), _const(win_t.shape), _const(dww.shape), _const((1, CONVW)),
                  _const((1, CONVW)), _const(wpw.shape), _const(wg.shape), _const((1, POOLW)), _const(wout.shape)],
        out_specs=[_tile(DM, True), _tile(DM, True), _tile(CONVW, True), _tile3(3 * CONVW // CH, True), _const((8, DM)),
                   _const((HALO_C, CONVW)), _const((len(WINS) * PGD, PGD))],
        scratch_shapes=[pltpu.VMEM((TOK + HALO_C, CONVW), F32), pltpu.VMEM((TOK + HALO_P, POOLW), F32), pltpu.VMEM((TOK + HALO_C, CONVW), F32),
                        pltpu.VMEM((TOK, 3 * CONVW), BF16)] + [pltpu.VMEM((7, TOK + HALO_C - 8, CONVW), F32)] * 2,
        compiler_params=pltpu.CompilerParams(dimension_semantics=("arbitrary",), vmem_limit_bytes=VMEM_LIMIT),
    )(dx1, x, y, uag, hc, hc, hd, pbv, mod8, g1, win_t, dww, lng, lnb, wpw, wg, psc, wout)


CHIPS = (0, 4, 2, 6)


def _wgrad_rs_call(a3s, bmap, bs, order, spack, dmodp, cact_t):
    nw = len(a3s)
    nchs = [a.shape[0] for a in a3s]
    rws = [n * CH // NDEV for n in nchs]
    cols = [bs[bmap[k]].shape[1] for k in range(nw)]
    tshapes = sorted({(rws[k], cols[k]) for k in range(nw)})
    srows = spack.shape[0]
    sub = 16
    nb = len(bs)

    def body(*refs):
        pos = 0

        def take(n):
            nonlocal pos
            pos += n
            return refs[pos - n : pos]

        a_refs, b_refs = take(nw), take(nb)
        spack_ref, dmodp_ref, cact_t_ref = take(3)
        o_refs = take(nw)
        ssum_ref, gbada_ref, gwada_ref = take(3)
        p_refs = take(nw)
        r1, cb, r2, hb = take(nw), take(nw), take(nw), take(nw)
        tmps = dict(zip(tshapes, take(len(tshapes))))
        abuf, bbuf, bbuf_n, obuf, obuf_n, srecv, mrecv = take(7)
        asem, osem, bsem, tsem, d_s, d_r, i_s, i_r, h_s, h_r, ssem_s, ssem_r, msem_s, msem_r = take(14)
        x, y, c_, me = _place()
        sib, xn, yn = (_flip(x, y, c_, r)[0] for r in (1, 4, 2))

        def rcopy(src, dst, ss, rs, dev):
            return pltpu.make_async_remote_copy(src_ref=src, dst_ref=dst, send_sem=ss, recv_sem=rs, device_id=dev, device_id_type=MESH)

        srecv[me] = spack_ref[...]
        mrecv[me] = dmodp_ref[...]
        sends = []
        for r in range(1, NDEV):
            dev, _ = _flip(x, y, c_, r)
            sends.append(rcopy(srecv.at[me], srecv.at[me], ssem_s.at[r], ssem_r.at[r], dev))
            sends.append(rcopy(mrecv.at[me], mrecv.at[me], msem_s.at[r], msem_r.at[r], dev))
        for cp in sends:
            cp.start()

        def relay(k):
            half = rws[k] // 2
            for h in range(2):
                rcopy(cb[k].at[2, pl.ds(h * half, half), :], hb[k].at[h], h_s.at[k, h], h_r.at[k, h], (x, y, c_)).wait_recv()

                def add_half(j, carry, k=k, h=h):
                    rr = pl.ds(pl.multiple_of(j * sub, sub), sub)
                    dst = pl.ds(pl.multiple_of(h * half + j * sub, sub), sub)
                    cb[k][1 - h, dst, :] = (cb[k][1 - h, dst, :].astype(F32) + hb[k][h, rr, :].astype(F32)).astype(BF16)
                    return carry

                lax.fori_loop(0, half // sub, add_half, 0)
            for q, dev in enumerate((xn, yn)):
                cp = rcopy(cb[k].at[q], r2[k].at[q], i_s.at[k, q], i_r.at[k, q], dev)
                cp.start()
                sends.append(cp)

        loaded, prev = None, None
        for k in order:
            wide = cols[k] == DM
            bb, ob = (bbuf, obuf) if wide else (bbuf_n, obuf_n)
            a_ref, p_ref, rw, tmp = a_refs[k], p_refs[k], rws[k], tmps[(rws[k], cols[k])]
            if bmap[k] != loaded:
                cp = pltpu.make_async_copy(b_refs[bmap[k]], bb, bsem)
                cp.start()
                cp.wait()
                loaded = bmap[k]

            def a_copy(m, slot, a_ref=a_ref):
                return pltpu.make_async_copy(a_ref.at[m], abuf.at[slot], asem.at[slot])

            def o_copy(m, slot, ob=ob, p_ref=p_ref):
                return pltpu.make_async_copy(ob.at[slot], p_ref.at[pl.ds(pl.multiple_of(m * CH, CH), CH), :], osem.at[slot])

            a_copy(0, 0).start()

            def step(m, carry, k=k, a_copy=a_copy, o_copy=o_copy, bb=bb, ob=ob):
                slot = lax.rem(m, 2)
                a_copy(m, slot).wait()

                @pl.when(m + 1 < nchs[k])
                def _():
                    a_copy(m + 1, 1 - slot).start()

                @pl.when(m >= 2)
                def _():
                    o_copy(m - 2, slot).wait()

                ob[slot] = _dot_tn(abuf[slot], bb[...]).astype(BF16)
                o_copy(m, slot).start()
                return carry

            lax.fori_loop(0, nchs[k], step, 0)
            for m in (nchs[k] - 2, nchs[k] - 1):
                o_copy(m, m % 2).wait()

            for q, r in enumerate(CHIPS):
                _, owner = _flip(x, y, c_, r | 1)
                cp = rcopy(_rows(p_ref, owner, rw), r1[k].at[q], d_s.at[k, q], d_r.at[k, q], sib)
                cp.start()
                sends.append(cp)
            for q in (3, 1, 2, 0):
                _, owner = _flip(x, y, c_, CHIPS[q])
                rcopy(_rows(p_ref, owner, rw), r1[k].at[q], d_s.at[k, q], d_r.at[k, q], (x, y, c_)).wait_recv()
                cp = pltpu.make_async_copy(_rows(p_ref, owner, rw), tmp, tsem)
                cp.start()
                cp.wait()

                def add_sib(j, carry, k=k, q=q, tmp=tmp):
                    rr = pl.ds(pl.multiple_of(j * sub, sub), sub)
                    t = tmp[rr, :].astype(F32) + r1[k][q, rr, :].astype(F32)
                    if q == 0:
                        o_refs[k][rr, :] = t
                    else:
                        cb[k][q - 1, rr, :] = t.astype(BF16)
                    return carry

                lax.fori_loop(0, rw // sub, add_sib, 0)
                if q == 3:
                    for h, dev in enumerate((xn, yn)):
                        cp = rcopy(cb[k].at[2, pl.ds(h * (rw // 2), rw // 2), :], hb[k].at[h], h_s.at[k, h], h_r.at[k, h], dev)
                        cp.start()
                        sends.append(cp)
            if prev is not None:
                relay(prev)
            prev = k
        relay(prev)

        for r in range(1, NDEV):
            _, pb = _flip(x, y, c_, r)
            rcopy(srecv.at[me], srecv.at[pb], ssem_s.at[r], ssem_r.at[r], (x, y, c_)).wait_recv()
            rcopy(mrecv.at[me], mrecv.at[pb], msem_s.at[r], msem_r.at[r], (x, y, c_)).wait_recv()
        tot = srecv[0]
        for s in range(1, NDEV):
            tot = tot + srecv[s]
        ssum_ref[...] = tot
        btot = mrecv[0]
        for s in range(1, NDEV):
            btot = btot + mrecv[s]
        gbada_ref[...] = btot
        rowm = lax.broadcasted_iota(jnp.int32, (8, MODW), 0)
        gw = jnp.zeros((DM, MODW), F32)
        for s in range(NDEV):
            drow = jnp.sum(jnp.where(rowm == me, mrecv[s], 0.0), axis=0, keepdims=True)
            gw = gw + cact_t_ref[:, s : s + 1] * drow
        gwada_ref[...] = gw

        for k in order:
            for q in range(2):
                rcopy(cb[k].at[q], r2[k].at[q], i_s.at[k, q], i_r.at[k, q], (x, y, c_)).wait_recv()

            def add_far(j, carry, k=k):
                rr = pl.ds(pl.multiple_of(j * sub, sub), sub)
                t = o_refs[k][rr, :]
                for q in range(2):
                    t = t + r2[k][q, rr, :].astype(F32)
                o_refs[k][rr, :] = t
                return carry

            lax.fori_loop(0, rws[k] // sub, add_far, 0)
        for cp in sends:
            cp.wait_send()

    out_shape = [jax.ShapeDtypeStruct((rws[k], cols[k]), F32) for k in range(nw)]
    out_shape += [jax.ShapeDtypeStruct((srows, DM), F32), jax.ShapeDtypeStruct((8, MODW), F32), jax.ShapeDtypeStruct((DM, MODW), F32)]
    out_shape += [jax.ShapeDtypeStruct((nchs[k] * CH, cols[k]), BF16) for k in range(nw)]
    scratch = [pltpu.VMEM((4, rws[k], cols[k]), BF16) for k in range(nw)]
    scratch += [pltpu.VMEM((3, rws[k], cols[k]), BF16) for k in range(nw)]
    scratch += [pltpu.VMEM((2, rws[k], cols[k]), BF16) for k in range(nw)]
    scratch += [pltpu.VMEM((2, rws[k] // 2, cols[k]), BF16) for k in range(nw)]
    scratch += [pltpu.VMEM(s, BF16) for s in tshapes]
    scratch += [pltpu.VMEM((2, SEQ, CH), BF16), pltpu.VMEM((SEQ, DM), BF16), pltpu.VMEM((SEQ, CONVW), BF16),
                pltpu.VMEM((2, CH, DM), BF16), pltpu.VMEM((2, CH, CONVW), BF16),
                pltpu.VMEM((NDEV, srows, DM), F32), pltpu.VMEM((NDEV, 8, MODW), F32)]
    scratch += [pltpu.SemaphoreType.DMA((2,))] * 2 + [pltpu.SemaphoreType.DMA] * 2
    scratch += [pltpu.SemaphoreType.DMA((nw, 4))] * 2 + [pltpu.SemaphoreType.DMA((nw, 2))] * 4 + [pltpu.SemaphoreType.DMA((NDEV,))] * 4
    outs = pl.pallas_call(
        body,
        name="wgrad_rs",
        out_shape=out_shape,
        in_specs=[_HBM] * (nw + nb) + [_VM] * 3,
        out_specs=[_VM] * (nw + 3) + [_HBM] * nw,
        scratch_shapes=scratch,
        compiler_params=pltpu.CompilerParams(vmem_limit_bytes=60 * 1024 * 1024),
    )(*a3s, *bs, spack, dmodp, cact_t)
    return outs[: nw + 3]


def _adam_call(name, ws, gs, ms, vs):
    n = len(ws)
    bc1 = 1.0 - ADAM_B1 ** ADAM_STEP
    bc2 = 1.0 - ADAM_B2 ** ADAM_STEP

    def body(*refs):
        for i in range(n):
            w, g, m, v = (refs[j * n + i][...] for j in range(4))
            m = ADAM_B1 * m + (1.0 - ADAM_B1) * g
            v = ADAM_B2 * v + (1.0 - ADAM_B2) * (g * g)
            m_hat = m / bc1
            v_hat = v / bc2
            refs[4 * n + i][...] = -ADAM_LR * (m_hat / (jnp.sqrt(v_hat) + ADAM_EPS) + ADAM_WD * w)
            refs[5 * n + i][...] = m
            refs[6 * n + i][...] = v

    shapes = [jax.ShapeDtypeStruct(w.shape, F32) for w in ws]
    outs = pl.pallas_call(
        body,
        name=name,
        out_shape=shapes * 3,
        in_specs=[_VM] * (4 * n),
        out_specs=[_VM] * (3 * n),
        compiler_params=pltpu.CompilerParams(vmem_limit_bytes=VMEM_LIMIT),
    )(*ws, *gs, *ms, *vs)
    return outs[:n], outs[n : 2 * n], outs[2 * n :]


def kernel(x, c, w_ada, b_ada, g_norm1, w_in, dw_w, dw_b, conv_ln_g, conv_ln_b, w_conv_pw, w_pool_group, pool_scale, w_out, g_norm2, w_ffn_gate, w_ffn_up, w_ffn_down, g_final, loss_target, m_w_ada, m_b_ada, m_g_norm1, m_w_in, m_dw_w, m_dw_b, m_conv_ln_g, m_conv_ln_b, m_w_conv_pw, m_w_pool_group, m_pool_scale, m_w_out, m_g_norm2, m_w_ffn_gate, m_w_ffn_up, m_w_ffn_down, m_g_final, v_w_ada, v_b_ada, v_g_norm1, v_w_in, v_dw_w, v_dw_b, v_conv_ln_g, v_conv_ln_b, v_w_conv_pw, v_w_pool_group, v_pool_scale, v_w_out, v_g_norm2, v_w_ffn_gate, v_w_ffn_up, v_w_ffn_down, v_g_final):
    me = 4 * lax.axis_index("x") + 2 * lax.axis_index("y") + lax.axis_index("c")
    xs, tgt = x[0], loss_target[0]
    b_my = lax.dynamic_slice(b_ada, (0, me * MODW), (1, MODW))
    win_t, wout, wpw, mod8, cact, dww8 = _gather_call(c, w_ada[0], b_my, dw_w[0], [w_in[0].T, w_out[0], w_conv_pw[0]])
    dww = jnp.pad(jnp.transpose(dww8, (1, 0, 2)).reshape(KCONV, CONVW), ((0, HALO_C - KCONV), (0, 0)))
    wgp = w_pool_group[0]

    h1b, uag, hc, hd, hsb3, pbv, ycb3, y, x1, wg_t, wu_t, wd = _fwd_mix_call(
        xs, mod8, g_norm1, win_t, dww, dw_b, conv_ln_g, conv_ln_b, wpw, wgp, pool_scale, wout,
        [w_ffn_gate[0].T, w_ffn_up[0].T, w_ffn_down[0]])
    h2b, a3, dfb, dg3, du3, dx1, facc = _ffn_call(x1, tgt, mod8, g_norm2, g_final.reshape(1, DM), wg_t, wu_t, wd)
    gx, dyb, dycb, dub3, macc, ddw, dwg = _bwd_mix_call(
        dx1, xs, y, uag, hc, hd, pbv, mod8, g_norm1, win_t, dww, conv_ln_g, conv_ln_b, wpw, wgp, pool_scale, wout)
    spack = jnp.concatenate(
        [macc[3:4], facc[3:4], facc[4:5], macc[4:6], facc[5:6], jnp.zeros((2, DM), F32), ddw.reshape(HALO_C // 2, DM), dwg.reshape(-1, DM)], axis=0)
    dmodp = jnp.concatenate([macc[0:3], facc[0:3]], axis=0).reshape(8, MODW)
    g_in_t, g_gate_t, g_up_t, g_down, g_out, g_pw, ssum, gbada, g_wada = _wgrad_rs_call(
        [dub3, dg3, du3, a3, ycb3, hsb3], [0, 1, 1, 2, 3, 4], [h1b, h2b, dfb, dyb, dycb], (4, 3, 1, 2, 0, 5), spack, dmodp, cact.T)

    loss = ssum[5, 0]
    ddw_all = ssum[8 : 8 + HALO_C // 2].reshape(HALO_C, CONVW)[:KCONV]
    grads = {
        "w_ada": g_wada,
        "b_ada": gbada.reshape(1, 6 * DM),
        "g_norm1": ssum[0:1],
        "w_in": g_in_t,
        "dw_w": lax.dynamic_slice(ddw_all, (0, me * (CONVW // NDEV)), (KCONV, CONVW // NDEV)),
        "dw_b": ssum[3:4, 0:CONVW],
        "conv_ln_g": ssum[3:4, CONVW:DM],
        "conv_ln_b": ssum[4:5, 0:CONVW],
        "w_conv_pw": g_pw,
        "w_pool_group": ssum[8 + HALO_C // 2 :].reshape(len(WINS) * PGD, PGD),
        "pool_scale": ssum[4:5, CONVW:DM],
        "w_out": g_out,
        "g_norm2": ssum[1:2],
        "w_ffn_gate": g_gate_t,
        "w_ffn_up": g_up_t,
        "w_ffn_down": g_down,
        "g_final": ssum[2:3],
    }
    given = dict(w_ada=(w_ada, m_w_ada, v_w_ada), b_ada=(b_ada, m_b_ada, v_b_ada), g_norm1=(g_norm1, m_g_norm1, v_g_norm1),
                 w_in=(w_in, m_w_in, v_w_in), dw_w=(dw_w, m_dw_w, v_dw_w), dw_b=(dw_b, m_dw_b, v_dw_b),
                 conv_ln_g=(conv_ln_g, m_conv_ln_g, v_conv_ln_g), conv_ln_b=(conv_ln_b, m_conv_ln_b, v_conv_ln_b),
                 w_conv_pw=(w_conv_pw, m_w_conv_pw, v_w_conv_pw), w_pool_group=(w_pool_group, m_w_pool_group, v_w_pool_group),
                 pool_scale=(pool_scale, m_pool_scale, v_pool_scale), w_out=(w_out, m_w_out, v_w_out), g_norm2=(g_norm2, m_g_norm2, v_g_norm2),
                 w_ffn_gate=(w_ffn_gate, m_w_ffn_gate, v_w_ffn_gate), w_ffn_up=(w_ffn_up, m_w_ffn_up, v_w_ffn_up),
                 w_ffn_down=(w_ffn_down, m_w_ffn_down, v_w_ffn_down), g_final=(g_final, m_g_final, v_g_final))
    names = list(given)
    groups = [["w_ada"], ["w_ffn_gate", "w_ffn_up"], ["w_ffn_down", "w_in", "w_out", "w_conv_pw"],
              ["b_ada", "g_norm1", "dw_w", "dw_b", "conv_ln_g", "conv_ln_b", "w_pool_group", "pool_scale", "g_norm2", "g_final"]]
    turned = ("w_in", "w_ffn_gate", "w_ffn_up")

    def work(n, a):
        return a[0].T if n in turned else a.reshape(grads[n].shape)

    def full(n, a):
        return a.T[None] if n in turned else a.reshape(given[n][0].shape)

    delta, new_m, new_v = {}, {}, {}
    for gi, grp in enumerate(groups):
        ds, ms, vs = _adam_call(f"adam{gi}", [work(n, given[n][0]) for n in grp], [grads[n] for n in grp],
                                [work(n, given[n][1]) for n in grp], [work(n, given[n][2]) for n in grp])
        for n, d_, m_, v_ in zip(grp, ds, ms, vs):
            delta[n], new_m[n], new_v[n] = d_, m_, v_

    return (loss, gx.reshape(x.shape), *[full(n, grads[n]) for n in names], *[full(n, delta[n]) for n in names],
            *[full(n, new_m[n]) for n in names], *[full(n, new_v[n]) for n in names])
```

```python
import functools

import jax
import jax.numpy as jnp
from jax import lax
from jax.experimental import pallas as pl
from jax.experimental.pallas import tpu as pltpu

F32, BF16 = jnp.float32, jnp.bfloat16
SEQ, DM = 2048, 1024
CONVW, POOLW = 512, 512
KCONV = 31
WINS = (2, 4, 8, 16)
PGD = 128
DFF = 2816
NDEV = 8
MODW = 6 * DM // NDEV
EPS = 1e-6
TOK = 256
NTILE = SEQ // TOK
CH = 256
NCH = DFF // CH
HALO_C, HALO_P = 32, 16
MESH = pl.DeviceIdType.MESH
VMEM_LIMIT = 56 * 1024 * 1024
ADAM_LR, ADAM_B1, ADAM_B2, ADAM_EPS, ADAM_WD, ADAM_STEP = 0.001, 0.9, 0.999, 1e-08, 0.01, 10
HI = lax.Precision.HIGHEST

_VM = pl.BlockSpec(memory_space=pltpu.VMEM)
_HBM = pl.BlockSpec(memory_space=pltpu.HBM)


def _place():
    x, y, c = lax.axis_index("x"), lax.axis_index("y"), lax.axis_index("c")
    return x, y, c, 4 * x + 2 * y + c


def _flip(x, y, c, r):
    px = 1 - x if r & 4 else x
    py = 1 - y if r & 2 else y
    pc = 1 - c if r & 1 else c
    return (px, py, pc), 4 * px + 2 * py + pc


def _rows(ref, blk, n):
    return ref.at[pl.ds(pl.multiple_of(blk * n, 16), n), :]


def _sig(z):
    return jax.nn.sigmoid(z)


def _dot_nt(a, b):
    return lax.dot_general(a, b, (((1,), (1,)), ((), ())), preferred_element_type=F32)


def _dot_nn(a, b):
    return lax.dot_general(a, b, (((1,), (0,)), ((), ())), preferred_element_type=F32)


def _dot_tn(a, b):
    return lax.dot_general(a, b, (((0,), (0,)), ((), ())), preferred_element_type=F32)


def _rcopy(src, dst, ss, rs, dev):
    return pltpu.make_async_remote_copy(src_ref=src, dst_ref=dst, send_sem=ss, recv_sem=rs, device_id=dev, device_id_type=MESH)


def _ag_sems(nw):
    return ([pltpu.SemaphoreType.DMA((nw, 3))] * 2 + [pltpu.SemaphoreType.DMA((nw, 4))] * 2 + [pltpu.SemaphoreType.DMA((nw, 2))] * 2
            + [pltpu.SemaphoreType.DMA((nw,))])


def _ag_plan(sbufs, g_refs, rws, sems):
    wsem_s, wsem_r, fsem_s, fsem_r, hsem_s, hsem_r, lsem = sems
    x, y, c_, me = _place()
    here = (x, y, c_)
    plans = []
    for k, (sb, g, n) in enumerate(zip(sbufs, g_refs, rws)):

        def blk(r, half=None, g=g, n=n):
            b = _flip(x, y, c_, r)[1]
            if half is None:
                return _rows(g, b, n)
            return g.at[pl.ds(pl.multiple_of(b * n + half * (n // 2), 16), n // 2), :]

        def same(ref, ss, rs, j, dev, k=k):
            return _rcopy(ref, ref, ss.at[k, j], rs.at[k, j], dev)

        sib, xn, yn = (_flip(x, y, c_, r)[0] for r in (1, 4, 2))
        plans.append(dict(
            local=pltpu.make_async_copy(sb, blk(0), lsem.at[k]),
            first=[_rcopy(sb, blk(0), wsem_s.at[k, j], wsem_r.at[k, j], dev) for j, dev in enumerate((sib, xn, yn))],
            got=[_rcopy(sb, blk(r), wsem_s.at[k, j], wsem_r.at[k, j], here) for j, r in enumerate((1, 4, 2))],
            passes=[same(blk(4), fsem_s, fsem_r, 0, sib), same(blk(2), fsem_s, fsem_r, 1, sib),
                    same(blk(6, 0), fsem_s, fsem_r, 2, sib), same(blk(6, 1), fsem_s, fsem_r, 3, sib)],
            passed=[same(blk(5), fsem_s, fsem_r, 0, here), same(blk(3), fsem_s, fsem_r, 1, here),
                    same(blk(7, 0), fsem_s, fsem_r, 2, here), same(blk(7, 1), fsem_s, fsem_r, 3, here)],
            halves=[same(blk(4, 0), hsem_s, hsem_r, 0, yn), same(blk(2, 1), hsem_s, hsem_r, 1, xn)],
            halved=[same(blk(6, 0), hsem_s, hsem_r, 0, here), same(blk(6, 1), hsem_s, hsem_r, 1, here)],
        ))
    return plans


def _ag_start(sbufs, g_refs, rws, sems):
    for p in _ag_plan(sbufs, g_refs, rws, sems):
        p["local"].start()
        for cp in p["first"]:
            cp.start()


def _ag_pass_on(sbufs, g_refs, rws, sems):
    plans = _ag_plan(sbufs, g_refs, rws, sems)
    for p in plans:
        for j in (0, 1):
            p["got"][j + 1].wait_recv()
            p["halves"][j].start()
            p["passes"][j].start()
    for p in plans:
        for j in (0, 1):
            p["halved"][j].wait_recv()
            p["passes"][j + 2].start()


def _ag_finish(sbufs, g_refs, rws, sems):
    plans = _ag_plan(sbufs, g_refs, rws, sems)
    for p in plans:
        p["got"][0].wait_recv()
        for cp in p["passed"]:
            cp.wait_recv()
    for p in plans:
        for cp in p["first"] + p["passes"] + p["halves"]:
            cp.wait_send()
        p["local"].wait()


def _gather_call(c, w_ada, b_my, dww, shards):
    nw = len(shards)
    rws = [s.shape[0] for s in shards]

    def body(*refs):
        c_ref, wada_ref, bmy_ref, dww_ref = refs[:4]
        s_refs = refs[4 : 4 + nw]
        g_refs = refs[4 + nw : 4 + 2 * nw]
        mod8_ref, cact_ref, dww8_ref = refs[4 + 2 * nw : 7 + 2 * nw]
        crecv, msend, mrecv = refs[7 + 2 * nw : 10 + 2 * nw]
        sbufs = refs[10 + 2 * nw : 10 + 3 * nw]
        csem_s, csem_r, dsem_s, dsem_r, msem_s, msem_r = refs[10 + 3 * nw : 16 + 3 * nw]
        ag_sems = refs[16 + 3 * nw :]
        x, y, c_, me = _place()

        def rcopy(src, dst, ss, rs, dev):
            return pltpu.make_async_remote_copy(src_ref=src, dst_ref=dst, send_sem=ss, recv_sem=rs, device_id=dev, device_id_type=MESH)

        crecv[me] = jnp.broadcast_to(c_ref[...], (8, DM))
        dww8_ref[me] = dww_ref[...]
        small = []
        for r in range(1, NDEV):
            dev, _ = _flip(x, y, c_, r)
            small.append(rcopy(crecv.at[me], crecv.at[me], csem_s.at[r], csem_r.at[r], dev))
            small.append(rcopy(dww8_ref.at[me], dww8_ref.at[me], dsem_s.at[r], dsem_r.at[r], dev))
        for cp in small:
            cp.start()

        for k in range(nw):
            sbufs[k][...] = s_refs[k][...].astype(BF16)
        _ag_start(sbufs, g_refs, rws, ag_sems)

        rowid = lax.broadcasted_iota(jnp.int32, (8, DM), 0)
        for r in range(1, NDEV):
            _, pb = _flip(x, y, c_, r)
            rcopy(crecv.at[me], crecv.at[pb], csem_s.at[r], csem_r.at[r], (x, y, c_)).wait_recv()
        call = jnp.zeros((8, DM), F32)
        for s in range(NDEV):
            call = jnp.where(rowid == s, crecv[s], call)
        cact = call * _sig(call)
        cact_ref[...] = cact
        modp = jnp.dot(cact, wada_ref[...], precision=HI, preferred_element_type=F32) + bmy_ref[...]
        rowm = lax.broadcasted_iota(jnp.int32, (8, MODW), 0)
        for b in range(NDEV):
            row = jnp.sum(jnp.where(rowm == b, modp, 0.0), axis=0, keepdims=True)
            msend[b] = jnp.broadcast_to(row, (8, MODW))
        mrecv[me] = msend[me]
        msends = []
        for r in range(1, NDEV):
            dev, pb = _flip(x, y, c_, r)
            cp = rcopy(msend.at[pb], mrecv.at[me], msem_s.at[r], msem_r.at[r], dev)
            cp.start()
            msends.append(cp)

        _ag_pass_on(sbufs, g_refs, rws, ag_sems)

        for r in range(1, NDEV):
            _, pb = _flip(x, y, c_, r)
            rcopy(msend.at[pb], mrecv.at[pb], msem_s.at[r], msem_r.at[r], (x, y, c_)).wait_recv()
        for s in range(NDEV):
            mod8_ref[:, s * MODW : (s + 1) * MODW] = mrecv[s]

        _ag_finish(sbufs, g_refs, rws, ag_sems)
        for r in range(1, NDEV):
            _, pb = _flip(x, y, c_, r)
            rcopy(dww8_ref.at[me], dww8_ref.at[pb], dsem_s.at[r], dsem_r.at[r], (x, y, c_)).wait_recv()
        for cp in small + msends:
            cp.wait_send()

    out_shape = [jax.ShapeDtypeStruct((NDEV * s.shape[0], s.shape[1]), BF16) for s in shards]
    out_shape += [
        jax.ShapeDtypeStruct((8, 6 * DM), F32),
        jax.ShapeDtypeStruct((8, DM), F32),
        jax.ShapeDtypeStruct((NDEV,) + dww.shape, F32),
    ]
    scratch = [pltpu.VMEM((NDEV, 8, DM), F32), pltpu.VMEM((NDEV, 8, MODW), F32), pltpu.VMEM((NDEV, 8, MODW), F32)]
    scratch += [pltpu.VMEM(s.shape, BF16) for s in shards]
    scratch += [pltpu.SemaphoreType.DMA((NDEV,))] * 6 + _ag_sems(nw)
    return pl.pallas_call(
        body,
        name="gather",
        out_shape=out_shape,
        in_specs=[_VM] * (4 + nw),
        out_specs=[_HBM] * nw + [_VM] * 3,
        scratch_shapes=scratch,
        compiler_params=pltpu.CompilerParams(vmem_limit_bytes=VMEM_LIMIT),
    )(c, w_ada, b_my, dww, *shards)


def _const(shape):
    return pl.BlockSpec(shape, lambda i: (0,) * len(shape))


def _tile(width, rev=False):
    if rev:
        return pl.BlockSpec((TOK, width), lambda i: (NTILE - 1 - i, 0))
    return pl.BlockSpec((TOK, width), lambda i: (i, 0))


def _tile3(nch, rev=False):
    if rev:
        return pl.BlockSpec((nch, TOK, CH), lambda i: (0, NTILE - 1 - i, 0))
    return pl.BlockSpec((nch, TOK, CH), lambda i: (0, i, 0))


def _norm_mod(x, g, sc, sh):
    r = lax.rsqrt(jnp.mean(x * x, axis=-1, keepdims=True) + EPS)
    xr = x * r
    return r, xr, xr * g * (1.0 + sc) + sh


def _fwd_mix_call(x, mod8, g1, win_t, dww, dwb, lng, lnb, wpw, wg, psc, wout, shards):
    ns = len(shards)
    rws = [s.shape[0] for s in shards]

    def body(x_ref, mod_ref, g1_ref, win_ref, dww_ref, dwb_ref, lng_ref, lnb_ref, wpw_ref, wg_ref, psc_ref, wout_ref, *rest):
        s_refs = rest[:ns]
        h1b_ref, uag_ref, hc_ref, hd_ref, hsb3_ref, pb_ref, ycb3_ref, y_ref, x1_ref = rest[ns : ns + 9]
        g_refs = rest[ns + 9 : 2 * ns + 9]
        hc_ext, up_ext, ycb_ref, stage = rest[2 * ns + 9 : 2 * ns + 13]
        sbufs = rest[2 * ns + 13 : 3 * ns + 13]
        ssem = rest[3 * ns + 13]
        ag_sems = rest[3 * ns + 14 :]
        i = pl.program_id(0)

        @pl.when(i == 0)
        def _():
            for k in range(ns):
                cp = pltpu.make_async_copy(s_refs[k], stage, ssem)
                cp.start()
                cp.wait()
                sbufs[k][...] = stage[...].astype(BF16)
            _ag_start(sbufs, g_refs, rws, ag_sems)
            hc_ext[0:HALO_C, :] = jnp.zeros((HALO_C, CONVW), F32)
            up_ext[0:HALO_P, :] = jnp.zeros((HALO_P, POOLW), F32)

        x = x_ref[...]
        sh1, sc1, gt1 = mod_ref[0:1, 0:DM], mod_ref[0:1, DM : 2 * DM], mod_ref[0:1, 2 * DM : 3 * DM]
        _, _, h1 = _norm_mod(x, g1_ref[...], sc1, sh1)
        h1b = h1.astype(BF16)
        h1b_ref[...] = h1b
        u = _dot_nt(h1b, win_ref[...])
        uag_ref[...] = u[:, : 2 * CONVW]
        hc = u[:, :CONVW] * _sig(u[:, CONVW : 2 * CONVW])
        hc_ref[...] = hc
        hc_ext[HALO_C : HALO_C + TOK, :] = hc
        up_ext[HALO_P : HALO_P + TOK, :] = u[:, 2 * CONVW :]

        acc = jnp.zeros((TOK, CONVW), F32)
        for k in range(KCONV):
            acc = acc + dww_ref[k : k + 1, :] * hc_ext[pl.ds(HALO_C - (KCONV - 1) + k, TOK), :]
        hd = acc + dwb_ref[...]
        hd_ref[...] = hd
        hc_ext[0:HALO_C, :] = hc_ext[TOK : TOK + HALO_C, :]
        mu = jnp.mean(hd, axis=-1, keepdims=True)
        dlt = hd - mu
        rstd = lax.rsqrt(jnp.mean(dlt * dlt, axis=-1, keepdims=True) + EPS)
        hl = dlt * rstd * lng_ref[...] + lnb_ref[...]
        hsb = (hl * _sig(hl)).astype(BF16)
        for j in range(CONVW // CH):
            hsb3_ref[j] = hsb[:, j * CH : (j + 1) * CH]
        ycb_ref[:, 0:CONVW] = _dot_nn(hsb, wpw_ref[...]).astype(BF16)

        tg = i * TOK + lax.broadcasted_iota(jnp.int32, (TOK, 1), 0)
        for g, w in enumerate(WINS):
            ln = slice(PGD * g, PGD * (g + 1))
            v = up_ext[pl.ds(HALO_P, TOK), ln]
            ssum = v
            for d in range(1, w):
                ssum = ssum + up_ext[pl.ds(HALO_P - d, TOK), ln]
            cnt = jnp.minimum(tg + 1, w).astype(F32)
            pb = (ssum / cnt - v).astype(BF16)
            pb_ref[:, ln] = pb
            z = _dot_nn(pb, wg_ref[g].astype(BF16))
            ycb_ref[:, CONVW + PGD * g : CONVW + PGD * (g + 1)] = (z * psc_ref[:, ln]).astype(BF16)
        up_ext[0:HALO_P, :] = up_ext[TOK : TOK + HALO_P, :]

        for j in range(DM // CH):
            ycb3_ref[j] = ycb_ref[:, j * CH : (j + 1) * CH]
        yv = _dot_nn(ycb_ref[...], wout_ref[...])
        y_ref[...] = yv
        x1_ref[...] = x + gt1 * yv

        @pl.when(i == NTILE - 1)
        def _():
            _ag_pass_on(sbufs, g_refs, rws, ag_sems)
            _ag_finish(sbufs, g_refs, rws, ag_sems)

    outs = [(DM, BF16), (2 * CONVW, F32), (CONVW, F32), (CONVW, F32), (-CONVW, BF16), (POOLW, BF16), (-DM, BF16), (DM, F32), (DM, F32)]
    return pl.pallas_call(
        body,
        name="fwd_mix",
        grid=(NTILE,),
        out_shape=[jax.ShapeDtypeStruct((SEQ, w) if w > 0 else (-w // CH, SEQ, CH), d) for w, d in outs]
        + [jax.ShapeDtypeStruct((NDEV * s.shape[0], s.shape[1]), BF16) for s in shards],
        in_specs=[_tile(DM), _const((8, 6 * DM)), _const((1, DM)), _const(win_t.shape), _const(dww.shape), _const((1, CONVW)),
                  _const((1, CONVW)), _const((1, CONVW)), _const(wpw.shape), _const(wg.shape), _const((1, POOLW)), _const(wout.shape)]
        + [_HBM] * ns,
        out_specs=[_tile(w) if w > 0 else _tile3(-w // CH) for w, _ in outs] + [_HBM] * ns,
        scratch_shapes=[pltpu.VMEM((TOK + HALO_C, CONVW), F32), pltpu.VMEM((TOK + HALO_P, POOLW), F32), pltpu.VMEM((TOK, DM), BF16),
                        pltpu.VMEM(shards[0].shape, F32)] + [pltpu.VMEM(s.shape, BF16) for s in shards]
        + [pltpu.SemaphoreType.DMA] + _ag_sems(ns),
        compiler_params=pltpu.CompilerParams(dimension_semantics=("arbitrary",), vmem_limit_bytes=VMEM_LIMIT),
    )(x, mod8, g1, win_t, dww, dwb, lng, lnb, wpw, wg, psc, wout, *shards)


def _ffn_call(x1, tgt, mod8, g2, gf, wg_t, wu_t, wd):
    def body(x1_ref, tgt_ref, mod_ref, g2_ref, gf_ref, wg_hbm, wu_hbm, wd_hbm,
             h2b_ref, a3_ref, dfb_ref, dg3_ref, du3_ref, dx1_ref, acc_ref,
             wg_ref, wu_ref, wd_ref, wsem):
        i = pl.program_id(0)

        @pl.when(i == 0)
        def _():
            cps = [pltpu.make_async_copy(s, d, wsem.at[n]) for n, (s, d) in enumerate(((wg_hbm, wg_ref), (wu_hbm, wu_ref), (wd_hbm, wd_ref)))]
            for cp in cps:
                cp.start()
            acc_ref[...] = jnp.zeros((8, DM), F32)
            for cp in cps:
                cp.wait()

        x1 = x1_ref[...]
        sh2, sc2, gt2 = mod_ref[0:1, 3 * DM : 4 * DM], mod_ref[0:1, 4 * DM : 5 * DM], mod_ref[0:1, 5 * DM : 6 * DM]
        g2 = g2_ref[...]
        r2, xr, h2 = _norm_mod(x1, g2, sc2, sh2)
        h2b = h2.astype(BF16)
        h2b_ref[...] = h2b
        gate = _dot_nt(h2b, wg_ref[...])
        up = _dot_nt(h2b, wu_ref[...])
        ab = (gate * _sig(gate) * up).astype(BF16)
        for j in range(NCH):
            a3_ref[j] = ab[:, j * CH : (j + 1) * CH]
        f = _dot_nn(ab, wd_ref[...])
        x2 = x1 + gt2 * f
        rf = lax.rsqrt(jnp.mean(x2 * x2, axis=-1, keepdims=True) + EPS)
        nf = x2 * rf
        gf_ = gf_ref[...]
        err = nf * gf_ - tgt_ref[...]
        loss = 0.5 * jnp.sum(jnp.sum(err * err, axis=-1, keepdims=True), axis=0, keepdims=True) * (1.0 / DM)
        dout = err * (1.0 / DM)
        dnf = dout * gf_
        dx2 = rf * (dnf - nf * jnp.mean(dnf * nf, axis=-1, keepdims=True))
        dfb = (gt2 * dx2).astype(BF16)
        dfb_ref[...] = dfb
        da = _dot_nt(dfb, wd_ref[...])
        sg = _sig(gate)
        dgb = (da * up * (sg * (1.0 + gate * (1.0 - sg)))).astype(BF16)
        dub = (da * (gate * sg)).astype(BF16)
        for j in range(NCH):
            dg3_ref[j] = dgb[:, j * CH : (j + 1) * CH]
            du3_ref[j] = dub[:, j * CH : (j + 1) * CH]
        dh2 = _dot_nn(dgb, wg_ref[...]) + _dot_nn(dub, wu_ref[...])
        dn2 = dh2 * (1.0 + sc2)
        dxr = dn2 * g2
        dx1_ref[...] = dx2 + r2 * (dxr - xr * jnp.mean(dxr * xr, axis=-1, keepdims=True))

        def colsum(v):
            return jnp.sum(v, axis=0, keepdims=True)

        acc_ref[0:1, :] += colsum(dh2)
        acc_ref[1:2, :] += colsum(dh2 * (xr * g2))
        acc_ref[2:3, :] += colsum(dx2 * f)
        acc_ref[3:4, :] += colsum(dn2 * xr)
        acc_ref[4:5, :] += colsum(dout * nf)
        acc_ref[5:6, :] += jnp.broadcast_to(loss, (1, DM))

    c3 = pl.BlockSpec((NCH, TOK, CH), lambda i: (0, i, 0))
    return pl.pallas_call(
        body,
        name="ffn",
        grid=(NTILE,),
        out_shape=[jax.ShapeDtypeStruct((SEQ, DM), BF16), jax.ShapeDtypeStruct((NCH, SEQ, CH), BF16), jax.ShapeDtypeStruct((SEQ, DM), BF16),
                   jax.ShapeDtypeStruct((NCH, SEQ, CH), BF16), jax.ShapeDtypeStruct((NCH, SEQ, CH), BF16),
                   jax.ShapeDtypeStruct((SEQ, DM), F32), jax.ShapeDtypeStruct((8, DM), F32)],
        in_specs=[_tile(DM), _tile(DM), _const((8, 6 * DM)), _const((1, DM)), _const((1, DM)), _HBM, _HBM, _HBM],
        out_specs=[_tile(DM), c3, _tile(DM), c3, c3, _tile(DM), _const((8, DM))],
        scratch_shapes=[pltpu.VMEM((DFF, DM), BF16)] * 3 + [pltpu.SemaphoreType.DMA((3,))],
        compiler_params=pltpu.CompilerParams(dimension_semantics=("arbitrary",), vmem_limit_bytes=VMEM_LIMIT),
    )(x1, tgt, mod8, g2, gf, wg_t, wu_t, wd)


def _bwd_mix_call(dx1, x, y, uag, hc, hd, pbv, mod8, g1, win_t, dww, lng, lnb, wpw, wg, psc, wout):
    hpt = TOK // HALO_C

    def body(dx1_ref, x_ref, y_ref, uag_ref, hc_ref, halo_ref, hd_ref, pb_ref, mod_ref, g1_ref, win_ref, dww_ref, lng_ref, lnb_ref,
             wpw_ref, wg_ref, psc_ref, wout_ref,
             gx_ref, dyb_ref, dycb_ref, dub3_ref, acc_ref, ddw_ref, dwg_ref, d_ext, q_ext, hcx, dub_ref, hrot, drot):
        i = pl.program_id(0)
        it = NTILE - 1 - i

        @pl.when(i == 0)
        def _():
            d_ext[TOK : TOK + HALO_C, :] = jnp.zeros((HALO_C, CONVW), F32)
            q_ext[TOK : TOK + HALO_P, :] = jnp.zeros((HALO_P, POOLW), F32)
            acc_ref[...] = jnp.zeros((8, DM), F32)
            ddw_ref[...] = jnp.zeros((HALO_C, CONVW), F32)
            dwg_ref[...] = jnp.zeros((len(WINS) * PGD, PGD), F32)

        def colsum(v):
            return jnp.sum(v, axis=0, keepdims=True)

        dx1 = dx1_ref[...]
        x = x_ref[...]
        sh1, sc1, gt1 = mod_ref[0:1, 0:DM], mod_ref[0:1, DM : 2 * DM], mod_ref[0:1, 2 * DM : 3 * DM]
        acc_ref[2:3, :] += colsum(dx1 * y_ref[...])
        dyb = (gt1 * dx1).astype(BF16)
        dyb_ref[...] = dyb
        dycat = _dot_nt(dyb, wout_ref[...])

        hd = hd_ref[...]
        mu = jnp.mean(hd, axis=-1, keepdims=True)
        dlt = hd - mu
        rstd = lax.rsqrt(jnp.mean(dlt * dlt, axis=-1, keepdims=True) + EPS)
        xhat = dlt * rstd
        lng = lng_ref[...]
        hl = xhat * lng + lnb_ref[...]
        sgl = _sig(hl)
        dycb = dycat[:, :CONVW].astype(BF16)
        dycb_ref[...] = dycb
        dhl = _dot_nt(dycb, wpw_ref[...]) * (sgl * (1.0 + hl * (1.0 - sgl)))
        acc_ref[5:6, 0:CONVW] += colsum(dhl)
        acc_ref[4:5, CONVW:DM] += colsum(dhl * xhat)
        dxh = dhl * lng
        dhd = rstd * (dxh - jnp.mean(dxh, axis=-1, keepdims=True) - xhat * jnp.mean(dxh * xhat, axis=-1, keepdims=True))
        acc_ref[4:5, 0:CONVW] += colsum(dhd)

        hcx[0:HALO_C, :] = jnp.where(it == 0, 0.0, halo_ref[...])
        hcx[HALO_C : HALO_C + TOK, :] = hc_ref[...]
        d_ext[0:TOK, :] = dhd
        for b in range(1, 8):
            hrot[b - 1] = hcx[pl.ds(b, TOK + HALO_C - 8), :]
            drot[b - 1] = d_ext[pl.ds(b, TOK + HALO_C - 8), :]

        def tap(base, rot, off):
            a, b = divmod(off, 8)
            return base[pl.ds(8 * a, TOK), :] if b == 0 else rot[b - 1, pl.ds(8 * a, TOK), :]

        dhc = jnp.zeros((TOK, CONVW), F32)
        for k in range(KCONV):
            ddw_ref[k : k + 1, :] += colsum(dhd * tap(hcx, hrot, HALO_C - (KCONV - 1) + k))
            dhc = dhc + dww_ref[k : k + 1, :] * tap(d_ext, drot, KCONV - 1 - k)
        d_ext[TOK : TOK + HALO_C, :] = d_ext[0:HALO_C, :]
        ua, ug = uag_ref[:, 0:CONVW], uag_ref[:, CONVW : 2 * CONVW]
        sgg = _sig(ug)
        dub_ref[:, 0:CONVW] = (dhc * sgg).astype(BF16)
        dub_ref[:, CONVW : 2 * CONVW] = (dhc * ua * sgg * (1.0 - sgg)).astype(BF16)

        tg = it * TOK + lax.broadcasted_iota(jnp.int32, (TOK, 1), 0)
        for g, w in enumerate(WINS):
            ln = slice(PGD * g, PGD * (g + 1))
            wgb = wg_ref[g].astype(BF16)
            pb = pb_ref[:, ln]
            dyp = dycat[:, CONVW + PGD * g : CONVW + PGD * (g + 1)]
            acc_ref[5:6, CONVW + PGD * g : CONVW + PGD * (g + 1)] += colsum(dyp * _dot_nn(pb, wgb))
            dzb = (dyp * psc_ref[:, ln]).astype(BF16)
            dwg_ref[PGD * g : PGD * (g + 1), :] += _dot_tn(pb, dzb)
            dp = _dot_nt(dzb, wgb)
            cnt = jnp.minimum(tg + 1, w).astype(F32)
            q_ext[0:TOK, ln] = dp / cnt
            dv = -dp
            for d in range(w):
                dv = dv + q_ext[pl.ds(d, TOK), ln]
            dub_ref[:, 2 * CONVW + PGD * g : 2 * CONVW + PGD * (g + 1)] = dv.astype(BF16)
        q_ext[TOK : TOK + HALO_P, :] = q_ext[0:HALO_P, :]

        for j in range(3 * CONVW // CH):
            dub3_ref[j] = dub_ref[:, j * CH : (j + 1) * CH]
        dh1 = _dot_nn(dub_ref[...], win_ref[...])
        g1 = g1_ref[...]
        r1 = lax.rsqrt(jnp.mean(x * x, axis=-1, keepdims=True) + EPS)
        xr = x * r1
        acc_ref[0:1, :] += colsum(dh1)
        acc_ref[1:2, :] += colsum(dh1 * (xr * g1))
        dn1 = dh1 * (1.0 + sc1)
        acc_ref[3:4, :] += colsum(dn1 * xr)
        dxr = dn1 * g1
        gx_ref[...] = dx1 + r1 * (dxr - xr * jnp.mean(dxr * xr, axis=-1, keepdims=True))

    halo = pl.BlockSpec((HALO_C, CONVW), lambda i: (jnp.maximum((NTILE - 1 - i) * hpt - 1, 0), 0))
    return pl.pallas_call(
        body,
        name="bwd_mix",
        grid=(NTILE,),
        out_shape=[jax.ShapeDtypeStruct((SEQ, DM), F32), jax.ShapeDtypeStruct((SEQ, DM), BF16), jax.ShapeDtypeStruct((SEQ, CONVW), BF16),
                   jax.ShapeDtypeStruct((3 * CONVW // CH, SEQ, CH), BF16), jax.ShapeDtypeStruct((8, DM), F32),
                   jax.ShapeDtypeStruct((HALO_C, CONVW), F32), jax.ShapeDtypeStruct((len(WINS) * PGD, PGD), F32)],
        in_specs=[_tile(DM, True), _tile(DM, True), _tile(DM, True), _tile(2 * CONVW, True), _tile(CONVW, True), halo, _tile(CONVW, True),
                  _tile(POOLW, True), _const((8, 6 * DM)), _const((1, DM)), _const(win_t.shape), _const(dww.shape), _const((1, CONVW)),
                  _const((1, CONVW)), _const(wpw.shape), _const(wg.shape), _const((1, POOLW)), _const(wout.shape)],
        out_specs=[_tile(DM, True), _tile(DM, True), _tile(CONVW, True), _tile3(3 * CONVW // CH, True), _const((8, DM)),
                   _const((HALO_C, CONVW)), _const((len(WINS) * PGD, PGD))],
        scratch_shapes=[pltpu.VMEM((TOK + HALO_C, CONVW), F32), pltpu.VMEM((TOK + HALO_P, POOLW), F32), pltpu.VMEM((TOK + HALO_C, CONVW), F32),
                        pltpu.VMEM((TOK, 3 * CONVW), BF16)] + [pltpu.VMEM((7, TOK + HALO_C - 8, CONVW), F32)] * 2,
        compiler_params=pltpu.CompilerParams(dimension_semantics=("arbitrary",), vmem_limit_bytes=VMEM_LIMIT),
    )(dx1, x, y, uag, hc, hc, hd, pbv, mod8, g1, win_t, dww, lng, lnb, wpw, wg, psc, wout)


CHIPS = (0, 4, 2, 6)


def _wgrad_rs_call(a3s, bmap, bs, order, spack, dmodp):
    nw = len(a3s)
    nchs = [a.shape[0] for a in a3s]
    rws = [n * CH // NDEV for n in nchs]
    cols = [bs[bmap[k]].shape[1] for k in range(nw)]
    srows = spack.shape[0]
    sub = 16
    nb = len(bs)

    def body(*refs):
        pos = 0

        def take(n):
            nonlocal pos
            pos += n
            return refs[pos - n : pos]

        a_refs, b_refs = take(nw), take(nb)
        spack_ref, dmodp_ref = take(2)
        o_refs = take(nw)
        ssum_ref, gbada_ref, dmy_ref = take(3)
        p_refs = take(nw)
        r1, cb, r2, hb = take(nw), take(nw), take(nw), take(nw)
        tmp4, abuf, bbuf, obuf, srecv, mrecv = take(6)
        asem, osem, bsem, tsem, d_s, d_r, i_s, i_r, h_s, h_r, ssem_s, ssem_r, msem_s, msem_r = take(14)
        x, y, c_, me = _place()
        sib, xn, yn = (_flip(x, y, c_, r)[0] for r in (1, 4, 2))

        def rcopy(src, dst, ss, rs, dev):
            return pltpu.make_async_remote_copy(src_ref=src, dst_ref=dst, send_sem=ss, recv_sem=rs, device_id=dev, device_id_type=MESH)

        srecv[me] = spack_ref[...]
        mrecv[me] = dmodp_ref[...]
        sends = []
        for r in range(1, NDEV):
            dev, _ = _flip(x, y, c_, r)
            sends.append(rcopy(srecv.at[me], srecv.at[me], ssem_s.at[r], ssem_r.at[r], dev))
            sends.append(rcopy(mrecv.at[me], mrecv.at[me], msem_s.at[r], msem_r.at[r], dev))
        for cp in sends:
            cp.start()

        def relay(k):
            half = rws[k] // 2
            for h in range(2):
                rcopy(cb[k].at[2, pl.ds(h * half, half), :], hb[k].at[h], h_s.at[k, h], h_r.at[k, h], (x, y, c_)).wait_recv()

                def add_half(j, carry, k=k, h=h):
                    rr = pl.ds(pl.multiple_of(j * sub, sub), sub)
                    dst = pl.ds(pl.multiple_of(h * half + j * sub, sub), sub)
                    cb[k][1 - h, dst, :] = (cb[k][1 - h, dst, :].astype(F32) + hb[k][h, rr, :].astype(F32)).astype(BF16)
                    return carry

                lax.fori_loop(0, half // sub, add_half, 0)
            for q, dev in enumerate((xn, yn)):
                cp = rcopy(cb[k].at[q], r2[k].at[q], i_s.at[k, q], i_r.at[k, q], dev)
                cp.start()
                sends.append(cp)

        def mine(k, q):
            _, owner = _flip(x, y, c_, CHIPS[q])
            return _rows(p_refs[k], owner, rws[k]), tmp4.at[q, pl.ds(0, rws[k]), pl.ds(0, cols[k])]

        def fetch(k):
            for q in range(4):
                pltpu.make_async_copy(*mine(k, q), tsem.at[q]).start()

        def presum(k):
            for q in (3, 1, 2, 0):
                src, tmp = mine(k, q)
                rcopy(src, r1[k].at[q], d_s.at[k, q], d_r.at[k, q], (x, y, c_)).wait_recv()
                pltpu.make_async_copy(src, tmp, tsem.at[q]).wait()

                def add_sib(j, carry, k=k, q=q, tmp=tmp):
                    rr = pl.ds(pl.multiple_of(j * sub, sub), sub)
                    t = tmp[rr, :].astype(F32) + r1[k][q, rr, :].astype(F32)
                    if q == 0:
                        o_refs[k][rr, :] = t
                    else:
                        cb[k][q - 1, rr, :] = t.astype(BF16)
                    return carry

                lax.fori_loop(0, rws[k] // sub, add_sib, 0)
                if q == 3:
                    for h, dev in enumerate((xn, yn)):
                        half = rws[k] // 2
                        cp = rcopy(cb[k].at[2, pl.ds(h * half, half), :], hb[k].at[h], h_s.at[k, h], h_r.at[k, h], dev)
                        cp.start()
                        sends.append(cp)

        uses = [bmap[k] for t, k in enumerate(order) if t == 0 or bmap[k] != bmap[order[t - 1]]]

        def b_copy(u):
            return pltpu.make_async_copy(b_refs[uses[u]], bbuf.at[u % 2, :, pl.ds(0, b_refs[uses[u]].shape[1])], bsem.at[u % 2])

        b_copy(0).start()
        u = -1
        for t, k in enumerate(order):
            a_ref, p_ref, rw = a_refs[k], p_refs[k], rws[k]
            if t == 0 or bmap[k] != bmap[order[t - 1]]:
                u += 1
                b_copy(u).wait()
                if u + 1 < len(uses):
                    b_copy(u + 1).start()
            bb = bbuf.at[u % 2, :, pl.ds(0, cols[k])]
            ob = obuf.at[:, :, pl.ds(0, cols[k])]

            def a_copy(m, slot, a_ref=a_ref):
                return pltpu.make_async_copy(a_ref.at[m], abuf.at[slot], asem.at[slot])

            def o_copy(m, slot, ob=ob, p_ref=p_ref):
                return pltpu.make_async_copy(ob.at[slot], p_ref.at[pl.ds(pl.multiple_of(m * CH, CH), CH), :], osem.at[slot])

            a_copy(0, 0).start()

            def step(m, carry, k=k, a_copy=a_copy, o_copy=o_copy, bb=bb, ob=ob):
                slot = lax.rem(m, 2)
                a_copy(m, slot).wait()

                @pl.when(m + 1 < nchs[k])
                def _():
                    a_copy(m + 1, 1 - slot).start()

                @pl.when(m >= 2)
                def _():
                    o_copy(m - 2, slot).wait()

                ob[slot] = _dot_tn(abuf[slot], bb[...]).astype(BF16)
                o_copy(m, slot).start()
                return carry

            lax.fori_loop(0, nchs[k], step, 0)
            for m in (nchs[k] - 2, nchs[k] - 1):
                o_copy(m, m % 2).wait()

            for q, r in enumerate(CHIPS):
                _, owner = _flip(x, y, c_, r | 1)
                cp = rcopy(_rows(p_ref, owner, rw), r1[k].at[q], d_s.at[k, q], d_r.at[k, q], sib)
                cp.start()
                sends.append(cp)
            if t >= 1:
                presum(order[t - 1])
            fetch(k)
            if t >= 2:
                relay(order[t - 2])
        presum(order[-1])
        relay(order[-2])
        relay(order[-1])

        for r in range(1, NDEV):
            _, pb = _flip(x, y, c_, r)
            rcopy(srecv.at[me], srecv.at[pb], ssem_s.at[r], ssem_r.at[r], (x, y, c_)).wait_recv()
            rcopy(mrecv.at[me], mrecv.at[pb], msem_s.at[r], msem_r.at[r], (x, y, c_)).wait_recv()
        tot = srecv[0]
        for s in range(1, NDEV):
            tot = tot + srecv[s]
        ssum_ref[...] = tot
        btot = mrecv[0]
        for s in range(1, NDEV):
            btot = btot + mrecv[s]
        gbada_ref[...] = btot
        rowm = lax.broadcasted_iota(jnp.int32, (8, MODW), 0)
        dmy = jnp.zeros((8, MODW), F32)
        for s in range(NDEV):
            drow = jnp.sum(jnp.where(rowm == me, mrecv[s], 0.0), axis=0, keepdims=True)
            dmy = jnp.where(rowm == s, drow, dmy)
        dmy_ref[...] = dmy

        for k in order:
            for q in range(2):
                rcopy(cb[k].at[q], r2[k].at[q], i_s.at[k, q], i_r.at[k, q], (x, y, c_)).wait_recv()

            def add_far(j, carry, k=k):
                rr = pl.ds(pl.multiple_of(j * sub, sub), sub)
                t = o_refs[k][rr, :]
                for q in range(2):
                    t = t + r2[k][q, rr, :].astype(F32)
                o_refs[k][rr, :] = t
                return carry

            lax.fori_loop(0, rws[k] // sub, add_far, 0)
        for cp in sends:
            cp.wait_send()

    out_shape = [jax.ShapeDtypeStruct((rws[k], cols[k]), F32) for k in range(nw)]
    out_shape += [jax.ShapeDtypeStruct((srows, DM), F32), jax.ShapeDtypeStruct((8, MODW), F32), jax.ShapeDtypeStruct((8, MODW), F32)]
    out_shape += [jax.ShapeDtypeStruct((nchs[k] * CH, cols[k]), BF16) for k in range(nw)]
    scratch = [pltpu.VMEM((4, rws[k], cols[k]), BF16) for k in range(nw)]
    scratch += [pltpu.VMEM((3, rws[k], cols[k]), BF16) for k in range(nw)]
    scratch += [pltpu.VMEM((2, rws[k], cols[k]), BF16) for k in range(nw)]
    scratch += [pltpu.VMEM((2, rws[k] // 2, cols[k]), BF16) for k in range(nw)]
    scratch += [pltpu.VMEM((4, max(rws), max(cols)), BF16)]
    scratch += [pltpu.VMEM((2, SEQ, CH), BF16), pltpu.VMEM((2, SEQ, max(cols)), BF16), pltpu.VMEM((2, CH, max(cols)), BF16),
                pltpu.VMEM((NDEV, srows, DM), F32), pltpu.VMEM((NDEV, 8, MODW), F32)]
    scratch += [pltpu.SemaphoreType.DMA((2,))] * 3 + [pltpu.SemaphoreType.DMA((4,))]
    scratch += [pltpu.SemaphoreType.DMA((nw, 4))] * 2 + [pltpu.SemaphoreType.DMA((nw, 2))] * 4 + [pltpu.SemaphoreType.DMA((NDEV,))] * 4
    outs = pl.pallas_call(
        body,
        name="wgrad_rs",
        out_shape=out_shape,
        in_specs=[_HBM] * (nw + nb) + [_VM] * 2,
        out_specs=[_VM] * (nw + 3) + [_HBM] * nw,
        scratch_shapes=scratch,
        compiler_params=pltpu.CompilerParams(vmem_limit_bytes=60 * 1024 * 1024),
    )(*a3s, *bs, spack, dmodp)
    return outs[: nw + 3]


def _adam_update(w, g, m, v):
    m = ADAM_B1 * m + (1.0 - ADAM_B1) * g
    v = ADAM_B2 * v + (1.0 - ADAM_B2) * (g * g)
    m_hat = m / (1.0 - ADAM_B1 ** ADAM_STEP)
    v_hat = v / (1.0 - ADAM_B2 ** ADAM_STEP)
    return -ADAM_LR * (m_hat / (jnp.sqrt(v_hat) + ADAM_EPS) + ADAM_WD * w), m, v


def _adam_ada_call(w, m, v, dmy, cact_t):
    def body(w_ref, m_ref, v_ref, dmy_ref, ct_ref, g_ref, d_ref, mo_ref, vo_ref):
        g = jnp.zeros((DM, MODW), F32)
        for s in range(NDEV):
            g = g + ct_ref[:, s : s + 1] * dmy_ref[s : s + 1, :]
        g_ref[...] = g
        d_ref[...], mo_ref[...], vo_ref[...] = _adam_update(w_ref[...], g, m_ref[...], v_ref[...])

    return pl.pallas_call(
        body,
        name="adam_ada",
        out_shape=[jax.ShapeDtypeStruct(w.shape, F32)] * 4,
        in_specs=[_VM] * 5,
        out_specs=[_VM] * 4,
        compiler_params=pltpu.CompilerParams(vmem_limit_bytes=VMEM_LIMIT),
    )(w, m, v, dmy, cact_t)


def _adam_call(name, ws, gs, ms, vs):
    n = len(ws)

    def body(*refs):
        for i in range(n):
            w, g, m, v = (refs[j * n + i][...] for j in range(4))
            refs[4 * n + i][...], refs[5 * n + i][...], refs[6 * n + i][...] = _adam_update(w, g, m, v)

    shapes = [jax.ShapeDtypeStruct(w.shape, F32) for w in ws]
    outs = pl.pallas_call(
        body,
        name=name,
        out_shape=shapes * 3,
        in_specs=[_VM] * (4 * n),
        out_specs=[_VM] * (3 * n),
        compiler_params=pltpu.CompilerParams(vmem_limit_bytes=VMEM_LIMIT),
    )(*ws, *gs, *ms, *vs)
    return outs[:n], outs[n : 2 * n], outs[2 * n :]


def kernel(x, c, w_ada, b_ada, g_norm1, w_in, dw_w, dw_b, conv_ln_g, conv_ln_b, w_conv_pw, w_pool_group, pool_scale, w_out, g_norm2, w_ffn_gate, w_ffn_up, w_ffn_down, g_final, loss_target, m_w_ada, m_b_ada, m_g_norm1, m_w_in, m_dw_w, m_dw_b, m_conv_ln_g, m_conv_ln_b, m_w_conv_pw, m_w_pool_group, m_pool_scale, m_w_out, m_g_norm2, m_w_ffn_gate, m_w_ffn_up, m_w_ffn_down, m_g_final, v_w_ada, v_b_ada, v_g_norm1, v_w_in, v_dw_w, v_dw_b, v_conv_ln_g, v_conv_ln_b, v_w_conv_pw, v_w_pool_group, v_pool_scale, v_w_out, v_g_norm2, v_w_ffn_gate, v_w_ffn_up, v_w_ffn_down, v_g_final):
    me = 4 * lax.axis_index("x") + 2 * lax.axis_index("y") + lax.axis_index("c")
    xs, tgt = x[0], loss_target[0]
    b_my = lax.dynamic_slice(b_ada, (0, me * MODW), (1, MODW))
    win_t, wout, wpw, mod8, cact, dww8 = _gather_call(c, w_ada[0], b_my, dw_w[0], [w_in[0].T, w_out[0], w_conv_pw[0]])
    dww = jnp.pad(jnp.transpose(dww8, (1, 0, 2)).reshape(KCONV, CONVW), ((0, HALO_C - KCONV), (0, 0)))
    wgp = w_pool_group[0]

    h1b, uag, hc, hd, hsb3, pbv, ycb3, y, x1, wg_t, wu_t, wd = _fwd_mix_call(
        xs, mod8, g_norm1, win_t, dww, dw_b, conv_ln_g, conv_ln_b, wpw, wgp, pool_scale, wout,
        [w_ffn_gate[0].T, w_ffn_up[0].T, w_ffn_down[0]])
    h2b, a3, dfb, dg3, du3, dx1, facc = _ffn_call(x1, tgt, mod8, g_norm2, g_final.reshape(1, DM), wg_t, wu_t, wd)
    gx, dyb, dycb, dub3, macc, ddw, dwg = _bwd_mix_call(
        dx1, xs, y, uag, hc, hd, pbv, mod8, g_norm1, win_t, dww, conv_ln_g, conv_ln_b, wpw, wgp, pool_scale, wout)
    spack = jnp.concatenate(
        [macc[3:4], facc[3:4], facc[4:5], macc[4:6], facc[5:6], jnp.zeros((2, DM), F32), ddw.reshape(HALO_C // 2, DM), dwg.reshape(-1, DM)], axis=0)
    dmodp = jnp.concatenate([macc[0:3], facc[0:3]], axis=0).reshape(8, MODW)
    g_in_t, g_gate_t, g_up_t, g_down, g_out, g_pw, ssum, gbada, dmy = _wgrad_rs_call(
        [dub3, dg3, du3, a3, ycb3, hsb3], [0, 1, 1, 2, 3, 4], [h1b, h2b, dfb, dyb, dycb], (3, 1, 2, 0, 4, 5), spack, dmodp)
    g_wada, d_wada, m_wada, v_wada = _adam_ada_call(w_ada[0], m_w_ada[0], v_w_ada[0], dmy, cact.T)

    loss = ssum[5, 0]
    ddw_all = ssum[8 : 8 + HALO_C // 2].reshape(HALO_C, CONVW)[:KCONV]
    grads = {
        "w_ada": g_wada,
        "b_ada": gbada.reshape(1, 6 * DM),
        "g_norm1": ssum[0:1],
        "w_in": g_in_t,
        "dw_w": lax.dynamic_slice(ddw_all, (0, me * (CONVW // NDEV)), (KCONV, CONVW // NDEV)),
        "dw_b": ssum[3:4, 0:CONVW],
        "conv_ln_g": ssum[3:4, CONVW:DM],
        "conv_ln_b": ssum[4:5, 0:CONVW],
        "w_conv_pw": g_pw,
        "w_pool_group": ssum[8 + HALO_C // 2 :].reshape(len(WINS) * PGD, PGD),
        "pool_scale": ssum[4:5, CONVW:DM],
        "w_out": g_out,
        "g_norm2": ssum[1:2],
        "w_ffn_gate": g_gate_t,
        "w_ffn_up": g_up_t,
        "w_ffn_down": g_down,
        "g_final": ssum[2:3],
    }
    given = dict(w_ada=(w_ada, m_w_ada, v_w_ada), b_ada=(b_ada, m_b_ada, v_b_ada), g_norm1=(g_norm1, m_g_norm1, v_g_norm1),
                 w_in=(w_in, m_w_in, v_w_in), dw_w=(dw_w, m_dw_w, v_dw_w), dw_b=(dw_b, m_dw_b, v_dw_b),
                 conv_ln_g=(conv_ln_g, m_conv_ln_g, v_conv_ln_g), conv_ln_b=(conv_ln_b, m_conv_ln_b, v_conv_ln_b),
                 w_conv_pw=(w_conv_pw, m_w_conv_pw, v_w_conv_pw), w_pool_group=(w_pool_group, m_w_pool_group, v_w_pool_group),
                 pool_scale=(pool_scale, m_pool_scale, v_pool_scale), w_out=(w_out, m_w_out, v_w_out), g_norm2=(g_norm2, m_g_norm2, v_g_norm2),
                 w_ffn_gate=(w_ffn_gate, m_w_ffn_gate, v_w_ffn_gate), w_ffn_up=(w_ffn_up, m_w_ffn_up, v_w_ffn_up),
                 w_ffn_down=(w_ffn_down, m_w_ffn_down, v_w_ffn_down), g_final=(g_final, m_g_final, v_g_final))
    names = list(given)
    groups = [["w_ffn_gate", "w_ffn_up"], ["w_ffn_down", "w_in", "w_out", "w_conv_pw"],
              ["b_ada", "g_norm1", "dw_w", "dw_b", "conv_ln_g", "conv_ln_b", "w_pool_group", "pool_scale", "g_norm2", "g_final"]]
    turned = ("w_in", "w_ffn_gate", "w_ffn_up")

    def work(n, a):
        return a[0].T if n in turned else a.reshape(grads[n].shape)

    def full(n, a):
        return a.T[None] if n in turned else a.reshape(given[n][0].shape)

    delta, new_m, new_v = {"w_ada": d_wada}, {"w_ada": m_wada}, {"w_ada": v_wada}
    for gi, grp in enumerate(groups):
        ds, ms, vs = _adam_call(f"adam{gi}", [work(n, given[n][0]) for n in grp], [grads[n] for n in grp],
                                [work(n, given[n][1]) for n in grp], [work(n, given[n][2]) for n in grp])
        for n, d_, m_, v_ in zip(grp, ds, ms, vs):
            delta[n], new_m[n], new_v[n] = d_, m_, v_

    return (loss, gx.reshape(x.shape), *[full(n, grads[n]) for n in names], *[full(n, delta[n]) for n in names],
            *[full(n, new_m[n]) for n in names], *[full(n, new_v[n]) for n in names])
```

```python
import functools

import jax
import jax.numpy as jnp
from jax import lax
from jax.experimental import pallas as pl
from jax.experimental.pallas import tpu as pltpu

F32, BF16 = jnp.float32, jnp.bfloat16
SEQ, DM = 2048, 1024
CONVW, POOLW = 512, 512
KCONV = 31
WINS = (2, 4, 8, 16)
PGD = 128
DFF = 2816
NDEV = 8
MODW = 6 * DM // NDEV
EPS = 1e-6
TOK = 256
NTILE = SEQ // TOK
CH = 256
NCH = DFF // CH
HALO_C, HALO_P = 32, 16
MESH = pl.DeviceIdType.MESH
VMEM_LIMIT = 56 * 1024 * 1024
ADAM_LR, ADAM_B1, ADAM_B2, ADAM_EPS, ADAM_WD, ADAM_STEP = 0.001, 0.9, 0.999, 1e-08, 0.01, 10
HI = lax.Precision.HIGHEST

_VM = pl.BlockSpec(memory_space=pltpu.VMEM)
_HBM = pl.BlockSpec(memory_space=pltpu.HBM)


def _place():
    x, y, c = lax.axis_index("x"), lax.axis_index("y"), lax.axis_index("c")
    return x, y, c, 4 * x + 2 * y + c


def _flip(x, y, c, r):
    px = 1 - x if r & 4 else x
    py = 1 - y if r & 2 else y
    pc = 1 - c if r & 1 else c
    return (px, py, pc), 4 * px + 2 * py + pc


def _rows(ref, blk, n):
    return ref.at[pl.ds(pl.multiple_of(blk * n, 16), n), :]


def _sig(z):
    return jax.nn.sigmoid(z)


def _dot_nt(a, b):
    return lax.dot_general(a, b, (((1,), (1,)), ((), ())), preferred_element_type=F32)


def _dot_nn(a, b):
    return lax.dot_general(a, b, (((1,), (0,)), ((), ())), preferred_element_type=F32)


def _dot_tn(a, b):
    return lax.dot_general(a, b, (((0,), (0,)), ((), ())), preferred_element_type=F32)


def _rcopy(src, dst, ss, rs, dev):
    return pltpu.make_async_remote_copy(src_ref=src, dst_ref=dst, send_sem=ss, recv_sem=rs, device_id=dev, device_id_type=MESH)


def _ag_sems(nw):
    return ([pltpu.SemaphoreType.DMA((nw, 3))] * 2 + [pltpu.SemaphoreType.DMA((nw, 4))] * 2 + [pltpu.SemaphoreType.DMA((nw, 2))] * 2
            + [pltpu.SemaphoreType.DMA((nw,))])


def _ag_plan(sbufs, g_refs, rws, sems):
    wsem_s, wsem_r, fsem_s, fsem_r, hsem_s, hsem_r, lsem = sems
    x, y, c_, me = _place()
    here = (x, y, c_)
    plans = []
    for k, (sb, g, n) in enumerate(zip(sbufs, g_refs, rws)):

        def blk(r, half=None, g=g, n=n):
            b = _flip(x, y, c_, r)[1]
            if half is None:
                return _rows(g, b, n)
            return g.at[pl.ds(pl.multiple_of(b * n + half * (n // 2), 16), n // 2), :]

        def same(ref, ss, rs, j, dev, k=k):
            return _rcopy(ref, ref, ss.at[k, j], rs.at[k, j], dev)

        sib, xn, yn = (_flip(x, y, c_, r)[0] for r in (1, 4, 2))
        plans.append(dict(
            local=pltpu.make_async_copy(sb, blk(0), lsem.at[k]),
            first=[_rcopy(sb, blk(0), wsem_s.at[k, j], wsem_r.at[k, j], dev) for j, dev in enumerate((sib, xn, yn))],
            got=[_rcopy(sb, blk(r), wsem_s.at[k, j], wsem_r.at[k, j], here) for j, r in enumerate((1, 4, 2))],
            passes=[same(blk(4), fsem_s, fsem_r, 0, sib), same(blk(2), fsem_s, fsem_r, 1, sib),
                    same(blk(6, 0), fsem_s, fsem_r, 2, sib), same(blk(6, 1), fsem_s, fsem_r, 3, sib)],
            passed=[same(blk(5), fsem_s, fsem_r, 0, here), same(blk(3), fsem_s, fsem_r, 1, here),
                    same(blk(7, 0), fsem_s, fsem_r, 2, here), same(blk(7, 1), fsem_s, fsem_r, 3, here)],
            halves=[same(blk(4, 0), hsem_s, hsem_r, 0, yn), same(blk(2, 1), hsem_s, hsem_r, 1, xn)],
            halved=[same(blk(6, 0), hsem_s, hsem_r, 0, here), same(blk(6, 1), hsem_s, hsem_r, 1, here)],
        ))
    return plans


def _ag_start(sbufs, g_refs, rws, sems):
    for p in _ag_plan(sbufs, g_refs, rws, sems):
        p["local"].start()
        for cp in p["first"]:
            cp.start()


def _ag_pass_on(sbufs, g_refs, rws, sems):
    plans = _ag_plan(sbufs, g_refs, rws, sems)
    for p in plans:
        for j in (0, 1):
            p["got"][j + 1].wait_recv()
            p["halves"][j].start()
            p["passes"][j].start()
    for p in plans:
        for j in (0, 1):
            p["halved"][j].wait_recv()
            p["passes"][j + 2].start()


def _ag_finish(sbufs, g_refs, rws, sems):
    plans = _ag_plan(sbufs, g_refs, rws, sems)
    for p in plans:
        p["got"][0].wait_recv()
        for cp in p["passed"]:
            cp.wait_recv()
    for p in plans:
        for cp in p["first"] + p["passes"] + p["halves"]:
            cp.wait_send()
        p["local"].wait()


def _gather_call(c, w_ada, b_my, dww, shards):
    nw = len(shards)
    rws = [s.shape[0] for s in shards]

    def body(*refs):
        c_ref, wada_ref, bmy_ref, dww_ref = refs[:4]
        s_refs = refs[4 : 4 + nw]
        g_refs = refs[4 + nw : 4 + 2 * nw]
        mod8_ref, cact_ref, dww8_ref = refs[4 + 2 * nw : 7 + 2 * nw]
        crecv, msend, mrecv = refs[7 + 2 * nw : 10 + 2 * nw]
        sbufs = refs[10 + 2 * nw : 10 + 3 * nw]
        csem_s, csem_r, dsem_s, dsem_r, msem_s, msem_r = refs[10 + 3 * nw : 16 + 3 * nw]
        ag_sems = refs[16 + 3 * nw :]
        x, y, c_, me = _place()

        def rcopy(src, dst, ss, rs, dev):
            return pltpu.make_async_remote_copy(src_ref=src, dst_ref=dst, send_sem=ss, recv_sem=rs, device_id=dev, device_id_type=MESH)

        crecv[me] = jnp.broadcast_to(c_ref[...], (8, DM))
        dww8_ref[me] = dww_ref[...]
        small = []
        for r in range(1, NDEV):
            dev, _ = _flip(x, y, c_, r)
            small.append(rcopy(crecv.at[me], crecv.at[me], csem_s.at[r], csem_r.at[r], dev))
            small.append(rcopy(dww8_ref.at[me], dww8_ref.at[me], dsem_s.at[r], dsem_r.at[r], dev))
        for cp in small:
            cp.start()

        for k in range(nw):
            sbufs[k][...] = s_refs[k][...].astype(BF16)
        _ag_start(sbufs, g_refs, rws, ag_sems)

        rowid = lax.broadcasted_iota(jnp.int32, (8, DM), 0)
        for r in range(1, NDEV):
            _, pb = _flip(x, y, c_, r)
            rcopy(crecv.at[me], crecv.at[pb], csem_s.at[r], csem_r.at[r], (x, y, c_)).wait_recv()
        call = jnp.zeros((8, DM), F32)
        for s in range(NDEV):
            call = jnp.where(rowid == s, crecv[s], call)
        cact = call * _sig(call)
        cact_ref[...] = cact
        modp = jnp.dot(cact, wada_ref[...], precision=HI, preferred_element_type=F32) + bmy_ref[...]
        rowm = lax.broadcasted_iota(jnp.int32, (8, MODW), 0)
        for b in range(NDEV):
            row = jnp.sum(jnp.where(rowm == b, modp, 0.0), axis=0, keepdims=True)
            msend[b] = jnp.broadcast_to(row, (8, MODW))
        mrecv[me] = msend[me]
        msends = []
        for r in range(1, NDEV):
            dev, pb = _flip(x, y, c_, r)
            cp = rcopy(msend.at[pb], mrecv.at[me], msem_s.at[r], msem_r.at[r], dev)
            cp.start()
            msends.append(cp)

        _ag_pass_on(sbufs, g_refs, rws, ag_sems)

        for r in range(1, NDEV):
            _, pb = _flip(x, y, c_, r)
            rcopy(msend.at[pb], mrecv.at[pb], msem_s.at[r], msem_r.at[r], (x, y, c_)).wait_recv()
        for s in range(NDEV):
            mod8_ref[:, s * MODW : (s + 1) * MODW] = mrecv[s]

        _ag_finish(sbufs, g_refs, rws, ag_sems)
        for r in range(1, NDEV):
            _, pb = _flip(x, y, c_, r)
            rcopy(dww8_ref.at[me], dww8_ref.at[pb], dsem_s.at[r], dsem_r.at[r], (x, y, c_)).wait_recv()
        for cp in small + msends:
            cp.wait_send()

    out_shape = [jax.ShapeDtypeStruct((NDEV * s.shape[0], s.shape[1]), BF16) for s in shards]
    out_shape += [
        jax.ShapeDtypeStruct((8, 6 * DM), F32),
        jax.ShapeDtypeStruct((8, DM), F32),
        jax.ShapeDtypeStruct((NDEV,) + dww.shape, F32),
    ]
    scratch = [pltpu.VMEM((NDEV, 8, DM), F32), pltpu.VMEM((NDEV, 8, MODW), F32), pltpu.VMEM((NDEV, 8, MODW), F32)]
    scratch += [pltpu.VMEM(s.shape, BF16) for s in shards]
    scratch += [pltpu.SemaphoreType.DMA((NDEV,))] * 6 + _ag_sems(nw)
    return pl.pallas_call(
        body,
        name="gather",
        out_shape=out_shape,
        in_specs=[_VM] * (4 + nw),
        out_specs=[_HBM] * nw + [_VM] * 3,
        scratch_shapes=scratch,
        compiler_params=pltpu.CompilerParams(vmem_limit_bytes=VMEM_LIMIT),
    )(c, w_ada, b_my, dww, *shards)


def _const(shape):
    return pl.BlockSpec(shape, lambda i: (0,) * len(shape))


def _tile(width, rev=False):
    if rev:
        return pl.BlockSpec((TOK, width), lambda i: (NTILE - 1 - i, 0))
    return pl.BlockSpec((TOK, width), lambda i: (i, 0))


def _tile3(nch, rev=False):
    if rev:
        return pl.BlockSpec((nch, TOK, CH), lambda i: (0, NTILE - 1 - i, 0))
    return pl.BlockSpec((nch, TOK, CH), lambda i: (0, i, 0))


def _norm_mod(x, g, sc, sh):
    r = lax.rsqrt(jnp.mean(x * x, axis=-1, keepdims=True) + EPS)
    xr = x * r
    return r, xr, xr * g * (1.0 + sc) + sh


def _fwd_mix_call(x, mod8, g1, win_t, dww, dwb, lng, lnb, wpw, wg, psc, wout, shards):
    ns = len(shards)
    rws = [s.shape[0] for s in shards]

    def body(x_ref, mod_ref, g1_ref, win_ref, dww_ref, dwb_ref, lng_ref, lnb_ref, wpw_ref, wg_ref, psc_ref, wout_ref, *rest):
        s_refs = rest[:ns]
        h1b_ref, uag_ref, hc_ref, hd_ref, hsb3_ref, pb_ref, ycb3_ref, y_ref, x1_ref = rest[ns : ns + 9]
        g_refs = rest[ns + 9 : 2 * ns + 9]
        hc_ext, up_ext, ycb_ref, stage = rest[2 * ns + 9 : 2 * ns + 13]
        sbufs = rest[2 * ns + 13 : 3 * ns + 13]
        ssem = rest[3 * ns + 13]
        ag_sems = rest[3 * ns + 14 :]
        i = pl.program_id(0)

        @pl.when(i == 0)
        def _():
            for k in range(ns):
                cp = pltpu.make_async_copy(s_refs[k], stage, ssem)
                cp.start()
                cp.wait()
                sbufs[k][...] = stage[...].astype(BF16)
            _ag_start(sbufs, g_refs, rws, ag_sems)
            hc_ext[0:HALO_C, :] = jnp.zeros((HALO_C, CONVW), F32)
            up_ext[0:HALO_P, :] = jnp.zeros((HALO_P, POOLW), F32)

        x = x_ref[...]
        sh1, sc1, gt1 = mod_ref[0:1, 0:DM], mod_ref[0:1, DM : 2 * DM], mod_ref[0:1, 2 * DM : 3 * DM]
        _, _, h1 = _norm_mod(x, g1_ref[...], sc1, sh1)
        h1b = h1.astype(BF16)
        h1b_ref[...] = h1b
        u = _dot_nt(h1b, win_ref[...])
        uag_ref[...] = u[:, : 2 * CONVW]
        hc = u[:, :CONVW] * _sig(u[:, CONVW : 2 * CONVW])
        hc_ref[...] = hc
        hc_ext[HALO_C : HALO_C + TOK, :] = hc
        up_ext[HALO_P : HALO_P + TOK, :] = u[:, 2 * CONVW :]

        acc = jnp.zeros((TOK, CONVW), F32)
        for k in range(KCONV):
            acc = acc + dww_ref[k : k + 1, :] * hc_ext[pl.ds(HALO_C - (KCONV - 1) + k, TOK), :]
        hd = acc + dwb_ref[...]
        hd_ref[...] = hd
        hc_ext[0:HALO_C, :] = hc_ext[TOK : TOK + HALO_C, :]
        mu = jnp.mean(hd, axis=-1, keepdims=True)
        dlt = hd - mu
        rstd = lax.rsqrt(jnp.mean(dlt * dlt, axis=-1, keepdims=True) + EPS)
        hl = dlt * rstd * lng_ref[...] + lnb_ref[...]
        hsb = (hl * _sig(hl)).astype(BF16)
        for j in range(CONVW // CH):
            hsb3_ref[j] = hsb[:, j * CH : (j + 1) * CH]
        ycb_ref[:, 0:CONVW] = _dot_nn(hsb, wpw_ref[...]).astype(BF16)

        tg = i * TOK + lax.broadcasted_iota(jnp.int32, (TOK, 1), 0)
        for g, w in enumerate(WINS):
            ln = slice(PGD * g, PGD * (g + 1))
            v = up_ext[pl.ds(HALO_P, TOK), ln]
            ssum = v
            for d in range(1, w):
                ssum = ssum + up_ext[pl.ds(HALO_P - d, TOK), ln]
            cnt = jnp.minimum(tg + 1, w).astype(F32)
            pb = (ssum / cnt - v).astype(BF16)
            pb_ref[:, ln] = pb
            z = _dot_nn(pb, wg_ref[g].astype(BF16))
            ycb_ref[:, CONVW + PGD * g : CONVW + PGD * (g + 1)] = (z * psc_ref[:, ln]).astype(BF16)
        up_ext[0:HALO_P, :] = up_ext[TOK : TOK + HALO_P, :]

        for j in range(DM // CH):
            ycb3_ref[j] = ycb_ref[:, j * CH : (j + 1) * CH]
        yv = _dot_nn(ycb_ref[...], wout_ref[...])
        y_ref[...] = yv
        x1_ref[...] = x + gt1 * yv

        @pl.when(i == NTILE - 1)
        def _():
            _ag_pass_on(sbufs, g_refs, rws, ag_sems)
            _ag_finish(sbufs, g_refs, rws, ag_sems)

    outs = [(DM, BF16), (2 * CONVW, F32), (CONVW, F32), (CONVW, F32), (-CONVW, BF16), (POOLW, BF16), (-DM, BF16), (DM, F32), (DM, F32)]
    return pl.pallas_call(
        body,
        name="fwd_mix",
        grid=(NTILE,),
        out_shape=[jax.ShapeDtypeStruct((SEQ, w) if w > 0 else (-w // CH, SEQ, CH), d) for w, d in outs]
        + [jax.ShapeDtypeStruct((NDEV * s.shape[0], s.shape[1]), BF16) for s in shards],
        in_specs=[_tile(DM), _const((8, 6 * DM)), _const((1, DM)), _const(win_t.shape), _const(dww.shape), _const((1, CONVW)),
                  _const((1, CONVW)), _const((1, CONVW)), _const(wpw.shape), _const(wg.shape), _const((1, POOLW)), _const(wout.shape)]
        + [_HBM] * ns,
        out_specs=[_tile(w) if w > 0 else _tile3(-w // CH) for w, _ in outs] + [_HBM] * ns,
        scratch_shapes=[pltpu.VMEM((TOK + HALO_C, CONVW), F32), pltpu.VMEM((TOK + HALO_P, POOLW), F32), pltpu.VMEM((TOK, DM), BF16),
                        pltpu.VMEM(shards[0].shape, F32)] + [pltpu.VMEM(s.shape, BF16) for s in shards]
        + [pltpu.SemaphoreType.DMA] + _ag_sems(ns),
        compiler_params=pltpu.CompilerParams(dimension_semantics=("arbitrary",), vmem_limit_bytes=VMEM_LIMIT),
    )(x, mod8, g1, win_t, dww, dwb, lng, lnb, wpw, wg, psc, wout, *shards)


def _ffn_call(x1, tgt, mod8, g2, gf, wg_t, wu_t, wd):
    def body(x1_ref, tgt_ref, mod_ref, g2_ref, gf_ref, wg_hbm, wu_hbm, wd_hbm,
             h2b_ref, a3_ref, dfb_ref, dg3_ref, du3_ref, dx1_ref, acc_ref,
             wg_ref, wu_ref, wd_ref, wsem):
        i = pl.program_id(0)

        @pl.when(i == 0)
        def _():
            cps = [pltpu.make_async_copy(s, d, wsem.at[n]) for n, (s, d) in enumerate(((wg_hbm, wg_ref), (wu_hbm, wu_ref), (wd_hbm, wd_ref)))]
            for cp in cps:
                cp.start()
            acc_ref[...] = jnp.zeros((8, DM), F32)
            for cp in cps:
                cp.wait()

        x1 = x1_ref[...]
        sh2, sc2, gt2 = mod_ref[0:1, 3 * DM : 4 * DM], mod_ref[0:1, 4 * DM : 5 * DM], mod_ref[0:1, 5 * DM : 6 * DM]
        g2 = g2_ref[...]
        r2, xr, h2 = _norm_mod(x1, g2, sc2, sh2)
        h2b = h2.astype(BF16)
        h2b_ref[...] = h2b
        gate = _dot_nt(h2b, wg_ref[...])
        up = _dot_nt(h2b, wu_ref[...])
        ab = (gate * _sig(gate) * up).astype(BF16)
        for j in range(NCH):
            a3_ref[j] = ab[:, j * CH : (j + 1) * CH]
        f = _dot_nn(ab, wd_ref[...])
        x2 = x1 + gt2 * f
        rf = lax.rsqrt(jnp.mean(x2 * x2, axis=-1, keepdims=True) + EPS)
        nf = x2 * rf
        gf_ = gf_ref[...]
        err = nf * gf_ - tgt_ref[...]
        loss = 0.5 * jnp.sum(jnp.sum(err * err, axis=-1, keepdims=True), axis=0, keepdims=True) * (1.0 / DM)
        dout = err * (1.0 / DM)
        dnf = dout * gf_
        dx2 = rf * (dnf - nf * jnp.mean(dnf * nf, axis=-1, keepdims=True))
        dfb = (gt2 * dx2).astype(BF16)
        dfb_ref[...] = dfb
        da = _dot_nt(dfb, wd_ref[...])
        sg = _sig(gate)
        dgb = (da * up * (sg * (1.0 + gate * (1.0 - sg)))).astype(BF16)
        dub = (da * (gate * sg)).astype(BF16)
        for j in range(NCH):
            dg3_ref[j] = dgb[:, j * CH : (j + 1) * CH]
            du3_ref[j] = dub[:, j * CH : (j + 1) * CH]
        dh2 = _dot_nn(dgb, wg_ref[...]) + _dot_nn(dub, wu_ref[...])
        dn2 = dh2 * (1.0 + sc2)
        dxr = dn2 * g2
        dx1_ref[...] = dx2 + r2 * (dxr - xr * jnp.mean(dxr * xr, axis=-1, keepdims=True))

        def colsum(v):
            return jnp.sum(v, axis=0, keepdims=True)

        acc_ref[0:1, :] += colsum(dh2)
        acc_ref[1:2, :] += colsum(dh2 * (xr * g2))
        acc_ref[2:3, :] += colsum(dx2 * f)
        acc_ref[3:4, :] += colsum(dn2 * xr)
        acc_ref[4:5, :] += colsum(dout * nf)
        acc_ref[5:6, :] += jnp.broadcast_to(loss, (1, DM))

    c3 = pl.BlockSpec((NCH, TOK, CH), lambda i: (0, i, 0))
    return pl.pallas_call(
        body,
        name="ffn",
        grid=(NTILE,),
        out_shape=[jax.ShapeDtypeStruct((SEQ, DM), BF16), jax.ShapeDtypeStruct((NCH, SEQ, CH), BF16), jax.ShapeDtypeStruct((SEQ, DM), BF16),
                   jax.ShapeDtypeStruct((NCH, SEQ, CH), BF16), jax.ShapeDtypeStruct((NCH, SEQ, CH), BF16),
                   jax.ShapeDtypeStruct((SEQ, DM), F32), jax.ShapeDtypeStruct((8, DM), F32)],
        in_specs=[_tile(DM), _tile(DM), _const((8, 6 * DM)), _const((1, DM)), _const((1, DM)), _HBM, _HBM, _HBM],
        out_specs=[_tile(DM), c3, _tile(DM), c3, c3, _tile(DM), _const((8, DM))],
        scratch_shapes=[pltpu.VMEM((DFF, DM), BF16)] * 3 + [pltpu.SemaphoreType.DMA((3,))],
        compiler_params=pltpu.CompilerParams(dimension_semantics=("arbitrary",), vmem_limit_bytes=VMEM_LIMIT),
    )(x1, tgt, mod8, g2, gf, wg_t, wu_t, wd)


def _bwd_mix_call(dx1, x, y, uag, hc, hd, pbv, mod8, g1, win_t, dww, lng, lnb, wpw, wg, psc, wout, sums):
    hpt = TOK // HALO_C
    ns = len(sums)

    def far_copies(s_refs, f_refs, fs_s, fs_r):
        x_, y_, c_, _ = _place()
        return [_rcopy(s_refs[k].at[q], f_refs[k].at[q], fs_s.at[k, q], fs_r.at[k, q], _flip(x_, y_, c_, r)[0])
                for k in range(ns) for q, r in enumerate((4, 2))]

    def body(dx1_ref, x_ref, y_ref, uag_ref, hc_ref, halo_ref, hd_ref, pb_ref, mod_ref, g1_ref, win_ref, dww_ref, lng_ref, lnb_ref,
             wpw_ref, wg_ref, psc_ref, wout_ref, *rest):
        s_refs = rest[:ns]
        gx_ref, dyb_ref, dycb_ref, dub3_ref, acc_ref, ddw_ref, dwg_ref = rest[ns : ns + 7]
        f_refs = rest[ns + 7 : 2 * ns + 7]
        d_ext, q_ext, hcx, dub_ref, hrot, drot, fs_s, fs_r = rest[2 * ns + 7 :]
        i = pl.program_id(0)
        it = NTILE - 1 - i

        @pl.when(i == 0)
        def _():
            for cp in far_copies(s_refs, f_refs, fs_s, fs_r):
                cp.start()
            d_ext[TOK : TOK + HALO_C, :] = jnp.zeros((HALO_C, CONVW), F32)
            q_ext[TOK : TOK + HALO_P, :] = jnp.zeros((HALO_P, POOLW), F32)
            acc_ref[...] = jnp.zeros((8, DM), F32)
            ddw_ref[...] = jnp.zeros((HALO_C, CONVW), F32)
            dwg_ref[...] = jnp.zeros((len(WINS) * PGD, PGD), F32)

        def colsum(v):
            return jnp.sum(v, axis=0, keepdims=True)

        dx1 = dx1_ref[...]
        x = x_ref[...]
        sh1, sc1, gt1 = mod_ref[0:1, 0:DM], mod_ref[0:1, DM : 2 * DM], mod_ref[0:1, 2 * DM : 3 * DM]
        acc_ref[2:3, :] += colsum(dx1 * y_ref[...])
        dyb = (gt1 * dx1).astype(BF16)
        dyb_ref[...] = dyb
        dycat = _dot_nt(dyb, wout_ref[...])

        hd = hd_ref[...]
        mu = jnp.mean(hd, axis=-1, keepdims=True)
        dlt = hd - mu
        rstd = lax.rsqrt(jnp.mean(dlt * dlt, axis=-1, keepdims=True) + EPS)
        xhat = dlt * rstd
        lng = lng_ref[...]
        hl = xhat * lng + lnb_ref[...]
        sgl = _sig(hl)
        dycb = dycat[:, :CONVW].astype(BF16)
        dycb_ref[...] = dycb
        dhl = _dot_nt(dycb, wpw_ref[...]) * (sgl * (1.0 + hl * (1.0 - sgl)))
        acc_ref[5:6, 0:CONVW] += colsum(dhl)
        acc_ref[4:5, CONVW:DM] += colsum(dhl * xhat)
        dxh = dhl * lng
        dhd = rstd * (dxh - jnp.mean(dxh, axis=-1, keepdims=True) - xhat * jnp.mean(dxh * xhat, axis=-1, keepdims=True))
        acc_ref[4:5, 0:CONVW] += colsum(dhd)

        hcx[0:HALO_C, :] = jnp.where(it == 0, 0.0, halo_ref[...])
        hcx[HALO_C : HALO_C + TOK, :] = hc_ref[...]
        d_ext[0:TOK, :] = dhd
        for b in range(1, 8):
            hrot[b - 1] = hcx[pl.ds(b, TOK + HALO_C - 8), :]
            drot[b - 1] = d_ext[pl.ds(b, TOK + HALO_C - 8), :]

        def tap(base, rot, off):
            a, b = divmod(off, 8)
            return base[pl.ds(8 * a, TOK), :] if b == 0 else rot[b - 1, pl.ds(8 * a, TOK), :]

        dhc = jnp.zeros((TOK, CONVW), F32)
        for k in range(KCONV):
            ddw_ref[k : k + 1, :] += colsum(dhd * tap(hcx, hrot, HALO_C - (KCONV - 1) + k))
            dhc = dhc + dww_ref[k : k + 1, :] * tap(d_ext, drot, KCONV - 1 - k)
        d_ext[TOK : TOK + HALO_C, :] = d_ext[0:HALO_C, :]
        ua, ug = uag_ref[:, 0:CONVW], uag_ref[:, CONVW : 2 * CONVW]
        sgg = _sig(ug)
        dub_ref[:, 0:CONVW] = (dhc * sgg).astype(BF16)
        dub_ref[:, CONVW : 2 * CONVW] = (dhc * ua * sgg * (1.0 - sgg)).astype(BF16)

        tg = it * TOK + lax.broadcasted_iota(jnp.int32, (TOK, 1), 0)
        for g, w in enumerate(WINS):
            ln = slice(PGD * g, PGD * (g + 1))
            wgb = wg_ref[g].astype(BF16)
            pb = pb_ref[:, ln]
            dyp = dycat[:, CONVW + PGD * g : CONVW + PGD * (g + 1)]
            acc_ref[5:6, CONVW + PGD * g : CONVW + PGD * (g + 1)] += colsum(dyp * _dot_nn(pb, wgb))
            dzb = (dyp * psc_ref[:, ln]).astype(BF16)
            dwg_ref[PGD * g : PGD * (g + 1), :] += _dot_tn(pb, dzb)
            dp = _dot_nt(dzb, wgb)
            cnt = jnp.minimum(tg + 1, w).astype(F32)
            q_ext[0:TOK, ln] = dp / cnt
            dv = -dp
            for d in range(w):
                dv = dv + q_ext[pl.ds(d, TOK), ln]
            dub_ref[:, 2 * CONVW + PGD * g : 2 * CONVW + PGD * (g + 1)] = dv.astype(BF16)
        q_ext[TOK : TOK + HALO_P, :] = q_ext[0:HALO_P, :]

        for j in range(3 * CONVW // CH):
            dub3_ref[j] = dub_ref[:, j * CH : (j + 1) * CH]
        dh1 = _dot_nn(dub_ref[...], win_ref[...])
        g1 = g1_ref[...]
        r1 = lax.rsqrt(jnp.mean(x * x, axis=-1, keepdims=True) + EPS)
        xr = x * r1
        acc_ref[0:1, :] += colsum(dh1)
        acc_ref[1:2, :] += colsum(dh1 * (xr * g1))
        dn1 = dh1 * (1.0 + sc1)
        acc_ref[3:4, :] += colsum(dn1 * xr)
        dxr = dn1 * g1
        gx_ref[...] = dx1 + r1 * (dxr - xr * jnp.mean(dxr * xr, axis=-1, keepdims=True))

        @pl.when(i == NTILE - 1)
        def _():
            for cp in far_copies(s_refs, f_refs, fs_s, fs_r):
                cp.wait()

    halo = pl.BlockSpec((HALO_C, CONVW), lambda i: (jnp.maximum((NTILE - 1 - i) * hpt - 1, 0), 0))
    return pl.pallas_call(
        body,
        name="bwd_mix",
        grid=(NTILE,),
        out_shape=[jax.ShapeDtypeStruct((SEQ, DM), F32), jax.ShapeDtypeStruct((SEQ, DM), BF16), jax.ShapeDtypeStruct((SEQ, CONVW), BF16),
                   jax.ShapeDtypeStruct((3 * CONVW // CH, SEQ, CH), BF16), jax.ShapeDtypeStruct((8, DM), F32),
                   jax.ShapeDtypeStruct((HALO_C, CONVW), F32), jax.ShapeDtypeStruct((len(WINS) * PGD, PGD), F32)]
        + [jax.ShapeDtypeStruct(s.shape, s.dtype) for s in sums],
        in_specs=[_tile(DM, True), _tile(DM, True), _tile(DM, True), _tile(2 * CONVW, True), _tile(CONVW, True), halo, _tile(CONVW, True),
                  _tile(POOLW, True), _const((8, 6 * DM)), _const((1, DM)), _const(win_t.shape), _const(dww.shape), _const((1, CONVW)),
                  _const((1, CONVW)), _const(wpw.shape), _const(wg.shape), _const((1, POOLW)), _const(wout.shape)] + [_HBM] * ns,
        out_specs=[_tile(DM, True), _tile(DM, True), _tile(CONVW, True), _tile3(3 * CONVW // CH, True), _const((8, DM)),
                   _const((HALO_C, CONVW)), _const((len(WINS) * PGD, PGD))] + [_HBM] * ns,
        scratch_shapes=[pltpu.VMEM((TOK + HALO_C, CONVW), F32), pltpu.VMEM((TOK + HALO_P, POOLW), F32), pltpu.VMEM((TOK + HALO_C, CONVW), F32),
                        pltpu.VMEM((TOK, 3 * CONVW), BF16)] + [pltpu.VMEM((7, TOK + HALO_C - 8, CONVW), F32)] * 2
        + [pltpu.SemaphoreType.DMA((ns, 2))] * 2,
        compiler_params=pltpu.CompilerParams(dimension_semantics=("arbitrary",), vmem_limit_bytes=VMEM_LIMIT),
    )(dx1, x, y, uag, hc, hc, hd, pbv, mod8, g1, win_t, dww, lng, lnb, wpw, wg, psc, wout, *sums)


CHIPS = (0, 4, 2, 6)


def _wgrad_rs_call(name, a3s, bmap, bs, order, small=None):
    nw = len(a3s)
    nchs = [a.shape[0] for a in a3s]
    rws = [n * CH // NDEV for n in nchs]
    cols = [bs[bmap[k]].shape[1] for k in range(nw)]
    last = small is not None
    srows = small[0].shape[0] if last else 0
    sub = 16
    nb = len(bs)

    def body(*refs):
        pos = 0

        def take(n):
            nonlocal pos
            pos += n
            return refs[pos - n : pos]

        a_refs, b_refs = take(nw), take(nb)
        if last:
            spack_ref, dmodp_ref = take(2)
        o_refs = take(nw)
        if last:
            ssum_ref, gbada_ref, dmy_ref = take(3)
        else:
            cbm_refs = take(nw)
        p_refs = take(nw)
        r1, cb, hb = take(nw), take(nw), take(nw)
        tmp4, abuf, bbuf, obuf = take(4)
        asem, osem, bsem, tsem, d_s, d_r, h_s, h_r = take(8)
        if last:
            r2 = take(nw)
            srecv, mrecv, i_s, i_r, ssem_s, ssem_r, msem_s, msem_r = take(8)
        else:
            (csem,) = take(1)
        x, y, c_, me = _place()
        sib, xn, yn = (_flip(x, y, c_, r)[0] for r in (1, 4, 2))

        def rcopy(src, dst, ss, rs, dev):
            return pltpu.make_async_remote_copy(src_ref=src, dst_ref=dst, send_sem=ss, recv_sem=rs, device_id=dev, device_id_type=MESH)

        sends, kept = [], []
        if last:
            srecv[me] = spack_ref[...]
            mrecv[me] = dmodp_ref[...]
            for r in range(1, NDEV):
                dev, _ = _flip(x, y, c_, r)
                sends.append(rcopy(srecv.at[me], srecv.at[me], ssem_s.at[r], ssem_r.at[r], dev))
                sends.append(rcopy(mrecv.at[me], mrecv.at[me], msem_s.at[r], msem_r.at[r], dev))
            for cp in sends:
                cp.start()

        def relay(k):
            half = rws[k] // 2
            for h in range(2):
                rcopy(cb[k].at[2, pl.ds(h * half, half), :], hb[k].at[h], h_s.at[k, h], h_r.at[k, h], (x, y, c_)).wait_recv()

                def add_half(j, carry, k=k, h=h):
                    rr = pl.ds(pl.multiple_of(j * sub, sub), sub)
                    dst = pl.ds(pl.multiple_of(h * half + j * sub, sub), sub)
                    cb[k][1 - h, dst, :] = (cb[k][1 - h, dst, :].astype(F32) + hb[k][h, rr, :].astype(F32)).astype(BF16)
                    return carry

                lax.fori_loop(0, half // sub, add_half, 0)
            if last:
                for q, dev in enumerate((xn, yn)):
                    cp = rcopy(cb[k].at[q], r2[k].at[q], i_s.at[k, q], i_r.at[k, q], dev)
                    cp.start()
                    sends.append(cp)
            else:
                cp = pltpu.make_async_copy(cb[k].at[pl.ds(0, 2)], cbm_refs[k], csem.at[k])
                cp.start()
                kept.append(cp)

        def mine(k, q):
            _, owner = _flip(x, y, c_, CHIPS[q])
            return _rows(p_refs[k], owner, rws[k]), tmp4.at[q, pl.ds(0, rws[k]), pl.ds(0, cols[k])]

        def fetch(k):
            for q in range(4):
                pltpu.make_async_copy(*mine(k, q), tsem.at[q]).start()

        def presum(k):
            for q in (3, 1, 2, 0):
                src, tmp = mine(k, q)
                rcopy(src, r1[k].at[q], d_s.at[k, q], d_r.at[k, q], (x, y, c_)).wait_recv()
                pltpu.make_async_copy(src, tmp, tsem.at[q]).wait()

                def add_sib(j, carry, k=k, q=q, tmp=tmp):
                    rr = pl.ds(pl.multiple_of(j * sub, sub), sub)
                    t = tmp[rr, :].astype(F32) + r1[k][q, rr, :].astype(F32)
                    if q == 0:
                        o_refs[k][rr, :] = t
                    else:
                        cb[k][q - 1, rr, :] = t.astype(BF16)
                    return carry

                lax.fori_loop(0, rws[k] // sub, add_sib, 0)
                if q == 3:
                    for h, dev in enumerate((xn, yn)):
                        half = rws[k] // 2
                        cp = rcopy(cb[k].at[2, pl.ds(h * half, half), :], hb[k].at[h], h_s.at[k, h], h_r.at[k, h], dev)
                        cp.start()
                        sends.append(cp)

        uses = [bmap[k] for t, k in enumerate(order) if t == 0 or bmap[k] != bmap[order[t - 1]]]

        def b_copy(u):
            return pltpu.make_async_copy(b_refs[uses[u]], bbuf.at[u % 2, :, pl.ds(0, b_refs[uses[u]].shape[1])], bsem.at[u % 2])

        b_copy(0).start()
        u = -1
        for t, k in enumerate(order):
            a_ref, p_ref, rw = a_refs[k], p_refs[k], rws[k]
            if t == 0 or bmap[k] != bmap[order[t - 1]]:
                u += 1
                b_copy(u).wait()
                if u + 1 < len(uses):
                    b_copy(u + 1).start()
            bb = bbuf.at[u % 2, :, pl.ds(0, cols[k])]
            ob = obuf.at[:, :, pl.ds(0, cols[k])]

            def a_copy(m, slot, a_ref=a_ref):
                return pltpu.make_async_copy(a_ref.at[m], abuf.at[slot], asem.at[slot])

            def o_copy(m, slot, ob=ob, p_ref=p_ref):
                return pltpu.make_async_copy(ob.at[slot], p_ref.at[pl.ds(pl.multiple_of(m * CH, CH), CH), :], osem.at[slot])

            a_copy(0, 0).start()

            def step(m, carry, k=k, a_copy=a_copy, o_copy=o_copy, bb=bb, ob=ob):
                slot = lax.rem(m, 2)
                a_copy(m, slot).wait()

                @pl.when(m + 1 < nchs[k])
                def _():
                    a_copy(m + 1, 1 - slot).start()

                @pl.when(m >= 2)
                def _():
                    o_copy(m - 2, slot).wait()

                ob[slot] = _dot_tn(abuf[slot], bb[...]).astype(BF16)
                o_copy(m, slot).start()
                return carry

            lax.fori_loop(0, nchs[k], step, 0)
            for m in (nchs[k] - 2, nchs[k] - 1):
                o_copy(m, m % 2).wait()

            for q, r in enumerate(CHIPS):
                _, owner = _flip(x, y, c_, r | 1)
                cp = rcopy(_rows(p_ref, owner, rw), r1[k].at[q], d_s.at[k, q], d_r.at[k, q], sib)
                cp.start()
                sends.append(cp)
            if t >= 1:
                presum(order[t - 1])
            fetch(k)
            if t >= 2:
                relay(order[t - 2])
        presum(order[-1])
        relay(order[-2])
        relay(order[-1])

        if last:
            for r in range(1, NDEV):
                _, pb = _flip(x, y, c_, r)
                rcopy(srecv.at[me], srecv.at[pb], ssem_s.at[r], ssem_r.at[r], (x, y, c_)).wait_recv()
                rcopy(mrecv.at[me], mrecv.at[pb], msem_s.at[r], msem_r.at[r], (x, y, c_)).wait_recv()
            tot = srecv[0]
            for s in range(1, NDEV):
                tot = tot + srecv[s]
            ssum_ref[...] = tot
            btot = mrecv[0]
            for s in range(1, NDEV):
                btot = btot + mrecv[s]
            gbada_ref[...] = btot
            rowm = lax.broadcasted_iota(jnp.int32, (8, MODW), 0)
            dmy = jnp.zeros((8, MODW), F32)
            for s in range(NDEV):
                drow = jnp.sum(jnp.where(rowm == me, mrecv[s], 0.0), axis=0, keepdims=True)
                dmy = jnp.where(rowm == s, drow, dmy)
            dmy_ref[...] = dmy

            for k in order:
                for q in range(2):
                    rcopy(cb[k].at[q], r2[k].at[q], i_s.at[k, q], i_r.at[k, q], (x, y, c_)).wait_recv()

                def add_far(j, carry, k=k):
                    rr = pl.ds(pl.multiple_of(j * sub, sub), sub)
                    t = o_refs[k][rr, :]
                    for q in range(2):
                        t = t + r2[k][q, rr, :].astype(F32)
                    o_refs[k][rr, :] = t
                    return carry

                lax.fori_loop(0, rws[k] // sub, add_far, 0)
        for cp in sends:
            cp.wait_send()
        for cp in kept:
            cp.wait()

    own = [jax.ShapeDtypeStruct((rws[k], cols[k]), F32) for k in range(nw)]
    parts = [jax.ShapeDtypeStruct((nchs[k] * CH, cols[k]), BF16) for k in range(nw)]
    scratch = [pltpu.VMEM((4, rws[k], cols[k]), BF16) for k in range(nw)]
    scratch += [pltpu.VMEM((3, rws[k], cols[k]), BF16) for k in range(nw)]
    scratch += [pltpu.VMEM((2, rws[k] // 2, cols[k]), BF16) for k in range(nw)]
    scratch += [pltpu.VMEM((4, max(rws), max(cols)), BF16)]
    scratch += [pltpu.VMEM((2, SEQ, CH), BF16), pltpu.VMEM((2, SEQ, max(cols)), BF16), pltpu.VMEM((2, CH, max(cols)), BF16)]
    scratch += [pltpu.SemaphoreType.DMA((2,))] * 3 + [pltpu.SemaphoreType.DMA((4,))]
    scratch += [pltpu.SemaphoreType.DMA((nw, 4))] * 2 + [pltpu.SemaphoreType.DMA((nw, 2))] * 2
    if last:
        out_shape = own + [jax.ShapeDtypeStruct((srows, DM), F32), jax.ShapeDtypeStruct((8, MODW), F32), jax.ShapeDtypeStruct((8, MODW), F32)] + parts
        out_specs = [_VM] * (nw + 3) + [_HBM] * nw
        scratch += [pltpu.VMEM((2, rws[k], cols[k]), BF16) for k in range(nw)]
        scratch += [pltpu.VMEM((NDEV, srows, DM), F32), pltpu.VMEM((NDEV, 8, MODW), F32)]
        scratch += [pltpu.SemaphoreType.DMA((nw, 2))] * 2 + [pltpu.SemaphoreType.DMA((NDEV,))] * 4
        keep = nw + 3
    else:
        out_shape = own + [jax.ShapeDtypeStruct((2, rws[k], cols[k]), BF16) for k in range(nw)] + parts
        out_specs = [_VM] * nw + [_HBM] * (2 * nw)
        scratch += [pltpu.SemaphoreType.DMA((nw,))]
        keep = 2 * nw
    outs = pl.pallas_call(
        body,
        name=name,
        out_shape=out_shape,
        in_specs=[_HBM] * (nw + nb) + [_VM] * (2 if last else 0),
        out_specs=out_specs,
        scratch_shapes=scratch,
        compiler_params=pltpu.CompilerParams(vmem_limit_bytes=60 * 1024 * 1024),
    )(*a3s, *bs, *(small or ()))
    return outs[:keep]


def _adam_update(w, g, m, v):
    m = ADAM_B1 * m + (1.0 - ADAM_B1) * g
    v = ADAM_B2 * v + (1.0 - ADAM_B2) * (g * g)
    m_hat = m / (1.0 - ADAM_B1 ** ADAM_STEP)
    v_hat = v / (1.0 - ADAM_B2 ** ADAM_STEP)
    return -ADAM_LR * (m_hat / (jnp.sqrt(v_hat) + ADAM_EPS) + ADAM_WD * w), m, v


def _adam_ada_call(w, m, v, dmy, cact_t):
    def body(w_ref, m_ref, v_ref, dmy_ref, ct_ref, g_ref, d_ref, mo_ref, vo_ref):
        g = jnp.zeros((DM, MODW), F32)
        for s in range(NDEV):
            g = g + ct_ref[:, s : s + 1] * dmy_ref[s : s + 1, :]
        g_ref[...] = g
        d_ref[...], mo_ref[...], vo_ref[...] = _adam_update(w_ref[...], g, m_ref[...], v_ref[...])

    return pl.pallas_call(
        body,
        name="adam_ada",
        out_shape=[jax.ShapeDtypeStruct(w.shape, F32)] * 4,
        in_specs=[_VM] * 5,
        out_specs=[_VM] * 4,
        compiler_params=pltpu.CompilerParams(vmem_limit_bytes=VMEM_LIMIT),
    )(w, m, v, dmy, cact_t)


def _adam_sum_call(name, ws, owns, fars, ms, vs):
    n = len(ws)

    def body(*refs):
        for i in range(n):
            w, own, far, m, v = (refs[j * n + i] for j in range(5))
            g = own[...] + far[0].astype(F32) + far[1].astype(F32)
            refs[5 * n + i][...] = g
            refs[6 * n + i][...], refs[7 * n + i][...], refs[8 * n + i][...] = _adam_update(w[...], g, m[...], v[...])

    shapes = [jax.ShapeDtypeStruct(w.shape, F32) for w in ws]
    outs = pl.pallas_call(
        body,
        name=name,
        out_shape=shapes * 4,
        in_specs=[_VM] * (5 * n),
        out_specs=[_VM] * (4 * n),
        compiler_params=pltpu.CompilerParams(vmem_limit_bytes=VMEM_LIMIT),
    )(*ws, *owns, *fars, *ms, *vs)
    return outs[:n], outs[n : 2 * n], outs[2 * n : 3 * n], outs[3 * n :]


def _adam_call(name, ws, gs, ms, vs):
    n = len(ws)

    def body(*refs):
        for i in range(n):
            w, g, m, v = (refs[j * n + i][...] for j in range(4))
            refs[4 * n + i][...], refs[5 * n + i][...], refs[6 * n + i][...] = _adam_update(w, g, m, v)

    shapes = [jax.ShapeDtypeStruct(w.shape, F32) for w in ws]
    outs = pl.pallas_call(
        body,
        name=name,
        out_shape=shapes * 3,
        in_specs=[_VM] * (4 * n),
        out_specs=[_VM] * (3 * n),
        compiler_params=pltpu.CompilerParams(vmem_limit_bytes=VMEM_LIMIT),
    )(*ws, *gs, *ms, *vs)
    return outs[:n], outs[n : 2 * n], outs[2 * n :]


def kernel(x, c, w_ada, b_ada, g_norm1, w_in, dw_w, dw_b, conv_ln_g, conv_ln_b, w_conv_pw, w_pool_group, pool_scale, w_out, g_norm2, w_ffn_gate, w_ffn_up, w_ffn_down, g_final, loss_target, m_w_ada, m_b_ada, m_g_norm1, m_w_in, m_dw_w, m_dw_b, m_conv_ln_g, m_conv_ln_b, m_w_conv_pw, m_w_pool_group, m_pool_scale, m_w_out, m_g_norm2, m_w_ffn_gate, m_w_ffn_up, m_w_ffn_down, m_g_final, v_w_ada, v_b_ada, v_g_norm1, v_w_in, v_dw_w, v_dw_b, v_conv_ln_g, v_conv_ln_b, v_w_conv_pw, v_w_pool_group, v_pool_scale, v_w_out, v_g_norm2, v_w_ffn_gate, v_w_ffn_up, v_w_ffn_down, v_g_final):
    me = 4 * lax.axis_index("x") + 2 * lax.axis_index("y") + lax.axis_index("c")
    xs, tgt = x[0], loss_target[0]
    b_my = lax.dynamic_slice(b_ada, (0, me * MODW), (1, MODW))
    win_t, wout, wpw, mod8, cact, dww8 = _gather_call(c, w_ada[0], b_my, dw_w[0], [w_in[0].T, w_out[0], w_conv_pw[0]])
    dww = jnp.pad(jnp.transpose(dww8, (1, 0, 2)).reshape(KCONV, CONVW), ((0, HALO_C - KCONV), (0, 0)))
    wgp = w_pool_group[0]

    h1b, uag, hc, hd, hsb3, pbv, ycb3, y, x1, wg_t, wu_t, wd = _fwd_mix_call(
        xs, mod8, g_norm1, win_t, dww, dw_b, conv_ln_g, conv_ln_b, wpw, wgp, pool_scale, wout,
        [w_ffn_gate[0].T, w_ffn_up[0].T, w_ffn_down[0]])
    h2b, a3, dfb, dg3, du3, dx1, facc = _ffn_call(x1, tgt, mod8, g_norm2, g_final.reshape(1, DM), wg_t, wu_t, wd)
    own_gate, own_up, own_down, s_gate, s_up, s_down = _wgrad_rs_call("wgrad_ffn", [dg3, du3, a3], [0, 0, 1], [h2b, dfb], (2, 0, 1))
    gx, dyb, dycb, dub3, macc, ddw, dwg, f_gate, f_up, f_down = _bwd_mix_call(
        dx1, xs, y, uag, hc, hd, pbv, mod8, g_norm1, win_t, dww, conv_ln_g, conv_ln_b, wpw, wgp, pool_scale, wout, [s_gate, s_up, s_down])
    spack = jnp.concatenate(
        [macc[3:4], facc[3:4], facc[4:5], macc[4:6], facc[5:6], jnp.zeros((2, DM), F32), ddw.reshape(HALO_C // 2, DM), dwg.reshape(-1, DM)], axis=0)
    dmodp = jnp.concatenate([macc[0:3], facc[0:3]], axis=0).reshape(8, MODW)
    g_in_t, g_out, g_pw, ssum, gbada, dmy = _wgrad_rs_call(
        "wgrad_rs", [dub3, ycb3, hsb3], [0, 1, 2], [h1b, dyb, dycb], (0, 1, 2), small=(spack, dmodp))
    g_wada, d_wada, m_wada, v_wada = _adam_ada_call(w_ada[0], m_w_ada[0], v_w_ada[0], dmy, cact.T)
    turned = ("w_in", "w_ffn_gate", "w_ffn_up")
    ffn = ("w_ffn_gate", "w_ffn_up", "w_ffn_down")
    ffn_given = dict(w_ffn_gate=(w_ffn_gate, m_w_ffn_gate, v_w_ffn_gate), w_ffn_up=(w_ffn_up, m_w_ffn_up, v_w_ffn_up),
                     w_ffn_down=(w_ffn_down, m_w_ffn_down, v_w_ffn_down))

    def ffn_work(j):
        return [ffn_given[n][j][0].T if n in turned else ffn_given[n][j][0] for n in ffn]

    g_ffn, d_ffn, m_ffn, v_ffn = _adam_sum_call(
        "adam_ffn", ffn_work(0), [own_gate, own_up, own_down], [f_gate, f_up, f_down], ffn_work(1), ffn_work(2))

    loss = ssum[5, 0]
    ddw_all = ssum[8 : 8 + HALO_C // 2].reshape(HALO_C, CONVW)[:KCONV]
    grads = {
        "w_ada": g_wada,
        "b_ada": gbada.reshape(1, 6 * DM),
        "g_norm1": ssum[0:1],
        "w_in": g_in_t,
        "dw_w": lax.dynamic_slice(ddw_all, (0, me * (CONVW // NDEV)), (KCONV, CONVW // NDEV)),
        "dw_b": ssum[3:4, 0:CONVW],
        "conv_ln_g": ssum[3:4, CONVW:DM],
        "conv_ln_b": ssum[4:5, 0:CONVW],
        "w_conv_pw": g_pw,
        "w_pool_group": ssum[8 + HALO_C // 2 :].reshape(len(WINS) * PGD, PGD),
        "pool_scale": ssum[4:5, CONVW:DM],
        "w_out": g_out,
        "g_norm2": ssum[1:2],
        "w_ffn_gate": g_ffn[0],
        "w_ffn_up": g_ffn[1],
        "w_ffn_down": g_ffn[2],
        "g_final": ssum[2:3],
    }
    given = dict(w_ada=(w_ada, m_w_ada, v_w_ada), b_ada=(b_ada, m_b_ada, v_b_ada), g_norm1=(g_norm1, m_g_norm1, v_g_norm1),
                 w_in=(w_in, m_w_in, v_w_in), dw_w=(dw_w, m_dw_w, v_dw_w), dw_b=(dw_b, m_dw_b, v_dw_b),
                 conv_ln_g=(conv_ln_g, m_conv_ln_g, v_conv_ln_g), conv_ln_b=(conv_ln_b, m_conv_ln_b, v_conv_ln_b),
                 w_conv_pw=(w_conv_pw, m_w_conv_pw, v_w_conv_pw), w_pool_group=(w_pool_group, m_w_pool_group, v_w_pool_group),
                 pool_scale=(pool_scale, m_pool_scale, v_pool_scale), w_out=(w_out, m_w_out, v_w_out), g_norm2=(g_norm2, m_g_norm2, v_g_norm2),
                 w_ffn_gate=(w_ffn_gate, m_w_ffn_gate, v_w_ffn_gate), w_ffn_up=(w_ffn_up, m_w_ffn_up, v_w_ffn_up),
                 w_ffn_down=(w_ffn_down, m_w_ffn_down, v_w_ffn_down), g_final=(g_final, m_g_final, v_g_final))
    names = list(given)
    groups = [["w_in", "w_out", "w_conv_pw"],
              ["b_ada", "g_norm1", "dw_w", "dw_b", "conv_ln_g", "conv_ln_b", "w_pool_group", "pool_scale", "g_norm2", "g_final"]]

    def work(n, a):
        return a[0].T if n in turned else a.reshape(grads[n].shape)

    def full(n, a):
        return a.T[None] if n in turned else a.reshape(given[n][0].shape)

    delta, new_m, new_v = {"w_ada": d_wada}, {"w_ada": m_wada}, {"w_ada": v_wada}
    for i, n in enumerate(ffn):
        delta[n], new_m[n], new_v[n] = d_ffn[i], m_ffn[i], v_ffn[i]
    for gi, grp in enumerate(groups):
        ds, ms, vs = _adam_call(f"adam{gi}", [work(n, given[n][0]) for n in grp], [grads[n] for n in grp],
                                [work(n, given[n][1]) for n in grp], [work(n, given[n][2]) for n in grp])
        for n, d_, m_, v_ in zip(grp, ds, ms, vs):
            delta[n], new_m[n], new_v[n] = d_, m_, v_

    return (loss, gx.reshape(x.shape), *[full(n, grads[n]) for n in names], *[full(n, delta[n]) for n in names],
            *[full(n, new_m[n]) for n in names], *[full(n, new_v[n]) for n in names])
```

```python
import functools

import jax
import jax.numpy as jnp
from jax import lax
from jax.experimental import pallas as pl
from jax.experimental.pallas import tpu as pltpu

F32, BF16 = jnp.float32, jnp.bfloat16
SEQ, DM = 2048, 1024
CONVW, POOLW = 512, 512
KCONV = 31
WINS = (2, 4, 8, 16)
PGD = 128
DFF = 2816
NDEV = 8
MODW = 6 * DM // NDEV
EPS = 1e-6
TOK = 256
NTILE = SEQ // TOK
CH = 256
NCH = DFF // CH
HALO_C, HALO_P = 32, 16
MESH = pl.DeviceIdType.MESH
VMEM_LIMIT = 56 * 1024 * 1024
ADAM_LR, ADAM_B1, ADAM_B2, ADAM_EPS, ADAM_WD, ADAM_STEP = 0.001, 0.9, 0.999, 1e-08, 0.01, 10
HI = lax.Precision.HIGHEST

_VM = pl.BlockSpec(memory_space=pltpu.VMEM)
_HBM = pl.BlockSpec(memory_space=pltpu.HBM)


def _place():
    x, y, c = lax.axis_index("x"), lax.axis_index("y"), lax.axis_index("c")
    return x, y, c, 4 * x + 2 * y + c


def _flip(x, y, c, r):
    px = 1 - x if r & 4 else x
    py = 1 - y if r & 2 else y
    pc = 1 - c if r & 1 else c
    return (px, py, pc), 4 * px + 2 * py + pc


def _rows(ref, blk, n):
    return ref.at[pl.ds(pl.multiple_of(blk * n, 16), n), :]


def _sig(z):
    return jax.nn.sigmoid(z)


def _dot_nt(a, b):
    return lax.dot_general(a, b, (((1,), (1,)), ((), ())), preferred_element_type=F32)


def _dot_nn(a, b):
    return lax.dot_general(a, b, (((1,), (0,)), ((), ())), preferred_element_type=F32)


def _dot_tn(a, b):
    return lax.dot_general(a, b, (((0,), (0,)), ((), ())), preferred_element_type=F32)


def _rcopy(src, dst, ss, rs, dev):
    return pltpu.make_async_remote_copy(src_ref=src, dst_ref=dst, send_sem=ss, recv_sem=rs, device_id=dev, device_id_type=MESH)


def _ag_sems(nw):
    return ([pltpu.SemaphoreType.DMA((nw, 3))] * 2 + [pltpu.SemaphoreType.DMA((nw, 4))] * 2 + [pltpu.SemaphoreType.DMA((nw, 2))] * 2
            + [pltpu.SemaphoreType.DMA((nw,))])


def _ag_plan(sbufs, g_refs, rws, sems):
    wsem_s, wsem_r, fsem_s, fsem_r, hsem_s, hsem_r, lsem = sems
    x, y, c_, me = _place()
    here = (x, y, c_)
    plans = []
    for k, (sb, g, n) in enumerate(zip(sbufs, g_refs, rws)):

        def blk(r, half=None, g=g, n=n):
            b = _flip(x, y, c_, r)[1]
            if half is None:
                return _rows(g, b, n)
            return g.at[pl.ds(pl.multiple_of(b * n + half * (n // 2), 16), n // 2), :]

        def same(ref, ss, rs, j, dev, k=k):
            return _rcopy(ref, ref, ss.at[k, j], rs.at[k, j], dev)

        sib, xn, yn = (_flip(x, y, c_, r)[0] for r in (1, 4, 2))
        plans.append(dict(
            local=pltpu.make_async_copy(sb, blk(0), lsem.at[k]),
            first=[_rcopy(sb, blk(0), wsem_s.at[k, j], wsem_r.at[k, j], dev) for j, dev in enumerate((sib, xn, yn))],
            got=[_rcopy(sb, blk(r), wsem_s.at[k, j], wsem_r.at[k, j], here) for j, r in enumerate((1, 4, 2))],
            passes=[same(blk(4), fsem_s, fsem_r, 0, sib), same(blk(2), fsem_s, fsem_r, 1, sib),
                    same(blk(6, 0), fsem_s, fsem_r, 2, sib), same(blk(6, 1), fsem_s, fsem_r, 3, sib)],
            passed=[same(blk(5), fsem_s, fsem_r, 0, here), same(blk(3), fsem_s, fsem_r, 1, here),
                    same(blk(7, 0), fsem_s, fsem_r, 2, here), same(blk(7, 1), fsem_s, fsem_r, 3, here)],
            halves=[same(blk(4, 0), hsem_s, hsem_r, 0, yn), same(blk(2, 1), hsem_s, hsem_r, 1, xn)],
            halved=[same(blk(6, 0), hsem_s, hsem_r, 0, here), same(blk(6, 1), hsem_s, hsem_r, 1, here)],
        ))
    return plans


def _ag_start(sbufs, g_refs, rws, sems):
    for p in _ag_plan(sbufs, g_refs, rws, sems):
        p["local"].start()
        for cp in p["first"]:
            cp.start()


def _ag_pass_on(sbufs, g_refs, rws, sems):
    plans = _ag_plan(sbufs, g_refs, rws, sems)
    for p in plans:
        for j in (0, 1):
            p["got"][j + 1].wait_recv()
            p["halves"][j].start()
            p["passes"][j].start()
    for p in plans:
        for j in (0, 1):
            p["halved"][j].wait_recv()
            p["passes"][j + 2].start()


def _ag_finish(sbufs, g_refs, rws, sems):
    plans = _ag_plan(sbufs, g_refs, rws, sems)
    for p in plans:
        p["got"][0].wait_recv()
        for cp in p["passed"]:
            cp.wait_recv()
    for p in plans:
        for cp in p["first"] + p["passes"] + p["halves"]:
            cp.wait_send()
        p["local"].wait()


def _gather_call(c, w_ada, b_my, dww, shards):
    nw = len(shards)
    rws = [s.shape[0] for s in shards]

    def body(*refs):
        c_ref, wada_ref, bmy_ref, dww_ref = refs[:4]
        s_refs = refs[4 : 4 + nw]
        g_refs = refs[4 + nw : 4 + 2 * nw]
        mod8_ref, cact_ref, dww8_ref = refs[4 + 2 * nw : 7 + 2 * nw]
        crecv, msend, mrecv = refs[7 + 2 * nw : 10 + 2 * nw]
        sbufs = refs[10 + 2 * nw : 10 + 3 * nw]
        csem_s, csem_r, dsem_s, dsem_r, msem_s, msem_r = refs[10 + 3 * nw : 16 + 3 * nw]
        ag_sems = refs[16 + 3 * nw :]
        x, y, c_, me = _place()

        def rcopy(src, dst, ss, rs, dev):
            return pltpu.make_async_remote_copy(src_ref=src, dst_ref=dst, send_sem=ss, recv_sem=rs, device_id=dev, device_id_type=MESH)

        crecv[me] = jnp.broadcast_to(c_ref[...], (8, DM))
        dww8_ref[me] = dww_ref[...]
        small = []
        for r in range(1, NDEV):
            dev, _ = _flip(x, y, c_, r)
            small.append(rcopy(crecv.at[me], crecv.at[me], csem_s.at[r], csem_r.at[r], dev))
            small.append(rcopy(dww8_ref.at[me], dww8_ref.at[me], dsem_s.at[r], dsem_r.at[r], dev))
        for cp in small:
            cp.start()

        for k in range(nw):
            sbufs[k][...] = s_refs[k][...].astype(BF16)
        _ag_start(sbufs, g_refs, rws, ag_sems)

        rowid = lax.broadcasted_iota(jnp.int32, (8, DM), 0)
        for r in range(1, NDEV):
            _, pb = _flip(x, y, c_, r)
            rcopy(crecv.at[me], crecv.at[pb], csem_s.at[r], csem_r.at[r], (x, y, c_)).wait_recv()
        call = jnp.zeros((8, DM), F32)
        for s in range(NDEV):
            call = jnp.where(rowid == s, crecv[s], call)
        cact = call * _sig(call)
        cact_ref[...] = cact
        modp = jnp.dot(cact, wada_ref[...], precision=HI, preferred_element_type=F32) + bmy_ref[...]
        rowm = lax.broadcasted_iota(jnp.int32, (8, MODW), 0)
        for b in range(NDEV):
            row = jnp.sum(jnp.where(rowm == b, modp, 0.0), axis=0, keepdims=True)
            msend[b] = jnp.broadcast_to(row, (8, MODW))
        mrecv[me] = msend[me]
        msends = []
        for r in range(1, NDEV):
            dev, pb = _flip(x, y, c_, r)
            cp = rcopy(msend.at[pb], mrecv.at[me], msem_s.at[r], msem_r.at[r], dev)
            cp.start()
            msends.append(cp)

        _ag_pass_on(sbufs, g_refs, rws, ag_sems)

        for r in range(1, NDEV):
            _, pb = _flip(x, y, c_, r)
            rcopy(msend.at[pb], mrecv.at[pb], msem_s.at[r], msem_r.at[r], (x, y, c_)).wait_recv()
        for s in range(NDEV):
            mod8_ref[:, s * MODW : (s + 1) * MODW] = mrecv[s]

        _ag_finish(sbufs, g_refs, rws, ag_sems)
        for r in range(1, NDEV):
            _, pb = _flip(x, y, c_, r)
            rcopy(dww8_ref.at[me], dww8_ref.at[pb], dsem_s.at[r], dsem_r.at[r], (x, y, c_)).wait_recv()
        for cp in small + msends:
            cp.wait_send()

    out_shape = [jax.ShapeDtypeStruct((NDEV * s.shape[0], s.shape[1]), BF16) for s in shards]
    out_shape += [
        jax.ShapeDtypeStruct((8, 6 * DM), F32),
        jax.ShapeDtypeStruct((8, DM), F32),
        jax.ShapeDtypeStruct((NDEV,) + dww.shape, F32),
    ]
    scratch = [pltpu.VMEM((NDEV, 8, DM), F32), pltpu.VMEM((NDEV, 8, MODW), F32), pltpu.VMEM((NDEV, 8, MODW), F32)]
    scratch += [pltpu.VMEM(s.shape, BF16) for s in shards]
    scratch += [pltpu.SemaphoreType.DMA((NDEV,))] * 6 + _ag_sems(nw)
    return pl.pallas_call(
        body,
        name="gather",
        out_shape=out_shape,
        in_specs=[_VM] * (4 + nw),
        out_specs=[_HBM] * nw + [_VM] * 3,
        scratch_shapes=scratch,
        compiler_params=pltpu.CompilerParams(vmem_limit_bytes=VMEM_LIMIT),
    )(c, w_ada, b_my, dww, *shards)


def _const(shape):
    return pl.BlockSpec(shape, lambda i: (0,) * len(shape))


def _tile(width, rev=False):
    if rev:
        return pl.BlockSpec((TOK, width), lambda i: (NTILE - 1 - i, 0))
    return pl.BlockSpec((TOK, width), lambda i: (i, 0))


def _tile3(nch, rev=False):
    if rev:
        return pl.BlockSpec((nch, TOK, CH), lambda i: (0, NTILE - 1 - i, 0))
    return pl.BlockSpec((nch, TOK, CH), lambda i: (0, i, 0))


def _norm_mod(x, g, sc, sh):
    r = lax.rsqrt(jnp.mean(x * x, axis=-1, keepdims=True) + EPS)
    xr = x * r
    return r, xr, xr * g * (1.0 + sc) + sh


def _fwd_mix_call(x, mod8, g1, win_t, dww, dwb, lng, lnb, wpw, wg, psc, wout, shards):
    ns = len(shards)
    rws = [s.shape[0] for s in shards]

    def body(x_ref, mod_ref, g1_ref, win_ref, dww_ref, dwb_ref, lng_ref, lnb_ref, wpw_ref, wg_ref, psc_ref, wout_ref, *rest):
        s_refs = rest[:ns]
        h1b_ref, uag_ref, hc_ref, hd_ref, hsb3_ref, pb_ref, ycb3_ref, y_ref, x1_ref = rest[ns : ns + 9]
        g_refs = rest[ns + 9 : 2 * ns + 9]
        hc_ext, up_ext, ycb_ref, stage = rest[2 * ns + 9 : 2 * ns + 13]
        sbufs = rest[2 * ns + 13 : 3 * ns + 13]
        ssem = rest[3 * ns + 13]
        ag_sems = rest[3 * ns + 14 :]
        i = pl.program_id(0)

        @pl.when(i == 0)
        def _():
            for k in range(ns):
                cp = pltpu.make_async_copy(s_refs[k], stage, ssem)
                cp.start()
                cp.wait()
                sbufs[k][...] = stage[...].astype(BF16)
            _ag_start(sbufs, g_refs, rws, ag_sems)
            hc_ext[0:HALO_C, :] = jnp.zeros((HALO_C, CONVW), F32)
            up_ext[0:HALO_P, :] = jnp.zeros((HALO_P, POOLW), F32)

        x = x_ref[...]
        sh1, sc1, gt1 = mod_ref[0:1, 0:DM], mod_ref[0:1, DM : 2 * DM], mod_ref[0:1, 2 * DM : 3 * DM]
        _, _, h1 = _norm_mod(x, g1_ref[...], sc1, sh1)
        h1b = h1.astype(BF16)
        h1b_ref[...] = h1b
        u = _dot_nt(h1b, win_ref[...])
        uag_ref[...] = u[:, : 2 * CONVW]
        hc = u[:, :CONVW] * _sig(u[:, CONVW : 2 * CONVW])
        hc_ref[...] = hc
        hc_ext[HALO_C : HALO_C + TOK, :] = hc
        up_ext[HALO_P : HALO_P + TOK, :] = u[:, 2 * CONVW :]

        acc = jnp.zeros((TOK, CONVW), F32)
        for k in range(KCONV):
            acc = acc + dww_ref[k : k + 1, :] * hc_ext[pl.ds(HALO_C - (KCONV - 1) + k, TOK), :]
        hd = acc + dwb_ref[...]
        hd_ref[...] = hd
        hc_ext[0:HALO_C, :] = hc_ext[TOK : TOK + HALO_C, :]
        mu = jnp.mean(hd, axis=-1, keepdims=True)
        dlt = hd - mu
        rstd = lax.rsqrt(jnp.mean(dlt * dlt, axis=-1, keepdims=True) + EPS)
        hl = dlt * rstd * lng_ref[...] + lnb_ref[...]
        hsb = (hl * _sig(hl)).astype(BF16)
        for j in range(CONVW // CH):
            hsb3_ref[j] = hsb[:, j * CH : (j + 1) * CH]
        ycb_ref[:, 0:CONVW] = _dot_nn(hsb, wpw_ref[...]).astype(BF16)

        tg = i * TOK + lax.broadcasted_iota(jnp.int32, (TOK, 1), 0)
        for g, w in enumerate(WINS):
            ln = slice(PGD * g, PGD * (g + 1))
            v = up_ext[pl.ds(HALO_P, TOK), ln]
            ssum = v
            for d in range(1, w):
                ssum = ssum + up_ext[pl.ds(HALO_P - d, TOK), ln]
            cnt = jnp.minimum(tg + 1, w).astype(F32)
            pb = (ssum / cnt - v).astype(BF16)
            pb_ref[:, ln] = pb
            z = _dot_nn(pb, wg_ref[g].astype(BF16))
            ycb_ref[:, CONVW + PGD * g : CONVW + PGD * (g + 1)] = (z * psc_ref[:, ln]).astype(BF16)
        up_ext[0:HALO_P, :] = up_ext[TOK : TOK + HALO_P, :]

        for j in range(DM // CH):
            ycb3_ref[j] = ycb_ref[:, j * CH : (j + 1) * CH]
        yv = _dot_nn(ycb_ref[...], wout_ref[...])
        y_ref[...] = yv
        x1_ref[...] = x + gt1 * yv

        @pl.when(i == NTILE - 1)
        def _():
            _ag_pass_on(sbufs, g_refs, rws, ag_sems)
            _ag_finish(sbufs, g_refs, rws, ag_sems)

    outs = [(DM, BF16), (2 * CONVW, F32), (CONVW, F32), (CONVW, F32), (-CONVW, BF16), (POOLW, BF16), (-DM, BF16), (DM, F32), (DM, F32)]
    return pl.pallas_call(
        body,
        name="fwd_mix",
        grid=(NTILE,),
        out_shape=[jax.ShapeDtypeStruct((SEQ, w) if w > 0 else (-w // CH, SEQ, CH), d) for w, d in outs]
        + [jax.ShapeDtypeStruct((NDEV * s.shape[0], s.shape[1]), BF16) for s in shards],
        in_specs=[_tile(DM), _const((8, 6 * DM)), _const((1, DM)), _const(win_t.shape), _const(dww.shape), _const((1, CONVW)),
                  _const((1, CONVW)), _const((1, CONVW)), _const(wpw.shape), _const(wg.shape), _const((1, POOLW)), _const(wout.shape)]
        + [_HBM] * ns,
        out_specs=[_tile(w) if w > 0 else _tile3(-w // CH) for w, _ in outs] + [_HBM] * ns,
        scratch_shapes=[pltpu.VMEM((TOK + HALO_C, CONVW), F32), pltpu.VMEM((TOK + HALO_P, POOLW), F32), pltpu.VMEM((TOK, DM), BF16),
                        pltpu.VMEM(shards[0].shape, F32)] + [pltpu.VMEM(s.shape, BF16) for s in shards]
        + [pltpu.SemaphoreType.DMA] + _ag_sems(ns),
        compiler_params=pltpu.CompilerParams(dimension_semantics=("arbitrary",), vmem_limit_bytes=VMEM_LIMIT),
    )(x, mod8, g1, win_t, dww, dwb, lng, lnb, wpw, wg, psc, wout, *shards)


def _ffn_call(x1, tgt, mod8, g2, gf, wg_t, wu_t, wd):
    def body(x1_ref, tgt_ref, mod_ref, g2_ref, gf_ref, wg_hbm, wu_hbm, wd_hbm,
             h2b_ref, a3_ref, dfb_ref, dg3_ref, du3_ref, dx1_ref, acc_ref,
             wg_ref, wu_ref, wd_ref, wsem):
        i = pl.program_id(0)

        @pl.when(i == 0)
        def _():
            cps = [pltpu.make_async_copy(s, d, wsem.at[n]) for n, (s, d) in enumerate(((wg_hbm, wg_ref), (wu_hbm, wu_ref), (wd_hbm, wd_ref)))]
            for cp in cps:
                cp.start()
            acc_ref[...] = jnp.zeros((8, DM), F32)
            for cp in cps:
                cp.wait()

        x1 = x1_ref[...]
        sh2, sc2, gt2 = mod_ref[0:1, 3 * DM : 4 * DM], mod_ref[0:1, 4 * DM : 5 * DM], mod_ref[0:1, 5 * DM : 6 * DM]
        g2 = g2_ref[...]
        r2, xr, h2 = _norm_mod(x1, g2, sc2, sh2)
        h2b = h2.astype(BF16)
        h2b_ref[...] = h2b
        gate = _dot_nt(h2b, wg_ref[...])
        up = _dot_nt(h2b, wu_ref[...])
        ab = (gate * _sig(gate) * up).astype(BF16)
        for j in range(NCH):
            a3_ref[j] = ab[:, j * CH : (j + 1) * CH]
        f = _dot_nn(ab, wd_ref[...])
        x2 = x1 + gt2 * f
        rf = lax.rsqrt(jnp.mean(x2 * x2, axis=-1, keepdims=True) + EPS)
        nf = x2 * rf
        gf_ = gf_ref[...]
        err = nf * gf_ - tgt_ref[...]
        loss = 0.5 * jnp.sum(jnp.sum(err * err, axis=-1, keepdims=True), axis=0, keepdims=True) * (1.0 / DM)
        dout = err * (1.0 / DM)
        dnf = dout * gf_
        dx2 = rf * (dnf - nf * jnp.mean(dnf * nf, axis=-1, keepdims=True))
        dfb = (gt2 * dx2).astype(BF16)
        dfb_ref[...] = dfb
        da = _dot_nt(dfb, wd_ref[...])
        sg = _sig(gate)
        dgb = (da * up * (sg * (1.0 + gate * (1.0 - sg)))).astype(BF16)
        dub = (da * (gate * sg)).astype(BF16)
        for j in range(NCH):
            dg3_ref[j] = dgb[:, j * CH : (j + 1) * CH]
            du3_ref[j] = dub[:, j * CH : (j + 1) * CH]
        dh2 = _dot_nn(dgb, wg_ref[...]) + _dot_nn(dub, wu_ref[...])
        dn2 = dh2 * (1.0 + sc2)
        dxr = dn2 * g2
        dx1_ref[...] = dx2 + r2 * (dxr - xr * jnp.mean(dxr * xr, axis=-1, keepdims=True))

        def colsum(v):
            return jnp.sum(v, axis=0, keepdims=True)

        acc_ref[0:1, :] += colsum(dh2)
        acc_ref[1:2, :] += colsum(dh2 * (xr * g2))
        acc_ref[2:3, :] += colsum(dx2 * f)
        acc_ref[3:4, :] += colsum(dn2 * xr)
        acc_ref[4:5, :] += colsum(dout * nf)
        acc_ref[5:6, :] += jnp.broadcast_to(loss, (1, DM))

    c3 = pl.BlockSpec((NCH, TOK, CH), lambda i: (0, i, 0))
    return pl.pallas_call(
        body,
        name="ffn",
        grid=(NTILE,),
        out_shape=[jax.ShapeDtypeStruct((SEQ, DM), BF16), jax.ShapeDtypeStruct((NCH, SEQ, CH), BF16), jax.ShapeDtypeStruct((SEQ, DM), BF16),
                   jax.ShapeDtypeStruct((NCH, SEQ, CH), BF16), jax.ShapeDtypeStruct((NCH, SEQ, CH), BF16),
                   jax.ShapeDtypeStruct((SEQ, DM), F32), jax.ShapeDtypeStruct((8, DM), F32)],
        in_specs=[_tile(DM), _tile(DM), _const((8, 6 * DM)), _const((1, DM)), _const((1, DM)), _HBM, _HBM, _HBM],
        out_specs=[_tile(DM), c3, _tile(DM), c3, c3, _tile(DM), _const((8, DM))],
        scratch_shapes=[pltpu.VMEM((DFF, DM), BF16)] * 3 + [pltpu.SemaphoreType.DMA((3,))],
        compiler_params=pltpu.CompilerParams(dimension_semantics=("arbitrary",), vmem_limit_bytes=VMEM_LIMIT),
    )(x1, tgt, mod8, g2, gf, wg_t, wu_t, wd)


def _bwd_mix_call(dx1, x, y, uag, hc, hd, pbv, mod8, g1, win_t, dww, lng, lnb, wpw, wg, psc, wout, sums):
    hpt = TOK // HALO_C
    ns = len(sums)

    def far_copies(s_refs, f_refs, fs_s, fs_r):
        x_, y_, c_, _ = _place()
        return [_rcopy(s_refs[k].at[q], f_refs[k].at[q], fs_s.at[k, q], fs_r.at[k, q], _flip(x_, y_, c_, r)[0])
                for k in range(ns) for q, r in enumerate((4, 2))]

    def body(dx1_ref, x_ref, y_ref, uag_ref, hc_ref, halo_ref, hd_ref, pb_ref, mod_ref, g1_ref, win_ref, dww_ref, lng_ref, lnb_ref,
             wpw_ref, wg_ref, psc_ref, wout_ref, *rest):
        s_refs = rest[:ns]
        gx_ref, dyb_ref, dycb_ref, dub3_ref, acc_ref, ddw_ref, dwg_ref = rest[ns : ns + 7]
        f_refs = rest[ns + 7 : 2 * ns + 7]
        d_ext, q_ext, hcx, dub_ref, hrot, drot, fs_s, fs_r = rest[2 * ns + 7 :]
        i = pl.program_id(0)
        it = NTILE - 1 - i

        @pl.when(i == 0)
        def _():
            for cp in far_copies(s_refs, f_refs, fs_s, fs_r):
                cp.start()
            d_ext[TOK : TOK + HALO_C, :] = jnp.zeros((HALO_C, CONVW), F32)
            q_ext[TOK : TOK + HALO_P, :] = jnp.zeros((HALO_P, POOLW), F32)
            acc_ref[...] = jnp.zeros((8, DM), F32)
            ddw_ref[...] = jnp.zeros((HALO_C, CONVW), F32)
            dwg_ref[...] = jnp.zeros((len(WINS) * PGD, PGD), F32)

        def colsum(v):
            return jnp.sum(v, axis=0, keepdims=True)

        dx1 = dx1_ref[...]
        x = x_ref[...]
        sh1, sc1, gt1 = mod_ref[0:1, 0:DM], mod_ref[0:1, DM : 2 * DM], mod_ref[0:1, 2 * DM : 3 * DM]
        acc_ref[2:3, :] += colsum(dx1 * y_ref[...])
        dyb = (gt1 * dx1).astype(BF16)
        dyb_ref[...] = dyb
        dycat = _dot_nt(dyb, wout_ref[...])

        hd = hd_ref[...]
        mu = jnp.mean(hd, axis=-1, keepdims=True)
        dlt = hd - mu
        rstd = lax.rsqrt(jnp.mean(dlt * dlt, axis=-1, keepdims=True) + EPS)
        xhat = dlt * rstd
        lng = lng_ref[...]
        hl = xhat * lng + lnb_ref[...]
        sgl = _sig(hl)
        dycb = dycat[:, :CONVW].astype(BF16)
        dycb_ref[...] = dycb
        dhl = _dot_nt(dycb, wpw_ref[...]) * (sgl * (1.0 + hl * (1.0 - sgl)))
        acc_ref[5:6, 0:CONVW] += colsum(dhl)
        acc_ref[4:5, CONVW:DM] += colsum(dhl * xhat)
        dxh = dhl * lng
        dhd = rstd * (dxh - jnp.mean(dxh, axis=-1, keepdims=True) - xhat * jnp.mean(dxh * xhat, axis=-1, keepdims=True))
        acc_ref[4:5, 0:CONVW] += colsum(dhd)

        hcx[0:HALO_C, :] = jnp.where(it == 0, 0.0, halo_ref[...])
        hcx[HALO_C : HALO_C + TOK, :] = hc_ref[...]
        d_ext[0:TOK, :] = dhd
        for b in range(1, 8):
            hrot[b - 1] = hcx[pl.ds(b, TOK + HALO_C - 8), :]
            drot[b - 1] = d_ext[pl.ds(b, TOK + HALO_C - 8), :]

        def tap(base, rot, off):
            a, b = divmod(off, 8)
            return base[pl.ds(8 * a, TOK), :] if b == 0 else rot[b - 1, pl.ds(8 * a, TOK), :]

        dhc = jnp.zeros((TOK, CONVW), F32)
        for k in range(KCONV):
            ddw_ref[k : k + 1, :] += colsum(dhd * tap(hcx, hrot, HALO_C - (KCONV - 1) + k))
            dhc = dhc + dww_ref[k : k + 1, :] * tap(d_ext, drot, KCONV - 1 - k)
        d_ext[TOK : TOK + HALO_C, :] = d_ext[0:HALO_C, :]
        ua, ug = uag_ref[:, 0:CONVW], uag_ref[:, CONVW : 2 * CONVW]
        sgg = _sig(ug)
        dub_ref[:, 0:CONVW] = (dhc * sgg).astype(BF16)
        dub_ref[:, CONVW : 2 * CONVW] = (dhc * ua * sgg * (1.0 - sgg)).astype(BF16)

        tg = it * TOK + lax.broadcasted_iota(jnp.int32, (TOK, 1), 0)
        for g, w in enumerate(WINS):
            ln = slice(PGD * g, PGD * (g + 1))
            wgb = wg_ref[g].astype(BF16)
            pb = pb_ref[:, ln]
            dyp = dycat[:, CONVW + PGD * g : CONVW + PGD * (g + 1)]
            acc_ref[5:6, CONVW + PGD * g : CONVW + PGD * (g + 1)] += colsum(dyp * _dot_nn(pb, wgb))
            dzb = (dyp * psc_ref[:, ln]).astype(BF16)
            dwg_ref[PGD * g : PGD * (g + 1), :] += _dot_tn(pb, dzb)
            dp = _dot_nt(dzb, wgb)
            cnt = jnp.minimum(tg + 1, w).astype(F32)
            q_ext[0:TOK, ln] = dp / cnt
            dv = -dp
            for d in range(w):
                dv = dv + q_ext[pl.ds(d, TOK), ln]
            dub_ref[:, 2 * CONVW + PGD * g : 2 * CONVW + PGD * (g + 1)] = dv.astype(BF16)
        q_ext[TOK : TOK + HALO_P, :] = q_ext[0:HALO_P, :]

        for j in range(3 * CONVW // CH):
            dub3_ref[j] = dub_ref[:, j * CH : (j + 1) * CH]
        dh1 = _dot_nn(dub_ref[...], win_ref[...])
        g1 = g1_ref[...]
        r1 = lax.rsqrt(jnp.mean(x * x, axis=-1, keepdims=True) + EPS)
        xr = x * r1
        acc_ref[0:1, :] += colsum(dh1)
        acc_ref[1:2, :] += colsum(dh1 * (xr * g1))
        dn1 = dh1 * (1.0 + sc1)
        acc_ref[3:4, :] += colsum(dn1 * xr)
        dxr = dn1 * g1
        gx_ref[...] = dx1 + r1 * (dxr - xr * jnp.mean(dxr * xr, axis=-1, keepdims=True))

        @pl.when(i == NTILE - 1)
        def _():
            for cp in far_copies(s_refs, f_refs, fs_s, fs_r):
                cp.wait()

    halo = pl.BlockSpec((HALO_C, CONVW), lambda i: (jnp.maximum((NTILE - 1 - i) * hpt - 1, 0), 0))
    return pl.pallas_call(
        body,
        name="bwd_mix",
        grid=(NTILE,),
        out_shape=[jax.ShapeDtypeStruct((SEQ, DM), F32), jax.ShapeDtypeStruct((SEQ, DM), BF16), jax.ShapeDtypeStruct((SEQ, CONVW), BF16),
                   jax.ShapeDtypeStruct((3 * CONVW // CH, SEQ, CH), BF16), jax.ShapeDtypeStruct((8, DM), F32),
                   jax.ShapeDtypeStruct((HALO_C, CONVW), F32), jax.ShapeDtypeStruct((len(WINS) * PGD, PGD), F32)]
        + [jax.ShapeDtypeStruct(s.shape, s.dtype) for s in sums],
        in_specs=[_tile(DM, True), _tile(DM, True), _tile(DM, True), _tile(2 * CONVW, True), _tile(CONVW, True), halo, _tile(CONVW, True),
                  _tile(POOLW, True), _const((8, 6 * DM)), _const((1, DM)), _const(win_t.shape), _const(dww.shape), _const((1, CONVW)),
                  _const((1, CONVW)), _const(wpw.shape), _const(wg.shape), _const((1, POOLW)), _const(wout.shape)] + [_HBM] * ns,
        out_specs=[_tile(DM, True), _tile(DM, True), _tile(CONVW, True), _tile3(3 * CONVW // CH, True), _const((8, DM)),
                   _const((HALO_C, CONVW)), _const((len(WINS) * PGD, PGD))] + [_HBM] * ns,
        scratch_shapes=[pltpu.VMEM((TOK + HALO_C, CONVW), F32), pltpu.VMEM((TOK + HALO_P, POOLW), F32), pltpu.VMEM((TOK + HALO_C, CONVW), F32),
                        pltpu.VMEM((TOK, 3 * CONVW), BF16)] + [pltpu.VMEM((7, TOK + HALO_C - 8, CONVW), F32)] * 2
        + [pltpu.SemaphoreType.DMA((ns, 2))] * 2,
        compiler_params=pltpu.CompilerParams(dimension_semantics=("arbitrary",), vmem_limit_bytes=VMEM_LIMIT),
    )(dx1, x, y, uag, hc, hc, hd, pbv, mod8, g1, win_t, dww, lng, lnb, wpw, wg, psc, wout, *sums)


CHIPS = (0, 4, 2, 6)
ASLOTS = 3


def _wgrad_rs_call(name, a3s, bmap, bs, order, small=None):
    nw = len(a3s)
    nchs = [a.shape[0] for a in a3s]
    rws = [n * CH // NDEV for n in nchs]
    cols = [bs[bmap[k]].shape[1] for k in range(nw)]
    last = small is not None
    srows = small[0].shape[0] if last else 0
    sub = 16
    nb = len(bs)

    def body(*refs):
        pos = 0

        def take(n):
            nonlocal pos
            pos += n
            return refs[pos - n : pos]

        a_refs, b_refs = take(nw), take(nb)
        if last:
            spack_ref, dmodp_ref = take(2)
        o_refs = take(nw)
        if last:
            ssum_ref, gbada_ref, dmy_ref = take(3)
        else:
            cbm_refs = take(nw)
        p_refs = take(nw)
        r1, cb, hb = take(nw), take(nw), take(nw)
        tmp4, abuf, bbuf, obuf = take(4)
        asem, osem, bsem, tsem, d_s, d_r, h_s, h_r = take(8)
        if last:
            r2 = take(nw)
            srecv, mrecv, i_s, i_r, ssem_s, ssem_r, msem_s, msem_r = take(8)
        else:
            (csem,) = take(1)
        x, y, c_, me = _place()
        sib, xn, yn = (_flip(x, y, c_, r)[0] for r in (1, 4, 2))

        def rcopy(src, dst, ss, rs, dev):
            return pltpu.make_async_remote_copy(src_ref=src, dst_ref=dst, send_sem=ss, recv_sem=rs, device_id=dev, device_id_type=MESH)

        sends, kept = [], []
        if last:
            srecv[me] = spack_ref[...]
            mrecv[me] = dmodp_ref[...]
            for r in range(1, NDEV):
                dev, _ = _flip(x, y, c_, r)
                sends.append(rcopy(srecv.at[me], srecv.at[me], ssem_s.at[r], ssem_r.at[r], dev))
                sends.append(rcopy(mrecv.at[me], mrecv.at[me], msem_s.at[r], msem_r.at[r], dev))
            for cp in sends:
                cp.start()

        def relay(k):
            half = rws[k] // 2
            for h in range(2):
                rcopy(cb[k].at[2, pl.ds(h * half, half), :], hb[k].at[h], h_s.at[k, h], h_r.at[k, h], (x, y, c_)).wait_recv()

                def add_half(j, carry, k=k, h=h):
                    rr = pl.ds(pl.multiple_of(j * sub, sub), sub)
                    dst = pl.ds(pl.multiple_of(h * half + j * sub, sub), sub)
                    cb[k][1 - h, dst, :] = (cb[k][1 - h, dst, :].astype(F32) + hb[k][h, rr, :].astype(F32)).astype(BF16)
                    return carry

                lax.fori_loop(0, half // sub, add_half, 0)
            if last:
                for q, dev in enumerate((xn, yn)):
                    cp = rcopy(cb[k].at[q], r2[k].at[q], i_s.at[k, q], i_r.at[k, q], dev)
                    cp.start()
                    sends.append(cp)
            else:
                cp = pltpu.make_async_copy(cb[k].at[pl.ds(0, 2)], cbm_refs[k], csem.at[k])
                cp.start()
                kept.append(cp)

        def mine(k, q):
            _, owner = _flip(x, y, c_, CHIPS[q])
            return _rows(p_refs[k], owner, rws[k]), tmp4.at[q, pl.ds(0, rws[k]), pl.ds(0, cols[k])]

        def fetch(k):
            for q in range(4):
                pltpu.make_async_copy(*mine(k, q), tsem.at[q]).start()

        def presum(k):
            for q in (3, 1, 2, 0):
                src, tmp = mine(k, q)
                rcopy(src, r1[k].at[q], d_s.at[k, q], d_r.at[k, q], (x, y, c_)).wait_recv()
                pltpu.make_async_copy(src, tmp, tsem.at[q]).wait()

                def add_sib(j, carry, k=k, q=q, tmp=tmp):
                    rr = pl.ds(pl.multiple_of(j * sub, sub), sub)
                    t = tmp[rr, :].astype(F32) + r1[k][q, rr, :].astype(F32)
                    if q == 0:
                        o_refs[k][rr, :] = t
                    else:
                        cb[k][q - 1, rr, :] = t.astype(BF16)
                    return carry

                lax.fori_loop(0, rws[k] // sub, add_sib, 0)
                if q == 3:
                    for h, dev in enumerate((xn, yn)):
                        half = rws[k] // 2
                        cp = rcopy(cb[k].at[2, pl.ds(h * half, half), :], hb[k].at[h], h_s.at[k, h], h_r.at[k, h], dev)
                        cp.start()
                        sends.append(cp)

        uses = [bmap[k] for t, k in enumerate(order) if t == 0 or bmap[k] != bmap[order[t - 1]]]

        def b_copy(u):
            return pltpu.make_async_copy(b_refs[uses[u]], bbuf.at[u % 2, :, pl.ds(0, b_refs[uses[u]].shape[1])], bsem.at[u % 2])

        b_copy(0).start()
        u = -1
        for t, k in enumerate(order):
            a_ref, p_ref, rw = a_refs[k], p_refs[k], rws[k]
            if t == 0 or bmap[k] != bmap[order[t - 1]]:
                u += 1
                b_copy(u).wait()
                if u + 1 < len(uses):
                    b_copy(u + 1).start()
            bb = bbuf.at[u % 2, :, pl.ds(0, cols[k])]
            ob = obuf.at[:, :, pl.ds(0, cols[k])]

            def a_copy(m, slot, a_ref=a_ref):
                return pltpu.make_async_copy(a_ref.at[m], abuf.at[slot], asem.at[slot])

            def o_copy(m, slot, ob=ob, p_ref=p_ref):
                return pltpu.make_async_copy(ob.at[slot], p_ref.at[pl.ds(pl.multiple_of(m * CH, CH), CH), :], osem.at[slot])

            for m in range(ASLOTS - 1):
                a_copy(m, m).start()

            def step(m, carry, k=k, a_copy=a_copy, o_copy=o_copy, bb=bb, ob=ob):
                slot, aslot = lax.rem(m, 2), lax.rem(m, ASLOTS)
                a_copy(m, aslot).wait()

                @pl.when(m + ASLOTS - 1 < nchs[k])
                def _():
                    a_copy(m + ASLOTS - 1, lax.rem(m + ASLOTS - 1, ASLOTS)).start()

                @pl.when(m >= 2)
                def _():
                    o_copy(m - 2, slot).wait()

                ob[slot] = _dot_tn(abuf[aslot], bb[...]).astype(BF16)
                o_copy(m, slot).start()
                return carry

            lax.fori_loop(0, nchs[k], step, 0)
            for m in (nchs[k] - 2, nchs[k] - 1):
                o_copy(m, m % 2).wait()

            for q, r in enumerate(CHIPS):
                _, owner = _flip(x, y, c_, r | 1)
                cp = rcopy(_rows(p_ref, owner, rw), r1[k].at[q], d_s.at[k, q], d_r.at[k, q], sib)
                cp.start()
                sends.append(cp)
            if t >= 1:
                presum(order[t - 1])
            fetch(k)
            if t >= 2:
                relay(order[t - 2])
        presum(order[-1])
        relay(order[-2])
        relay(order[-1])

        if last:
            for r in range(1, NDEV):
                _, pb = _flip(x, y, c_, r)
                rcopy(srecv.at[me], srecv.at[pb], ssem_s.at[r], ssem_r.at[r], (x, y, c_)).wait_recv()
                rcopy(mrecv.at[me], mrecv.at[pb], msem_s.at[r], msem_r.at[r], (x, y, c_)).wait_recv()
            tot = srecv[0]
            for s in range(1, NDEV):
                tot = tot + srecv[s]
            ssum_ref[...] = tot
            btot = mrecv[0]
            for s in range(1, NDEV):
                btot = btot + mrecv[s]
            gbada_ref[...] = btot
            rowm = lax.broadcasted_iota(jnp.int32, (8, MODW), 0)
            dmy = jnp.zeros((8, MODW), F32)
            for s in range(NDEV):
                drow = jnp.sum(jnp.where(rowm == me, mrecv[s], 0.0), axis=0, keepdims=True)
                dmy = jnp.where(rowm == s, drow, dmy)
            dmy_ref[...] = dmy

            for k in order:
                for q in range(2):
                    rcopy(cb[k].at[q], r2[k].at[q], i_s.at[k, q], i_r.at[k, q], (x, y, c_)).wait_recv()

                def add_far(j, carry, k=k):
                    rr = pl.ds(pl.multiple_of(j * sub, sub), sub)
                    t = o_refs[k][rr, :]
                    for q in range(2):
                        t = t + r2[k][q, rr, :].astype(F32)
                    o_refs[k][rr, :] = t
                    return carry

                lax.fori_loop(0, rws[k] // sub, add_far, 0)
        for cp in sends:
            cp.wait_send()
        for cp in kept:
            cp.wait()

    own = [jax.ShapeDtypeStruct((rws[k], cols[k]), F32) for k in range(nw)]
    parts = [jax.ShapeDtypeStruct((nchs[k] * CH, cols[k]), BF16) for k in range(nw)]
    scratch = [pltpu.VMEM((4, rws[k], cols[k]), BF16) for k in range(nw)]
    scratch += [pltpu.VMEM((3, rws[k], cols[k]), BF16) for k in range(nw)]
    scratch += [pltpu.VMEM((2, rws[k] // 2, cols[k]), BF16) for k in range(nw)]
    scratch += [pltpu.VMEM((4, max(rws), max(cols)), BF16)]
    scratch += [pltpu.VMEM((ASLOTS, SEQ, CH), BF16), pltpu.VMEM((2, SEQ, max(cols)), BF16), pltpu.VMEM((2, CH, max(cols)), BF16)]
    scratch += [pltpu.SemaphoreType.DMA((ASLOTS,))] + [pltpu.SemaphoreType.DMA((2,))] * 2 + [pltpu.SemaphoreType.DMA((4,))]
    scratch += [pltpu.SemaphoreType.DMA((nw, 4))] * 2 + [pltpu.SemaphoreType.DMA((nw, 2))] * 2
    if last:
        out_shape = own + [jax.ShapeDtypeStruct((srows, DM), F32), jax.ShapeDtypeStruct((8, MODW), F32), jax.ShapeDtypeStruct((8, MODW), F32)] + parts
        out_specs = [_VM] * (nw + 3) + [_HBM] * nw
        scratch += [pltpu.VMEM((2, rws[k], cols[k]), BF16) for k in range(nw)]
        scratch += [pltpu.VMEM((NDEV, srows, DM), F32), pltpu.VMEM((NDEV, 8, MODW), F32)]
        scratch += [pltpu.SemaphoreType.DMA((nw, 2))] * 2 + [pltpu.SemaphoreType.DMA((NDEV,))] * 4
        keep = nw + 3
    else:
        out_shape = own + [jax.ShapeDtypeStruct((2, rws[k], cols[k]), BF16) for k in range(nw)] + parts
        out_specs = [_VM] * nw + [_HBM] * (2 * nw)
        scratch += [pltpu.SemaphoreType.DMA((nw,))]
        keep = 2 * nw
    outs = pl.pallas_call(
        body,
        name=name,
        out_shape=out_shape,
        in_specs=[_HBM] * (nw + nb) + [_VM] * (2 if last else 0),
        out_specs=out_specs,
        scratch_shapes=scratch,
        compiler_params=pltpu.CompilerParams(vmem_limit_bytes=60 * 1024 * 1024),
    )(*a3s, *bs, *(small or ()))
    return outs[:keep]


def _adam_update(w, g, m, v):
    m = ADAM_B1 * m + (1.0 - ADAM_B1) * g
    v = ADAM_B2 * v + (1.0 - ADAM_B2) * (g * g)
    m_hat = m / (1.0 - ADAM_B1 ** ADAM_STEP)
    v_hat = v / (1.0 - ADAM_B2 ** ADAM_STEP)
    return -ADAM_LR * (m_hat / (jnp.sqrt(v_hat) + ADAM_EPS) + ADAM_WD * w), m, v


def _adam_ada_call(w, m, v, dmy, cact_t):
    def body(w_ref, m_ref, v_ref, dmy_ref, ct_ref, g_ref, d_ref, mo_ref, vo_ref):
        g = jnp.zeros((DM // 4, MODW), F32)
        for s in range(NDEV):
            g = g + ct_ref[:, s : s + 1] * dmy_ref[s : s + 1, :]
        g_ref[...] = g
        d_ref[...], mo_ref[...], vo_ref[...] = _adam_update(w_ref[...], g, m_ref[...], v_ref[...])

    blk = pl.BlockSpec((DM // 4, MODW), lambda i: (i, 0))
    return pl.pallas_call(
        body,
        name="adam_ada",
        grid=(4,),
        out_shape=[jax.ShapeDtypeStruct(w.shape, F32)] * 4,
        in_specs=[blk] * 3 + [_const((8, MODW)), pl.BlockSpec((DM // 4, NDEV), lambda i: (i, 0))],
        out_specs=[blk] * 4,
        compiler_params=pltpu.CompilerParams(dimension_semantics=("arbitrary",), vmem_limit_bytes=VMEM_LIMIT),
    )(w, m, v, dmy, cact_t)


def _adam_sum_call(name, ws, owns, fars, ms, vs):
    n = len(ws)

    def body(*refs):
        for i in range(n):
            w, own, far, m, v = (refs[j * n + i] for j in range(5))
            g = own[...] + far[0].astype(F32) + far[1].astype(F32)
            refs[5 * n + i][...] = g
            refs[6 * n + i][...], refs[7 * n + i][...], refs[8 * n + i][...] = _adam_update(w[...], g, m[...], v[...])

    shapes = [jax.ShapeDtypeStruct(w.shape, F32) for w in ws]
    blks = [pl.BlockSpec((w.shape[0] // 2, w.shape[1]), lambda i: (i, 0)) for w in ws]
    fblks = [pl.BlockSpec((2, w.shape[0] // 2, w.shape[1]), lambda i: (0, i, 0)) for w in ws]
    outs = pl.pallas_call(
        body,
        name=name,
        grid=(2,),
        out_shape=shapes * 4,
        in_specs=blks * 2 + fblks + blks * 2,
        out_specs=blks * 4,
        compiler_params=pltpu.CompilerParams(dimension_semantics=("arbitrary",), vmem_limit_bytes=VMEM_LIMIT),
    )(*ws, *owns, *fars, *ms, *vs)
    return outs[:n], outs[n : 2 * n], outs[2 * n : 3 * n], outs[3 * n :]


def _adam_call(name, ws, gs, ms, vs):
    n = len(ws)

    def body(*refs):
        for i in range(n):
            w, g, m, v = (refs[j * n + i][...] for j in range(4))
            refs[4 * n + i][...], refs[5 * n + i][...], refs[6 * n + i][...] = _adam_update(w, g, m, v)

    shapes = [jax.ShapeDtypeStruct(w.shape, F32) for w in ws]
    outs = pl.pallas_call(
        body,
        name=name,
        out_shape=shapes * 3,
        in_specs=[_VM] * (4 * n),
        out_specs=[_VM] * (3 * n),
        compiler_params=pltpu.CompilerParams(vmem_limit_bytes=VMEM_LIMIT),
    )(*ws, *gs, *ms, *vs)
    return outs[:n], outs[n : 2 * n], outs[2 * n :]


def kernel(x, c, w_ada, b_ada, g_norm1, w_in, dw_w, dw_b, conv_ln_g, conv_ln_b, w_conv_pw, w_pool_group, pool_scale, w_out, g_norm2, w_ffn_gate, w_ffn_up, w_ffn_down, g_final, loss_target, m_w_ada, m_b_ada, m_g_norm1, m_w_in, m_dw_w, m_dw_b, m_conv_ln_g, m_conv_ln_b, m_w_conv_pw, m_w_pool_group, m_pool_scale, m_w_out, m_g_norm2, m_w_ffn_gate, m_w_ffn_up, m_w_ffn_down, m_g_final, v_w_ada, v_b_ada, v_g_norm1, v_w_in, v_dw_w, v_dw_b, v_conv_ln_g, v_conv_ln_b, v_w_conv_pw, v_w_pool_group, v_pool_scale, v_w_out, v_g_norm2, v_w_ffn_gate, v_w_ffn_up, v_w_ffn_down, v_g_final):
    me = 4 * lax.axis_index("x") + 2 * lax.axis_index("y") + lax.axis_index("c")
    xs, tgt = x[0], loss_target[0]
    b_my = lax.dynamic_slice(b_ada, (0, me * MODW), (1, MODW))
    win_t, wout, wpw, mod8, cact, dww8 = _gather_call(c, w_ada[0], b_my, dw_w[0], [w_in[0].T, w_out[0], w_conv_pw[0]])
    dww = jnp.pad(jnp.transpose(dww8, (1, 0, 2)).reshape(KCONV, CONVW), ((0, HALO_C - KCONV), (0, 0)))
    wgp = w_pool_group[0]

    h1b, uag, hc, hd, hsb3, pbv, ycb3, y, x1, wg_t, wu_t, wd = _fwd_mix_call(
        xs, mod8, g_norm1, win_t, dww, dw_b, conv_ln_g, conv_ln_b, wpw, wgp, pool_scale, wout,
        [w_ffn_gate[0].T, w_ffn_up[0].T, w_ffn_down[0]])
    h2b, a3, dfb, dg3, du3, dx1, facc = _ffn_call(x1, tgt, mod8, g_norm2, g_final.reshape(1, DM), wg_t, wu_t, wd)
    own_gate, own_up, own_down, s_gate, s_up, s_down = _wgrad_rs_call("wgrad_ffn", [dg3, du3, a3], [0, 0, 1], [h2b, dfb], (2, 0, 1))
    gx, dyb, dycb, dub3, macc, ddw, dwg, f_gate, f_up, f_down = _bwd_mix_call(
        dx1, xs, y, uag, hc, hd, pbv, mod8, g_norm1, win_t, dww, conv_ln_g, conv_ln_b, wpw, wgp, pool_scale, wout, [s_gate, s_up, s_down])
    spack = jnp.concatenate(
        [macc[3:4], facc[3:4], facc[4:5], macc[4:6], facc[5:6], jnp.zeros((2, DM), F32), ddw.reshape(HALO_C // 2, DM), dwg.reshape(-1, DM)], axis=0)
    dmodp = jnp.concatenate([macc[0:3], facc[0:3]], axis=0).reshape(8, MODW)
    g_in_t, g_out, g_pw, ssum, gbada, dmy = _wgrad_rs_call(
        "wgrad_rs", [dub3, ycb3, hsb3], [0, 1, 2], [h1b, dyb, dycb], (0, 1, 2), small=(spack, dmodp))
    g_wada, d_wada, m_wada, v_wada = _adam_ada_call(w_ada[0], m_w_ada[0], v_w_ada[0], dmy, cact.T)
    turned = ("w_in", "w_ffn_gate", "w_ffn_up")
    ffn = ("w_ffn_gate", "w_ffn_up", "w_ffn_down")
    ffn_given = dict(w_ffn_gate=(w_ffn_gate, m_w_ffn_gate, v_w_ffn_gate), w_ffn_up=(w_ffn_up, m_w_ffn_up, v_w_ffn_up),
                     w_ffn_down=(w_ffn_down, m_w_ffn_down, v_w_ffn_down))

    def ffn_work(j):
        return [ffn_given[n][j][0].T if n in turned else ffn_given[n][j][0] for n in ffn]

    g_ffn, d_ffn, m_ffn, v_ffn = _adam_sum_call(
        "adam_ffn", ffn_work(0), [own_gate, own_up, own_down], [f_gate, f_up, f_down], ffn_work(1), ffn_work(2))

    loss = ssum[5, 0]
    ddw_all = ssum[8 : 8 + HALO_C // 2].reshape(HALO_C, CONVW)[:KCONV]
    grads = {
        "w_ada": g_wada,
        "b_ada": gbada.reshape(1, 6 * DM),
        "g_norm1": ssum[0:1],
        "w_in": g_in_t,
        "dw_w": lax.dynamic_slice(ddw_all, (0, me * (CONVW // NDEV)), (KCONV, CONVW // NDEV)),
        "dw_b": ssum[3:4, 0:CONVW],
        "conv_ln_g": ssum[3:4, CONVW:DM],
        "conv_ln_b": ssum[4:5, 0:CONVW],
        "w_conv_pw": g_pw,
        "w_pool_group": ssum[8 + HALO_C // 2 :].reshape(len(WINS) * PGD, PGD),
        "pool_scale": ssum[4:5, CONVW:DM],
        "w_out": g_out,
        "g_norm2": ssum[1:2],
        "w_ffn_gate": g_ffn[0],
        "w_ffn_up": g_ffn[1],
        "w_ffn_down": g_ffn[2],
        "g_final": ssum[2:3],
    }
    given = dict(w_ada=(w_ada, m_w_ada, v_w_ada), b_ada=(b_ada, m_b_ada, v_b_ada), g_norm1=(g_norm1, m_g_norm1, v_g_norm1),
                 w_in=(w_in, m_w_in, v_w_in), dw_w=(dw_w, m_dw_w, v_dw_w), dw_b=(dw_b, m_dw_b, v_dw_b),
                 conv_ln_g=(conv_ln_g, m_conv_ln_g, v_conv_ln_g), conv_ln_b=(conv_ln_b, m_conv_ln_b, v_conv_ln_b),
                 w_conv_pw=(w_conv_pw, m_w_conv_pw, v_w_conv_pw), w_pool_group=(w_pool_group, m_w_pool_group, v_w_pool_group),
                 pool_scale=(pool_scale, m_pool_scale, v_pool_scale), w_out=(w_out, m_w_out, v_w_out), g_norm2=(g_norm2, m_g_norm2, v_g_norm2),
                 w_ffn_gate=(w_ffn_gate, m_w_ffn_gate, v_w_ffn_gate), w_ffn_up=(w_ffn_up, m_w_ffn_up, v_w_ffn_up),
                 w_ffn_down=(w_ffn_down, m_w_ffn_down, v_w_ffn_down), g_final=(g_final, m_g_final, v_g_final))
    names = list(given)
    groups = [["w_in", "w_out", "w_conv_pw"],
              ["b_ada", "g_norm1", "dw_w", "dw_b", "conv_ln_g", "conv_ln_b", "w_pool_group", "pool_scale", "g_norm2", "g_final"]]

    def work(n, a):
        return a[0].T if n in turned else a.reshape(grads[n].shape)

    def full(n, a):
        return a.T[None] if n in turned else a.reshape(given[n][0].shape)

    delta, new_m, new_v = {"w_ada": d_wada}, {"w_ada": m_wada}, {"w_ada": v_wada}
    for i, n in enumerate(ffn):
        delta[n], new_m[n], new_v[n] = d_ffn[i], m_ffn[i], v_ffn[i]
    for gi, grp in enumerate(groups):
        ds, ms, vs = _adam_call(f"adam{gi}", [work(n, given[n][0]) for n in grp], [grads[n] for n in grp],
                                [work(n, given[n][1]) for n in grp], [work(n, given[n][2]) for n in grp])
        for n, d_, m_, v_ in zip(grp, ds, ms, vs):
            delta[n], new_m[n], new_v[n] = d_, m_, v_

    return (loss, gx.reshape(x.shape), *[full(n, grads[n]) for n in names], *[full(n, delta[n]) for n in names],
            *[full(n, new_m[n]) for n in names], *[full(n, new_v[n]) for n in names])
```

```python
import functools

import jax
import jax.numpy as jnp
from jax import lax
from jax.experimental import pallas as pl
from jax.experimental.pallas import tpu as pltpu

F32, BF16 = jnp.float32, jnp.bfloat16
SEQ, DM = 2048, 1024
CONVW, POOLW = 512, 512
KCONV = 31
WINS = (2, 4, 8, 16)
PGD = 128
DFF = 2816
NDEV = 8
MODW = 6 * DM // NDEV
EPS = 1e-6
TOK = 256
NTILE = SEQ // TOK
CH = 256
NCH = DFF // CH
HALO_C, HALO_P = 32, 16
MESH = pl.DeviceIdType.MESH
VMEM_LIMIT = 56 * 1024 * 1024
ADAM_LR, ADAM_B1, ADAM_B2, ADAM_EPS, ADAM_WD, ADAM_STEP = 0.001, 0.9, 0.999, 1e-08, 0.01, 10
HI = lax.Precision.HIGHEST

_VM = pl.BlockSpec(memory_space=pltpu.VMEM)
_HBM = pl.BlockSpec(memory_space=pltpu.HBM)


def _place():
    x, y, c = lax.axis_index("x"), lax.axis_index("y"), lax.axis_index("c")
    return x, y, c, 4 * x + 2 * y + c


def _flip(x, y, c, r):
    px = 1 - x if r & 4 else x
    py = 1 - y if r & 2 else y
    pc = 1 - c if r & 1 else c
    return (px, py, pc), 4 * px + 2 * py + pc


def _rows(ref, blk, n):
    return ref.at[pl.ds(pl.multiple_of(blk * n, 16), n), :]


def _sig(z):
    return jax.nn.sigmoid(z)


def _dot_nt(a, b):
    return lax.dot_general(a, b, (((1,), (1,)), ((), ())), preferred_element_type=F32)


def _dot_nn(a, b):
    return lax.dot_general(a, b, (((1,), (0,)), ((), ())), preferred_element_type=F32)


def _dot_tn(a, b):
    return lax.dot_general(a, b, (((0,), (0,)), ((), ())), preferred_element_type=F32)


def _rcopy(src, dst, ss, rs, dev):
    return pltpu.make_async_remote_copy(src_ref=src, dst_ref=dst, send_sem=ss, recv_sem=rs, device_id=dev, device_id_type=MESH)


def _ag_sems(nw):
    return ([pltpu.SemaphoreType.DMA((nw, 3))] * 2 + [pltpu.SemaphoreType.DMA((nw, 4))] * 2 + [pltpu.SemaphoreType.DMA((nw, 2))] * 2
            + [pltpu.SemaphoreType.DMA((nw,))])


def _ag_plan(sbufs, g_refs, rws, sems):
    wsem_s, wsem_r, fsem_s, fsem_r, hsem_s, hsem_r, lsem = sems
    x, y, c_, me = _place()
    here = (x, y, c_)
    plans = []
    for k, (sb, g, n) in enumerate(zip(sbufs, g_refs, rws)):

        def blk(r, half=None, g=g, n=n):
            b = _flip(x, y, c_, r)[1]
            if half is None:
                return _rows(g, b, n)
            return g.at[pl.ds(pl.multiple_of(b * n + half * (n // 2), 16), n // 2), :]

        def same(ref, ss, rs, j, dev, k=k):
            return _rcopy(ref, ref, ss.at[k, j], rs.at[k, j], dev)

        sib, xn, yn = (_flip(x, y, c_, r)[0] for r in (1, 4, 2))
        plans.append(dict(
            local=pltpu.make_async_copy(sb, blk(0), lsem.at[k]),
            first=[_rcopy(sb, blk(0), wsem_s.at[k, j], wsem_r.at[k, j], dev) for j, dev in enumerate((sib, xn, yn))],
            got=[_rcopy(sb, blk(r), wsem_s.at[k, j], wsem_r.at[k, j], here) for j, r in enumerate((1, 4, 2))],
            passes=[same(blk(4), fsem_s, fsem_r, 0, sib), same(blk(2), fsem_s, fsem_r, 1, sib),
                    same(blk(6, 0), fsem_s, fsem_r, 2, sib), same(blk(6, 1), fsem_s, fsem_r, 3, sib)],
            passed=[same(blk(5), fsem_s, fsem_r, 0, here), same(blk(3), fsem_s, fsem_r, 1, here),
                    same(blk(7, 0), fsem_s, fsem_r, 2, here), same(blk(7, 1), fsem_s, fsem_r, 3, here)],
            halves=[same(blk(4, 0), hsem_s, hsem_r, 0, yn), same(blk(2, 1), hsem_s, hsem_r, 1, xn)],
            halved=[same(blk(6, 0), hsem_s, hsem_r, 0, here), same(blk(6, 1), hsem_s, hsem_r, 1, here)],
        ))
    return plans


def _ag_start(sbufs, g_refs, rws, sems):
    for p in _ag_plan(sbufs, g_refs, rws, sems):
        p["local"].start()
        for cp in p["first"]:
            cp.start()


def _ag_pass_on(sbufs, g_refs, rws, sems):
    plans = _ag_plan(sbufs, g_refs, rws, sems)
    for p in plans:
        for j in (0, 1):
            p["got"][j + 1].wait_recv()
            p["halves"][j].start()
            p["passes"][j].start()
    for p in plans:
        for j in (0, 1):
            p["halved"][j].wait_recv()
            p["passes"][j + 2].start()


def _ag_finish(sbufs, g_refs, rws, sems):
    plans = _ag_plan(sbufs, g_refs, rws, sems)
    for p in plans:
        p["got"][0].wait_recv()
        for cp in p["passed"]:
            cp.wait_recv()
    for p in plans:
        for cp in p["first"] + p["passes"] + p["halves"]:
            cp.wait_send()
        p["local"].wait()


def _gather_call(c, w_ada, b_my, dww, shards):
    nw = len(shards)
    rws = [s.shape[0] for s in shards]

    def body(*refs):
        c_ref, wada_ref, bmy_ref, dww_ref = refs[:4]
        s_refs = refs[4 : 4 + nw]
        g_refs = refs[4 + nw : 4 + 2 * nw]
        mod8_ref, cact_ref, dww8_ref = refs[4 + 2 * nw : 7 + 2 * nw]
        crecv, msend, mrecv = refs[7 + 2 * nw : 10 + 2 * nw]
        sbufs = refs[10 + 2 * nw : 10 + 3 * nw]
        csem_s, csem_r, dsem_s, dsem_r, msem_s, msem_r = refs[10 + 3 * nw : 16 + 3 * nw]
        ag_sems = refs[16 + 3 * nw :]
        x, y, c_, me = _place()

        def rcopy(src, dst, ss, rs, dev):
            return pltpu.make_async_remote_copy(src_ref=src, dst_ref=dst, send_sem=ss, recv_sem=rs, device_id=dev, device_id_type=MESH)

        crecv[me] = jnp.broadcast_to(c_ref[...], (8, DM))
        dww8_ref[me] = dww_ref[...]
        small = []
        for r in range(1, NDEV):
            dev, _ = _flip(x, y, c_, r)
            small.append(rcopy(crecv.at[me], crecv.at[me], csem_s.at[r], csem_r.at[r], dev))
            small.append(rcopy(dww8_ref.at[me], dww8_ref.at[me], dsem_s.at[r], dsem_r.at[r], dev))
        for cp in small:
            cp.start()

        for k in range(nw):
            sbufs[k][...] = s_refs[k][...].astype(BF16)
        _ag_start(sbufs, g_refs, rws, ag_sems)

        rowid = lax.broadcasted_iota(jnp.int32, (8, DM), 0)
        for r in range(1, NDEV):
            _, pb = _flip(x, y, c_, r)
            rcopy(crecv.at[me], crecv.at[pb], csem_s.at[r], csem_r.at[r], (x, y, c_)).wait_recv()
        call = jnp.zeros((8, DM), F32)
        for s in range(NDEV):
            call = jnp.where(rowid == s, crecv[s], call)
        cact = call * _sig(call)
        cact_ref[...] = cact
        modp = jnp.dot(cact, wada_ref[...], precision=HI, preferred_element_type=F32) + bmy_ref[...]
        rowm = lax.broadcasted_iota(jnp.int32, (8, MODW), 0)
        for b in range(NDEV):
            row = jnp.sum(jnp.where(rowm == b, modp, 0.0), axis=0, keepdims=True)
            msend[b] = jnp.broadcast_to(row, (8, MODW))
        mrecv[me] = msend[me]
        msends = []
        for r in range(1, NDEV):
            dev, pb = _flip(x, y, c_, r)
            cp = rcopy(msend.at[pb], mrecv.at[me], msem_s.at[r], msem_r.at[r], dev)
            cp.start()
            msends.append(cp)

        _ag_pass_on(sbufs, g_refs, rws, ag_sems)

        for r in range(1, NDEV):
            _, pb = _flip(x, y, c_, r)
            rcopy(msend.at[pb], mrecv.at[pb], msem_s.at[r], msem_r.at[r], (x, y, c_)).wait_recv()
        for s in range(NDEV):
            mod8_ref[:, s * MODW : (s + 1) * MODW] = mrecv[s]

        _ag_finish(sbufs, g_refs, rws, ag_sems)
        for r in range(1, NDEV):
            _, pb = _flip(x, y, c_, r)
            rcopy(dww8_ref.at[me], dww8_ref.at[pb], dsem_s.at[r], dsem_r.at[r], (x, y, c_)).wait_recv()
        for cp in small + msends:
            cp.wait_send()

    out_shape = [jax.ShapeDtypeStruct((NDEV * s.shape[0], s.shape[1]), BF16) for s in shards]
    out_shape += [
        jax.ShapeDtypeStruct((8, 6 * DM), F32),
        jax.ShapeDtypeStruct((8, DM), F32),
        jax.ShapeDtypeStruct((NDEV,) + dww.shape, F32),
    ]
    scratch = [pltpu.VMEM((NDEV, 8, DM), F32), pltpu.VMEM((NDEV, 8, MODW), F32), pltpu.VMEM((NDEV, 8, MODW), F32)]
    scratch += [pltpu.VMEM(s.shape, BF16) for s in shards]
    scratch += [pltpu.SemaphoreType.DMA((NDEV,))] * 6 + _ag_sems(nw)
    return pl.pallas_call(
        body,
        name="gather",
        out_shape=out_shape,
        in_specs=[_VM] * (4 + nw),
        out_specs=[_HBM] * nw + [_VM] * 3,
        scratch_shapes=scratch,
        compiler_params=pltpu.CompilerParams(vmem_limit_bytes=VMEM_LIMIT),
    )(c, w_ada, b_my, dww, *shards)


def _const(shape):
    return pl.BlockSpec(shape, lambda i: (0,) * len(shape))


def _tile(width, rev=False):
    if rev:
        return pl.BlockSpec((TOK, width), lambda i: (NTILE - 1 - i, 0))
    return pl.BlockSpec((TOK, width), lambda i: (i, 0))


def _tile3(nch, rev=False):
    if rev:
        return pl.BlockSpec((nch, CH, TOK), lambda i: (0, 0, NTILE - 1 - i))
    return pl.BlockSpec((nch, CH, TOK), lambda i: (0, 0, i))


def _put3(ref, val):
    for j in range(val.shape[1] // CH):
        ref[j] = val[:, j * CH : (j + 1) * CH].T


def _norm_mod(x, g, sc, sh):
    r = lax.rsqrt(jnp.mean(x * x, axis=-1, keepdims=True) + EPS)
    xr = x * r
    return r, xr, xr * g * (1.0 + sc) + sh


def _fwd_mix_call(x, mod8, g1, win_t, dww, dwb, lng, lnb, wpw, wg, psc, wout, shards):
    ns = len(shards)
    rws = [s.shape[0] for s in shards]

    def body(x_ref, mod_ref, g1_ref, win_ref, dww_ref, dwb_ref, lng_ref, lnb_ref, wpw_ref, wg_ref, psc_ref, wout_ref, *rest):
        s_refs = rest[:ns]
        h1b_ref, uag_ref, hc_ref, hd_ref, hsb3_ref, pb_ref, ycb3_ref, y_ref, x1_ref = rest[ns : ns + 9]
        g_refs = rest[ns + 9 : 2 * ns + 9]
        hc_ext, up_ext, ycb_ref, stage = rest[2 * ns + 9 : 2 * ns + 13]
        sbufs = rest[2 * ns + 13 : 3 * ns + 13]
        ssem = rest[3 * ns + 13]
        ag_sems = rest[3 * ns + 14 :]
        i = pl.program_id(0)

        @pl.when(i == 0)
        def _():
            for k in range(ns):
                cp = pltpu.make_async_copy(s_refs[k], stage, ssem)
                cp.start()
                cp.wait()
                sbufs[k][...] = stage[...].astype(BF16)
            _ag_start(sbufs, g_refs, rws, ag_sems)
            hc_ext[0:HALO_C, :] = jnp.zeros((HALO_C, CONVW), F32)
            up_ext[0:HALO_P, :] = jnp.zeros((HALO_P, POOLW), F32)

        x = x_ref[...]
        sh1, sc1, gt1 = mod_ref[0:1, 0:DM], mod_ref[0:1, DM : 2 * DM], mod_ref[0:1, 2 * DM : 3 * DM]
        _, _, h1 = _norm_mod(x, g1_ref[...], sc1, sh1)
        h1b = h1.astype(BF16)
        h1b_ref[...] = h1b
        u = _dot_nt(h1b, win_ref[...])
        uag_ref[...] = u[:, : 2 * CONVW]
        hc = u[:, :CONVW] * _sig(u[:, CONVW : 2 * CONVW])
        hc_ref[...] = hc
        hc_ext[HALO_C : HALO_C + TOK, :] = hc
        up_ext[HALO_P : HALO_P + TOK, :] = u[:, 2 * CONVW :]

        acc = jnp.zeros((TOK, CONVW), F32)
        for k in range(KCONV):
            acc = acc + dww_ref[k : k + 1, :] * hc_ext[pl.ds(HALO_C - (KCONV - 1) + k, TOK), :]
        hd = acc + dwb_ref[...]
        hd_ref[...] = hd
        hc_ext[0:HALO_C, :] = hc_ext[TOK : TOK + HALO_C, :]
        mu = jnp.mean(hd, axis=-1, keepdims=True)
        dlt = hd - mu
        rstd = lax.rsqrt(jnp.mean(dlt * dlt, axis=-1, keepdims=True) + EPS)
        hl = dlt * rstd * lng_ref[...] + lnb_ref[...]
        hsb = (hl * _sig(hl)).astype(BF16)
        _put3(hsb3_ref, hsb)
        ycb_ref[:, 0:CONVW] = _dot_nn(hsb, wpw_ref[...]).astype(BF16)

        tg = i * TOK + lax.broadcasted_iota(jnp.int32, (TOK, 1), 0)
        for g, w in enumerate(WINS):
            ln = slice(PGD * g, PGD * (g + 1))
            v = up_ext[pl.ds(HALO_P, TOK), ln]
            ssum = v
            for d in range(1, w):
                ssum = ssum + up_ext[pl.ds(HALO_P - d, TOK), ln]
            cnt = jnp.minimum(tg + 1, w).astype(F32)
            pb = (ssum / cnt - v).astype(BF16)
            pb_ref[:, ln] = pb
            z = _dot_nn(pb, wg_ref[g].astype(BF16))
            ycb_ref[:, CONVW + PGD * g : CONVW + PGD * (g + 1)] = (z * psc_ref[:, ln]).astype(BF16)
        up_ext[0:HALO_P, :] = up_ext[TOK : TOK + HALO_P, :]

        _put3(ycb3_ref, ycb_ref[...])
        yv = _dot_nn(ycb_ref[...], wout_ref[...])
        y_ref[...] = yv
        x1_ref[...] = x + gt1 * yv

        @pl.when(i == NTILE - 1)
        def _():
            _ag_pass_on(sbufs, g_refs, rws, ag_sems)
            _ag_finish(sbufs, g_refs, rws, ag_sems)

    outs = [(DM, BF16), (2 * CONVW, F32), (CONVW, F32), (CONVW, F32), (-CONVW, BF16), (POOLW, BF16), (-DM, BF16), (DM, F32), (DM, F32)]
    return pl.pallas_call(
        body,
        name="fwd_mix",
        grid=(NTILE,),
        out_shape=[jax.ShapeDtypeStruct((SEQ, w) if w > 0 else (-w // CH, CH, SEQ), d) for w, d in outs]
        + [jax.ShapeDtypeStruct((NDEV * s.shape[0], s.shape[1]), BF16) for s in shards],
        in_specs=[_tile(DM), _const((8, 6 * DM)), _const((1, DM)), _const(win_t.shape), _const(dww.shape), _const((1, CONVW)),
                  _const((1, CONVW)), _const((1, CONVW)), _const(wpw.shape), _const(wg.shape), _const((1, POOLW)), _const(wout.shape)]
        + [_HBM] * ns,
        out_specs=[_tile(w) if w > 0 else _tile3(-w // CH) for w, _ in outs] + [_HBM] * ns,
        scratch_shapes=[pltpu.VMEM((TOK + HALO_C, CONVW), F32), pltpu.VMEM((TOK + HALO_P, POOLW), F32), pltpu.VMEM((TOK, DM), BF16),
                        pltpu.VMEM(shards[0].shape, F32)] + [pltpu.VMEM(s.shape, BF16) for s in shards]
        + [pltpu.SemaphoreType.DMA] + _ag_sems(ns),
        compiler_params=pltpu.CompilerParams(dimension_semantics=("arbitrary",), vmem_limit_bytes=VMEM_LIMIT),
    )(x, mod8, g1, win_t, dww, dwb, lng, lnb, wpw, wg, psc, wout, *shards)


def _ffn_call(x1, tgt, mod8, g2, gf, wg_t, wu_t, wd):
    def body(x1_ref, tgt_ref, mod_ref, g2_ref, gf_ref, wg_hbm, wu_hbm, wd_hbm,
             h2b_ref, a3_ref, dfb_ref, dg3_ref, du3_ref, dx1_ref, acc_ref,
             wg_ref, wu_ref, wd_ref, wsem):
        i = pl.program_id(0)

        @pl.when(i == 0)
        def _():
            cps = [pltpu.make_async_copy(s, d, wsem.at[n]) for n, (s, d) in enumerate(((wg_hbm, wg_ref), (wu_hbm, wu_ref), (wd_hbm, wd_ref)))]
            for cp in cps:
                cp.start()
            acc_ref[...] = jnp.zeros((8, DM), F32)
            for cp in cps:
                cp.wait()

        x1 = x1_ref[...]
        sh2, sc2, gt2 = mod_ref[0:1, 3 * DM : 4 * DM], mod_ref[0:1, 4 * DM : 5 * DM], mod_ref[0:1, 5 * DM : 6 * DM]
        g2 = g2_ref[...]
        r2, xr, h2 = _norm_mod(x1, g2, sc2, sh2)
        h2b = h2.astype(BF16)
        h2b_ref[...] = h2b
        gate = _dot_nt(h2b, wg_ref[...])
        up = _dot_nt(h2b, wu_ref[...])
        ab = (gate * _sig(gate) * up).astype(BF16)
        _put3(a3_ref, ab)
        f = _dot_nn(ab, wd_ref[...])
        x2 = x1 + gt2 * f
        rf = lax.rsqrt(jnp.mean(x2 * x2, axis=-1, keepdims=True) + EPS)
        nf = x2 * rf
        gf_ = gf_ref[...]
        err = nf * gf_ - tgt_ref[...]
        loss = 0.5 * jnp.sum(jnp.sum(err * err, axis=-1, keepdims=True), axis=0, keepdims=True) * (1.0 / DM)
        dout = err * (1.0 / DM)
        dnf = dout * gf_
        dx2 = rf * (dnf - nf * jnp.mean(dnf * nf, axis=-1, keepdims=True))
        dfb = (gt2 * dx2).astype(BF16)
        dfb_ref[...] = dfb
        da = _dot_nt(dfb, wd_ref[...])
        sg = _sig(gate)
        dgb = (da * up * (sg * (1.0 + gate * (1.0 - sg)))).astype(BF16)
        dub = (da * (gate * sg)).astype(BF16)
        _put3(dg3_ref, dgb)
        _put3(du3_ref, dub)
        dh2 = _dot_nn(dgb, wg_ref[...]) + _dot_nn(dub, wu_ref[...])
        dn2 = dh2 * (1.0 + sc2)
        dxr = dn2 * g2
        dx1_ref[...] = dx2 + r2 * (dxr - xr * jnp.mean(dxr * xr, axis=-1, keepdims=True))

        def colsum(v):
            return jnp.sum(v, axis=0, keepdims=True)

        acc_ref[0:1, :] += colsum(dh2)
        acc_ref[1:2, :] += colsum(dh2 * (xr * g2))
        acc_ref[2:3, :] += colsum(dx2 * f)
        acc_ref[3:4, :] += colsum(dn2 * xr)
        acc_ref[4:5, :] += colsum(dout * nf)
        acc_ref[5:6, :] += jnp.broadcast_to(loss, (1, DM))

    c3 = _tile3(NCH)
    return pl.pallas_call(
        body,
        name="ffn",
        grid=(NTILE,),
        out_shape=[jax.ShapeDtypeStruct((SEQ, DM), BF16), jax.ShapeDtypeStruct((NCH, CH, SEQ), BF16), jax.ShapeDtypeStruct((SEQ, DM), BF16),
                   jax.ShapeDtypeStruct((NCH, CH, SEQ), BF16), jax.ShapeDtypeStruct((NCH, CH, SEQ), BF16),
                   jax.ShapeDtypeStruct((SEQ, DM), F32), jax.ShapeDtypeStruct((8, DM), F32)],
        in_specs=[_tile(DM), _tile(DM), _const((8, 6 * DM)), _const((1, DM)), _const((1, DM)), _HBM, _HBM, _HBM],
        out_specs=[_tile(DM), c3, _tile(DM), c3, c3, _tile(DM), _const((8, DM))],
        scratch_shapes=[pltpu.VMEM((DFF, DM), BF16)] * 3 + [pltpu.SemaphoreType.DMA((3,))],
        compiler_params=pltpu.CompilerParams(dimension_semantics=("arbitrary",), vmem_limit_bytes=VMEM_LIMIT),
    )(x1, tgt, mod8, g2, gf, wg_t, wu_t, wd)


def _bwd_mix_call(dx1, x, y, uag, hc, hd, pbv, mod8, g1, win_t, dww, lng, lnb, wpw, wg, psc, wout, sums):
    hpt = TOK // HALO_C
    ns = len(sums)

    def far_copies(s_refs, f_refs, fs_s, fs_r):
        x_, y_, c_, _ = _place()
        return [_rcopy(s_refs[k].at[q], f_refs[k].at[q], fs_s.at[k, q], fs_r.at[k, q], _flip(x_, y_, c_, r)[0])
                for k in range(ns) for q, r in enumerate((4, 2, 6)[: sums[k].shape[0]])]

    def body(dx1_ref, x_ref, y_ref, uag_ref, hc_ref, halo_ref, hd_ref, pb_ref, mod_ref, g1_ref, win_ref, dww_ref, lng_ref, lnb_ref,
             wpw_ref, wg_ref, psc_ref, wout_ref, *rest):
        s_refs = rest[:ns]
        gx_ref, dyb_ref, dycb_ref, dub3_ref, acc_ref, ddw_ref, dwg_ref = rest[ns : ns + 7]
        f_refs = rest[ns + 7 : 2 * ns + 7]
        d_ext, q_ext, hcx, dub_ref, hrot, drot, fs_s, fs_r = rest[2 * ns + 7 :]
        i = pl.program_id(0)
        it = NTILE - 1 - i

        @pl.when(i == 0)
        def _():
            for cp in far_copies(s_refs, f_refs, fs_s, fs_r):
                cp.start()
            d_ext[TOK : TOK + HALO_C, :] = jnp.zeros((HALO_C, CONVW), F32)
            q_ext[TOK : TOK + HALO_P, :] = jnp.zeros((HALO_P, POOLW), F32)
            acc_ref[...] = jnp.zeros((8, DM), F32)
            ddw_ref[...] = jnp.zeros((HALO_C, CONVW), F32)
            dwg_ref[...] = jnp.zeros((len(WINS) * PGD, PGD), F32)

        def colsum(v):
            return jnp.sum(v, axis=0, keepdims=True)

        dx1 = dx1_ref[...]
        x = x_ref[...]
        sh1, sc1, gt1 = mod_ref[0:1, 0:DM], mod_ref[0:1, DM : 2 * DM], mod_ref[0:1, 2 * DM : 3 * DM]
        acc_ref[2:3, :] += colsum(dx1 * y_ref[...])
        dyb = (gt1 * dx1).astype(BF16)
        dyb_ref[...] = dyb
        dycat = _dot_nt(dyb, wout_ref[...])

        hd = hd_ref[...]
        mu = jnp.mean(hd, axis=-1, keepdims=True)
        dlt = hd - mu
        rstd = lax.rsqrt(jnp.mean(dlt * dlt, axis=-1, keepdims=True) + EPS)
        xhat = dlt * rstd
        lng = lng_ref[...]
        hl = xhat * lng + lnb_ref[...]
        sgl = _sig(hl)
        dycb = dycat[:, :CONVW].astype(BF16)
        dycb_ref[...] = dycb
        dhl = _dot_nt(dycb, wpw_ref[...]) * (sgl * (1.0 + hl * (1.0 - sgl)))
        acc_ref[5:6, 0:CONVW] += colsum(dhl)
        acc_ref[4:5, CONVW:DM] += colsum(dhl * xhat)
        dxh = dhl * lng
        dhd = rstd * (dxh - jnp.mean(dxh, axis=-1, keepdims=True) - xhat * jnp.mean(dxh * xhat, axis=-1, keepdims=True))
        acc_ref[4:5, 0:CONVW] += colsum(dhd)

        hcx[0:HALO_C, :] = jnp.where(it == 0, 0.0, halo_ref[...])
        hcx[HALO_C : HALO_C + TOK, :] = hc_ref[...]
        d_ext[0:TOK, :] = dhd
        for b in range(1, 8):
            hrot[b - 1] = hcx[pl.ds(b, TOK + HALO_C - 8), :]
            drot[b - 1] = d_ext[pl.ds(b, TOK + HALO_C - 8), :]

        def tap(base, rot, off):
            a, b = divmod(off, 8)
            return base[pl.ds(8 * a, TOK), :] if b == 0 else rot[b - 1, pl.ds(8 * a, TOK), :]

        dhc = jnp.zeros((TOK, CONVW), F32)
        for k in range(KCONV):
            ddw_ref[k : k + 1, :] += colsum(dhd * tap(hcx, hrot, HALO_C - (KCONV - 1) + k))
            dhc = dhc + dww_ref[k : k + 1, :] * tap(d_ext, drot, KCONV - 1 - k)
        d_ext[TOK : TOK + HALO_C, :] = d_ext[0:HALO_C, :]
        ua, ug = uag_ref[:, 0:CONVW], uag_ref[:, CONVW : 2 * CONVW]
        sgg = _sig(ug)
        dub_ref[:, 0:CONVW] = (dhc * sgg).astype(BF16)
        dub_ref[:, CONVW : 2 * CONVW] = (dhc * ua * sgg * (1.0 - sgg)).astype(BF16)

        tg = it * TOK + lax.broadcasted_iota(jnp.int32, (TOK, 1), 0)
        for g, w in enumerate(WINS):
            ln = slice(PGD * g, PGD * (g + 1))
            wgb = wg_ref[g].astype(BF16)
            pb = pb_ref[:, ln]
            dyp = dycat[:, CONVW + PGD * g : CONVW + PGD * (g + 1)]
            acc_ref[5:6, CONVW + PGD * g : CONVW + PGD * (g + 1)] += colsum(dyp * _dot_nn(pb, wgb))
            dzb = (dyp * psc_ref[:, ln]).astype(BF16)
            dwg_ref[PGD * g : PGD * (g + 1), :] += _dot_tn(pb, dzb)
            dp = _dot_nt(dzb, wgb)
            cnt = jnp.minimum(tg + 1, w).astype(F32)
            q_ext[0:TOK, ln] = dp / cnt
            dv = -dp
            for d in range(w):
                dv = dv + q_ext[pl.ds(d, TOK), ln]
            dub_ref[:, 2 * CONVW + PGD * g : 2 * CONVW + PGD * (g + 1)] = dv.astype(BF16)
        q_ext[TOK : TOK + HALO_P, :] = q_ext[0:HALO_P, :]

        _put3(dub3_ref, dub_ref[...])
        dh1 = _dot_nn(dub_ref[...], win_ref[...])
        g1 = g1_ref[...]
        r1 = lax.rsqrt(jnp.mean(x * x, axis=-1, keepdims=True) + EPS)
        xr = x * r1
        acc_ref[0:1, :] += colsum(dh1)
        acc_ref[1:2, :] += colsum(dh1 * (xr * g1))
        dn1 = dh1 * (1.0 + sc1)
        acc_ref[3:4, :] += colsum(dn1 * xr)
        dxr = dn1 * g1
        gx_ref[...] = dx1 + r1 * (dxr - xr * jnp.mean(dxr * xr, axis=-1, keepdims=True))

        @pl.when(i == NTILE - 1)
        def _():
            for cp in far_copies(s_refs, f_refs, fs_s, fs_r):
                cp.wait()

    halo = pl.BlockSpec((HALO_C, CONVW), lambda i: (jnp.maximum((NTILE - 1 - i) * hpt - 1, 0), 0))
    return pl.pallas_call(
        body,
        name="bwd_mix",
        grid=(NTILE,),
        out_shape=[jax.ShapeDtypeStruct((SEQ, DM), F32), jax.ShapeDtypeStruct((SEQ, DM), BF16), jax.ShapeDtypeStruct((SEQ, CONVW), BF16),
                   jax.ShapeDtypeStruct((3 * CONVW // CH, CH, SEQ), BF16), jax.ShapeDtypeStruct((8, DM), F32),
                   jax.ShapeDtypeStruct((HALO_C, CONVW), F32), jax.ShapeDtypeStruct((len(WINS) * PGD, PGD), F32)]
        + [jax.ShapeDtypeStruct(s.shape, s.dtype) for s in sums],
        in_specs=[_tile(DM, True), _tile(DM, True), _tile(DM, True), _tile(2 * CONVW, True), _tile(CONVW, True), halo, _tile(CONVW, True),
                  _tile(POOLW, True), _const((8, 6 * DM)), _const((1, DM)), _const(win_t.shape), _const(dww.shape), _const((1, CONVW)),
                  _const((1, CONVW)), _const(wpw.shape), _const(wg.shape), _const((1, POOLW)), _const(wout.shape)] + [_HBM] * ns,
        out_specs=[_tile(DM, True), _tile(DM, True), _tile(CONVW, True), _tile3(3 * CONVW // CH, True), _const((8, DM)),
                   _const((HALO_C, CONVW)), _const((len(WINS) * PGD, PGD))] + [_HBM] * ns,
        scratch_shapes=[pltpu.VMEM((TOK + HALO_C, CONVW), F32), pltpu.VMEM((TOK + HALO_P, POOLW), F32), pltpu.VMEM((TOK + HALO_C, CONVW), F32),
                        pltpu.VMEM((TOK, 3 * CONVW), BF16)] + [pltpu.VMEM((7, TOK + HALO_C - 8, CONVW), F32)] * 2
        + [pltpu.SemaphoreType.DMA((ns, 3))] * 2,
        compiler_params=pltpu.CompilerParams(dimension_semantics=("arbitrary",), vmem_limit_bytes=VMEM_LIMIT),
    )(dx1, x, y, uag, hc, hc, hd, pbv, mod8, g1, win_t, dww, lng, lnb, wpw, wg, psc, wout, *sums)


CHIPS = (0, 4, 2, 6)
ASLOTS = 3


def _wgrad_rs_call(name, a3s, bmap, bs, order, small=None):
    nw = len(a3s)
    nchs = [a.shape[0] for a in a3s]
    rws = [n * CH // NDEV for n in nchs]
    cols = [bs[bmap[k]].shape[1] for k in range(nw)]
    last = small is not None
    whole = None if last else order[-1]
    srows, brows = (small[0].shape[0], small[1].shape[0]) if last else (0, 0)
    sub = 16
    nb = len(bs)

    def body(*refs):
        pos = 0

        def take(n):
            nonlocal pos
            pos += n
            return refs[pos - n : pos]

        a_refs, b_refs = take(nw), take(nb)
        if last:
            spack_ref, bulk_ref, dmodp_ref = take(3)
        o_refs = take(nw)
        if last:
            ssum_ref, bsum_ref, gbada_ref, dmy_ref = take(4)
        else:
            cbm_refs = take(nw)
        p_refs = take(nw)
        r1, cb, hb = take(nw), take(nw), take(nw)
        tmp4, abuf, bbuf, obuf = take(4)
        asem, osem, bsem, tsem, d_s, d_r, h_s, h_r = take(8)
        if last:
            r2 = take(nw)
            srecv, brecv, mrecv, i_s, i_r, ssem_s, ssem_r, msem_s, msem_r, bsem_s, bsem_r, fsem_s, fsem_r = take(13)
        else:
            (csem,) = take(1)
        x, y, c_, me = _place()
        sib, xn, yn = (_flip(x, y, c_, r)[0] for r in (1, 4, 2))

        def rcopy(src, dst, ss, rs, dev):
            return pltpu.make_async_remote_copy(src_ref=src, dst_ref=dst, send_sem=ss, recv_sem=rs, device_id=dev, device_id_type=MESH)

        sends, kept = [], []
        if last:
            srecv[me] = spack_ref[...]
            brecv[me] = bulk_ref[...]
            mrecv[me] = dmodp_ref[...]
            for r in range(1, NDEV):
                dev, _ = _flip(x, y, c_, r)
                sends.append(rcopy(srecv.at[me], srecv.at[me], ssem_s.at[r], ssem_r.at[r], dev))
                sends.append(rcopy(mrecv.at[me], mrecv.at[me], msem_s.at[r], msem_r.at[r], dev))
            for j, r in enumerate((1, 4, 2, 6)):
                sends.append(rcopy(brecv.at[me], brecv.at[me], bsem_s.at[j], bsem_r.at[j], _flip(x, y, c_, r)[0]))
            for cp in sends:
                cp.start()

        def pass_bulk():
            for j, r in enumerate((4, 2, 6)):
                blk = brecv.at[_flip(x, y, c_, r)[1]]
                rcopy(blk, blk, bsem_s.at[j + 1], bsem_r.at[j + 1], (x, y, c_)).wait_recv()
                cp = rcopy(blk, blk, fsem_s.at[j], fsem_r.at[j], sib)
                cp.start()
                sends.append(cp)

        def relay(k):
            if k == whole:
                return
            half = rws[k] // 2
            for h in range(2):
                rcopy(cb[k].at[2, pl.ds(h * half, half), :], hb[k].at[h], h_s.at[k, h], h_r.at[k, h], (x, y, c_)).wait_recv()

                def add_half(j, carry, k=k, h=h):
                    rr = pl.ds(pl.multiple_of(j * sub, sub), sub)
                    dst = pl.ds(pl.multiple_of(h * half + j * sub, sub), sub)
                    cb[k][1 - h, dst, :] = (cb[k][1 - h, dst, :].astype(F32) + hb[k][h, rr, :].astype(F32)).astype(BF16)
                    return carry

                lax.fori_loop(0, half // sub, add_half, 0)
            if last:
                for q, dev in enumerate((xn, yn)):
                    cp = rcopy(cb[k].at[q], r2[k].at[q], i_s.at[k, q], i_r.at[k, q], dev)
                    cp.start()
                    sends.append(cp)
            else:
                cp = pltpu.make_async_copy(cb[k].at[pl.ds(0, 2)], cbm_refs[k], csem.at[k])
                cp.start()
                kept.append(cp)

        def mine(k, q):
            _, owner = _flip(x, y, c_, CHIPS[q])
            return _rows(p_refs[k], owner, rws[k]), tmp4.at[q, pl.ds(0, rws[k]), pl.ds(0, cols[k])]

        def fetch(k):
            for q in range(4):
                pltpu.make_async_copy(*mine(k, q), tsem.at[q]).start()

        def presum(k):
            for q in (3, 1, 2, 0):
                src, tmp = mine(k, q)
                rcopy(src, r1[k].at[q], d_s.at[k, q], d_r.at[k, q], (x, y, c_)).wait_recv()
                pltpu.make_async_copy(src, tmp, tsem.at[q]).wait()

                def add_sib(j, carry, k=k, q=q, tmp=tmp):
                    rr = pl.ds(pl.multiple_of(j * sub, sub), sub)
                    t = tmp[rr, :].astype(F32) + r1[k][q, rr, :].astype(F32)
                    if q == 0:
                        o_refs[k][rr, :] = t
                    else:
                        cb[k][q - 1, rr, :] = t.astype(BF16)
                    return carry

                lax.fori_loop(0, rws[k] // sub, add_sib, 0)
                if q == 3 and k != whole:
                    for h, dev in enumerate((xn, yn)):
                        half = rws[k] // 2
                        cp = rcopy(cb[k].at[2, pl.ds(h * half, half), :], hb[k].at[h], h_s.at[k, h], h_r.at[k, h], dev)
                        cp.start()
                        sends.append(cp)
            if k == whole:
                cp = pltpu.make_async_copy(cb[k], cbm_refs[k], csem.at[k])
                cp.start()
                kept.append(cp)

        uses = [bmap[k] for t, k in enumerate(order) if t == 0 or bmap[k] != bmap[order[t - 1]]]

        def b_copy(u):
            return pltpu.make_async_copy(b_refs[uses[u]], bbuf.at[u % 2, :, pl.ds(0, b_refs[uses[u]].shape[1])], bsem.at[u % 2])

        b_copy(0).start()
        u = -1
        for t, k in enumerate(order):
            a_ref, p_ref, rw = a_refs[k], p_refs[k], rws[k]
            if t == 0 or bmap[k] != bmap[order[t - 1]]:
                u += 1
                b_copy(u).wait()
                if u + 1 < len(uses):
                    b_copy(u + 1).start()
            bb = bbuf.at[u % 2, :, pl.ds(0, cols[k])]
            ob = obuf.at[:, :, pl.ds(0, cols[k])]

            def a_copy(m, slot, a_ref=a_ref):
                return pltpu.make_async_copy(a_ref.at[m], abuf.at[slot], asem.at[slot])

            def o_copy(m, slot, ob=ob, p_ref=p_ref):
                return pltpu.make_async_copy(ob.at[slot], p_ref.at[pl.ds(pl.multiple_of(m * CH, CH), CH), :], osem.at[slot])

            for m in range(ASLOTS - 1):
                a_copy(m, m).start()

            def step(m, carry, k=k, a_copy=a_copy, o_copy=o_copy, bb=bb, ob=ob):
                slot, aslot = lax.rem(m, 2), lax.rem(m, ASLOTS)
                a_copy(m, aslot).wait()

                @pl.when(m + ASLOTS - 1 < nchs[k])
                def _():
                    a_copy(m + ASLOTS - 1, lax.rem(m + ASLOTS - 1, ASLOTS)).start()

                @pl.when(m >= 2)
                def _():
                    o_copy(m - 2, slot).wait()

                ob[slot] = _dot_nn(abuf[aslot], bb[...]).astype(BF16)
                o_copy(m, slot).start()
                return carry

            lax.fori_loop(0, nchs[k], step, 0)
            for m in (nchs[k] - 2, nchs[k] - 1):
                o_copy(m, m % 2).wait()

            for q, r in enumerate(CHIPS):
                _, owner = _flip(x, y, c_, r | 1)
                cp = rcopy(_rows(p_ref, owner, rw), r1[k].at[q], d_s.at[k, q], d_r.at[k, q], sib)
                cp.start()
                sends.append(cp)
            if last and t == 0:
                pass_bulk()
            if t >= 1:
                presum(order[t - 1])
            fetch(k)
            if t >= 2:
                relay(order[t - 2])
        presum(order[-1])
        relay(order[-2])
        relay(order[-1])

        if last:
            for r in range(1, NDEV):
                _, pb = _flip(x, y, c_, r)
                rcopy(srecv.at[me], srecv.at[pb], ssem_s.at[r], ssem_r.at[r], (x, y, c_)).wait_recv()
                rcopy(mrecv.at[me], mrecv.at[pb], msem_s.at[r], msem_r.at[r], (x, y, c_)).wait_recv()
            blk = brecv.at[_flip(x, y, c_, 1)[1]]
            rcopy(blk, blk, bsem_s.at[0], bsem_r.at[0], (x, y, c_)).wait_recv()
            for j, r in enumerate((5, 3, 7)):
                blk = brecv.at[_flip(x, y, c_, r)[1]]
                rcopy(blk, blk, fsem_s.at[j], fsem_r.at[j], (x, y, c_)).wait_recv()
            tot = srecv[0]
            btl = brecv[0].astype(F32)
            for s in range(1, NDEV):
                tot = tot + srecv[s]
                btl = btl + brecv[s].astype(F32)
            ssum_ref[...] = tot
            bsum_ref[...] = btl
            btot = mrecv[0]
            for s in range(1, NDEV):
                btot = btot + mrecv[s]
            gbada_ref[...] = btot
            rowm = lax.broadcasted_iota(jnp.int32, (8, MODW), 0)
            dmy = jnp.zeros((8, MODW), F32)
            for s in range(NDEV):
                drow = jnp.sum(jnp.where(rowm == me, mrecv[s], 0.0), axis=0, keepdims=True)
                dmy = jnp.where(rowm == s, drow, dmy)
            dmy_ref[...] = dmy

            for k in order:
                for q in range(2):
                    rcopy(cb[k].at[q], r2[k].at[q], i_s.at[k, q], i_r.at[k, q], (x, y, c_)).wait_recv()

                def add_far(j, carry, k=k):
                    rr = pl.ds(pl.multiple_of(j * sub, sub), sub)
                    t = o_refs[k][rr, :]
                    for q in range(2):
                        t = t + r2[k][q, rr, :].astype(F32)
                    o_refs[k][rr, :] = t
                    return carry

                lax.fori_loop(0, rws[k] // sub, add_far, 0)
        for cp in sends:
            cp.wait_send()
        for cp in kept:
            cp.wait()

    own = [jax.ShapeDtypeStruct((rws[k], cols[k]), F32) for k in range(nw)]
    parts = [jax.ShapeDtypeStruct((nchs[k] * CH, cols[k]), BF16) for k in range(nw)]
    scratch = [pltpu.VMEM((4, rws[k], cols[k]), BF16) for k in range(nw)]
    scratch += [pltpu.VMEM((3, rws[k], cols[k]), BF16) for k in range(nw)]
    scratch += [pltpu.VMEM((2, rws[k] // 2, cols[k]), BF16) for k in range(nw)]
    scratch += [pltpu.VMEM((4, max(rws), max(cols)), BF16)]
    scratch += [pltpu.VMEM((ASLOTS, CH, SEQ), BF16), pltpu.VMEM((2, SEQ, max(cols)), BF16), pltpu.VMEM((2, CH, max(cols)), BF16)]
    scratch += [pltpu.SemaphoreType.DMA((ASLOTS,))] + [pltpu.SemaphoreType.DMA((2,))] * 2 + [pltpu.SemaphoreType.DMA((4,))]
    scratch += [pltpu.SemaphoreType.DMA((nw, 4))] * 2 + [pltpu.SemaphoreType.DMA((nw, 2))] * 2
    if last:
        out_shape = own + [jax.ShapeDtypeStruct((srows, DM), F32), jax.ShapeDtypeStruct((brows, DM), F32),
                           jax.ShapeDtypeStruct((8, MODW), F32), jax.ShapeDtypeStruct((8, MODW), F32)] + parts
        out_specs = [_VM] * (nw + 4) + [_HBM] * nw
        scratch += [pltpu.VMEM((2, rws[k], cols[k]), BF16) for k in range(nw)]
        scratch += [pltpu.VMEM((NDEV, srows, DM), F32), pltpu.VMEM((NDEV, brows, DM), BF16), pltpu.VMEM((NDEV, 8, MODW), F32)]
        scratch += [pltpu.SemaphoreType.DMA((nw, 2))] * 2 + [pltpu.SemaphoreType.DMA((NDEV,))] * 4
        scratch += [pltpu.SemaphoreType.DMA((4,))] * 2 + [pltpu.SemaphoreType.DMA((3,))] * 2
        keep = nw + 4
    else:
        out_shape = own + [jax.ShapeDtypeStruct((3 if k == whole else 2, rws[k], cols[k]), BF16) for k in range(nw)] + parts
        out_specs = [_VM] * nw + [_HBM] * (2 * nw)
        scratch += [pltpu.SemaphoreType.DMA((nw,))]
        keep = 2 * nw
    outs = pl.pallas_call(
        body,
        name=name,
        out_shape=out_shape,
        in_specs=[_HBM] * (nw + nb) + [_VM] * (3 if last else 0),
        out_specs=out_specs,
        scratch_shapes=scratch,
        compiler_params=pltpu.CompilerParams(vmem_limit_bytes=60 * 1024 * 1024),
    )(*a3s, *bs, *(small or ()))
    return outs[:keep]


def _adam_update(w, g, m, v):
    m = ADAM_B1 * m + (1.0 - ADAM_B1) * g
    v = ADAM_B2 * v + (1.0 - ADAM_B2) * (g * g)
    m_hat = m / (1.0 - ADAM_B1 ** ADAM_STEP)
    v_hat = v / (1.0 - ADAM_B2 ** ADAM_STEP)
    return -ADAM_LR * (m_hat / (jnp.sqrt(v_hat) + ADAM_EPS) + ADAM_WD * w), m, v


def _adam_ada_call(w, m, v, dmy, cact_t):
    def body(w_ref, m_ref, v_ref, dmy_ref, ct_ref, g_ref, d_ref, mo_ref, vo_ref):
        g = jnp.zeros((DM // 4, MODW), F32)
        for s in range(NDEV):
            g = g + ct_ref[:, s : s + 1] * dmy_ref[s : s + 1, :]
        g_ref[...] = g
        d_ref[...], mo_ref[...], vo_ref[...] = _adam_update(w_ref[...], g, m_ref[...], v_ref[...])

    blk = pl.BlockSpec((DM // 4, MODW), lambda i: (i, 0))
    return pl.pallas_call(
        body,
        name="adam_ada",
        grid=(4,),
        out_shape=[jax.ShapeDtypeStruct(w.shape, F32)] * 4,
        in_specs=[blk] * 3 + [_const((8, MODW)), pl.BlockSpec((DM // 4, NDEV), lambda i: (i, 0))],
        out_specs=[blk] * 4,
        compiler_params=pltpu.CompilerParams(dimension_semantics=("arbitrary",), vmem_limit_bytes=VMEM_LIMIT),
    )(w, m, v, dmy, cact_t)


def _adam_sum_call(name, ws, owns, fars, ms, vs):
    n = len(ws)

    def body(*refs):
        for i in range(n):
            w, own, far, m, v = (refs[j * n + i] for j in range(5))
            g = own[...]
            for q in range(fars[i].shape[0]):
                g = g + far[q].astype(F32)
            refs[5 * n + i][...] = g
            refs[6 * n + i][...], refs[7 * n + i][...], refs[8 * n + i][...] = _adam_update(w[...], g, m[...], v[...])

    shapes = [jax.ShapeDtypeStruct(w.shape, F32) for w in ws]
    blks = [pl.BlockSpec((w.shape[0] // 2, w.shape[1]), lambda i: (i, 0)) for w in ws]
    fblks = [pl.BlockSpec((f.shape[0], w.shape[0] // 2, w.shape[1]), lambda i: (0, i, 0)) for w, f in zip(ws, fars)]
    outs = pl.pallas_call(
        body,
        name=name,
        grid=(2,),
        out_shape=shapes * 4,
        in_specs=blks * 2 + fblks + blks * 2,
        out_specs=blks * 4,
        compiler_params=pltpu.CompilerParams(dimension_semantics=("arbitrary",), vmem_limit_bytes=VMEM_LIMIT),
    )(*ws, *owns, *fars, *ms, *vs)
    return outs[:n], outs[n : 2 * n], outs[2 * n : 3 * n], outs[3 * n :]


def _adam_call(name, ws, gs, ms, vs):
    n = len(ws)

    def body(*refs):
        for i in range(n):
            w, g, m, v = (refs[j * n + i][...] for j in range(4))
            refs[4 * n + i][...], refs[5 * n + i][...], refs[6 * n + i][...] = _adam_update(w, g, m, v)

    shapes = [jax.ShapeDtypeStruct(w.shape, F32) for w in ws]
    outs = pl.pallas_call(
        body,
        name=name,
        out_shape=shapes * 3,
        in_specs=[_VM] * (4 * n),
        out_specs=[_VM] * (3 * n),
        compiler_params=pltpu.CompilerParams(vmem_limit_bytes=VMEM_LIMIT),
    )(*ws, *gs, *ms, *vs)
    return outs[:n], outs[n : 2 * n], outs[2 * n :]


def kernel(x, c, w_ada, b_ada, g_norm1, w_in, dw_w, dw_b, conv_ln_g, conv_ln_b, w_conv_pw, w_pool_group, pool_scale, w_out, g_norm2, w_ffn_gate, w_ffn_up, w_ffn_down, g_final, loss_target, m_w_ada, m_b_ada, m_g_norm1, m_w_in, m_dw_w, m_dw_b, m_conv_ln_g, m_conv_ln_b, m_w_conv_pw, m_w_pool_group, m_pool_scale, m_w_out, m_g_norm2, m_w_ffn_gate, m_w_ffn_up, m_w_ffn_down, m_g_final, v_w_ada, v_b_ada, v_g_norm1, v_w_in, v_dw_w, v_dw_b, v_conv_ln_g, v_conv_ln_b, v_w_conv_pw, v_w_pool_group, v_pool_scale, v_w_out, v_g_norm2, v_w_ffn_gate, v_w_ffn_up, v_w_ffn_down, v_g_final):
    me = 4 * lax.axis_index("x") + 2 * lax.axis_index("y") + lax.axis_index("c")
    xs, tgt = x[0], loss_target[0]
    b_my = lax.dynamic_slice(b_ada, (0, me * MODW), (1, MODW))
    win_t, wout, wpw, mod8, cact, dww8 = _gather_call(c, w_ada[0], b_my, dw_w[0], [w_in[0].T, w_out[0], w_conv_pw[0]])
    dww = jnp.pad(jnp.transpose(dww8, (1, 0, 2)).reshape(KCONV, CONVW), ((0, HALO_C - KCONV), (0, 0)))
    wgp = w_pool_group[0]

    h1b, uag, hc, hd, hsb3, pbv, ycb3, y, x1, wg_t, wu_t, wd = _fwd_mix_call(
        xs, mod8, g_norm1, win_t, dww, dw_b, conv_ln_g, conv_ln_b, wpw, wgp, pool_scale, wout,
        [w_ffn_gate[0].T, w_ffn_up[0].T, w_ffn_down[0]])
    h2b, a3, dfb, dg3, du3, dx1, facc = _ffn_call(x1, tgt, mod8, g_norm2, g_final.reshape(1, DM), wg_t, wu_t, wd)
    own_gate, own_up, own_down, s_gate, s_up, s_down = _wgrad_rs_call("wgrad_ffn", [dg3, du3, a3], [0, 0, 1], [h2b, dfb], (2, 0, 1))
    gx, dyb, dycb, dub3, macc, ddw, dwg, f_gate, f_up, f_down = _bwd_mix_call(
        dx1, xs, y, uag, hc, hd, pbv, mod8, g_norm1, win_t, dww, conv_ln_g, conv_ln_b, wpw, wgp, pool_scale, wout, [s_gate, s_up, s_down])
    spack = jnp.concatenate([macc[3:4], facc[3:4], facc[4:5], macc[4:6], facc[5:6], jnp.zeros((2, DM), F32)], axis=0)
    bulk = jnp.concatenate([ddw.reshape(HALO_C // 2, DM), dwg.reshape(-1, DM)], axis=0).astype(BF16)
    dmodp = jnp.concatenate([macc[0:3], facc[0:3]], axis=0).reshape(8, MODW)
    g_in_t, g_out, g_pw, ssum, bsum, gbada, dmy = _wgrad_rs_call(
        "wgrad_rs", [dub3, ycb3, hsb3], [0, 1, 2], [h1b, dyb, dycb], (0, 1, 2), small=(spack, bulk, dmodp))
    g_wada, d_wada, m_wada, v_wada = _adam_ada_call(w_ada[0], m_w_ada[0], v_w_ada[0], dmy, cact.T)
    turned = ("w_in", "w_ffn_gate", "w_ffn_up")
    ffn = ("w_ffn_gate", "w_ffn_up", "w_ffn_down")
    ffn_given = dict(w_ffn_gate=(w_ffn_gate, m_w_ffn_gate, v_w_ffn_gate), w_ffn_up=(w_ffn_up, m_w_ffn_up, v_w_ffn_up),
                     w_ffn_down=(w_ffn_down, m_w_ffn_down, v_w_ffn_down))

    def ffn_work(j):
        return [ffn_given[n][j][0].T if n in turned else ffn_given[n][j][0] for n in ffn]

    g_ffn, d_ffn, m_ffn, v_ffn = _adam_sum_call(
        "adam_ffn", ffn_work(0), [own_gate, own_up, own_down], [f_gate, f_up, f_down], ffn_work(1), ffn_work(2))

    loss = ssum[5, 0]
    ddw_all = bsum[: HALO_C // 2].reshape(HALO_C, CONVW)[:KCONV]
    grads = {
        "w_ada": g_wada,
        "b_ada": gbada.reshape(1, 6 * DM),
        "g_norm1": ssum[0:1],
        "w_in": g_in_t,
        "dw_w": lax.dynamic_slice(ddw_all, (0, me * (CONVW // NDEV)), (KCONV, CONVW // NDEV)),
        "dw_b": ssum[3:4, 0:CONVW],
        "conv_ln_g": ssum[3:4, CONVW:DM],
        "conv_ln_b": ssum[4:5, 0:CONVW],
        "w_conv_pw": g_pw,
        "w_pool_group": bsum[HALO_C // 2 :].reshape(len(WINS) * PGD, PGD),
        "pool_scale": ssum[4:5, CONVW:DM],
        "w_out": g_out,
        "g_norm2": ssum[1:2],
        "w_ffn_gate": g_ffn[0],
        "w_ffn_up": g_ffn[1],
        "w_ffn_down": g_ffn[2],
        "g_final": ssum[2:3],
    }
    given = dict(w_ada=(w_ada, m_w_ada, v_w_ada), b_ada=(b_ada, m_b_ada, v_b_ada), g_norm1=(g_norm1, m_g_norm1, v_g_norm1),
                 w_in=(w_in, m_w_in, v_w_in), dw_w=(dw_w, m_dw_w, v_dw_w), dw_b=(dw_b, m_dw_b, v_dw_b),
                 conv_ln_g=(conv_ln_g, m_conv_ln_g, v_conv_ln_g), conv_ln_b=(conv_ln_b, m_conv_ln_b, v_conv_ln_b),
                 w_conv_pw=(w_conv_pw, m_w_conv_pw, v_w_conv_pw), w_pool_group=(w_pool_group, m_w_pool_group, v_w_pool_group),
                 pool_scale=(pool_scale, m_pool_scale, v_pool_scale), w_out=(w_out, m_w_out, v_w_out), g_norm2=(g_norm2, m_g_norm2, v_g_norm2),
                 w_ffn_gate=(w_ffn_gate, m_w_ffn_gate, v_w_ffn_gate), w_ffn_up=(w_ffn_up, m_w_ffn_up, v_w_ffn_up),
                 w_ffn_down=(w_ffn_down, m_w_ffn_down, v_w_ffn_down), g_final=(g_final, m_g_final, v_g_final))
    names = list(given)
    groups = [["w_in", "w_out", "w_conv_pw"],
              ["b_ada", "g_norm1", "dw_w", "dw_b", "conv_ln_g", "conv_ln_b", "w_pool_group", "pool_scale", "g_norm2", "g_final"]]

    def work(n, a):
        return a[0].T if n in turned else a.reshape(grads[n].shape)

    def full(n, a):
        return a.T[None] if n in turned else a.reshape(given[n][0].shape)

    delta, new_m, new_v = {"w_ada": d_wada}, {"w_ada": m_wada}, {"w_ada": v_wada}
    for i, n in enumerate(ffn):
        delta[n], new_m[n], new_v[n] = d_ffn[i], m_ffn[i], v_ffn[i]
    for gi, grp in enumerate(groups):
        ds, ms, vs = _adam_call(f"adam{gi}", [work(n, given[n][0]) for n in grp], [grads[n] for n in grp],
                                [work(n, given[n][1]) for n in grp], [work(n, given[n][2]) for n in grp])
        for n, d_, m_, v_ in zip(grp, ds, ms, vs):
            delta[n], new_m[n], new_v[n] = d_, m_, v_

    return (loss, gx.reshape(x.shape), *[full(n, grads[n]) for n in names], *[full(n, delta[n]) for n in names],
            *[full(n, new_m[n]) for n in names], *[full(n, new_v[n]) for n in names])
```

```python
import functools

import jax
import jax.numpy as jnp
from jax import lax
from jax.experimental import pallas as pl
from jax.experimental.pallas import tpu as pltpu

F32, BF16 = jnp.float32, jnp.bfloat16
SEQ, DM = 2048, 1024
CONVW, POOLW = 512, 512
KCONV = 31
WINS = (2, 4, 8, 16)
PGD = 128
DFF = 2816
NDEV = 8
MODW = 6 * DM // NDEV
EPS = 1e-6
TOK = 256
NTILE = SEQ // TOK
CH = 256
NCH = DFF // CH
HALO_C, HALO_P = 32, 16
MESH = pl.DeviceIdType.MESH
VMEM_LIMIT = 56 * 1024 * 1024
ADAM_LR, ADAM_B1, ADAM_B2, ADAM_EPS, ADAM_WD, ADAM_STEP = 0.001, 0.9, 0.999, 1e-08, 0.01, 10
HI = lax.Precision.HIGHEST

_VM = pl.BlockSpec(memory_space=pltpu.VMEM)
_HBM = pl.BlockSpec(memory_space=pltpu.HBM)


def _place():
    x, y, c = lax.axis_index("x"), lax.axis_index("y"), lax.axis_index("c")
    return x, y, c, 4 * x + 2 * y + c


def _flip(x, y, c, r):
    px = 1 - x if r & 4 else x
    py = 1 - y if r & 2 else y
    pc = 1 - c if r & 1 else c
    return (px, py, pc), 4 * px + 2 * py + pc


def _rows(ref, blk, n):
    return ref.at[pl.ds(pl.multiple_of(blk * n, 16), n), :]


def _sig(z):
    return jax.nn.sigmoid(z)


def _dot_nt(a, b):
    return lax.dot_general(a, b, (((1,), (1,)), ((), ())), preferred_element_type=F32)


def _dot_nn(a, b):
    return lax.dot_general(a, b, (((1,), (0,)), ((), ())), preferred_element_type=F32)


def _dot_tn(a, b):
    return lax.dot_general(a, b, (((0,), (0,)), ((), ())), preferred_element_type=F32)


def _rcopy(src, dst, ss, rs, dev):
    return pltpu.make_async_remote_copy(src_ref=src, dst_ref=dst, send_sem=ss, recv_sem=rs, device_id=dev, device_id_type=MESH)


def _ag_sems(nw):
    return ([pltpu.SemaphoreType.DMA((nw, 3))] * 2 + [pltpu.SemaphoreType.DMA((nw, 4))] * 2 + [pltpu.SemaphoreType.DMA((nw, 2))] * 2
            + [pltpu.SemaphoreType.DMA((nw,))])


def _ag_plan(sbufs, g_refs, rws, sems):
    wsem_s, wsem_r, fsem_s, fsem_r, hsem_s, hsem_r, lsem = sems
    x, y, c_, me = _place()
    here = (x, y, c_)
    plans = []
    for k, (sb, g, n) in enumerate(zip(sbufs, g_refs, rws)):

        def blk(r, half=None, g=g, n=n):
            b = _flip(x, y, c_, r)[1]
            if half is None:
                return _rows(g, b, n)
            return g.at[pl.ds(pl.multiple_of(b * n + half * (n // 2), 16), n // 2), :]

        def same(ref, ss, rs, j, dev, k=k):
            return _rcopy(ref, ref, ss.at[k, j], rs.at[k, j], dev)

        sib, xn, yn = (_flip(x, y, c_, r)[0] for r in (1, 4, 2))
        plans.append(dict(
            local=pltpu.make_async_copy(sb, blk(0), lsem.at[k]),
            first=[_rcopy(sb, blk(0), wsem_s.at[k, j], wsem_r.at[k, j], dev) for j, dev in enumerate((sib, xn, yn))],
            got=[_rcopy(sb, blk(r), wsem_s.at[k, j], wsem_r.at[k, j], here) for j, r in enumerate((1, 4, 2))],
            passes=[same(blk(4), fsem_s, fsem_r, 0, sib), same(blk(2), fsem_s, fsem_r, 1, sib),
                    same(blk(6, 0), fsem_s, fsem_r, 2, sib), same(blk(6, 1), fsem_s, fsem_r, 3, sib)],
            passed=[same(blk(5), fsem_s, fsem_r, 0, here), same(blk(3), fsem_s, fsem_r, 1, here),
                    same(blk(7, 0), fsem_s, fsem_r, 2, here), same(blk(7, 1), fsem_s, fsem_r, 3, here)],
            halves=[same(blk(4, 0), hsem_s, hsem_r, 0, yn), same(blk(2, 1), hsem_s, hsem_r, 1, xn)],
            halved=[same(blk(6, 0), hsem_s, hsem_r, 0, here), same(blk(6, 1), hsem_s, hsem_r, 1, here)],
        ))
    return plans


def _ag_start(sbufs, g_refs, rws, sems):
    for p in _ag_plan(sbufs, g_refs, rws, sems):
        p["local"].start()
        for cp in p["first"]:
            cp.start()


def _ag_pass_on(sbufs, g_refs, rws, sems):
    plans = _ag_plan(sbufs, g_refs, rws, sems)
    for p in plans:
        for j in (0, 1):
            p["got"][j + 1].wait_recv()
            p["halves"][j].start()
            p["passes"][j].start()
    for p in plans:
        for j in (0, 1):
            p["halved"][j].wait_recv()
            p["passes"][j + 2].start()


def _ag_finish(sbufs, g_refs, rws, sems):
    plans = _ag_plan(sbufs, g_refs, rws, sems)
    for p in plans:
        p["got"][0].wait_recv()
        for cp in p["passed"]:
            cp.wait_recv()
    for p in plans:
        for cp in p["first"] + p["passes"] + p["halves"]:
            cp.wait_send()
        p["local"].wait()


def _gather_call(c, w_ada, b_my, dww, shards):
    nw = len(shards)
    rws = [s.shape[0] for s in shards]

    def body(*refs):
        c_ref, wada_ref, bmy_ref, dww_ref = refs[:4]
        s_refs = refs[4 : 4 + nw]
        g_refs = refs[4 + nw : 4 + 2 * nw]
        mod8_ref, cact_ref, dww8_ref = refs[4 + 2 * nw : 7 + 2 * nw]
        crecv, msend, mrecv = refs[7 + 2 * nw : 10 + 2 * nw]
        sbufs = refs[10 + 2 * nw : 10 + 3 * nw]
        csem_s, csem_r, dsem_s, dsem_r, msem_s, msem_r = refs[10 + 3 * nw : 16 + 3 * nw]
        ag_sems = refs[16 + 3 * nw :]
        x, y, c_, me = _place()

        def rcopy(src, dst, ss, rs, dev):
            return pltpu.make_async_remote_copy(src_ref=src, dst_ref=dst, send_sem=ss, recv_sem=rs, device_id=dev, device_id_type=MESH)

        crecv[me] = jnp.broadcast_to(c_ref[...], (8, DM))
        dww8_ref[me] = dww_ref[...]
        small = []
        for r in range(1, NDEV):
            dev, _ = _flip(x, y, c_, r)
            small.append(rcopy(crecv.at[me], crecv.at[me], csem_s.at[r], csem_r.at[r], dev))
            small.append(rcopy(dww8_ref.at[me], dww8_ref.at[me], dsem_s.at[r], dsem_r.at[r], dev))
        for cp in small:
            cp.start()

        for k in range(nw):
            sbufs[k][...] = s_refs[k][...].astype(BF16)
        _ag_start(sbufs, g_refs, rws, ag_sems)

        rowid = lax.broadcasted_iota(jnp.int32, (8, DM), 0)
        for r in range(1, NDEV):
            _, pb = _flip(x, y, c_, r)
            rcopy(crecv.at[me], crecv.at[pb], csem_s.at[r], csem_r.at[r], (x, y, c_)).wait_recv()
        call = jnp.zeros((8, DM), F32)
        for s in range(NDEV):
            call = jnp.where(rowid == s, crecv[s], call)
        cact = call * _sig(call)
        cact_ref[...] = cact
        modp = jnp.dot(cact, wada_ref[...], precision=HI, preferred_element_type=F32) + bmy_ref[...]
        rowm = lax.broadcasted_iota(jnp.int32, (8, MODW), 0)
        for b in range(NDEV):
            row = jnp.sum(jnp.where(rowm == b, modp, 0.0), axis=0, keepdims=True)
            msend[b] = jnp.broadcast_to(row, (8, MODW))
        mrecv[me] = msend[me]
        msends = []
        for r in range(1, NDEV):
            dev, pb = _flip(x, y, c_, r)
            cp = rcopy(msend.at[pb], mrecv.at[me], msem_s.at[r], msem_r.at[r], dev)
            cp.start()
            msends.append(cp)

        _ag_pass_on(sbufs, g_refs, rws, ag_sems)

        for r in range(1, NDEV):
            _, pb = _flip(x, y, c_, r)
            rcopy(msend.at[pb], mrecv.at[pb], msem_s.at[r], msem_r.at[r], (x, y, c_)).wait_recv()
        for s in range(NDEV):
            mod8_ref[:, s * MODW : (s + 1) * MODW] = mrecv[s]

        _ag_finish(sbufs, g_refs, rws, ag_sems)
        for r in range(1, NDEV):
            _, pb = _flip(x, y, c_, r)
            rcopy(dww8_ref.at[me], dww8_ref.at[pb], dsem_s.at[r], dsem_r.at[r], (x, y, c_)).wait_recv()
        for cp in small + msends:
            cp.wait_send()

    out_shape = [jax.ShapeDtypeStruct((NDEV * s.shape[0], s.shape[1]), BF16) for s in shards]
    out_shape += [
        jax.ShapeDtypeStruct((8, 6 * DM), F32),
        jax.ShapeDtypeStruct((8, DM), F32),
        jax.ShapeDtypeStruct((NDEV,) + dww.shape, F32),
    ]
    scratch = [pltpu.VMEM((NDEV, 8, DM), F32), pltpu.VMEM((NDEV, 8, MODW), F32), pltpu.VMEM((NDEV, 8, MODW), F32)]
    scratch += [pltpu.VMEM(s.shape, BF16) for s in shards]
    scratch += [pltpu.SemaphoreType.DMA((NDEV,))] * 6 + _ag_sems(nw)
    return pl.pallas_call(
        body,
        name="gather",
        out_shape=out_shape,
        in_specs=[_VM] * (4 + nw),
        out_specs=[_HBM] * nw + [_VM] * 3,
        scratch_shapes=scratch,
        compiler_params=pltpu.CompilerParams(vmem_limit_bytes=VMEM_LIMIT),
    )(c, w_ada, b_my, dww, *shards)


def _const(shape):
    return pl.BlockSpec(shape, lambda i: (0,) * len(shape))


def _tile(width, rev=False):
    if rev:
        return pl.BlockSpec((TOK, width), lambda i: (NTILE - 1 - i, 0))
    return pl.BlockSpec((TOK, width), lambda i: (i, 0))


def _tile3(nch, rev=False):
    if rev:
        return pl.BlockSpec((nch, CH, TOK), lambda i: (0, 0, NTILE - 1 - i))
    return pl.BlockSpec((nch, CH, TOK), lambda i: (0, 0, i))


def _put3(ref, val):
    for j in range(val.shape[1] // CH):
        ref[j] = val[:, j * CH : (j + 1) * CH].T


def _norm_mod(x, g, sc, sh):
    r = lax.rsqrt(jnp.mean(x * x, axis=-1, keepdims=True) + EPS)
    xr = x * r
    return r, xr, xr * g * (1.0 + sc) + sh


def _fwd_mix_call(x, mod8, g1, win_t, dww, dwb, lng, lnb, wpw, wg, psc, wout, shards):
    ns = len(shards)
    rws = [s.shape[0] for s in shards]

    def body(x_ref, mod_ref, g1_ref, win_ref, dww_ref, dwb_ref, lng_ref, lnb_ref, wpw_ref, wg_ref, psc_ref, wout_ref, *rest):
        s_refs = rest[:ns]
        h1b_ref, uag_ref, hc_ref, hd_ref, hsb3_ref, pb_ref, ycb3_ref, y_ref, x1_ref = rest[ns : ns + 9]
        g_refs = rest[ns + 9 : 2 * ns + 9]
        hc_ext, up_ext, ycb_ref, stage = rest[2 * ns + 9 : 2 * ns + 13]
        sbufs = rest[2 * ns + 13 : 3 * ns + 13]
        ssem = rest[3 * ns + 13]
        ag_sems = rest[3 * ns + 14 :]
        i = pl.program_id(0)

        @pl.when(i == 0)
        def _():
            for k in range(ns):
                cp = pltpu.make_async_copy(s_refs[k], stage, ssem)
                cp.start()
                cp.wait()
                sbufs[k][...] = stage[...].astype(BF16)
            _ag_start(sbufs, g_refs, rws, ag_sems)
            hc_ext[0:HALO_C, :] = jnp.zeros((HALO_C, CONVW), F32)
            up_ext[0:HALO_P, :] = jnp.zeros((HALO_P, POOLW), F32)

        x = x_ref[...]
        sh1, sc1, gt1 = mod_ref[0:1, 0:DM], mod_ref[0:1, DM : 2 * DM], mod_ref[0:1, 2 * DM : 3 * DM]
        _, _, h1 = _norm_mod(x, g1_ref[...], sc1, sh1)
        h1b = h1.astype(BF16)
        h1b_ref[...] = h1b
        u = _dot_nt(h1b, win_ref[...])
        uag_ref[...] = u[:, : 2 * CONVW]
        hc = u[:, :CONVW] * _sig(u[:, CONVW : 2 * CONVW])
        hc_ref[...] = hc
        hc_ext[HALO_C : HALO_C + TOK, :] = hc
        up_ext[HALO_P : HALO_P + TOK, :] = u[:, 2 * CONVW :]

        acc = jnp.zeros((TOK, CONVW), F32)
        for k in range(KCONV):
            acc = acc + dww_ref[k : k + 1, :] * hc_ext[pl.ds(HALO_C - (KCONV - 1) + k, TOK), :]
        hd = acc + dwb_ref[...]
        hd_ref[...] = hd
        hc_ext[0:HALO_C, :] = hc_ext[TOK : TOK + HALO_C, :]
        mu = jnp.mean(hd, axis=-1, keepdims=True)
        dlt = hd - mu
        rstd = lax.rsqrt(jnp.mean(dlt * dlt, axis=-1, keepdims=True) + EPS)
        hl = dlt * rstd * lng_ref[...] + lnb_ref[...]
        hsb = (hl * _sig(hl)).astype(BF16)
        _put3(hsb3_ref, hsb)
        ycb_ref[:, 0:CONVW] = _dot_nn(hsb, wpw_ref[...]).astype(BF16)

        tg = i * TOK + lax.broadcasted_iota(jnp.int32, (TOK, 1), 0)
        for g, w in enumerate(WINS):
            ln = slice(PGD * g, PGD * (g + 1))
            v = up_ext[pl.ds(HALO_P, TOK), ln]
            ssum = v
            for d in range(1, w):
                ssum = ssum + up_ext[pl.ds(HALO_P - d, TOK), ln]
            cnt = jnp.minimum(tg + 1, w).astype(F32)
            pb = (ssum / cnt - v).astype(BF16)
            pb_ref[:, ln] = pb
            z = _dot_nn(pb, wg_ref[g].astype(BF16))
            ycb_ref[:, CONVW + PGD * g : CONVW + PGD * (g + 1)] = (z * psc_ref[:, ln]).astype(BF16)
        up_ext[0:HALO_P, :] = up_ext[TOK : TOK + HALO_P, :]

        _put3(ycb3_ref, ycb_ref[...])
        yv = _dot_nn(ycb_ref[...], wout_ref[...])
        y_ref[...] = yv
        x1_ref[...] = x + gt1 * yv

        @pl.when(i == NTILE - 1)
        def _():
            _ag_pass_on(sbufs, g_refs, rws, ag_sems)
            _ag_finish(sbufs, g_refs, rws, ag_sems)

    outs = [(DM, BF16), (2 * CONVW, F32), (CONVW, F32), (CONVW, F32), (-CONVW, BF16), (POOLW, BF16), (-DM, BF16), (DM, F32), (DM, F32)]
    return pl.pallas_call(
        body,
        name="fwd_mix",
        grid=(NTILE,),
        out_shape=[jax.ShapeDtypeStruct((SEQ, w) if w > 0 else (-w // CH, CH, SEQ), d) for w, d in outs]
        + [jax.ShapeDtypeStruct((NDEV * s.shape[0], s.shape[1]), BF16) for s in shards],
        in_specs=[_tile(DM), _const((8, 6 * DM)), _const((1, DM)), _const(win_t.shape), _const(dww.shape), _const((1, CONVW)),
                  _const((1, CONVW)), _const((1, CONVW)), _const(wpw.shape), _const(wg.shape), _const((1, POOLW)), _const(wout.shape)]
        + [_HBM] * ns,
        out_specs=[_tile(w) if w > 0 else _tile3(-w // CH) for w, _ in outs] + [_HBM] * ns,
        scratch_shapes=[pltpu.VMEM((TOK + HALO_C, CONVW), F32), pltpu.VMEM((TOK + HALO_P, POOLW), F32), pltpu.VMEM((TOK, DM), BF16),
                        pltpu.VMEM(shards[0].shape, F32)] + [pltpu.VMEM(s.shape, BF16) for s in shards]
        + [pltpu.SemaphoreType.DMA] + _ag_sems(ns),
        compiler_params=pltpu.CompilerParams(dimension_semantics=("arbitrary",), vmem_limit_bytes=VMEM_LIMIT),
    )(x, mod8, g1, win_t, dww, dwb, lng, lnb, wpw, wg, psc, wout, *shards)


def _ffn_call(x1, tgt, mod8, g2, gf, wg_t, wu_t, wd):
    def body(x1_ref, tgt_ref, mod_ref, g2_ref, gf_ref, wg_hbm, wu_hbm, wd_hbm,
             h2b_ref, a3_ref, dfb_ref, dg3_ref, du3_ref, dx1_ref, acc_ref,
             wg_ref, wu_ref, wd_ref, wsem):
        i = pl.program_id(0)

        @pl.when(i == 0)
        def _():
            cps = [pltpu.make_async_copy(s, d, wsem.at[n]) for n, (s, d) in enumerate(((wg_hbm, wg_ref), (wu_hbm, wu_ref), (wd_hbm, wd_ref)))]
            for cp in cps:
                cp.start()
            acc_ref[...] = jnp.zeros((8, DM), F32)
            for cp in cps:
                cp.wait()

        x1 = x1_ref[...]
        sh2, sc2, gt2 = mod_ref[0:1, 3 * DM : 4 * DM], mod_ref[0:1, 4 * DM : 5 * DM], mod_ref[0:1, 5 * DM : 6 * DM]
        g2 = g2_ref[...]
        r2, xr, h2 = _norm_mod(x1, g2, sc2, sh2)
        h2b = h2.astype(BF16)
        h2b_ref[...] = h2b
        gate = _dot_nt(h2b, wg_ref[...])
        up = _dot_nt(h2b, wu_ref[...])
        ab = (gate * _sig(gate) * up).astype(BF16)
        _put3(a3_ref, ab)
        f = _dot_nn(ab, wd_ref[...])
        x2 = x1 + gt2 * f
        rf = lax.rsqrt(jnp.mean(x2 * x2, axis=-1, keepdims=True) + EPS)
        nf = x2 * rf
        gf_ = gf_ref[...]
        err = nf * gf_ - tgt_ref[...]
        loss = 0.5 * jnp.sum(jnp.sum(err * err, axis=-1, keepdims=True), axis=0, keepdims=True) * (1.0 / DM)
        dout = err * (1.0 / DM)
        dnf = dout * gf_
        dx2 = rf * (dnf - nf * jnp.mean(dnf * nf, axis=-1, keepdims=True))
        dfb = (gt2 * dx2).astype(BF16)
        dfb_ref[...] = dfb
        da = _dot_nt(dfb, wd_ref[...])
        sg = _sig(gate)
        dgb = (da * up * (sg * (1.0 + gate * (1.0 - sg)))).astype(BF16)
        dub = (da * (gate * sg)).astype(BF16)
        _put3(dg3_ref, dgb)
        _put3(du3_ref, dub)
        dh2 = _dot_nn(dgb, wg_ref[...]) + _dot_nn(dub, wu_ref[...])
        dn2 = dh2 * (1.0 + sc2)
        dxr = dn2 * g2
        dx1_ref[...] = dx2 + r2 * (dxr - xr * jnp.mean(dxr * xr, axis=-1, keepdims=True))

        def colsum(v):
            return jnp.sum(v, axis=0, keepdims=True)

        acc_ref[0:1, :] += colsum(dh2)
        acc_ref[1:2, :] += colsum(dh2 * (xr * g2))
        acc_ref[2:3, :] += colsum(dx2 * f)
        acc_ref[3:4, :] += colsum(dn2 * xr)
        acc_ref[4:5, :] += colsum(dout * nf)
        acc_ref[5:6, :] += jnp.broadcast_to(loss, (1, DM))

    c3 = _tile3(NCH)
    return pl.pallas_call(
        body,
        name="ffn",
        grid=(NTILE,),
        out_shape=[jax.ShapeDtypeStruct((SEQ, DM), BF16), jax.ShapeDtypeStruct((NCH, CH, SEQ), BF16), jax.ShapeDtypeStruct((SEQ, DM), BF16),
                   jax.ShapeDtypeStruct((NCH, CH, SEQ), BF16), jax.ShapeDtypeStruct((NCH, CH, SEQ), BF16),
                   jax.ShapeDtypeStruct((SEQ, DM), F32), jax.ShapeDtypeStruct((8, DM), F32)],
        in_specs=[_tile(DM), _tile(DM), _const((8, 6 * DM)), _const((1, DM)), _const((1, DM)), _HBM, _HBM, _HBM],
        out_specs=[_tile(DM), c3, _tile(DM), c3, c3, _tile(DM), _const((8, DM))],
        scratch_shapes=[pltpu.VMEM((DFF, DM), BF16)] * 3 + [pltpu.SemaphoreType.DMA((3,))],
        compiler_params=pltpu.CompilerParams(dimension_semantics=("arbitrary",), vmem_limit_bytes=VMEM_LIMIT),
    )(x1, tgt, mod8, g2, gf, wg_t, wu_t, wd)


def _bwd_mix_call(dx1, x, y, uag, hc, hd, pbv, mod8, g1, win_t, dww, lng, lnb, wpw, wg, psc, wout, ycb3, hsb3, sums):
    hpt = TOK // HALO_C
    ns = len(sums)

    def far_copies(s_refs, f_refs, fs_s, fs_r):
        x_, y_, c_, _ = _place()
        return [_rcopy(s_refs[k].at[q], f_refs[k].at[q], fs_s.at[k, q], fs_r.at[k, q], _flip(x_, y_, c_, r)[0])
                for k in range(ns) for q, r in enumerate((4, 2, 6)[: sums[k].shape[0]])]

    def body(dx1_ref, x_ref, y_ref, uag_ref, hc_ref, halo_ref, hd_ref, pb_ref, mod_ref, g1_ref, win_ref, dww_ref, lng_ref, lnb_ref,
             wpw_ref, wg_ref, psc_ref, wout_ref, ycb3_ref, hsb3_ref, *rest):
        s_refs = rest[:ns]
        gx_ref, dub3_ref, acc_ref, ddw_ref, dwg_ref, pout_ref, ppw_ref = rest[ns : ns + 7]
        f_refs = rest[ns + 7 : 2 * ns + 7]
        d_ext, q_ext, hcx, dub_ref, hrot, drot, aout, apw, fs_s, fs_r = rest[2 * ns + 7 :]
        i = pl.program_id(0)
        it = NTILE - 1 - i

        @pl.when(i == 0)
        def _():
            for cp in far_copies(s_refs, f_refs, fs_s, fs_r):
                cp.start()
            aout[...] = jnp.zeros((DM, DM), F32)
            apw[...] = jnp.zeros((CONVW, CONVW), F32)
            d_ext[TOK : TOK + HALO_C, :] = jnp.zeros((HALO_C, CONVW), F32)
            q_ext[TOK : TOK + HALO_P, :] = jnp.zeros((HALO_P, POOLW), F32)
            acc_ref[...] = jnp.zeros((8, DM), F32)
            ddw_ref[...] = jnp.zeros((HALO_C, CONVW), F32)
            dwg_ref[...] = jnp.zeros((len(WINS) * PGD, PGD), F32)

        def colsum(v):
            return jnp.sum(v, axis=0, keepdims=True)

        dx1 = dx1_ref[...]
        x = x_ref[...]
        sh1, sc1, gt1 = mod_ref[0:1, 0:DM], mod_ref[0:1, DM : 2 * DM], mod_ref[0:1, 2 * DM : 3 * DM]
        acc_ref[2:3, :] += colsum(dx1 * y_ref[...])
        dyb = (gt1 * dx1).astype(BF16)
        for j in range(DM // CH):
            aout[j * CH : (j + 1) * CH, :] += _dot_nn(ycb3_ref[j], dyb)
        dycat = _dot_nt(dyb, wout_ref[...])

        hd = hd_ref[...]
        mu = jnp.mean(hd, axis=-1, keepdims=True)
        dlt = hd - mu
        rstd = lax.rsqrt(jnp.mean(dlt * dlt, axis=-1, keepdims=True) + EPS)
        xhat = dlt * rstd
        lng = lng_ref[...]
        hl = xhat * lng + lnb_ref[...]
        sgl = _sig(hl)
        dycb = dycat[:, :CONVW].astype(BF16)
        for j in range(CONVW // CH):
            apw[j * CH : (j + 1) * CH, :] += _dot_nn(hsb3_ref[j], dycb)
        dhl = _dot_nt(dycb, wpw_ref[...]) * (sgl * (1.0 + hl * (1.0 - sgl)))
        acc_ref[5:6, 0:CONVW] += colsum(dhl)
        acc_ref[4:5, CONVW:DM] += colsum(dhl * xhat)
        dxh = dhl * lng
        dhd = rstd * (dxh - jnp.mean(dxh, axis=-1, keepdims=True) - xhat * jnp.mean(dxh * xhat, axis=-1, keepdims=True))
        acc_ref[4:5, 0:CONVW] += colsum(dhd)

        hcx[0:HALO_C, :] = jnp.where(it == 0, 0.0, halo_ref[...])
        hcx[HALO_C : HALO_C + TOK, :] = hc_ref[...]
        d_ext[0:TOK, :] = dhd
        for b in range(1, 8):
            hrot[b - 1] = hcx[pl.ds(b, TOK + HALO_C - 8), :]
            drot[b - 1] = d_ext[pl.ds(b, TOK + HALO_C - 8), :]

        def tap(base, rot, off):
            a, b = divmod(off, 8)
            return base[pl.ds(8 * a, TOK), :] if b == 0 else rot[b - 1, pl.ds(8 * a, TOK), :]

        dhc = jnp.zeros((TOK, CONVW), F32)
        for k in range(KCONV):
            ddw_ref[k : k + 1, :] += colsum(dhd * tap(hcx, hrot, HALO_C - (KCONV - 1) + k))
            dhc = dhc + dww_ref[k : k + 1, :] * tap(d_ext, drot, KCONV - 1 - k)
        d_ext[TOK : TOK + HALO_C, :] = d_ext[0:HALO_C, :]
        ua, ug = uag_ref[:, 0:CONVW], uag_ref[:, CONVW : 2 * CONVW]
        sgg = _sig(ug)
        dub_ref[:, 0:CONVW] = (dhc * sgg).astype(BF16)
        dub_ref[:, CONVW : 2 * CONVW] = (dhc * ua * sgg * (1.0 - sgg)).astype(BF16)

        tg = it * TOK + lax.broadcasted_iota(jnp.int32, (TOK, 1), 0)
        for g, w in enumerate(WINS):
            ln = slice(PGD * g, PGD * (g + 1))
            wgb = wg_ref[g].astype(BF16)
            pb = pb_ref[:, ln]
            dyp = dycat[:, CONVW + PGD * g : CONVW + PGD * (g + 1)]
            acc_ref[5:6, CONVW + PGD * g : CONVW + PGD * (g + 1)] += colsum(dyp * _dot_nn(pb, wgb))
            dzb = (dyp * psc_ref[:, ln]).astype(BF16)
            dwg_ref[PGD * g : PGD * (g + 1), :] += _dot_tn(pb, dzb)
            dp = _dot_nt(dzb, wgb)
            cnt = jnp.minimum(tg + 1, w).astype(F32)
            q_ext[0:TOK, ln] = dp / cnt
            dv = -dp
            for d in range(w):
                dv = dv + q_ext[pl.ds(d, TOK), ln]
            dub_ref[:, 2 * CONVW + PGD * g : 2 * CONVW + PGD * (g + 1)] = dv.astype(BF16)
        q_ext[TOK : TOK + HALO_P, :] = q_ext[0:HALO_P, :]

        _put3(dub3_ref, dub_ref[...])
        dh1 = _dot_nn(dub_ref[...], win_ref[...])
        g1 = g1_ref[...]
        r1 = lax.rsqrt(jnp.mean(x * x, axis=-1, keepdims=True) + EPS)
        xr = x * r1
        acc_ref[0:1, :] += colsum(dh1)
        acc_ref[1:2, :] += colsum(dh1 * (xr * g1))
        dn1 = dh1 * (1.0 + sc1)
        acc_ref[3:4, :] += colsum(dn1 * xr)
        dxr = dn1 * g1
        gx_ref[...] = dx1 + r1 * (dxr - xr * jnp.mean(dxr * xr, axis=-1, keepdims=True))

        @pl.when(i == NTILE - 1)
        def _():
            pout_ref[...] = aout[...].astype(BF16)
            ppw_ref[...] = apw[...].astype(BF16)
            for cp in far_copies(s_refs, f_refs, fs_s, fs_r):
                cp.wait()

    halo = pl.BlockSpec((HALO_C, CONVW), lambda i: (jnp.maximum((NTILE - 1 - i) * hpt - 1, 0), 0))
    return pl.pallas_call(
        body,
        name="bwd_mix",
        grid=(NTILE,),
        out_shape=[jax.ShapeDtypeStruct((SEQ, DM), F32), jax.ShapeDtypeStruct((3 * CONVW // CH, CH, SEQ), BF16), jax.ShapeDtypeStruct((8, DM), F32),
                   jax.ShapeDtypeStruct((HALO_C, CONVW), F32), jax.ShapeDtypeStruct((len(WINS) * PGD, PGD), F32),
                   jax.ShapeDtypeStruct((DM, DM), BF16), jax.ShapeDtypeStruct((CONVW, CONVW), BF16)]
        + [jax.ShapeDtypeStruct(s.shape, s.dtype) for s in sums],
        in_specs=[_tile(DM, True), _tile(DM, True), _tile(DM, True), _tile(2 * CONVW, True), _tile(CONVW, True), halo, _tile(CONVW, True),
                  _tile(POOLW, True), _const((8, 6 * DM)), _const((1, DM)), _const(win_t.shape), _const(dww.shape), _const((1, CONVW)),
                  _const((1, CONVW)), _const(wpw.shape), _const(wg.shape), _const((1, POOLW)), _const(wout.shape),
                  _tile3(DM // CH, True), _tile3(CONVW // CH, True)] + [_HBM] * ns,
        out_specs=[_tile(DM, True), _tile3(3 * CONVW // CH, True), _const((8, DM)),
                   _const((HALO_C, CONVW)), _const((len(WINS) * PGD, PGD)), _const((DM, DM)), _const((CONVW, CONVW))] + [_HBM] * ns,
        scratch_shapes=[pltpu.VMEM((TOK + HALO_C, CONVW), F32), pltpu.VMEM((TOK + HALO_P, POOLW), F32), pltpu.VMEM((TOK + HALO_C, CONVW), F32),
                        pltpu.VMEM((TOK, 3 * CONVW), BF16)] + [pltpu.VMEM((7, TOK + HALO_C - 8, CONVW), F32)] * 2
        + [pltpu.VMEM((DM, DM), F32), pltpu.VMEM((CONVW, CONVW), F32)] + [pltpu.SemaphoreType.DMA((ns, 3))] * 2,
        compiler_params=pltpu.CompilerParams(dimension_semantics=("arbitrary",), vmem_limit_bytes=VMEM_LIMIT),
    )(dx1, x, y, uag, hc, hc, hd, pbv, mod8, g1, win_t, dww, lng, lnb, wpw, wg, psc, wout, ycb3, hsb3, *sums)


CHIPS = (0, 4, 2, 6)
ASLOTS = 3


def _wgrad_rs_call(name, a3s, bmap, bs, order, small=None):
    nw = len(a3s)
    pre = [a.ndim == 2 for a in a3s]
    nchs = [a.shape[0] // CH if p else a.shape[0] for a, p in zip(a3s, pre)]
    rws = [n * CH // NDEV for n in nchs]
    cols = [a3s[k].shape[1] if pre[k] else bs[bmap[k]].shape[1] for k in range(nw)]
    last = small is not None
    whole = None if last else order[-1]
    srows, brows = (small[0].shape[0], small[1].shape[0]) if last else (0, 0)
    sub = 16
    nb = len(bs)

    def body(*refs):
        pos = 0

        def take(n):
            nonlocal pos
            pos += n
            return refs[pos - n : pos]

        a_refs, b_refs = take(nw), take(nb)
        if last:
            spack_ref, bulk_ref, dmodp_ref = take(3)
        o_refs = take(nw)
        if last:
            ssum_ref, bsum_ref, gbada_ref, dmy_ref = take(4)
        else:
            cbm_refs = take(nw)
        made = iter(take(nw - sum(pre)))
        p_refs = [a_refs[k] if pre[k] else next(made) for k in range(nw)]
        r1, cb, hb = take(nw), take(nw), take(nw)
        tmp4, abuf, bbuf, obuf = take(4)
        asem, osem, bsem, tsem, d_s, d_r, h_s, h_r = take(8)
        if last:
            r2 = take(nw)
            srecv, brecv, mrecv, i_s, i_r, ssem_s, ssem_r, msem_s, msem_r, bsem_s, bsem_r, fsem_s, fsem_r = take(13)
        else:
            (csem,) = take(1)
        x, y, c_, me = _place()
        sib, xn, yn = (_flip(x, y, c_, r)[0] for r in (1, 4, 2))

        def rcopy(src, dst, ss, rs, dev):
            return pltpu.make_async_remote_copy(src_ref=src, dst_ref=dst, send_sem=ss, recv_sem=rs, device_id=dev, device_id_type=MESH)

        sends, kept = [], []
        if last:
            srecv[me] = spack_ref[...]
            brecv[me] = bulk_ref[...]
            mrecv[me] = dmodp_ref[...]
            for r in range(1, NDEV):
                dev, _ = _flip(x, y, c_, r)
                sends.append(rcopy(srecv.at[me], srecv.at[me], ssem_s.at[r], ssem_r.at[r], dev))
                sends.append(rcopy(mrecv.at[me], mrecv.at[me], msem_s.at[r], msem_r.at[r], dev))
            for j, r in enumerate((1, 4, 2, 6)):
                sends.append(rcopy(brecv.at[me], brecv.at[me], bsem_s.at[j], bsem_r.at[j], _flip(x, y, c_, r)[0]))
            for cp in sends:
                cp.start()

        def pass_bulk():
            for j, r in enumerate((4, 2, 6)):
                blk = brecv.at[_flip(x, y, c_, r)[1]]
                rcopy(blk, blk, bsem_s.at[j + 1], bsem_r.at[j + 1], (x, y, c_)).wait_recv()
                cp = rcopy(blk, blk, fsem_s.at[j], fsem_r.at[j], sib)
                cp.start()
                sends.append(cp)

        def relay(k):
            if k == whole:
                return
            half = rws[k] // 2
            for h in range(2):
                rcopy(cb[k].at[2, pl.ds(h * half, half), :], hb[k].at[h], h_s.at[k, h], h_r.at[k, h], (x, y, c_)).wait_recv()

                def add_half(j, carry, k=k, h=h):
                    rr = pl.ds(pl.multiple_of(j * sub, sub), sub)
                    dst = pl.ds(pl.multiple_of(h * half + j * sub, sub), sub)
                    cb[k][1 - h, dst, :] = (cb[k][1 - h, dst, :].astype(F32) + hb[k][h, rr, :].astype(F32)).astype(BF16)
                    return carry

                lax.fori_loop(0, half // sub, add_half, 0)
            if last:
                for q, dev in enumerate((xn, yn)):
                    cp = rcopy(cb[k].at[q], r2[k].at[q], i_s.at[k, q], i_r.at[k, q], dev)
                    cp.start()
                    sends.append(cp)
            else:
                cp = pltpu.make_async_copy(cb[k].at[pl.ds(0, 2)], cbm_refs[k], csem.at[k])
                cp.start()
                kept.append(cp)

        def mine(k, q):
            _, owner = _flip(x, y, c_, CHIPS[q])
            return _rows(p_refs[k], owner, rws[k]), tmp4.at[q, pl.ds(0, rws[k]), pl.ds(0, cols[k])]

        def fetch(k):
            for q in range(4):
                pltpu.make_async_copy(*mine(k, q), tsem.at[q]).start()

        def presum(k):
            for q in (3, 1, 2, 0):
                src, tmp = mine(k, q)
                rcopy(src, r1[k].at[q], d_s.at[k, q], d_r.at[k, q], (x, y, c_)).wait_recv()
                pltpu.make_async_copy(src, tmp, tsem.at[q]).wait()

                def add_sib(j, carry, k=k, q=q, tmp=tmp):
                    rr = pl.ds(pl.multiple_of(j * sub, sub), sub)
                    t = tmp[rr, :].astype(F32) + r1[k][q, rr, :].astype(F32)
                    if q == 0:
                        o_refs[k][rr, :] = t
                    else:
                        cb[k][q - 1, rr, :] = t.astype(BF16)
                    return carry

                lax.fori_loop(0, rws[k] // sub, add_sib, 0)
                if q == 3 and k != whole:
                    for h, dev in enumerate((xn, yn)):
                        half = rws[k] // 2
                        cp = rcopy(cb[k].at[2, pl.ds(h * half, half), :], hb[k].at[h], h_s.at[k, h], h_r.at[k, h], dev)
                        cp.start()
                        sends.append(cp)
            if k == whole:
                cp = pltpu.make_async_copy(cb[k], cbm_refs[k], csem.at[k])
                cp.start()
                kept.append(cp)

        made_order = [k for k in order if not pre[k]]
        uses = [bmap[k] for t, k in enumerate(made_order) if t == 0 or bmap[k] != bmap[made_order[t - 1]]]

        def b_copy(u):
            return pltpu.make_async_copy(b_refs[uses[u]], bbuf.at[u % 2, :, pl.ds(0, b_refs[uses[u]].shape[1])], bsem.at[u % 2])

        def matmuls(k, u):
            a_ref, p_ref = a_refs[k], p_refs[k]
            bb = bbuf.at[u % 2, :, pl.ds(0, cols[k])]
            ob = obuf.at[:, :, pl.ds(0, cols[k])]

            def a_copy(m, slot):
                return pltpu.make_async_copy(a_ref.at[m], abuf.at[slot], asem.at[slot])

            def o_copy(m, slot):
                return pltpu.make_async_copy(ob.at[slot], p_ref.at[pl.ds(pl.multiple_of(m * CH, CH), CH), :], osem.at[slot])

            for m in range(ASLOTS - 1):
                a_copy(m, m).start()

            def step(m, carry):
                slot, aslot = lax.rem(m, 2), lax.rem(m, ASLOTS)
                a_copy(m, aslot).wait()

                @pl.when(m + ASLOTS - 1 < nchs[k])
                def _():
                    a_copy(m + ASLOTS - 1, lax.rem(m + ASLOTS - 1, ASLOTS)).start()

                @pl.when(m >= 2)
                def _():
                    o_copy(m - 2, slot).wait()

                ob[slot] = _dot_nn(abuf[aslot], bb[...]).astype(BF16)
                o_copy(m, slot).start()
                return carry

            lax.fori_loop(0, nchs[k], step, 0)
            for m in (nchs[k] - 2, nchs[k] - 1):
                o_copy(m, m % 2).wait()

        b_copy(0).start()
        u, before = -1, None
        for t, k in enumerate(order):
            p_ref, rw = p_refs[k], rws[k]
            if not pre[k]:
                if before is None or bmap[k] != bmap[before]:
                    u += 1
                    b_copy(u).wait()
                    if u + 1 < len(uses):
                        b_copy(u + 1).start()
                before = k
                matmuls(k, u)

            for q, r in enumerate(CHIPS):
                _, owner = _flip(x, y, c_, r | 1)
                cp = rcopy(_rows(p_ref, owner, rw), r1[k].at[q], d_s.at[k, q], d_r.at[k, q], sib)
                cp.start()
                sends.append(cp)
            if last and k == made_order[0]:
                pass_bulk()
            if t >= 1:
                presum(order[t - 1])
            fetch(k)
            if t >= 2:
                relay(order[t - 2])
        presum(order[-1])
        relay(order[-2])
        relay(order[-1])

        if last:
            for r in range(1, NDEV):
                _, pb = _flip(x, y, c_, r)
                rcopy(srecv.at[me], srecv.at[pb], ssem_s.at[r], ssem_r.at[r], (x, y, c_)).wait_recv()
                rcopy(mrecv.at[me], mrecv.at[pb], msem_s.at[r], msem_r.at[r], (x, y, c_)).wait_recv()
            blk = brecv.at[_flip(x, y, c_, 1)[1]]
            rcopy(blk, blk, bsem_s.at[0], bsem_r.at[0], (x, y, c_)).wait_recv()
            for j, r in enumerate((5, 3, 7)):
                blk = brecv.at[_flip(x, y, c_, r)[1]]
                rcopy(blk, blk, fsem_s.at[j], fsem_r.at[j], (x, y, c_)).wait_recv()
            tot = srecv[0]
            btl = brecv[0].astype(F32)
            for s in range(1, NDEV):
                tot = tot + srecv[s]
                btl = btl + brecv[s].astype(F32)
            ssum_ref[...] = tot
            bsum_ref[...] = btl
            btot = mrecv[0]
            for s in range(1, NDEV):
                btot = btot + mrecv[s]
            gbada_ref[...] = btot
            rowm = lax.broadcasted_iota(jnp.int32, (8, MODW), 0)
            dmy = jnp.zeros((8, MODW), F32)
            for s in range(NDEV):
                drow = jnp.sum(jnp.where(rowm == me, mrecv[s], 0.0), axis=0, keepdims=True)
                dmy = jnp.where(rowm == s, drow, dmy)
            dmy_ref[...] = dmy

            for k in order:
                for q in range(2):
                    rcopy(cb[k].at[q], r2[k].at[q], i_s.at[k, q], i_r.at[k, q], (x, y, c_)).wait_recv()

                def add_far(j, carry, k=k):
                    rr = pl.ds(pl.multiple_of(j * sub, sub), sub)
                    t = o_refs[k][rr, :]
                    for q in range(2):
                        t = t + r2[k][q, rr, :].astype(F32)
                    o_refs[k][rr, :] = t
                    return carry

                lax.fori_loop(0, rws[k] // sub, add_far, 0)
        for cp in sends:
            cp.wait_send()
        for cp in kept:
            cp.wait()

    own = [jax.ShapeDtypeStruct((rws[k], cols[k]), F32) for k in range(nw)]
    parts = [jax.ShapeDtypeStruct((nchs[k] * CH, cols[k]), BF16) for k in range(nw) if not pre[k]]
    scratch = [pltpu.VMEM((4, rws[k], cols[k]), BF16) for k in range(nw)]
    scratch += [pltpu.VMEM((3, rws[k], cols[k]), BF16) for k in range(nw)]
    scratch += [pltpu.VMEM((2, rws[k] // 2, cols[k]), BF16) for k in range(nw)]
    scratch += [pltpu.VMEM((4, max(rws), max(cols)), BF16)]
    scratch += [pltpu.VMEM((ASLOTS, CH, SEQ), BF16), pltpu.VMEM((2, SEQ, max(cols)), BF16), pltpu.VMEM((2, CH, max(cols)), BF16)]
    scratch += [pltpu.SemaphoreType.DMA((ASLOTS,))] + [pltpu.SemaphoreType.DMA((2,))] * 2 + [pltpu.SemaphoreType.DMA((4,))]
    scratch += [pltpu.SemaphoreType.DMA((nw, 4))] * 2 + [pltpu.SemaphoreType.DMA((nw, 2))] * 2
    if last:
        out_shape = own + [jax.ShapeDtypeStruct((srows, DM), F32), jax.ShapeDtypeStruct((brows, DM), F32),
                           jax.ShapeDtypeStruct((8, MODW), F32), jax.ShapeDtypeStruct((8, MODW), F32)] + parts
        out_specs = [_VM] * (nw + 4) + [_HBM] * len(parts)
        scratch += [pltpu.VMEM((2, rws[k], cols[k]), BF16) for k in range(nw)]
        scratch += [pltpu.VMEM((NDEV, srows, DM), F32), pltpu.VMEM((NDEV, brows, DM), BF16), pltpu.VMEM((NDEV, 8, MODW), F32)]
        scratch += [pltpu.SemaphoreType.DMA((nw, 2))] * 2 + [pltpu.SemaphoreType.DMA((NDEV,))] * 4
        scratch += [pltpu.SemaphoreType.DMA((4,))] * 2 + [pltpu.SemaphoreType.DMA((3,))] * 2
        keep = nw + 4
    else:
        out_shape = own + [jax.ShapeDtypeStruct((3 if k == whole else 2, rws[k], cols[k]), BF16) for k in range(nw)] + parts
        out_specs = [_VM] * nw + [_HBM] * (nw + len(parts))
        scratch += [pltpu.SemaphoreType.DMA((nw,))]
        keep = 2 * nw
    outs = pl.pallas_call(
        body,
        name=name,
        out_shape=out_shape,
        in_specs=[_HBM] * (nw + nb) + [_VM] * (3 if last else 0),
        out_specs=out_specs,
        scratch_shapes=scratch,
        compiler_params=pltpu.CompilerParams(vmem_limit_bytes=60 * 1024 * 1024),
    )(*a3s, *bs, *(small or ()))
    return outs[:keep]


def _adam_update(w, g, m, v):
    m = ADAM_B1 * m + (1.0 - ADAM_B1) * g
    v = ADAM_B2 * v + (1.0 - ADAM_B2) * (g * g)
    m_hat = m / (1.0 - ADAM_B1 ** ADAM_STEP)
    v_hat = v / (1.0 - ADAM_B2 ** ADAM_STEP)
    return -ADAM_LR * (m_hat / (jnp.sqrt(v_hat) + ADAM_EPS) + ADAM_WD * w), m, v


def _adam_ada_call(w, m, v, dmy, cact_t):
    def body(w_ref, m_ref, v_ref, dmy_ref, ct_ref, g_ref, d_ref, mo_ref, vo_ref):
        g = jnp.zeros((DM // 4, MODW), F32)
        for s in range(NDEV):
            g = g + ct_ref[:, s : s + 1] * dmy_ref[s : s + 1, :]
        g_ref[...] = g
        d_ref[...], mo_ref[...], vo_ref[...] = _adam_update(w_ref[...], g, m_ref[...], v_ref[...])

    blk = pl.BlockSpec((DM // 4, MODW), lambda i: (i, 0))
    return pl.pallas_call(
        body,
        name="adam_ada",
        grid=(4,),
        out_shape=[jax.ShapeDtypeStruct(w.shape, F32)] * 4,
        in_specs=[blk] * 3 + [_const((8, MODW)), pl.BlockSpec((DM // 4, NDEV), lambda i: (i, 0))],
        out_specs=[blk] * 4,
        compiler_params=pltpu.CompilerParams(dimension_semantics=("arbitrary",), vmem_limit_bytes=VMEM_LIMIT),
    )(w, m, v, dmy, cact_t)


def _adam_sum_call(name, ws, owns, fars, ms, vs):
    n = len(ws)

    def body(*refs):
        for i in range(n):
            w, own, far, m, v = (refs[j * n + i] for j in range(5))
            g = own[...]
            for q in range(fars[i].shape[0]):
                g = g + far[q].astype(F32)
            refs[5 * n + i][...] = g
            refs[6 * n + i][...], refs[7 * n + i][...], refs[8 * n + i][...] = _adam_update(w[...], g, m[...], v[...])

    shapes = [jax.ShapeDtypeStruct(w.shape, F32) for w in ws]
    blks = [pl.BlockSpec((w.shape[0] // 2, w.shape[1]), lambda i: (i, 0)) for w in ws]
    fblks = [pl.BlockSpec((f.shape[0], w.shape[0] // 2, w.shape[1]), lambda i: (0, i, 0)) for w, f in zip(ws, fars)]
    outs = pl.pallas_call(
        body,
        name=name,
        grid=(2,),
        out_shape=shapes * 4,
        in_specs=blks * 2 + fblks + blks * 2,
        out_specs=blks * 4,
        compiler_params=pltpu.CompilerParams(dimension_semantics=("arbitrary",), vmem_limit_bytes=VMEM_LIMIT),
    )(*ws, *owns, *fars, *ms, *vs)
    return outs[:n], outs[n : 2 * n], outs[2 * n : 3 * n], outs[3 * n :]


def _adam_call(name, ws, gs, ms, vs):
    n = len(ws)

    def body(*refs):
        for i in range(n):
            w, g, m, v = (refs[j * n + i][...] for j in range(4))
            refs[4 * n + i][...], refs[5 * n + i][...], refs[6 * n + i][...] = _adam_update(w, g, m, v)

    shapes = [jax.ShapeDtypeStruct(w.shape, F32) for w in ws]
    outs = pl.pallas_call(
        body,
        name=name,
        out_shape=shapes * 3,
        in_specs=[_VM] * (4 * n),
        out_specs=[_VM] * (3 * n),
        compiler_params=pltpu.CompilerParams(vmem_limit_bytes=VMEM_LIMIT),
    )(*ws, *gs, *ms, *vs)
    return outs[:n], outs[n : 2 * n], outs[2 * n :]


def kernel(x, c, w_ada, b_ada, g_norm1, w_in, dw_w, dw_b, conv_ln_g, conv_ln_b, w_conv_pw, w_pool_group, pool_scale, w_out, g_norm2, w_ffn_gate, w_ffn_up, w_ffn_down, g_final, loss_target, m_w_ada, m_b_ada, m_g_norm1, m_w_in, m_dw_w, m_dw_b, m_conv_ln_g, m_conv_ln_b, m_w_conv_pw, m_w_pool_group, m_pool_scale, m_w_out, m_g_norm2, m_w_ffn_gate, m_w_ffn_up, m_w_ffn_down, m_g_final, v_w_ada, v_b_ada, v_g_norm1, v_w_in, v_dw_w, v_dw_b, v_conv_ln_g, v_conv_ln_b, v_w_conv_pw, v_w_pool_group, v_pool_scale, v_w_out, v_g_norm2, v_w_ffn_gate, v_w_ffn_up, v_w_ffn_down, v_g_final):
    me = 4 * lax.axis_index("x") + 2 * lax.axis_index("y") + lax.axis_index("c")
    xs, tgt = x[0], loss_target[0]
    b_my = lax.dynamic_slice(b_ada, (0, me * MODW), (1, MODW))
    win_t, wout, wpw, mod8, cact, dww8 = _gather_call(c, w_ada[0], b_my, dw_w[0], [w_in[0].T, w_out[0], w_conv_pw[0]])
    dww = jnp.pad(jnp.transpose(dww8, (1, 0, 2)).reshape(KCONV, CONVW), ((0, HALO_C - KCONV), (0, 0)))
    wgp = w_pool_group[0]

    h1b, uag, hc, hd, hsb3, pbv, ycb3, y, x1, wg_t, wu_t, wd = _fwd_mix_call(
        xs, mod8, g_norm1, win_t, dww, dw_b, conv_ln_g, conv_ln_b, wpw, wgp, pool_scale, wout,
        [w_ffn_gate[0].T, w_ffn_up[0].T, w_ffn_down[0]])
    h2b, a3, dfb, dg3, du3, dx1, facc = _ffn_call(x1, tgt, mod8, g_norm2, g_final.reshape(1, DM), wg_t, wu_t, wd)
    own_gate, own_up, own_down, s_gate, s_up, s_down = _wgrad_rs_call("wgrad_ffn", [dg3, du3, a3], [0, 0, 1], [h2b, dfb], (2, 0, 1))
    gx, dub3, macc, ddw, dwg, p_out, p_pw, f_gate, f_up, f_down = _bwd_mix_call(
        dx1, xs, y, uag, hc, hd, pbv, mod8, g_norm1, win_t, dww, conv_ln_g, conv_ln_b, wpw, wgp, pool_scale, wout, ycb3, hsb3,
        [s_gate, s_up, s_down])
    spack = jnp.concatenate([macc[3:4], facc[3:4], facc[4:5], macc[4:6], facc[5:6], jnp.zeros((2, DM), F32)], axis=0)
    bulk = jnp.concatenate([ddw.reshape(HALO_C // 2, DM), dwg.reshape(-1, DM)], axis=0).astype(BF16)
    dmodp = jnp.concatenate([macc[0:3], facc[0:3]], axis=0).reshape(8, MODW)
    g_in_t, g_out, g_pw, ssum, bsum, gbada, dmy = _wgrad_rs_call(
        "wgrad_rs", [dub3, p_out, p_pw], [0, None, None], [h1b], (1, 2, 0), small=(spack, bulk, dmodp))
    g_wada, d_wada, m_wada, v_wada = _adam_ada_call(w_ada[0], m_w_ada[0], v_w_ada[0], dmy, cact.T)
    turned = ("w_in", "w_ffn_gate", "w_ffn_up")
    ffn = ("w_ffn_gate", "w_ffn_up", "w_ffn_down")
    ffn_given = dict(w_ffn_gate=(w_ffn_gate, m_w_ffn_gate, v_w_ffn_gate), w_ffn_up=(w_ffn_up, m_w_ffn_up, v_w_ffn_up),
                     w_ffn_down=(w_ffn_down, m_w_ffn_down, v_w_ffn_down))

    def ffn_work(j):
        return [ffn_given[n][j][0].T if n in turned else ffn_given[n][j][0] for n in ffn]

    g_ffn, d_ffn, m_ffn, v_ffn = _adam_sum_call(
        "adam_ffn", ffn_work(0), [own_gate, own_up, own_down], [f_gate, f_up, f_down], ffn_work(1), ffn_work(2))

    loss = ssum[5, 0]
    ddw_all = bsum[: HALO_C // 2].reshape(HALO_C, CONVW)[:KCONV]
    grads = {
        "w_ada": g_wada,
        "b_ada": gbada.reshape(1, 6 * DM),
        "g_norm1": ssum[0:1],
        "w_in": g_in_t,
        "dw_w": lax.dynamic_slice(ddw_all, (0, me * (CONVW // NDEV)), (KCONV, CONVW // NDEV)),
        "dw_b": ssum[3:4, 0:CONVW],
        "conv_ln_g": ssum[3:4, CONVW:DM],
        "conv_ln_b": ssum[4:5, 0:CONVW],
        "w_conv_pw": g_pw,
        "w_pool_group": bsum[HALO_C // 2 :].reshape(len(WINS) * PGD, PGD),
        "pool_scale": ssum[4:5, CONVW:DM],
        "w_out": g_out,
        "g_norm2": ssum[1:2],
        "w_ffn_gate": g_ffn[0],
        "w_ffn_up": g_ffn[1],
        "w_ffn_down": g_ffn[2],
        "g_final": ssum[2:3],
    }
    given = dict(w_ada=(w_ada, m_w_ada, v_w_ada), b_ada=(b_ada, m_b_ada, v_b_ada), g_norm1=(g_norm1, m_g_norm1, v_g_norm1),
                 w_in=(w_in, m_w_in, v_w_in), dw_w=(dw_w, m_dw_w, v_dw_w), dw_b=(dw_b, m_dw_b, v_dw_b),
                 conv_ln_g=(conv_ln_g, m_conv_ln_g, v_conv_ln_g), conv_ln_b=(conv_ln_b, m_conv_ln_b, v_conv_ln_b),
                 w_conv_pw=(w_conv_pw, m_w_conv_pw, v_w_conv_pw), w_pool_group=(w_pool_group, m_w_pool_group, v_w_pool_group),
                 pool_scale=(pool_scale, m_pool_scale, v_pool_scale), w_out=(w_out, m_w_out, v_w_out), g_norm2=(g_norm2, m_g_norm2, v_g_norm2),
                 w_ffn_gate=(w_ffn_gate, m_w_ffn_gate, v_w_ffn_gate), w_ffn_up=(w_ffn_up, m_w_ffn_up, v_w_ffn_up),
                 w_ffn_down=(w_ffn_down, m_w_ffn_down, v_w_ffn_down), g_final=(g_final, m_g_final, v_g_final))
    names = list(given)
    groups = [["w_in", "w_out", "w_conv_pw"],
              ["b_ada", "g_norm1", "dw_w", "dw_b", "conv_ln_g", "conv_ln_b", "w_pool_group", "pool_scale", "g_norm2", "g_final"]]

    def work(n, a):
        return a[0].T if n in turned else a.reshape(grads[n].shape)

    def full(n, a):
        return a.T[None] if n in turned else a.reshape(given[n][0].shape)

    delta, new_m, new_v = {"w_ada": d_wada}, {"w_ada": m_wada}, {"w_ada": v_wada}
    for i, n in enumerate(ffn):
        delta[n], new_m[n], new_v[n] = d_ffn[i], m_ffn[i], v_ffn[i]
    for gi, grp in enumerate(groups):
        ds, ms, vs = _adam_call(f"adam{gi}", [work(n, given[n][0]) for n in grp], [grads[n] for n in grp],
                                [work(n, given[n][1]) for n in grp], [work(n, given[n][2]) for n in grp])
        for n, d_, m_, v_ in zip(grp, ds, ms, vs):
            delta[n], new_m[n], new_v[n] = d_, m_, v_

    return (loss, gx.reshape(x.shape), *[full(n, grads[n]) for n in names], *[full(n, delta[n]) for n in names],
            *[full(n, new_m[n]) for n in names], *[full(n, new_v[n]) for n in names])
```

```python
import functools

import jax
import jax.numpy as jnp
from jax import lax
from jax.experimental import pallas as pl
from jax.experimental.pallas import tpu as pltpu

F32, BF16 = jnp.float32, jnp.bfloat16
SEQ, DM = 2048, 1024
CONVW, POOLW = 512, 512
KCONV = 31
WINS = (2, 4, 8, 16)
PGD = 128
DFF = 2816
NDEV = 8
MODW = 6 * DM // NDEV
EPS = 1e-6
TOK = 256
NTILE = SEQ // TOK
CH = 256
NCH = DFF // CH
HALO_C, HALO_P = 32, 16
MESH = pl.DeviceIdType.MESH
VMEM_LIMIT = 56 * 1024 * 1024
ADAM_LR, ADAM_B1, ADAM_B2, ADAM_EPS, ADAM_WD, ADAM_STEP = 0.001, 0.9, 0.999, 1e-08, 0.01, 10
HI = lax.Precision.HIGHEST

_VM = pl.BlockSpec(memory_space=pltpu.VMEM)
_HBM = pl.BlockSpec(memory_space=pltpu.HBM)


def _place():
    x, y, c = lax.axis_index("x"), lax.axis_index("y"), lax.axis_index("c")
    return x, y, c, 4 * x + 2 * y + c


def _flip(x, y, c, r):
    px = 1 - x if r & 4 else x
    py = 1 - y if r & 2 else y
    pc = 1 - c if r & 1 else c
    return (px, py, pc), 4 * px + 2 * py + pc


def _rows(ref, blk, n):
    return ref.at[pl.ds(pl.multiple_of(blk * n, 16), n), :]


def _sig(z):
    return jax.nn.sigmoid(z)


def _dot_nt(a, b):
    return lax.dot_general(a, b, (((1,), (1,)), ((), ())), preferred_element_type=F32)


def _dot_nn(a, b):
    return lax.dot_general(a, b, (((1,), (0,)), ((), ())), preferred_element_type=F32)


def _dot_tn(a, b):
    return lax.dot_general(a, b, (((0,), (0,)), ((), ())), preferred_element_type=F32)


def _rcopy(src, dst, ss, rs, dev):
    return pltpu.make_async_remote_copy(src_ref=src, dst_ref=dst, send_sem=ss, recv_sem=rs, device_id=dev, device_id_type=MESH)


def _ag_sems(nw):
    return ([pltpu.SemaphoreType.DMA((nw, 3))] * 2 + [pltpu.SemaphoreType.DMA((nw, 4))] * 2 + [pltpu.SemaphoreType.DMA((nw, 2))] * 2
            + [pltpu.SemaphoreType.DMA((nw,))])


def _ag_plan(sbufs, g_refs, rws, sems):
    wsem_s, wsem_r, fsem_s, fsem_r, hsem_s, hsem_r, lsem = sems
    x, y, c_, me = _place()
    here = (x, y, c_)
    plans = []
    for k, (sb, g, n) in enumerate(zip(sbufs, g_refs, rws)):

        def blk(r, half=None, g=g, n=n):
            b = _flip(x, y, c_, r)[1]
            if half is None:
                return _rows(g, b, n)
            return g.at[pl.ds(pl.multiple_of(b * n + half * (n // 2), 16), n // 2), :]

        def same(ref, ss, rs, j, dev, k=k):
            return _rcopy(ref, ref, ss.at[k, j], rs.at[k, j], dev)

        sib, xn, yn = (_flip(x, y, c_, r)[0] for r in (1, 4, 2))
        plans.append(dict(
            local=pltpu.make_async_copy(sb, blk(0), lsem.at[k]),
            first=[_rcopy(sb, blk(0), wsem_s.at[k, j], wsem_r.at[k, j], dev) for j, dev in enumerate((sib, xn, yn))],
            got=[_rcopy(sb, blk(r), wsem_s.at[k, j], wsem_r.at[k, j], here) for j, r in enumerate((1, 4, 2))],
            passes=[same(blk(4), fsem_s, fsem_r, 0, sib), same(blk(2), fsem_s, fsem_r, 1, sib),
                    same(blk(6, 0), fsem_s, fsem_r, 2, sib), same(blk(6, 1), fsem_s, fsem_r, 3, sib)],
            passed=[same(blk(5), fsem_s, fsem_r, 0, here), same(blk(3), fsem_s, fsem_r, 1, here),
                    same(blk(7, 0), fsem_s, fsem_r, 2, here), same(blk(7, 1), fsem_s, fsem_r, 3, here)],
            halves=[same(blk(4, 0), hsem_s, hsem_r, 0, yn), same(blk(2, 1), hsem_s, hsem_r, 1, xn)],
            halved=[same(blk(6, 0), hsem_s, hsem_r, 0, here), same(blk(6, 1), hsem_s, hsem_r, 1, here)],
        ))
    return plans


def _ag_start(sbufs, g_refs, rws, sems):
    for p in _ag_plan(sbufs, g_refs, rws, sems):
        p["local"].start()
        for cp in p["first"]:
            cp.start()


def _ag_pass_on(sbufs, g_refs, rws, sems):
    plans = _ag_plan(sbufs, g_refs, rws, sems)
    for p in plans:
        for j in (0, 1):
            p["got"][j + 1].wait_recv()
            p["halves"][j].start()
            p["passes"][j].start()
    for p in plans:
        for j in (0, 1):
            p["halved"][j].wait_recv()
            p["passes"][j + 2].start()


def _ag_finish(sbufs, g_refs, rws, sems):
    plans = _ag_plan(sbufs, g_refs, rws, sems)
    for p in plans:
        p["got"][0].wait_recv()
        for cp in p["passed"]:
            cp.wait_recv()
    for p in plans:
        for cp in p["first"] + p["passes"] + p["halves"]:
            cp.wait_send()
        p["local"].wait()


def _gather_call(c, w_ada, b_my, dww, shards):
    nw = len(shards)
    rws = [s.shape[0] for s in shards]

    def body(*refs):
        c_ref, wada_ref, bmy_ref, dww_ref = refs[:4]
        s_refs = refs[4 : 4 + nw]
        g_refs = refs[4 + nw : 4 + 2 * nw]
        mod8_ref, cact_ref, dww8_ref = refs[4 + 2 * nw : 7 + 2 * nw]
        crecv, msend, mrecv = refs[7 + 2 * nw : 10 + 2 * nw]
        sbufs = refs[10 + 2 * nw : 10 + 3 * nw]
        csem_s, csem_r, dsem_s, dsem_r, msem_s, msem_r = refs[10 + 3 * nw : 16 + 3 * nw]
        ag_sems = refs[16 + 3 * nw :]
        x, y, c_, me = _place()

        def rcopy(src, dst, ss, rs, dev):
            return pltpu.make_async_remote_copy(src_ref=src, dst_ref=dst, send_sem=ss, recv_sem=rs, device_id=dev, device_id_type=MESH)

        crecv[me] = jnp.broadcast_to(c_ref[...], (8, DM))
        dww8_ref[me] = dww_ref[...]
        small = []
        for r in range(1, NDEV):
            dev, _ = _flip(x, y, c_, r)
            small.append(rcopy(crecv.at[me], crecv.at[me], csem_s.at[r], csem_r.at[r], dev))
            small.append(rcopy(dww8_ref.at[me], dww8_ref.at[me], dsem_s.at[r], dsem_r.at[r], dev))
        for cp in small:
            cp.start()

        for k in range(nw):
            sbufs[k][...] = s_refs[k][...].astype(BF16)
        _ag_start(sbufs, g_refs, rws, ag_sems)

        rowid = lax.broadcasted_iota(jnp.int32, (8, DM), 0)
        for r in range(1, NDEV):
            _, pb = _flip(x, y, c_, r)
            rcopy(crecv.at[me], crecv.at[pb], csem_s.at[r], csem_r.at[r], (x, y, c_)).wait_recv()
        call = jnp.zeros((8, DM), F32)
        for s in range(NDEV):
            call = jnp.where(rowid == s, crecv[s], call)
        cact = call * _sig(call)
        cact_ref[...] = cact
        modp = jnp.dot(cact, wada_ref[...], precision=HI, preferred_element_type=F32) + bmy_ref[...]
        rowm = lax.broadcasted_iota(jnp.int32, (8, MODW), 0)
        for b in range(NDEV):
            row = jnp.sum(jnp.where(rowm == b, modp, 0.0), axis=0, keepdims=True)
            msend[b] = jnp.broadcast_to(row, (8, MODW))
        mrecv[me] = msend[me]
        msends = []
        for r in range(1, NDEV):
            dev, pb = _flip(x, y, c_, r)
            cp = rcopy(msend.at[pb], mrecv.at[me], msem_s.at[r], msem_r.at[r], dev)
            cp.start()
            msends.append(cp)

        _ag_pass_on(sbufs, g_refs, rws, ag_sems)

        for r in range(1, NDEV):
            _, pb = _flip(x, y, c_, r)
            rcopy(msend.at[pb], mrecv.at[pb], msem_s.at[r], msem_r.at[r], (x, y, c_)).wait_recv()
        for s in range(NDEV):
            mod8_ref[:, s * MODW : (s + 1) * MODW] = mrecv[s]

        _ag_finish(sbufs, g_refs, rws, ag_sems)
        for r in range(1, NDEV):
            _, pb = _flip(x, y, c_, r)
            rcopy(dww8_ref.at[me], dww8_ref.at[pb], dsem_s.at[r], dsem_r.at[r], (x, y, c_)).wait_recv()
        for cp in small + msends:
            cp.wait_send()

    out_shape = [jax.ShapeDtypeStruct((NDEV * s.shape[0], s.shape[1]), BF16) for s in shards]
    out_shape += [
        jax.ShapeDtypeStruct((8, 6 * DM), F32),
        jax.ShapeDtypeStruct((8, DM), F32),
        jax.ShapeDtypeStruct((NDEV,) + dww.shape, F32),
    ]
    scratch = [pltpu.VMEM((NDEV, 8, DM), F32), pltpu.VMEM((NDEV, 8, MODW), F32), pltpu.VMEM((NDEV, 8, MODW), F32)]
    scratch += [pltpu.VMEM(s.shape, BF16) for s in shards]
    scratch += [pltpu.SemaphoreType.DMA((NDEV,))] * 6 + _ag_sems(nw)
    return pl.pallas_call(
        body,
        name="gather",
        out_shape=out_shape,
        in_specs=[_VM] * (4 + nw),
        out_specs=[_HBM] * nw + [_VM] * 3,
        scratch_shapes=scratch,
        compiler_params=pltpu.CompilerParams(vmem_limit_bytes=VMEM_LIMIT),
    )(c, w_ada, b_my, dww, *shards)


def _const(shape):
    return pl.BlockSpec(shape, lambda i: (0,) * len(shape))


def _tile(width, rev=False):
    if rev:
        return pl.BlockSpec((TOK, width), lambda i: (NTILE - 1 - i, 0))
    return pl.BlockSpec((TOK, width), lambda i: (i, 0))


def _tile3(nch, rev=False):
    if rev:
        return pl.BlockSpec((nch, CH, TOK), lambda i: (0, 0, NTILE - 1 - i))
    return pl.BlockSpec((nch, CH, TOK), lambda i: (0, 0, i))


def _put3(ref, val):
    for j in range(val.shape[1] // CH):
        ref[j] = val[:, j * CH : (j + 1) * CH].T


def _norm_mod(x, g, sc, sh):
    r = lax.rsqrt(jnp.mean(x * x, axis=-1, keepdims=True) + EPS)
    xr = x * r
    return r, xr, xr * g * (1.0 + sc) + sh


def _fwd_mix_call(x, mod8, g1, win_t, dww, dwb, lng, lnb, wpw, wg, psc, wout, shards):
    ns = len(shards)
    rws = [s.shape[0] for s in shards]

    def body(x_ref, mod_ref, g1_ref, win_ref, dww_ref, dwb_ref, lng_ref, lnb_ref, wpw_ref, wg_ref, psc_ref, wout_ref, *rest):
        s_refs = rest[:ns]
        h1b_ref, uag_ref, hc_ref, hd_ref, hsb3_ref, pb_ref, ycb3_ref, y_ref, x1_ref = rest[ns : ns + 9]
        g_refs = rest[ns + 9 : 2 * ns + 9]
        hc_ext, up_ext, ycb_ref, stage = rest[2 * ns + 9 : 2 * ns + 13]
        sbufs = rest[2 * ns + 13 : 3 * ns + 13]
        ssem = rest[3 * ns + 13]
        ag_sems = rest[3 * ns + 14 :]
        i = pl.program_id(0)

        @pl.when(i == 0)
        def _():
            for k in range(ns):
                cp = pltpu.make_async_copy(s_refs[k], stage, ssem)
                cp.start()
                cp.wait()
                sbufs[k][...] = stage[...].astype(BF16)
            _ag_start(sbufs, g_refs, rws, ag_sems)
            hc_ext[0:HALO_C, :] = jnp.zeros((HALO_C, CONVW), F32)
            up_ext[0:HALO_P, :] = jnp.zeros((HALO_P, POOLW), F32)

        x = x_ref[...]
        sh1, sc1, gt1 = mod_ref[0:1, 0:DM], mod_ref[0:1, DM : 2 * DM], mod_ref[0:1, 2 * DM : 3 * DM]
        _, _, h1 = _norm_mod(x, g1_ref[...], sc1, sh1)
        h1b = h1.astype(BF16)
        h1b_ref[...] = h1b
        u = _dot_nt(h1b, win_ref[...])
        uag_ref[...] = u[:, : 2 * CONVW]
        hc = u[:, :CONVW] * _sig(u[:, CONVW : 2 * CONVW])
        hc_ref[...] = hc
        hc_ext[HALO_C : HALO_C + TOK, :] = hc
        up_ext[HALO_P : HALO_P + TOK, :] = u[:, 2 * CONVW :]

        acc = jnp.zeros((TOK, CONVW), F32)
        for k in range(KCONV):
            acc = acc + dww_ref[k : k + 1, :] * hc_ext[pl.ds(HALO_C - (KCONV - 1) + k, TOK), :]
        hd = acc + dwb_ref[...]
        hd_ref[...] = hd
        hc_ext[0:HALO_C, :] = hc_ext[TOK : TOK + HALO_C, :]
        mu = jnp.mean(hd, axis=-1, keepdims=True)
        dlt = hd - mu
        rstd = lax.rsqrt(jnp.mean(dlt * dlt, axis=-1, keepdims=True) + EPS)
        hl = dlt * rstd * lng_ref[...] + lnb_ref[...]
        hsb = (hl * _sig(hl)).astype(BF16)
        _put3(hsb3_ref, hsb)
        ycb_ref[:, 0:CONVW] = _dot_nn(hsb, wpw_ref[...]).astype(BF16)

        tg = i * TOK + lax.broadcasted_iota(jnp.int32, (TOK, 1), 0)
        for g, w in enumerate(WINS):
            ln = slice(PGD * g, PGD * (g + 1))
            v = up_ext[pl.ds(HALO_P, TOK), ln]
            ssum = v
            for d in range(1, w):
                ssum = ssum + up_ext[pl.ds(HALO_P - d, TOK), ln]
            cnt = jnp.minimum(tg + 1, w).astype(F32)
            pb = (ssum / cnt - v).astype(BF16)
            pb_ref[:, ln] = pb
            z = _dot_nn(pb, wg_ref[g].astype(BF16))
            ycb_ref[:, CONVW + PGD * g : CONVW + PGD * (g + 1)] = (z * psc_ref[:, ln]).astype(BF16)
        up_ext[0:HALO_P, :] = up_ext[TOK : TOK + HALO_P, :]

        _put3(ycb3_ref, ycb_ref[...])
        yv = _dot_nn(ycb_ref[...], wout_ref[...])
        y_ref[...] = yv
        x1_ref[...] = x + gt1 * yv

        @pl.when(i == NTILE - 1)
        def _():
            _ag_pass_on(sbufs, g_refs, rws, ag_sems)
            _ag_finish(sbufs, g_refs, rws, ag_sems)

    outs = [(DM, BF16), (2 * CONVW, F32), (CONVW, F32), (CONVW, F32), (-CONVW, BF16), (POOLW, BF16), (-DM, BF16), (DM, F32), (DM, F32)]
    return pl.pallas_call(
        body,
        name="fwd_mix",
        grid=(NTILE,),
        out_shape=[jax.ShapeDtypeStruct((SEQ, w) if w > 0 else (-w // CH, CH, SEQ), d) for w, d in outs]
        + [jax.ShapeDtypeStruct((NDEV * s.shape[0], s.shape[1]), BF16) for s in shards],
        in_specs=[_tile(DM), _const((8, 6 * DM)), _const((1, DM)), _const(win_t.shape), _const(dww.shape), _const((1, CONVW)),
                  _const((1, CONVW)), _const((1, CONVW)), _const(wpw.shape), _const(wg.shape), _const((1, POOLW)), _const(wout.shape)]
        + [_HBM] * ns,
        out_specs=[_tile(w) if w > 0 else _tile3(-w // CH) for w, _ in outs] + [_HBM] * ns,
        scratch_shapes=[pltpu.VMEM((TOK + HALO_C, CONVW), F32), pltpu.VMEM((TOK + HALO_P, POOLW), F32), pltpu.VMEM((TOK, DM), BF16),
                        pltpu.VMEM(shards[0].shape, F32)] + [pltpu.VMEM(s.shape, BF16) for s in shards]
        + [pltpu.SemaphoreType.DMA] + _ag_sems(ns),
        compiler_params=pltpu.CompilerParams(dimension_semantics=("arbitrary",), vmem_limit_bytes=VMEM_LIMIT),
    )(x, mod8, g1, win_t, dww, dwb, lng, lnb, wpw, wg, psc, wout, *shards)


def _ffn_call(x1, tgt, mod8, g2, gf, wg_t, wu_t, wd):
    def body(x1_ref, tgt_ref, mod_ref, g2_ref, gf_ref, wg_hbm, wu_hbm, wd_hbm,
             h2b_ref, a3_ref, dfb_ref, dg3_ref, du3_ref, dx1_ref, acc_ref,
             wg_ref, wu_ref, wd_ref, wsem):
        i = pl.program_id(0)

        @pl.when(i == 0)
        def _():
            cps = [pltpu.make_async_copy(s, d, wsem.at[n]) for n, (s, d) in enumerate(((wg_hbm, wg_ref), (wu_hbm, wu_ref), (wd_hbm, wd_ref)))]
            for cp in cps:
                cp.start()
            acc_ref[...] = jnp.zeros((8, DM), F32)
            for cp in cps:
                cp.wait()

        x1 = x1_ref[...]
        sh2, sc2, gt2 = mod_ref[0:1, 3 * DM : 4 * DM], mod_ref[0:1, 4 * DM : 5 * DM], mod_ref[0:1, 5 * DM : 6 * DM]
        g2 = g2_ref[...]
        r2, xr, h2 = _norm_mod(x1, g2, sc2, sh2)
        h2b = h2.astype(BF16)
        h2b_ref[...] = h2b
        gate = _dot_nt(h2b, wg_ref[...])
        up = _dot_nt(h2b, wu_ref[...])
        ab = (gate * _sig(gate) * up).astype(BF16)
        _put3(a3_ref, ab)
        f = _dot_nn(ab, wd_ref[...])
        x2 = x1 + gt2 * f
        rf = lax.rsqrt(jnp.mean(x2 * x2, axis=-1, keepdims=True) + EPS)
        nf = x2 * rf
        gf_ = gf_ref[...]
        err = nf * gf_ - tgt_ref[...]
        loss = 0.5 * jnp.sum(jnp.sum(err * err, axis=-1, keepdims=True), axis=0, keepdims=True) * (1.0 / DM)
        dout = err * (1.0 / DM)
        dnf = dout * gf_
        dx2 = rf * (dnf - nf * jnp.mean(dnf * nf, axis=-1, keepdims=True))
        dfb = (gt2 * dx2).astype(BF16)
        dfb_ref[...] = dfb
        da = _dot_nt(dfb, wd_ref[...])
        sg = _sig(gate)
        dgb = (da * up * (sg * (1.0 + gate * (1.0 - sg)))).astype(BF16)
        dub = (da * (gate * sg)).astype(BF16)
        _put3(dg3_ref, dgb)
        _put3(du3_ref, dub)
        dh2 = _dot_nn(dgb, wg_ref[...]) + _dot_nn(dub, wu_ref[...])
        dn2 = dh2 * (1.0 + sc2)
        dxr = dn2 * g2
        dx1_ref[...] = dx2 + r2 * (dxr - xr * jnp.mean(dxr * xr, axis=-1, keepdims=True))

        def colsum(v):
            return jnp.sum(v, axis=0, keepdims=True)

        acc_ref[0:1, :] += colsum(dh2)
        acc_ref[1:2, :] += colsum(dh2 * (xr * g2))
        acc_ref[2:3, :] += colsum(dx2 * f)
        acc_ref[3:4, :] += colsum(dn2 * xr)
        acc_ref[4:5, :] += colsum(dout * nf)
        acc_ref[5:6, :] += jnp.broadcast_to(loss, (1, DM))

    c3 = _tile3(NCH)
    return pl.pallas_call(
        body,
        name="ffn",
        grid=(NTILE,),
        out_shape=[jax.ShapeDtypeStruct((SEQ, DM), BF16), jax.ShapeDtypeStruct((NCH, CH, SEQ), BF16), jax.ShapeDtypeStruct((SEQ, DM), BF16),
                   jax.ShapeDtypeStruct((NCH, CH, SEQ), BF16), jax.ShapeDtypeStruct((NCH, CH, SEQ), BF16),
                   jax.ShapeDtypeStruct((SEQ, DM), F32), jax.ShapeDtypeStruct((8, DM), F32)],
        in_specs=[_tile(DM), _tile(DM), _const((8, 6 * DM)), _const((1, DM)), _const((1, DM)), _HBM, _HBM, _HBM],
        out_specs=[_tile(DM), c3, _tile(DM), c3, c3, _tile(DM), _const((8, DM))],
        scratch_shapes=[pltpu.VMEM((DFF, DM), BF16)] * 3 + [pltpu.SemaphoreType.DMA((3,))],
        compiler_params=pltpu.CompilerParams(dimension_semantics=("arbitrary",), vmem_limit_bytes=VMEM_LIMIT),
    )(x1, tgt, mod8, g2, gf, wg_t, wu_t, wd)


def _bwd_mix_call(dx1, x, y, uag, hc, hd, pbv, mod8, g1, win_t, dww, lng, lnb, wpw, wg, psc, wout, ycb3, hsb3, sums):
    hpt = TOK // HALO_C
    ns = len(sums)

    def far_copies(s_refs, f_refs, fs_s, fs_r):
        x_, y_, c_, _ = _place()
        return [_rcopy(s_refs[k].at[q], f_refs[k].at[q], fs_s.at[k, q], fs_r.at[k, q], _flip(x_, y_, c_, r)[0])
                for k in range(ns) for q, r in enumerate((4, 2, 6)[: sums[k].shape[0]])]

    def body(dx1_ref, x_ref, y_ref, uag_ref, hc_ref, halo_ref, hd_ref, pb_ref, mod_ref, g1_ref, win_ref, dww_ref, lng_ref, lnb_ref,
             wpw_ref, wg_ref, psc_ref, wout_ref, ycb3_ref, hsb3_ref, *rest):
        s_refs = rest[:ns]
        gx_ref, dub3_ref, acc_ref, ddw_ref, dwg_ref, pout_ref, ppw_ref = rest[ns : ns + 7]
        f_refs = rest[ns + 7 : 2 * ns + 7]
        d_ext, q_ext, hcx, dub_ref, hrot, drot, aout, apw, fs_s, fs_r = rest[2 * ns + 7 :]
        i = pl.program_id(0)
        it = NTILE - 1 - i

        @pl.when(i == 0)
        def _():
            for cp in far_copies(s_refs, f_refs, fs_s, fs_r):
                cp.start()
            aout[...] = jnp.zeros((DM, DM), F32)
            apw[...] = jnp.zeros((CONVW, CONVW), F32)
            d_ext[TOK : TOK + HALO_C, :] = jnp.zeros((HALO_C, CONVW), F32)
            q_ext[TOK : TOK + HALO_P, :] = jnp.zeros((HALO_P, POOLW), F32)
            acc_ref[...] = jnp.zeros((8, DM), F32)
            ddw_ref[...] = jnp.zeros((HALO_C, CONVW), F32)
            dwg_ref[...] = jnp.zeros((len(WINS) * PGD, PGD), F32)

        def colsum(v):
            return jnp.sum(v, axis=0, keepdims=True)

        dx1 = dx1_ref[...]
        x = x_ref[...]
        sh1, sc1, gt1 = mod_ref[0:1, 0:DM], mod_ref[0:1, DM : 2 * DM], mod_ref[0:1, 2 * DM : 3 * DM]
        acc_ref[2:3, :] += colsum(dx1 * y_ref[...])
        dyb = (gt1 * dx1).astype(BF16)
        for j in range(DM // CH):
            aout[j * CH : (j + 1) * CH, :] += _dot_nn(ycb3_ref[j], dyb)
        dycat = _dot_nt(dyb, wout_ref[...])

        hd = hd_ref[...]
        mu = jnp.mean(hd, axis=-1, keepdims=True)
        dlt = hd - mu
        rstd = lax.rsqrt(jnp.mean(dlt * dlt, axis=-1, keepdims=True) + EPS)
        xhat = dlt * rstd
        lng = lng_ref[...]
        hl = xhat * lng + lnb_ref[...]
        sgl = _sig(hl)
        dycb = dycat[:, :CONVW].astype(BF16)
        for j in range(CONVW // CH):
            apw[j * CH : (j + 1) * CH, :] += _dot_nn(hsb3_ref[j], dycb)
        dhl = _dot_nt(dycb, wpw_ref[...]) * (sgl * (1.0 + hl * (1.0 - sgl)))
        acc_ref[5:6, 0:CONVW] += colsum(dhl)
        acc_ref[4:5, CONVW:DM] += colsum(dhl * xhat)
        dxh = dhl * lng
        dhd = rstd * (dxh - jnp.mean(dxh, axis=-1, keepdims=True) - xhat * jnp.mean(dxh * xhat, axis=-1, keepdims=True))
        acc_ref[4:5, 0:CONVW] += colsum(dhd)

        hcx[0:HALO_C, :] = jnp.where(it == 0, 0.0, halo_ref[...])
        hcx[HALO_C : HALO_C + TOK, :] = hc_ref[...]
        d_ext[0:TOK, :] = dhd
        for b in range(1, 8):
            hrot[b - 1] = hcx[pl.ds(b, TOK + HALO_C - 8), :]
            drot[b - 1] = d_ext[pl.ds(b, TOK + HALO_C - 8), :]

        def tap(base, rot, off):
            a, b = divmod(off, 8)
            return base[pl.ds(8 * a, TOK), :] if b == 0 else rot[b - 1, pl.ds(8 * a, TOK), :]

        dhc = jnp.zeros((TOK, CONVW), F32)
        for k in range(KCONV):
            ddw_ref[k : k + 1, :] += colsum(dhd * tap(hcx, hrot, HALO_C - (KCONV - 1) + k))
            dhc = dhc + dww_ref[k : k + 1, :] * tap(d_ext, drot, KCONV - 1 - k)
        d_ext[TOK : TOK + HALO_C, :] = d_ext[0:HALO_C, :]
        ua, ug = uag_ref[:, 0:CONVW], uag_ref[:, CONVW : 2 * CONVW]
        sgg = _sig(ug)
        dub_ref[:, 0:CONVW] = (dhc * sgg).astype(BF16)
        dub_ref[:, CONVW : 2 * CONVW] = (dhc * ua * sgg * (1.0 - sgg)).astype(BF16)

        tg = it * TOK + lax.broadcasted_iota(jnp.int32, (TOK, 1), 0)
        for g, w in enumerate(WINS):
            ln = slice(PGD * g, PGD * (g + 1))
            wgb = wg_ref[g].astype(BF16)
            pb = pb_ref[:, ln]
            dyp = dycat[:, CONVW + PGD * g : CONVW + PGD * (g + 1)]
            acc_ref[5:6, CONVW + PGD * g : CONVW + PGD * (g + 1)] += colsum(dyp * _dot_nn(pb, wgb))
            dzb = (dyp * psc_ref[:, ln]).astype(BF16)
            dwg_ref[PGD * g : PGD * (g + 1), :] += _dot_tn(pb, dzb)
            dp = _dot_nt(dzb, wgb)
            cnt = jnp.minimum(tg + 1, w).astype(F32)
            q_ext[0:TOK, ln] = dp / cnt
            dv = -dp
            for d in range(w):
                dv = dv + q_ext[pl.ds(d, TOK), ln]
            dub_ref[:, 2 * CONVW + PGD * g : 2 * CONVW + PGD * (g + 1)] = dv.astype(BF16)
        q_ext[TOK : TOK + HALO_P, :] = q_ext[0:HALO_P, :]

        _put3(dub3_ref, dub_ref[...])
        dh1 = _dot_nn(dub_ref[...], win_ref[...])
        g1 = g1_ref[...]
        r1 = lax.rsqrt(jnp.mean(x * x, axis=-1, keepdims=True) + EPS)
        xr = x * r1
        acc_ref[0:1, :] += colsum(dh1)
        acc_ref[1:2, :] += colsum(dh1 * (xr * g1))
        dn1 = dh1 * (1.0 + sc1)
        acc_ref[3:4, :] += colsum(dn1 * xr)
        dxr = dn1 * g1
        gx_ref[...] = dx1 + r1 * (dxr - xr * jnp.mean(dxr * xr, axis=-1, keepdims=True))

        @pl.when(i == NTILE - 1)
        def _():
            pout_ref[...] = aout[...].astype(BF16)
            ppw_ref[...] = apw[...].astype(BF16)
            for cp in far_copies(s_refs, f_refs, fs_s, fs_r):
                cp.wait()

    halo = pl.BlockSpec((HALO_C, CONVW), lambda i: (jnp.maximum((NTILE - 1 - i) * hpt - 1, 0), 0))
    return pl.pallas_call(
        body,
        name="bwd_mix",
        grid=(NTILE,),
        out_shape=[jax.ShapeDtypeStruct((SEQ, DM), F32), jax.ShapeDtypeStruct((3 * CONVW // CH, CH, SEQ), BF16), jax.ShapeDtypeStruct((8, DM), F32),
                   jax.ShapeDtypeStruct((HALO_C, CONVW), F32), jax.ShapeDtypeStruct((len(WINS) * PGD, PGD), F32),
                   jax.ShapeDtypeStruct((DM, DM), BF16), jax.ShapeDtypeStruct((CONVW, CONVW), BF16)]
        + [jax.ShapeDtypeStruct(s.shape, s.dtype) for s in sums],
        in_specs=[_tile(DM, True), _tile(DM, True), _tile(DM, True), _tile(2 * CONVW, True), _tile(CONVW, True), halo, _tile(CONVW, True),
                  _tile(POOLW, True), _const((8, 6 * DM)), _const((1, DM)), _const(win_t.shape), _const(dww.shape), _const((1, CONVW)),
                  _const((1, CONVW)), _const(wpw.shape), _const(wg.shape), _const((1, POOLW)), _const(wout.shape),
                  _tile3(DM // CH, True), _tile3(CONVW // CH, True)] + [_HBM] * ns,
        out_specs=[_tile(DM, True), _tile3(3 * CONVW // CH, True), _const((8, DM)),
                   _const((HALO_C, CONVW)), _const((len(WINS) * PGD, PGD)), _const((DM, DM)), _const((CONVW, CONVW))] + [_HBM] * ns,
        scratch_shapes=[pltpu.VMEM((TOK + HALO_C, CONVW), F32), pltpu.VMEM((TOK + HALO_P, POOLW), F32), pltpu.VMEM((TOK + HALO_C, CONVW), F32),
                        pltpu.VMEM((TOK, 3 * CONVW), BF16)] + [pltpu.VMEM((7, TOK + HALO_C - 8, CONVW), F32)] * 2
        + [pltpu.VMEM((DM, DM), F32), pltpu.VMEM((CONVW, CONVW), F32)] + [pltpu.SemaphoreType.DMA((ns, 3))] * 2,
        compiler_params=pltpu.CompilerParams(dimension_semantics=("arbitrary",), vmem_limit_bytes=VMEM_LIMIT),
    )(dx1, x, y, uag, hc, hc, hd, pbv, mod8, g1, win_t, dww, lng, lnb, wpw, wg, psc, wout, ycb3, hsb3, *sums)


CHIPS = (0, 4, 2, 6)
ASLOTS = 3


def _wgrad_rs_call(name, a3s, bmap, bs, order, small=None):
    nw = len(a3s)
    pre = [a.ndim == 2 for a in a3s]
    nchs = [a.shape[0] // CH if p else a.shape[0] for a, p in zip(a3s, pre)]
    rws = [n * CH // NDEV for n in nchs]
    cols = [a3s[k].shape[1] if pre[k] else bs[bmap[k]].shape[1] for k in range(nw)]
    last = small is not None
    whole = None if last else order[-1]
    srows, brows = (small[0].shape[0], small[1].shape[0]) if last else (0, 0)
    sub = 16
    nb = len(bs)

    def body(*refs):
        pos = 0

        def take(n):
            nonlocal pos
            pos += n
            return refs[pos - n : pos]

        a_refs, b_refs = take(nw), take(nb)
        if last:
            spack_ref, bulk_ref, dmodp_ref = take(3)
        o_refs = take(nw)
        if last:
            ssum_ref, bsum_ref, gbada_ref, dmy_ref = take(4)
        else:
            cbm_refs = take(nw)
        made = iter(take(nw - sum(pre)))
        p_refs = [a_refs[k] if pre[k] else next(made) for k in range(nw)]
        r1, cb, hb = take(nw), take(nw), take(nw)
        tmp4, abuf, bbuf, obuf = take(4)
        asem, osem, bsem, tsem, d_s, d_r, h_s, h_r = take(8)
        if last:
            r2 = take(nw)
            srecv, brecv, mrecv, i_s, i_r, ssem_s, ssem_r, msem_s, msem_r, bsem_s, bsem_r, fsem_s, fsem_r = take(13)
        else:
            (csem,) = take(1)
        x, y, c_, me = _place()
        sib, xn, yn = (_flip(x, y, c_, r)[0] for r in (1, 4, 2))

        def rcopy(src, dst, ss, rs, dev):
            return pltpu.make_async_remote_copy(src_ref=src, dst_ref=dst, send_sem=ss, recv_sem=rs, device_id=dev, device_id_type=MESH)

        sends, kept = [], []
        if last:
            srecv[me] = spack_ref[...]
            brecv[me] = bulk_ref[...]
            mrecv[me] = dmodp_ref[...]
            for r in range(1, NDEV):
                dev, _ = _flip(x, y, c_, r)
                sends.append(rcopy(srecv.at[me], srecv.at[me], ssem_s.at[r], ssem_r.at[r], dev))
                sends.append(rcopy(mrecv.at[me], mrecv.at[me], msem_s.at[r], msem_r.at[r], dev))
            for j, r in enumerate((1, 4, 2, 6)):
                sends.append(rcopy(brecv.at[me], brecv.at[me], bsem_s.at[j], bsem_r.at[j], _flip(x, y, c_, r)[0]))
            for cp in sends:
                cp.start()

        def pass_bulk():
            for j, r in enumerate((4, 2, 6)):
                blk = brecv.at[_flip(x, y, c_, r)[1]]
                rcopy(blk, blk, bsem_s.at[j + 1], bsem_r.at[j + 1], (x, y, c_)).wait_recv()
                cp = rcopy(blk, blk, fsem_s.at[j], fsem_r.at[j], sib)
                cp.start()
                sends.append(cp)

        def relay(k):
            if k == whole:
                return
            half = rws[k] // 2
            for h in range(2):
                rcopy(cb[k].at[2, pl.ds(h * half, half), :], hb[k].at[h], h_s.at[k, h], h_r.at[k, h], (x, y, c_)).wait_recv()

                def add_half(j, carry, k=k, h=h):
                    rr = pl.ds(pl.multiple_of(j * sub, sub), sub)
                    dst = pl.ds(pl.multiple_of(h * half + j * sub, sub), sub)
                    cb[k][1 - h, dst, :] = (cb[k][1 - h, dst, :].astype(F32) + hb[k][h, rr, :].astype(F32)).astype(BF16)
                    return carry

                lax.fori_loop(0, half // sub, add_half, 0)
            if last:
                for q, dev in enumerate((xn, yn)):
                    cp = rcopy(cb[k].at[q], r2[k].at[q], i_s.at[k, q], i_r.at[k, q], dev)
                    cp.start()
                    sends.append(cp)
            else:
                cp = pltpu.make_async_copy(cb[k].at[pl.ds(0, 2)], cbm_refs[k], csem.at[k])
                cp.start()
                kept.append(cp)

        def mine(k, q):
            _, owner = _flip(x, y, c_, CHIPS[q])
            return _rows(p_refs[k], owner, rws[k]), tmp4.at[q, pl.ds(0, rws[k]), pl.ds(0, cols[k])]

        def fetch(k):
            for q in range(4):
                pltpu.make_async_copy(*mine(k, q), tsem.at[q]).start()

        def presum(k):
            for q in (3, 1, 2, 0):
                src, tmp = mine(k, q)
                rcopy(src, r1[k].at[q], d_s.at[k, q], d_r.at[k, q], (x, y, c_)).wait_recv()
                pltpu.make_async_copy(src, tmp, tsem.at[q]).wait()

                def add_sib(j, carry, k=k, q=q, tmp=tmp):
                    rr = pl.ds(pl.multiple_of(j * sub, sub), sub)
                    t = tmp[rr, :].astype(F32) + r1[k][q, rr, :].astype(F32)
                    if q == 0:
                        o_refs[k][rr, :] = t
                    else:
                        cb[k][q - 1, rr, :] = t.astype(BF16)
                    return carry

                lax.fori_loop(0, rws[k] // sub, add_sib, 0)
                if q == 3 and k != whole:
                    for h, dev in enumerate((xn, yn)):
                        half = rws[k] // 2
                        cp = rcopy(cb[k].at[2, pl.ds(h * half, half), :], hb[k].at[h], h_s.at[k, h], h_r.at[k, h], dev)
                        cp.start()
                        sends.append(cp)
            if k == whole:
                cp = pltpu.make_async_copy(cb[k], cbm_refs[k], csem.at[k])
                cp.start()
                kept.append(cp)

        made_order = [k for k in order if not pre[k]]
        uses = [bmap[k] for t, k in enumerate(made_order) if t == 0 or bmap[k] != bmap[made_order[t - 1]]]

        def b_copy(u):
            return pltpu.make_async_copy(b_refs[uses[u]], bbuf.at[u % 2, :, pl.ds(0, b_refs[uses[u]].shape[1])], bsem.at[u % 2])

        def matmuls(k, u):
            a_ref, p_ref = a_refs[k], p_refs[k]
            bb = bbuf.at[u % 2, :, pl.ds(0, cols[k])]
            ob = obuf.at[:, :, pl.ds(0, cols[k])]

            def a_copy(m, slot):
                return pltpu.make_async_copy(a_ref.at[m], abuf.at[slot], asem.at[slot])

            def o_copy(m, slot):
                return pltpu.make_async_copy(ob.at[slot], p_ref.at[pl.ds(pl.multiple_of(m * CH, CH), CH), :], osem.at[slot])

            for m in range(ASLOTS - 1):
                a_copy(m, m).start()

            def step(m, carry):
                slot, aslot = lax.rem(m, 2), lax.rem(m, ASLOTS)
                a_copy(m, aslot).wait()

                @pl.when(m + ASLOTS - 1 < nchs[k])
                def _():
                    a_copy(m + ASLOTS - 1, lax.rem(m + ASLOTS - 1, ASLOTS)).start()

                @pl.when(m >= 2)
                def _():
                    o_copy(m - 2, slot).wait()

                ob[slot] = _dot_nn(abuf[aslot], bb[...]).astype(BF16)
                o_copy(m, slot).start()
                return carry

            lax.fori_loop(0, nchs[k], step, 0)
            for m in (nchs[k] - 2, nchs[k] - 1):
                o_copy(m, m % 2).wait()

        b_copy(0).start()
        u, before = -1, None
        for t, k in enumerate(order):
            p_ref, rw = p_refs[k], rws[k]
            if not pre[k]:
                if before is None or bmap[k] != bmap[before]:
                    u += 1
                    b_copy(u).wait()
                    if u + 1 < len(uses):
                        b_copy(u + 1).start()
                before = k
                matmuls(k, u)

            for q, r in enumerate(CHIPS):
                _, owner = _flip(x, y, c_, r | 1)
                cp = rcopy(_rows(p_ref, owner, rw), r1[k].at[q], d_s.at[k, q], d_r.at[k, q], sib)
                cp.start()
                sends.append(cp)
            if last and k == made_order[0]:
                pass_bulk()
            if t >= 1:
                presum(order[t - 1])
            fetch(k)
            if t >= 2:
                relay(order[t - 2])
        presum(order[-1])
        relay(order[-2])
        relay(order[-1])

        if last:
            for r in range(1, NDEV):
                _, pb = _flip(x, y, c_, r)
                rcopy(srecv.at[me], srecv.at[pb], ssem_s.at[r], ssem_r.at[r], (x, y, c_)).wait_recv()
                rcopy(mrecv.at[me], mrecv.at[pb], msem_s.at[r], msem_r.at[r], (x, y, c_)).wait_recv()
            blk = brecv.at[_flip(x, y, c_, 1)[1]]
            rcopy(blk, blk, bsem_s.at[0], bsem_r.at[0], (x, y, c_)).wait_recv()
            for j, r in enumerate((5, 3, 7)):
                blk = brecv.at[_flip(x, y, c_, r)[1]]
                rcopy(blk, blk, fsem_s.at[j], fsem_r.at[j], (x, y, c_)).wait_recv()
            tot = srecv[0]
            btl = brecv[0].astype(F32)
            for s in range(1, NDEV):
                tot = tot + srecv[s]
                btl = btl + brecv[s].astype(F32)
            ssum_ref[...] = tot
            bsum_ref[...] = btl
            btot = mrecv[0]
            for s in range(1, NDEV):
                btot = btot + mrecv[s]
            gbada_ref[...] = btot
            rowm = lax.broadcasted_iota(jnp.int32, (8, MODW), 0)
            dmy = jnp.zeros((8, MODW), F32)
            for s in range(NDEV):
                drow = jnp.sum(jnp.where(rowm == me, mrecv[s], 0.0), axis=0, keepdims=True)
                dmy = jnp.where(rowm == s, drow, dmy)
            dmy_ref[...] = dmy

            for k in order:
                for q in range(2):
                    rcopy(cb[k].at[q], r2[k].at[q], i_s.at[k, q], i_r.at[k, q], (x, y, c_)).wait_recv()

                def add_far(j, carry, k=k):
                    rr = pl.ds(pl.multiple_of(j * sub, sub), sub)
                    t = o_refs[k][rr, :]
                    for q in range(2):
                        t = t + r2[k][q, rr, :].astype(F32)
                    o_refs[k][rr, :] = t
                    return carry

                lax.fori_loop(0, rws[k] // sub, add_far, 0)
        for cp in sends:
            cp.wait_send()
        for cp in kept:
            cp.wait()

    own = [jax.ShapeDtypeStruct((rws[k], cols[k]), F32) for k in range(nw)]
    parts = [jax.ShapeDtypeStruct((nchs[k] * CH, cols[k]), BF16) for k in range(nw) if not pre[k]]
    scratch = [pltpu.VMEM((4, rws[k], cols[k]), BF16) for k in range(nw)]
    scratch += [pltpu.VMEM((3, rws[k], cols[k]), BF16) for k in range(nw)]
    scratch += [pltpu.VMEM((2, rws[k] // 2, cols[k]), BF16) for k in range(nw)]
    scratch += [pltpu.VMEM((4, max(rws), max(cols)), BF16)]
    scratch += [pltpu.VMEM((ASLOTS, CH, SEQ), BF16), pltpu.VMEM((2, SEQ, max(cols)), BF16), pltpu.VMEM((2, CH, max(cols)), BF16)]
    scratch += [pltpu.SemaphoreType.DMA((ASLOTS,))] + [pltpu.SemaphoreType.DMA((2,))] * 2 + [pltpu.SemaphoreType.DMA((4,))]
    scratch += [pltpu.SemaphoreType.DMA((nw, 4))] * 2 + [pltpu.SemaphoreType.DMA((nw, 2))] * 2
    if last:
        out_shape = own + [jax.ShapeDtypeStruct((srows, DM), F32), jax.ShapeDtypeStruct((brows, DM), F32),
                           jax.ShapeDtypeStruct((8, MODW), F32), jax.ShapeDtypeStruct((8, MODW), F32)] + parts
        out_specs = [_VM] * (nw + 4) + [_HBM] * len(parts)
        scratch += [pltpu.VMEM((2, rws[k], cols[k]), BF16) for k in range(nw)]
        scratch += [pltpu.VMEM((NDEV, srows, DM), F32), pltpu.VMEM((NDEV, brows, DM), BF16), pltpu.VMEM((NDEV, 8, MODW), F32)]
        scratch += [pltpu.SemaphoreType.DMA((nw, 2))] * 2 + [pltpu.SemaphoreType.DMA((NDEV,))] * 4
        scratch += [pltpu.SemaphoreType.DMA((4,))] * 2 + [pltpu.SemaphoreType.DMA((3,))] * 2
        keep = nw + 4
    else:
        out_shape = own + [jax.ShapeDtypeStruct((3 if k == whole else 2, rws[k], cols[k]), BF16) for k in range(nw)] + parts
        out_specs = [_VM] * nw + [_HBM] * (nw + len(parts))
        scratch += [pltpu.SemaphoreType.DMA((nw,))]
        keep = 2 * nw
    outs = pl.pallas_call(
        body,
        name=name,
        out_shape=out_shape,
        in_specs=[_HBM] * (nw + nb) + [_VM] * (3 if last else 0),
        out_specs=out_specs,
        scratch_shapes=scratch,
        compiler_params=pltpu.CompilerParams(vmem_limit_bytes=60 * 1024 * 1024),
    )(*a3s, *bs, *(small or ()))
    return outs[:keep]


def _adam_update(w, g, m, v):
    m = ADAM_B1 * m + (1.0 - ADAM_B1) * g
    v = ADAM_B2 * v + (1.0 - ADAM_B2) * (g * g)
    m_hat = m / (1.0 - ADAM_B1 ** ADAM_STEP)
    v_hat = v / (1.0 - ADAM_B2 ** ADAM_STEP)
    return -ADAM_LR * (m_hat / (jnp.sqrt(v_hat) + ADAM_EPS) + ADAM_WD * w), m, v


def _adam_ada_call(w, m, v, dmy, cact_t, gx):
    def body(w_ref, m_ref, v_ref, dmy_ref, ct_ref, gx_ref, g_ref, d_ref, mo_ref, vo_ref, gxo_ref):
        g = jnp.zeros((DM // 4, MODW), F32)
        for s in range(NDEV):
            g = g + ct_ref[:, s : s + 1] * dmy_ref[s : s + 1, :]
        g_ref[...] = g
        d_ref[...], mo_ref[...], vo_ref[...] = _adam_update(w_ref[...], g, m_ref[...], v_ref[...])
        gxo_ref[...] = gx_ref[...]

    blk = pl.BlockSpec((DM // 4, MODW), lambda i: (i, 0))
    xblk = pl.BlockSpec((SEQ // 4, DM), lambda i: (i, 0))
    return pl.pallas_call(
        body,
        name="adam_ada",
        grid=(4,),
        out_shape=[jax.ShapeDtypeStruct(w.shape, F32)] * 4 + [jax.ShapeDtypeStruct(gx.shape, F32)],
        in_specs=[blk] * 3 + [_const((8, MODW)), pl.BlockSpec((DM // 4, NDEV), lambda i: (i, 0)), xblk],
        out_specs=[blk] * 4 + [xblk],
        compiler_params=pltpu.CompilerParams(dimension_semantics=("arbitrary",), vmem_limit_bytes=VMEM_LIMIT),
    )(w, m, v, dmy, cact_t, gx)


def _adam_sum_call(name, ws, owns, fars, ms, vs):
    n = len(ws)

    def body(*refs):
        for i in range(n):
            w, own, far, m, v = (refs[j * n + i] for j in range(5))
            g = own[...]
            for q in range(fars[i].shape[0]):
                g = g + far[q].astype(F32)
            refs[5 * n + i][...] = g
            refs[6 * n + i][...], refs[7 * n + i][...], refs[8 * n + i][...] = _adam_update(w[...], g, m[...], v[...])

    shapes = [jax.ShapeDtypeStruct(w.shape, F32) for w in ws]
    blks = [pl.BlockSpec((w.shape[0] // 2, w.shape[1]), lambda i: (i, 0)) for w in ws]
    fblks = [pl.BlockSpec((f.shape[0], w.shape[0] // 2, w.shape[1]), lambda i: (0, i, 0)) for w, f in zip(ws, fars)]
    outs = pl.pallas_call(
        body,
        name=name,
        grid=(2,),
        out_shape=shapes * 4,
        in_specs=blks * 2 + fblks + blks * 2,
        out_specs=blks * 4,
        compiler_params=pltpu.CompilerParams(dimension_semantics=("arbitrary",), vmem_limit_bytes=VMEM_LIMIT),
    )(*ws, *owns, *fars, *ms, *vs)
    return outs[:n], outs[n : 2 * n], outs[2 * n : 3 * n], outs[3 * n :]


def _adam_call(name, ws, gs, ms, vs):
    n = len(ws)

    def body(*refs):
        for i in range(n):
            w, g, m, v = (refs[j * n + i][...] for j in range(4))
            refs[4 * n + i][...], refs[5 * n + i][...], refs[6 * n + i][...] = _adam_update(w, g, m, v)

    shapes = [jax.ShapeDtypeStruct(w.shape, F32) for w in ws]
    outs = pl.pallas_call(
        body,
        name=name,
        out_shape=shapes * 3,
        in_specs=[_VM] * (4 * n),
        out_specs=[_VM] * (3 * n),
        compiler_params=pltpu.CompilerParams(vmem_limit_bytes=VMEM_LIMIT),
    )(*ws, *gs, *ms, *vs)
    return outs[:n], outs[n : 2 * n], outs[2 * n :]


def kernel(x, c, w_ada, b_ada, g_norm1, w_in, dw_w, dw_b, conv_ln_g, conv_ln_b, w_conv_pw, w_pool_group, pool_scale, w_out, g_norm2, w_ffn_gate, w_ffn_up, w_ffn_down, g_final, loss_target, m_w_ada, m_b_ada, m_g_norm1, m_w_in, m_dw_w, m_dw_b, m_conv_ln_g, m_conv_ln_b, m_w_conv_pw, m_w_pool_group, m_pool_scale, m_w_out, m_g_norm2, m_w_ffn_gate, m_w_ffn_up, m_w_ffn_down, m_g_final, v_w_ada, v_b_ada, v_g_norm1, v_w_in, v_dw_w, v_dw_b, v_conv_ln_g, v_conv_ln_b, v_w_conv_pw, v_w_pool_group, v_pool_scale, v_w_out, v_g_norm2, v_w_ffn_gate, v_w_ffn_up, v_w_ffn_down, v_g_final):
    me = 4 * lax.axis_index("x") + 2 * lax.axis_index("y") + lax.axis_index("c")
    xs, tgt = x[0], loss_target[0]
    b_my = lax.dynamic_slice(b_ada, (0, me * MODW), (1, MODW))
    win_t, wout, wpw, mod8, cact, dww8 = _gather_call(c, w_ada[0], b_my, dw_w[0], [w_in[0].T, w_out[0], w_conv_pw[0]])
    dww = jnp.pad(jnp.transpose(dww8, (1, 0, 2)).reshape(KCONV, CONVW), ((0, HALO_C - KCONV), (0, 0)))
    wgp = w_pool_group[0]

    h1b, uag, hc, hd, hsb3, pbv, ycb3, y, x1, wg_t, wu_t, wd = _fwd_mix_call(
        xs, mod8, g_norm1, win_t, dww, dw_b, conv_ln_g, conv_ln_b, wpw, wgp, pool_scale, wout,
        [w_ffn_gate[0].T, w_ffn_up[0].T, w_ffn_down[0]])
    h2b, a3, dfb, dg3, du3, dx1, facc = _ffn_call(x1, tgt, mod8, g_norm2, g_final.reshape(1, DM), wg_t, wu_t, wd)
    own_gate, own_up, own_down, s_gate, s_up, s_down = _wgrad_rs_call("wgrad_ffn", [dg3, du3, a3], [0, 0, 1], [h2b, dfb], (2, 0, 1))
    gx, dub3, macc, ddw, dwg, p_out, p_pw, f_gate, f_up, f_down = _bwd_mix_call(
        dx1, xs, y, uag, hc, hd, pbv, mod8, g_norm1, win_t, dww, conv_ln_g, conv_ln_b, wpw, wgp, pool_scale, wout, ycb3, hsb3,
        [s_gate, s_up, s_down])
    spack = jnp.concatenate([macc[3:4], facc[3:4], facc[4:5], macc[4:6], facc[5:6], jnp.zeros((2, DM), F32)], axis=0)
    bulk = jnp.concatenate([ddw.reshape(HALO_C // 2, DM), dwg.reshape(-1, DM)], axis=0).astype(BF16)
    dmodp = jnp.concatenate([macc[0:3], facc[0:3]], axis=0).reshape(8, MODW)
    g_in_t, g_out, g_pw, ssum, bsum, gbada, dmy = _wgrad_rs_call(
        "wgrad_rs", [dub3, p_out, p_pw], [0, None, None], [h1b], (1, 2, 0), small=(spack, bulk, dmodp))
    g_wada, d_wada, m_wada, v_wada, gx = _adam_ada_call(w_ada[0], m_w_ada[0], v_w_ada[0], dmy, cact.T, gx)
    turned = ("w_in", "w_ffn_gate", "w_ffn_up")
    ffn = ("w_ffn_gate", "w_ffn_up", "w_ffn_down")
    ffn_given = dict(w_ffn_gate=(w_ffn_gate, m_w_ffn_gate, v_w_ffn_gate), w_ffn_up=(w_ffn_up, m_w_ffn_up, v_w_ffn_up),
                     w_ffn_down=(w_ffn_down, m_w_ffn_down, v_w_ffn_down))

    def ffn_work(j):
        return [ffn_given[n][j][0].T if n in turned else ffn_given[n][j][0] for n in ffn]

    g_ffn, d_ffn, m_ffn, v_ffn = _adam_sum_call(
        "adam_ffn", ffn_work(0), [own_gate, own_up, own_down], [f_gate, f_up, f_down], ffn_work(1), ffn_work(2))

    loss = ssum[5, 0]
    ddw_all = bsum[: HALO_C // 2].reshape(HALO_C, CONVW)[:KCONV]
    grads = {
        "w_ada": g_wada,
        "b_ada": gbada.reshape(1, 6 * DM),
        "g_norm1": ssum[0:1],
        "w_in": g_in_t,
        "dw_w": lax.dynamic_slice(ddw_all, (0, me * (CONVW // NDEV)), (KCONV, CONVW // NDEV)),
        "dw_b": ssum[3:4, 0:CONVW],
        "conv_ln_g": ssum[3:4, CONVW:DM],
        "conv_ln_b": ssum[4:5, 0:CONVW],
        "w_conv_pw": g_pw,
        "w_pool_group": bsum[HALO_C // 2 :].reshape(len(WINS) * PGD, PGD),
        "pool_scale": ssum[4:5, CONVW:DM],
        "w_out": g_out,
        "g_norm2": ssum[1:2],
        "w_ffn_gate": g_ffn[0],
        "w_ffn_up": g_ffn[1],
        "w_ffn_down": g_ffn[2],
        "g_final": ssum[2:3],
    }
    given = dict(w_ada=(w_ada, m_w_ada, v_w_ada), b_ada=(b_ada, m_b_ada, v_b_ada), g_norm1=(g_norm1, m_g_norm1, v_g_norm1),
                 w_in=(w_in, m_w_in, v_w_in), dw_w=(dw_w, m_dw_w, v_dw_w), dw_b=(dw_b, m_dw_b, v_dw_b),
                 conv_ln_g=(conv_ln_g, m_conv_ln_g, v_conv_ln_g), conv_ln_b=(conv_ln_b, m_conv_ln_b, v_conv_ln_b),
                 w_conv_pw=(w_conv_pw, m_w_conv_pw, v_w_conv_pw), w_pool_group=(w_pool_group, m_w_pool_group, v_w_pool_group),
                 pool_scale=(pool_scale, m_pool_scale, v_pool_scale), w_out=(w_out, m_w_out, v_w_out), g_norm2=(g_norm2, m_g_norm2, v_g_norm2),
                 w_ffn_gate=(w_ffn_gate, m_w_ffn_gate, v_w_ffn_gate), w_ffn_up=(w_ffn_up, m_w_ffn_up, v_w_ffn_up),
                 w_ffn_down=(w_ffn_down, m_w_ffn_down, v_w_ffn_down), g_final=(g_final, m_g_final, v_g_final))
    names = list(given)
    groups = [["w_in", "w_out", "w_conv_pw"],
              ["b_ada", "g_norm1", "dw_w", "dw_b", "conv_ln_g", "conv_ln_b", "w_pool_group", "pool_scale", "g_norm2", "g_final"]]

    def work(n, a):
        return a[0].T if n in turned else a.reshape(grads[n].shape)

    def full(n, a):
        return a.T[None] if n in turned else a.reshape(given[n][0].shape)

    delta, new_m, new_v = {"w_ada": d_wada}, {"w_ada": m_wada}, {"w_ada": v_wada}
    for i, n in enumerate(ffn):
        delta[n], new_m[n], new_v[n] = d_ffn[i], m_ffn[i], v_ffn[i]
    for gi, grp in enumerate(groups):
        ds, ms, vs = _adam_call(f"adam{gi}", [work(n, given[n][0]) for n in grp], [grads[n] for n in grp],
                                [work(n, given[n][1]) for n in grp], [work(n, given[n][2]) for n in grp])
        for n, d_, m_, v_ in zip(grp, ds, ms, vs):
            delta[n], new_m[n], new_v[n] = d_, m_, v_

    return (loss, gx.reshape(x.shape), *[full(n, grads[n]) for n in names], *[full(n, delta[n]) for n in names],
            *[full(n, new_m[n]) for n in names], *[full(n, new_v[n]) for n in names])
```

```python
import functools

import jax
import jax.numpy as jnp
from jax import lax
from jax.experimental import pallas as pl
from jax.experimental.pallas import tpu as pltpu

F32, BF16 = jnp.float32, jnp.bfloat16
SEQ, DM = 2048, 1024
CONVW, POOLW = 512, 512
KCONV = 31
WINS = (2, 4, 8, 16)
PGD = 128
DFF = 2816
NDEV = 8
MODW = 6 * DM // NDEV
EPS = 1e-6
TOK = 256
NTILE = SEQ // TOK
CH = 256
NCH = DFF // CH
HALO_C, HALO_P = 32, 16
MESH = pl.DeviceIdType.MESH
VMEM_LIMIT = 56 * 1024 * 1024
ADAM_LR, ADAM_B1, ADAM_B2, ADAM_EPS, ADAM_WD, ADAM_STEP = 0.001, 0.9, 0.999, 1e-08, 0.01, 10
HI = lax.Precision.HIGHEST

_VM = pl.BlockSpec(memory_space=pltpu.VMEM)
_HBM = pl.BlockSpec(memory_space=pltpu.HBM)


def _place():
    x, y, c = lax.axis_index("x"), lax.axis_index("y"), lax.axis_index("c")
    return x, y, c, 4 * x + 2 * y + c


def _flip(x, y, c, r):
    px = 1 - x if r & 4 else x
    py = 1 - y if r & 2 else y
    pc = 1 - c if r & 1 else c
    return (px, py, pc), 4 * px + 2 * py + pc


def _rows(ref, blk, n):
    return ref.at[pl.ds(pl.multiple_of(blk * n, 16), n), :]


def _sig(z):
    return jax.nn.sigmoid(z)


def _dot_nt(a, b):
    return lax.dot_general(a, b, (((1,), (1,)), ((), ())), preferred_element_type=F32)


def _dot_nn(a, b):
    return lax.dot_general(a, b, (((1,), (0,)), ((), ())), preferred_element_type=F32)


def _dot_tn(a, b):
    return lax.dot_general(a, b, (((0,), (0,)), ((), ())), preferred_element_type=F32)


def _rcopy(src, dst, ss, rs, dev):
    return pltpu.make_async_remote_copy(src_ref=src, dst_ref=dst, send_sem=ss, recv_sem=rs, device_id=dev, device_id_type=MESH)


def _ag_sems(nw):
    return ([pltpu.SemaphoreType.DMA((nw, 3))] * 2 + [pltpu.SemaphoreType.DMA((nw, 4))] * 2 + [pltpu.SemaphoreType.DMA((nw, 2))] * 2
            + [pltpu.SemaphoreType.DMA((nw,))])


def _ag_plan(sbufs, g_refs, rws, sems):
    wsem_s, wsem_r, fsem_s, fsem_r, hsem_s, hsem_r, lsem = sems
    x, y, c_, me = _place()
    here = (x, y, c_)
    plans = []
    for k, (sb, g, n) in enumerate(zip(sbufs, g_refs, rws)):

        def blk(r, half=None, g=g, n=n):
            b = _flip(x, y, c_, r)[1]
            if half is None:
                return _rows(g, b, n)
            return g.at[pl.ds(pl.multiple_of(b * n + half * (n // 2), 16), n // 2), :]

        def same(ref, ss, rs, j, dev, k=k):
            return _rcopy(ref, ref, ss.at[k, j], rs.at[k, j], dev)

        sib, xn, yn = (_flip(x, y, c_, r)[0] for r in (1, 4, 2))
        plans.append(dict(
            local=pltpu.make_async_copy(sb, blk(0), lsem.at[k]),
            first=[_rcopy(sb, blk(0), wsem_s.at[k, j], wsem_r.at[k, j], dev) for j, dev in enumerate((sib, xn, yn))],
            got=[_rcopy(sb, blk(r), wsem_s.at[k, j], wsem_r.at[k, j], here) for j, r in enumerate((1, 4, 2))],
            passes=[same(blk(4), fsem_s, fsem_r, 0, sib), same(blk(2), fsem_s, fsem_r, 1, sib),
                    same(blk(6, 0), fsem_s, fsem_r, 2, sib), same(blk(6, 1), fsem_s, fsem_r, 3, sib)],
            passed=[same(blk(5), fsem_s, fsem_r, 0, here), same(blk(3), fsem_s, fsem_r, 1, here),
                    same(blk(7, 0), fsem_s, fsem_r, 2, here), same(blk(7, 1), fsem_s, fsem_r, 3, here)],
            halves=[same(blk(4, 0), hsem_s, hsem_r, 0, yn), same(blk(2, 1), hsem_s, hsem_r, 1, xn)],
            halved=[same(blk(6, 0), hsem_s, hsem_r, 0, here), same(blk(6, 1), hsem_s, hsem_r, 1, here)],
        ))
    return plans


def _ag_start(sbufs, g_refs, rws, sems):
    for p in _ag_plan(sbufs, g_refs, rws, sems):
        p["local"].start()
        for cp in p["first"]:
            cp.start()


def _ag_pass_on(sbufs, g_refs, rws, sems):
    plans = _ag_plan(sbufs, g_refs, rws, sems)
    for p in plans:
        for j in (0, 1):
            p["got"][j + 1].wait_recv()
            p["halves"][j].start()
            p["passes"][j].start()
    for p in plans:
        for j in (0, 1):
            p["halved"][j].wait_recv()
            p["passes"][j + 2].start()


def _ag_finish(sbufs, g_refs, rws, sems):
    plans = _ag_plan(sbufs, g_refs, rws, sems)
    for p in plans:
        p["got"][0].wait_recv()
        for cp in p["passed"]:
            cp.wait_recv()
    for p in plans:
        for cp in p["first"] + p["passes"] + p["halves"]:
            cp.wait_send()
        p["local"].wait()


def _gather_call(c, w_ada, b_my, dww, shards):
    nw = len(shards)
    rws = [s.shape[0] for s in shards]

    def body(*refs):
        c_ref, wada_ref, bmy_ref, dww_ref = refs[:4]
        s_refs = refs[4 : 4 + nw]
        g_refs = refs[4 + nw : 4 + 2 * nw]
        mod8_ref, cact_ref, dww8_ref = refs[4 + 2 * nw : 7 + 2 * nw]
        crecv, msend, mrecv = refs[7 + 2 * nw : 10 + 2 * nw]
        sbufs = refs[10 + 2 * nw : 10 + 3 * nw]
        csem_s, csem_r, dsem_s, dsem_r, msem_s, msem_r = refs[10 + 3 * nw : 16 + 3 * nw]
        ag_sems = refs[16 + 3 * nw :]
        x, y, c_, me = _place()

        def rcopy(src, dst, ss, rs, dev):
            return pltpu.make_async_remote_copy(src_ref=src, dst_ref=dst, send_sem=ss, recv_sem=rs, device_id=dev, device_id_type=MESH)

        crecv[me] = jnp.broadcast_to(c_ref[...], (8, DM))
        dww8_ref[me] = dww_ref[...]
        small = []
        for r in range(1, NDEV):
            dev, _ = _flip(x, y, c_, r)
            small.append(rcopy(crecv.at[me], crecv.at[me], csem_s.at[r], csem_r.at[r], dev))
            small.append(rcopy(dww8_ref.at[me], dww8_ref.at[me], dsem_s.at[r], dsem_r.at[r], dev))
        for cp in small:
            cp.start()

        for k in range(nw):
            sbufs[k][...] = s_refs[k][...].astype(BF16)
        _ag_start(sbufs, g_refs, rws, ag_sems)

        rowid = lax.broadcasted_iota(jnp.int32, (8, DM), 0)
        for r in range(1, NDEV):
            _, pb = _flip(x, y, c_, r)
            rcopy(crecv.at[me], crecv.at[pb], csem_s.at[r], csem_r.at[r], (x, y, c_)).wait_recv()
        call = jnp.zeros((8, DM), F32)
        for s in range(NDEV):
            call = jnp.where(rowid == s, crecv[s], call)
        cact = call * _sig(call)
        cact_ref[...] = cact
        modp = jnp.dot(cact, wada_ref[...], precision=HI, preferred_element_type=F32) + bmy_ref[...]
        rowm = lax.broadcasted_iota(jnp.int32, (8, MODW), 0)
        for b in range(NDEV):
            row = jnp.sum(jnp.where(rowm == b, modp, 0.0), axis=0, keepdims=True)
            msend[b] = jnp.broadcast_to(row, (8, MODW))
        mrecv[me] = msend[me]
        msends = []
        for r in range(1, NDEV):
            dev, pb = _flip(x, y, c_, r)
            cp = rcopy(msend.at[pb], mrecv.at[me], msem_s.at[r], msem_r.at[r], dev)
            cp.start()
            msends.append(cp)

        _ag_pass_on(sbufs, g_refs, rws, ag_sems)

        for r in range(1, NDEV):
            _, pb = _flip(x, y, c_, r)
            rcopy(msend.at[pb], mrecv.at[pb], msem_s.at[r], msem_r.at[r], (x, y, c_)).wait_recv()
        for s in range(NDEV):
            mod8_ref[:, s * MODW : (s + 1) * MODW] = mrecv[s]

        _ag_finish(sbufs, g_refs, rws, ag_sems)
        for r in range(1, NDEV):
            _, pb = _flip(x, y, c_, r)
            rcopy(dww8_ref.at[me], dww8_ref.at[pb], dsem_s.at[r], dsem_r.at[r], (x, y, c_)).wait_recv()
        for cp in small + msends:
            cp.wait_send()

    out_shape = [jax.ShapeDtypeStruct((NDEV * s.shape[0], s.shape[1]), BF16) for s in shards]
    out_shape += [
        jax.ShapeDtypeStruct((8, 6 * DM), F32),
        jax.ShapeDtypeStruct((8, DM), F32),
        jax.ShapeDtypeStruct((NDEV,) + dww.shape, F32),
    ]
    scratch = [pltpu.VMEM((NDEV, 8, DM), F32), pltpu.VMEM((NDEV, 8, MODW), F32), pltpu.VMEM((NDEV, 8, MODW), F32)]
    scratch += [pltpu.VMEM(s.shape, BF16) for s in shards]
    scratch += [pltpu.SemaphoreType.DMA((NDEV,))] * 6 + _ag_sems(nw)
    return pl.pallas_call(
        body,
        name="gather",
        out_shape=out_shape,
        in_specs=[_VM] * (4 + nw),
        out_specs=[_HBM] * nw + [_VM] * 3,
        scratch_shapes=scratch,
        compiler_params=pltpu.CompilerParams(vmem_limit_bytes=VMEM_LIMIT),
    )(c, w_ada, b_my, dww, *shards)


def _const(shape):
    return pl.BlockSpec(shape, lambda i: (0,) * len(shape))


def _tile(width, rev=False):
    if rev:
        return pl.BlockSpec((TOK, width), lambda i: (NTILE - 1 - i, 0))
    return pl.BlockSpec((TOK, width), lambda i: (i, 0))


def _tile3(nch, rev=False):
    if rev:
        return pl.BlockSpec((nch, CH, TOK), lambda i: (0, 0, NTILE - 1 - i))
    return pl.BlockSpec((nch, CH, TOK), lambda i: (0, 0, i))


def _put3(ref, val):
    for j in range(val.shape[1] // CH):
        ref[j] = val[:, j * CH : (j + 1) * CH].T


def _norm_mod(x, g, sc, sh):
    r = lax.rsqrt(jnp.mean(x * x, axis=-1, keepdims=True) + EPS)
    xr = x * r
    return r, xr, xr * g * (1.0 + sc) + sh


def _fwd_mix_call(x, mod8, g1, win_t, dww, dwb, lng, lnb, wpw, wg, psc, wout, shards):
    ns = len(shards)
    rws = [s.shape[0] for s in shards]

    def body(x_ref, mod_ref, g1_ref, win_ref, dww_ref, dwb_ref, lng_ref, lnb_ref, wpw_ref, wg_ref, psc_ref, wout_ref, *rest):
        s_refs = rest[:ns]
        h1b_ref, uag_ref, hc_ref, hd_ref, hsb3_ref, pb_ref, ycb3_ref, y_ref, x1_ref = rest[ns : ns + 9]
        g_refs = rest[ns + 9 : 2 * ns + 9]
        hc_ext, up_ext, ycb_ref, stage = rest[2 * ns + 9 : 2 * ns + 13]
        sbufs = rest[2 * ns + 13 : 3 * ns + 13]
        ssem = rest[3 * ns + 13]
        ag_sems = rest[3 * ns + 14 :]
        i = pl.program_id(0)

        @pl.when(i == 0)
        def _():
            for k in range(ns):
                cp = pltpu.make_async_copy(s_refs[k], stage, ssem)
                cp.start()
                cp.wait()
                sbufs[k][...] = stage[...].astype(BF16)
            _ag_start(sbufs, g_refs, rws, ag_sems)
            hc_ext[0:HALO_C, :] = jnp.zeros((HALO_C, CONVW), F32)
            up_ext[0:HALO_P, :] = jnp.zeros((HALO_P, POOLW), F32)

        x = x_ref[...]
        sh1, sc1, gt1 = mod_ref[0:1, 0:DM], mod_ref[0:1, DM : 2 * DM], mod_ref[0:1, 2 * DM : 3 * DM]
        _, _, h1 = _norm_mod(x, g1_ref[...], sc1, sh1)
        h1b = h1.astype(BF16)
        h1b_ref[...] = h1b
        u = _dot_nt(h1b, win_ref[...])
        uag_ref[...] = u[:, : 2 * CONVW]
        hc = u[:, :CONVW] * _sig(u[:, CONVW : 2 * CONVW])
        hc_ref[...] = hc
        hc_ext[HALO_C : HALO_C + TOK, :] = hc
        up_ext[HALO_P : HALO_P + TOK, :] = u[:, 2 * CONVW :]

        acc = jnp.zeros((TOK, CONVW), F32)
        for k in range(KCONV):
            acc = acc + dww_ref[k : k + 1, :] * hc_ext[pl.ds(HALO_C - (KCONV - 1) + k, TOK), :]
        hd = acc + dwb_ref[...]
        hd_ref[...] = hd
        hc_ext[0:HALO_C, :] = hc_ext[TOK : TOK + HALO_C, :]
        mu = jnp.mean(hd, axis=-1, keepdims=True)
        dlt = hd - mu
        rstd = lax.rsqrt(jnp.mean(dlt * dlt, axis=-1, keepdims=True) + EPS)
        hl = dlt * rstd * lng_ref[...] + lnb_ref[...]
        hsb = (hl * _sig(hl)).astype(BF16)
        _put3(hsb3_ref, hsb)
        ycb_ref[:, 0:CONVW] = _dot_nn(hsb, wpw_ref[...]).astype(BF16)

        tg = i * TOK + lax.broadcasted_iota(jnp.int32, (TOK, 1), 0)
        for g, w in enumerate(WINS):
            ln = slice(PGD * g, PGD * (g + 1))
            v = up_ext[pl.ds(HALO_P, TOK), ln]
            ssum = v
            for d in range(1, w):
                ssum = ssum + up_ext[pl.ds(HALO_P - d, TOK), ln]
            cnt = jnp.minimum(tg + 1, w).astype(F32)
            pb = (ssum / cnt - v).astype(BF16)
            pb_ref[:, ln] = pb
            z = _dot_nn(pb, wg_ref[g].astype(BF16))
            ycb_ref[:, CONVW + PGD * g : CONVW + PGD * (g + 1)] = (z * psc_ref[:, ln]).astype(BF16)
        up_ext[0:HALO_P, :] = up_ext[TOK : TOK + HALO_P, :]

        _put3(ycb3_ref, ycb_ref[...])
        yv = _dot_nn(ycb_ref[...], wout_ref[...])
        y_ref[...] = yv
        x1_ref[...] = x + gt1 * yv

        @pl.when(i == NTILE - 1)
        def _():
            _ag_pass_on(sbufs, g_refs, rws, ag_sems)
            _ag_finish(sbufs, g_refs, rws, ag_sems)

    outs = [(DM, BF16), (2 * CONVW, F32), (CONVW, F32), (CONVW, F32), (-CONVW, BF16), (POOLW, BF16), (-DM, BF16), (DM, F32), (DM, F32)]
    return pl.pallas_call(
        body,
        name="fwd_mix",
        grid=(NTILE,),
        out_shape=[jax.ShapeDtypeStruct((SEQ, w) if w > 0 else (-w // CH, CH, SEQ), d) for w, d in outs]
        + [jax.ShapeDtypeStruct((NDEV * s.shape[0], s.shape[1]), BF16) for s in shards],
        in_specs=[_tile(DM), _const((8, 6 * DM)), _const((1, DM)), _const(win_t.shape), _const(dww.shape), _const((1, CONVW)),
                  _const((1, CONVW)), _const((1, CONVW)), _const(wpw.shape), _const(wg.shape), _const((1, POOLW)), _const(wout.shape)]
        + [_HBM] * ns,
        out_specs=[_tile(w) if w > 0 else _tile3(-w // CH) for w, _ in outs] + [_HBM] * ns,
        scratch_shapes=[pltpu.VMEM((TOK + HALO_C, CONVW), F32), pltpu.VMEM((TOK + HALO_P, POOLW), F32), pltpu.VMEM((TOK, DM), BF16),
                        pltpu.VMEM(shards[0].shape, F32)] + [pltpu.VMEM(s.shape, BF16) for s in shards]
        + [pltpu.SemaphoreType.DMA] + _ag_sems(ns),
        compiler_params=pltpu.CompilerParams(dimension_semantics=("arbitrary",), vmem_limit_bytes=VMEM_LIMIT),
    )(x, mod8, g1, win_t, dww, dwb, lng, lnb, wpw, wg, psc, wout, *shards)


def _ffn_call(x1, tgt, mod8, g2, gf, wg_t, wu_t, wd):
    def body(x1_ref, tgt_ref, mod_ref, g2_ref, gf_ref, wg_hbm, wu_hbm, wd_hbm,
             h2b_ref, a3_ref, dfb_ref, dg3_ref, du3_ref, dx1_ref, acc_ref,
             wg_ref, wu_ref, wd_ref, wsem):
        i = pl.program_id(0)

        @pl.when(i == 0)
        def _():
            cps = [pltpu.make_async_copy(s, d, wsem.at[n]) for n, (s, d) in enumerate(((wg_hbm, wg_ref), (wu_hbm, wu_ref), (wd_hbm, wd_ref)))]
            for cp in cps:
                cp.start()
            acc_ref[...] = jnp.zeros((8, DM), F32)
            for cp in cps:
                cp.wait()

        x1 = x1_ref[...]
        sh2, sc2, gt2 = mod_ref[0:1, 3 * DM : 4 * DM], mod_ref[0:1, 4 * DM : 5 * DM], mod_ref[0:1, 5 * DM : 6 * DM]
        g2 = g2_ref[...]
        r2, xr, h2 = _norm_mod(x1, g2, sc2, sh2)
        h2b = h2.astype(BF16)
        h2b_ref[...] = h2b
        gate = _dot_nt(h2b, wg_ref[...])
        up = _dot_nt(h2b, wu_ref[...])
        ab = (gate * _sig(gate) * up).astype(BF16)
        _put3(a3_ref, ab)
        f = _dot_nn(ab, wd_ref[...])
        x2 = x1 + gt2 * f
        rf = lax.rsqrt(jnp.mean(x2 * x2, axis=-1, keepdims=True) + EPS)
        nf = x2 * rf
        gf_ = gf_ref[...]
        err = nf * gf_ - tgt_ref[...]
        loss = 0.5 * jnp.sum(jnp.sum(err * err, axis=-1, keepdims=True), axis=0, keepdims=True) * (1.0 / DM)
        dout = err * (1.0 / DM)
        dnf = dout * gf_
        dx2 = rf * (dnf - nf * jnp.mean(dnf * nf, axis=-1, keepdims=True))
        dfb = (gt2 * dx2).astype(BF16)
        dfb_ref[...] = dfb
        da = _dot_nt(dfb, wd_ref[...])
        sg = _sig(gate)
        dgb = (da * up * (sg * (1.0 + gate * (1.0 - sg)))).astype(BF16)
        dub = (da * (gate * sg)).astype(BF16)
        _put3(dg3_ref, dgb)
        _put3(du3_ref, dub)
        dh2 = _dot_nn(dgb, wg_ref[...]) + _dot_nn(dub, wu_ref[...])
        dn2 = dh2 * (1.0 + sc2)
        dxr = dn2 * g2
        dx1_ref[...] = dx2 + r2 * (dxr - xr * jnp.mean(dxr * xr, axis=-1, keepdims=True))

        def colsum(v):
            return jnp.sum(v, axis=0, keepdims=True)

        acc_ref[0:1, :] += colsum(dh2)
        acc_ref[1:2, :] += colsum(dh2 * (xr * g2))
        acc_ref[2:3, :] += colsum(dx2 * f)
        acc_ref[3:4, :] += colsum(dn2 * xr)
        acc_ref[4:5, :] += colsum(dout * nf)
        acc_ref[5:6, :] += jnp.broadcast_to(loss, (1, DM))

    c3 = _tile3(NCH)
    return pl.pallas_call(
        body,
        name="ffn",
        grid=(NTILE,),
        out_shape=[jax.ShapeDtypeStruct((SEQ, DM), BF16), jax.ShapeDtypeStruct((NCH, CH, SEQ), BF16), jax.ShapeDtypeStruct((SEQ, DM), BF16),
                   jax.ShapeDtypeStruct((NCH, CH, SEQ), BF16), jax.ShapeDtypeStruct((NCH, CH, SEQ), BF16),
                   jax.ShapeDtypeStruct((SEQ, DM), F32), jax.ShapeDtypeStruct((8, DM), F32)],
        in_specs=[_tile(DM), _tile(DM), _const((8, 6 * DM)), _const((1, DM)), _const((1, DM)), _HBM, _HBM, _HBM],
        out_specs=[_tile(DM), c3, _tile(DM), c3, c3, _tile(DM), _const((8, DM))],
        scratch_shapes=[pltpu.VMEM((DFF, DM), BF16)] * 3 + [pltpu.SemaphoreType.DMA((3,))],
        compiler_params=pltpu.CompilerParams(dimension_semantics=("arbitrary",), vmem_limit_bytes=VMEM_LIMIT),
    )(x1, tgt, mod8, g2, gf, wg_t, wu_t, wd)


def _bwd_mix_call(dx1, x, y, uag, hc, hd, pbv, mod8, g1, win_t, dww, lng, lnb, wpw, wg, psc, wout, ycb3, hsb3, sums):
    hpt = TOK // HALO_C
    ns = len(sums)

    def far_copies(s_refs, f_refs, fs_s, fs_r):
        x_, y_, c_, _ = _place()
        return [_rcopy(s_refs[k].at[q], f_refs[k].at[q], fs_s.at[k, q], fs_r.at[k, q], _flip(x_, y_, c_, r)[0])
                for k in range(ns) for q, r in enumerate((4, 2, 6)[: sums[k].shape[0]])]

    def body(dx1_ref, x_ref, y_ref, uag_ref, hc_ref, halo_ref, hd_ref, pb_ref, mod_ref, g1_ref, win_ref, dww_ref, lng_ref, lnb_ref,
             wpw_ref, wg_ref, psc_ref, wout_ref, ycb3_ref, hsb3_ref, *rest):
        s_refs = rest[:ns]
        gx_ref, dub3_ref, acc_ref, ddw_ref, dwg_ref, pout_ref, ppw_ref = rest[ns : ns + 7]
        f_refs = rest[ns + 7 : 2 * ns + 7]
        d_ext, q_ext, hcx, dub_ref, hrot, drot, aout, apw, fs_s, fs_r = rest[2 * ns + 7 :]
        i = pl.program_id(0)
        it = NTILE - 1 - i

        @pl.when(i == 0)
        def _():
            for cp in far_copies(s_refs, f_refs, fs_s, fs_r):
                cp.start()
            aout[...] = jnp.zeros((DM, DM), F32)
            apw[...] = jnp.zeros((CONVW, CONVW), F32)
            d_ext[TOK : TOK + HALO_C, :] = jnp.zeros((HALO_C, CONVW), F32)
            q_ext[TOK : TOK + HALO_P, :] = jnp.zeros((HALO_P, POOLW), F32)
            acc_ref[...] = jnp.zeros((8, DM), F32)
            ddw_ref[...] = jnp.zeros((HALO_C, CONVW), F32)
            dwg_ref[...] = jnp.zeros((len(WINS) * PGD, PGD), F32)

        def colsum(v):
            return jnp.sum(v, axis=0, keepdims=True)

        dx1 = dx1_ref[...]
        x = x_ref[...]
        sh1, sc1, gt1 = mod_ref[0:1, 0:DM], mod_ref[0:1, DM : 2 * DM], mod_ref[0:1, 2 * DM : 3 * DM]
        acc_ref[2:3, :] += colsum(dx1 * y_ref[...])
        dyb = (gt1 * dx1).astype(BF16)
        for j in range(DM // CH):
            aout[j * CH : (j + 1) * CH, :] += _dot_nn(ycb3_ref[j], dyb)
        dycat = _dot_nt(dyb, wout_ref[...])

        hd = hd_ref[...]
        mu = jnp.mean(hd, axis=-1, keepdims=True)
        dlt = hd - mu
        rstd = lax.rsqrt(jnp.mean(dlt * dlt, axis=-1, keepdims=True) + EPS)
        xhat = dlt * rstd
        lng = lng_ref[...]
        hl = xhat * lng + lnb_ref[...]
        sgl = _sig(hl)
        dycb = dycat[:, :CONVW].astype(BF16)
        for j in range(CONVW // CH):
            apw[j * CH : (j + 1) * CH, :] += _dot_nn(hsb3_ref[j], dycb)
        dhl = _dot_nt(dycb, wpw_ref[...]) * (sgl * (1.0 + hl * (1.0 - sgl)))
        acc_ref[5:6, 0:CONVW] += colsum(dhl)
        acc_ref[4:5, CONVW:DM] += colsum(dhl * xhat)
        dxh = dhl * lng
        dhd = rstd * (dxh - jnp.mean(dxh, axis=-1, keepdims=True) - xhat * jnp.mean(dxh * xhat, axis=-1, keepdims=True))
        acc_ref[4:5, 0:CONVW] += colsum(dhd)

        hcx[0:HALO_C, :] = jnp.where(it == 0, 0.0, halo_ref[...])
        hcx[HALO_C : HALO_C + TOK, :] = hc_ref[...]
        d_ext[0:TOK, :] = dhd
        for b in range(1, 8):
            hrot[b - 1] = hcx[pl.ds(b, TOK + HALO_C - 8), :]
            drot[b - 1] = d_ext[pl.ds(b, TOK + HALO_C - 8), :]

        def tap(base, rot, off):
            a, b = divmod(off, 8)
            return base[pl.ds(8 * a, TOK), :] if b == 0 else rot[b - 1, pl.ds(8 * a, TOK), :]

        dhc = jnp.zeros((TOK, CONVW), F32)
        for k in range(KCONV):
            ddw_ref[k : k + 1, :] += colsum(dhd * tap(hcx, hrot, HALO_C - (KCONV - 1) + k))
            dhc = dhc + dww_ref[k : k + 1, :] * tap(d_ext, drot, KCONV - 1 - k)
        d_ext[TOK : TOK + HALO_C, :] = d_ext[0:HALO_C, :]
        ua, ug = uag_ref[:, 0:CONVW], uag_ref[:, CONVW : 2 * CONVW]
        sgg = _sig(ug)
        dub_ref[:, 0:CONVW] = (dhc * sgg).astype(BF16)
        dub_ref[:, CONVW : 2 * CONVW] = (dhc * ua * sgg * (1.0 - sgg)).astype(BF16)

        tg = it * TOK + lax.broadcasted_iota(jnp.int32, (TOK, 1), 0)
        for g, w in enumerate(WINS):
            ln = slice(PGD * g, PGD * (g + 1))
            wgb = wg_ref[g].astype(BF16)
            pb = pb_ref[:, ln]
            dyp = dycat[:, CONVW + PGD * g : CONVW + PGD * (g + 1)]
            acc_ref[5:6, CONVW + PGD * g : CONVW + PGD * (g + 1)] += colsum(dyp * _dot_nn(pb, wgb))
            dzb = (dyp * psc_ref[:, ln]).astype(BF16)
            dwg_ref[PGD * g : PGD * (g + 1), :] += _dot_tn(pb, dzb)
            dp = _dot_nt(dzb, wgb)
            cnt = jnp.minimum(tg + 1, w).astype(F32)
            q_ext[0:TOK, ln] = dp / cnt
            dv = -dp
            for d in range(w):
                dv = dv + q_ext[pl.ds(d, TOK), ln]
            dub_ref[:, 2 * CONVW + PGD * g : 2 * CONVW + PGD * (g + 1)] = dv.astype(BF16)
        q_ext[TOK : TOK + HALO_P, :] = q_ext[0:HALO_P, :]

        _put3(dub3_ref, dub_ref[...])
        dh1 = _dot_nn(dub_ref[...], win_ref[...])
        g1 = g1_ref[...]
        r1 = lax.rsqrt(jnp.mean(x * x, axis=-1, keepdims=True) + EPS)
        xr = x * r1
        acc_ref[0:1, :] += colsum(dh1)
        acc_ref[1:2, :] += colsum(dh1 * (xr * g1))
        dn1 = dh1 * (1.0 + sc1)
        acc_ref[3:4, :] += colsum(dn1 * xr)
        dxr = dn1 * g1
        gx_ref[...] = dx1 + r1 * (dxr - xr * jnp.mean(dxr * xr, axis=-1, keepdims=True))

        @pl.when(i == NTILE - 1)
        def _():
            pout_ref[...] = aout[...].astype(BF16)
            ppw_ref[...] = apw[...].astype(BF16)
            for cp in far_copies(s_refs, f_refs, fs_s, fs_r):
                cp.wait()

    halo = pl.BlockSpec((HALO_C, CONVW), lambda i: (jnp.maximum((NTILE - 1 - i) * hpt - 1, 0), 0))
    return pl.pallas_call(
        body,
        name="bwd_mix",
        grid=(NTILE,),
        out_shape=[jax.ShapeDtypeStruct((SEQ, DM), F32), jax.ShapeDtypeStruct((3 * CONVW // CH, CH, SEQ), BF16), jax.ShapeDtypeStruct((8, DM), F32),
                   jax.ShapeDtypeStruct((HALO_C, CONVW), F32), jax.ShapeDtypeStruct((len(WINS) * PGD, PGD), F32),
                   jax.ShapeDtypeStruct((DM, DM), BF16), jax.ShapeDtypeStruct((CONVW, CONVW), BF16)]
        + [jax.ShapeDtypeStruct(s.shape, s.dtype) for s in sums],
        in_specs=[_tile(DM, True), _tile(DM, True), _tile(DM, True), _tile(2 * CONVW, True), _tile(CONVW, True), halo, _tile(CONVW, True),
                  _tile(POOLW, True), _const((8, 6 * DM)), _const((1, DM)), _const(win_t.shape), _const(dww.shape), _const((1, CONVW)),
                  _const((1, CONVW)), _const(wpw.shape), _const(wg.shape), _const((1, POOLW)), _const(wout.shape),
                  _tile3(DM // CH, True), _tile3(CONVW // CH, True)] + [_HBM] * ns,
        out_specs=[_tile(DM, True), _tile3(3 * CONVW // CH, True), _const((8, DM)),
                   _const((HALO_C, CONVW)), _const((len(WINS) * PGD, PGD)), _const((DM, DM)), _const((CONVW, CONVW))] + [_HBM] * ns,
        scratch_shapes=[pltpu.VMEM((TOK + HALO_C, CONVW), F32), pltpu.VMEM((TOK + HALO_P, POOLW), F32), pltpu.VMEM((TOK + HALO_C, CONVW), F32),
                        pltpu.VMEM((TOK, 3 * CONVW), BF16)] + [pltpu.VMEM((7, TOK + HALO_C - 8, CONVW), F32)] * 2
        + [pltpu.VMEM((DM, DM), F32), pltpu.VMEM((CONVW, CONVW), F32)] + [pltpu.SemaphoreType.DMA((ns, 3))] * 2,
        compiler_params=pltpu.CompilerParams(dimension_semantics=("arbitrary",), vmem_limit_bytes=VMEM_LIMIT),
    )(dx1, x, y, uag, hc, hc, hd, pbv, mod8, g1, win_t, dww, lng, lnb, wpw, wg, psc, wout, ycb3, hsb3, *sums)


CHIPS = (0, 4, 2, 6)
ASLOTS = 3


def _wgrad_rs_call(name, a3s, bmap, bs, order, small=None):
    nw = len(a3s)
    pre = [a.ndim == 2 for a in a3s]
    nchs = [a.shape[0] // CH if p else a.shape[0] for a, p in zip(a3s, pre)]
    rws = [n * CH // NDEV for n in nchs]
    cols = [a3s[k].shape[1] if pre[k] else bs[bmap[k]].shape[1] for k in range(nw)]
    last = small is not None
    whole = None if last else order[-1]
    srows, brows = (small[0].shape[0], small[1].shape[0]) if last else (0, 0)
    sub = 16
    nb = len(bs)

    def body(*refs):
        pos = 0

        def take(n):
            nonlocal pos
            pos += n
            return refs[pos - n : pos]

        a_refs, b_refs = take(nw), take(nb)
        if last:
            spack_ref, bulk_ref, dmodp_ref = take(3)
        o_refs = take(nw)
        if last:
            ssum_ref, bsum_ref, gbada_ref, dmy_ref = take(4)
        else:
            cbm_refs = take(nw)
        made = iter(take(nw - sum(pre)))
        p_refs = [a_refs[k] if pre[k] else next(made) for k in range(nw)]
        r1, cb, hb = take(nw), take(nw), take(nw)
        tmp4, abuf, bbuf, obuf = take(4)
        asem, osem, bsem, tsem, d_s, d_r, h_s, h_r = take(8)
        if last:
            r2 = take(nw)
            srecv, brecv, mrecv, i_s, i_r, ssem_s, ssem_r, msem_s, msem_r, bsem_s, bsem_r, fsem_s, fsem_r = take(13)
        else:
            (csem,) = take(1)
        x, y, c_, me = _place()
        sib, xn, yn = (_flip(x, y, c_, r)[0] for r in (1, 4, 2))

        def rcopy(src, dst, ss, rs, dev):
            return pltpu.make_async_remote_copy(src_ref=src, dst_ref=dst, send_sem=ss, recv_sem=rs, device_id=dev, device_id_type=MESH)

        sends, kept = [], []
        if last:
            srecv[me] = spack_ref[...]
            brecv[me] = bulk_ref[...]
            mrecv[me] = dmodp_ref[...]
            for r in range(1, NDEV):
                dev, _ = _flip(x, y, c_, r)
                sends.append(rcopy(srecv.at[me], srecv.at[me], ssem_s.at[r], ssem_r.at[r], dev))
                sends.append(rcopy(mrecv.at[me], mrecv.at[me], msem_s.at[r], msem_r.at[r], dev))
            for j, r in enumerate((1, 4, 2, 6)):
                sends.append(rcopy(brecv.at[me], brecv.at[me], bsem_s.at[j], bsem_r.at[j], _flip(x, y, c_, r)[0]))
            for cp in sends:
                cp.start()

        def pass_bulk():
            for j, r in enumerate((4, 2, 6)):
                blk = brecv.at[_flip(x, y, c_, r)[1]]
                rcopy(blk, blk, bsem_s.at[j + 1], bsem_r.at[j + 1], (x, y, c_)).wait_recv()
                cp = rcopy(blk, blk, fsem_s.at[j], fsem_r.at[j], sib)
                cp.start()
                sends.append(cp)

        def relay(k):
            if k == whole:
                return
            half = rws[k] // 2
            for h in range(2):
                rcopy(cb[k].at[2, pl.ds(h * half, half), :], hb[k].at[h], h_s.at[k, h], h_r.at[k, h], (x, y, c_)).wait_recv()

                def add_half(j, carry, k=k, h=h):
                    rr = pl.ds(pl.multiple_of(j * sub, sub), sub)
                    dst = pl.ds(pl.multiple_of(h * half + j * sub, sub), sub)
                    cb[k][1 - h, dst, :] = (cb[k][1 - h, dst, :].astype(F32) + hb[k][h, rr, :].astype(F32)).astype(BF16)
                    return carry

                lax.fori_loop(0, half // sub, add_half, 0)
            if last:
                for q, dev in enumerate((xn, yn)):
                    cp = rcopy(cb[k].at[q], r2[k].at[q], i_s.at[k, q], i_r.at[k, q], dev)
                    cp.start()
                    sends.append(cp)
            else:
                cp = pltpu.make_async_copy(cb[k].at[pl.ds(0, 2)], cbm_refs[k], csem.at[k])
                cp.start()
                kept.append(cp)

        def mine(k, q):
            _, owner = _flip(x, y, c_, CHIPS[q])
            return _rows(p_refs[k], owner, rws[k]), tmp4.at[q, pl.ds(0, rws[k]), pl.ds(0, cols[k])]

        def fetch(k):
            for q in range(4):
                pltpu.make_async_copy(*mine(k, q), tsem.at[q]).start()

        def presum(k):
            for q in (3, 1, 2, 0):
                src, tmp = mine(k, q)
                rcopy(src, r1[k].at[q], d_s.at[k, q], d_r.at[k, q], (x, y, c_)).wait_recv()
                pltpu.make_async_copy(src, tmp, tsem.at[q]).wait()

                def add_sib(j, carry, k=k, q=q, tmp=tmp):
                    rr = pl.ds(pl.multiple_of(j * sub, sub), sub)
                    t = tmp[rr, :].astype(F32) + r1[k][q, rr, :].astype(F32)
                    if q == 0:
                        o_refs[k][rr, :] = t
                    else:
                        cb[k][q - 1, rr, :] = t.astype(BF16)
                    return carry

                lax.fori_loop(0, rws[k] // sub, add_sib, 0)
                if q == 3 and k != whole:
                    for h, dev in enumerate((xn, yn)):
                        half = rws[k] // 2
                        cp = rcopy(cb[k].at[2, pl.ds(h * half, half), :], hb[k].at[h], h_s.at[k, h], h_r.at[k, h], dev)
                        cp.start()
                        sends.append(cp)
            if k == whole:
                cp = pltpu.make_async_copy(cb[k], cbm_refs[k], csem.at[k])
                cp.start()
                kept.append(cp)

        made_order = [k for k in order if not pre[k]]
        uses = [bmap[k] for t, k in enumerate(made_order) if t == 0 or bmap[k] != bmap[made_order[t - 1]]]

        def b_copy(u):
            return pltpu.make_async_copy(b_refs[uses[u]], bbuf.at[u % 2, :, pl.ds(0, b_refs[uses[u]].shape[1])], bsem.at[u % 2])

        def matmuls(k, u):
            a_ref, p_ref = a_refs[k], p_refs[k]
            bb = bbuf.at[u % 2, :, pl.ds(0, cols[k])]
            ob = obuf.at[:, :, pl.ds(0, cols[k])]

            def a_copy(m, slot):
                return pltpu.make_async_copy(a_ref.at[m], abuf.at[slot], asem.at[slot])

            def o_copy(m, slot):
                return pltpu.make_async_copy(ob.at[slot], p_ref.at[pl.ds(pl.multiple_of(m * CH, CH), CH), :], osem.at[slot])

            for m in range(ASLOTS - 1):
                a_copy(m, m).start()

            def step(m, carry):
                slot, aslot = lax.rem(m, 2), lax.rem(m, ASLOTS)
                a_copy(m, aslot).wait()

                @pl.when(m + ASLOTS - 1 < nchs[k])
                def _():
                    a_copy(m + ASLOTS - 1, lax.rem(m + ASLOTS - 1, ASLOTS)).start()

                @pl.when(m >= 2)
                def _():
                    o_copy(m - 2, slot).wait()

                ob[slot] = _dot_nn(abuf[aslot], bb[...]).astype(BF16)
                o_copy(m, slot).start()
                return carry

            lax.fori_loop(0, nchs[k], step, 0)
            for m in (nchs[k] - 2, nchs[k] - 1):
                o_copy(m, m % 2).wait()

        b_copy(0).start()
        u, before = -1, None
        for t, k in enumerate(order):
            p_ref, rw = p_refs[k], rws[k]
            if not pre[k]:
                if before is None or bmap[k] != bmap[before]:
                    u += 1
                    b_copy(u).wait()
                    if u + 1 < len(uses):
                        b_copy(u + 1).start()
                before = k
                matmuls(k, u)

            for q, r in enumerate(CHIPS):
                _, owner = _flip(x, y, c_, r | 1)
                cp = rcopy(_rows(p_ref, owner, rw), r1[k].at[q], d_s.at[k, q], d_r.at[k, q], sib)
                cp.start()
                sends.append(cp)
            if last and k == made_order[0]:
                pass_bulk()
            if t >= 1:
                presum(order[t - 1])
            fetch(k)
            if t >= 2:
                relay(order[t - 2])
        presum(order[-1])
        relay(order[-2])
        relay(order[-1])

        if last:
            for r in range(1, NDEV):
                _, pb = _flip(x, y, c_, r)
                rcopy(srecv.at[me], srecv.at[pb], ssem_s.at[r], ssem_r.at[r], (x, y, c_)).wait_recv()
                rcopy(mrecv.at[me], mrecv.at[pb], msem_s.at[r], msem_r.at[r], (x, y, c_)).wait_recv()
            blk = brecv.at[_flip(x, y, c_, 1)[1]]
            rcopy(blk, blk, bsem_s.at[0], bsem_r.at[0], (x, y, c_)).wait_recv()
            for j, r in enumerate((5, 3, 7)):
                blk = brecv.at[_flip(x, y, c_, r)[1]]
                rcopy(blk, blk, fsem_s.at[j], fsem_r.at[j], (x, y, c_)).wait_recv()
            tot = srecv[0]
            btl = brecv[0].astype(F32)
            for s in range(1, NDEV):
                tot = tot + srecv[s]
                btl = btl + brecv[s].astype(F32)
            ssum_ref[...] = tot
            bsum_ref[...] = btl
            btot = mrecv[0]
            for s in range(1, NDEV):
                btot = btot + mrecv[s]
            gbada_ref[...] = btot
            rowm = lax.broadcasted_iota(jnp.int32, (8, MODW), 0)
            dmy = jnp.zeros((8, MODW), F32)
            for s in range(NDEV):
                drow = jnp.sum(jnp.where(rowm == me, mrecv[s], 0.0), axis=0, keepdims=True)
                dmy = jnp.where(rowm == s, drow, dmy)
            dmy_ref[...] = dmy

            for k in order:
                for q in range(2):
                    rcopy(cb[k].at[q], r2[k].at[q], i_s.at[k, q], i_r.at[k, q], (x, y, c_)).wait_recv()

                def add_far(j, carry, k=k):
                    rr = pl.ds(pl.multiple_of(j * sub, sub), sub)
                    t = o_refs[k][rr, :]
                    for q in range(2):
                        t = t + r2[k][q, rr, :].astype(F32)
                    o_refs[k][rr, :] = t
                    return carry

                lax.fori_loop(0, rws[k] // sub, add_far, 0)
        for cp in sends:
            cp.wait_send()
        for cp in kept:
            cp.wait()

    own = [jax.ShapeDtypeStruct((rws[k], cols[k]), F32) for k in range(nw)]
    parts = [jax.ShapeDtypeStruct((nchs[k] * CH, cols[k]), BF16) for k in range(nw) if not pre[k]]
    scratch = [pltpu.VMEM((4, rws[k], cols[k]), BF16) for k in range(nw)]
    scratch += [pltpu.VMEM((3, rws[k], cols[k]), BF16) for k in range(nw)]
    scratch += [pltpu.VMEM((2, rws[k] // 2, cols[k]), BF16) for k in range(nw)]
    scratch += [pltpu.VMEM((4, max(rws), max(cols)), BF16)]
    scratch += [pltpu.VMEM((ASLOTS, CH, SEQ), BF16), pltpu.VMEM((2, SEQ, max(cols)), BF16), pltpu.VMEM((2, CH, max(cols)), BF16)]
    scratch += [pltpu.SemaphoreType.DMA((ASLOTS,))] + [pltpu.SemaphoreType.DMA((2,))] * 2 + [pltpu.SemaphoreType.DMA((4,))]
    scratch += [pltpu.SemaphoreType.DMA((nw, 4))] * 2 + [pltpu.SemaphoreType.DMA((nw, 2))] * 2
    if last:
        out_shape = own + [jax.ShapeDtypeStruct((srows, DM), F32), jax.ShapeDtypeStruct((brows, DM), F32),
                           jax.ShapeDtypeStruct((8, MODW), F32), jax.ShapeDtypeStruct((8, MODW), F32)] + parts
        out_specs = [_VM] * (nw + 4) + [_HBM] * len(parts)
        scratch += [pltpu.VMEM((2, rws[k], cols[k]), BF16) for k in range(nw)]
        scratch += [pltpu.VMEM((NDEV, srows, DM), F32), pltpu.VMEM((NDEV, brows, DM), BF16), pltpu.VMEM((NDEV, 8, MODW), F32)]
        scratch += [pltpu.SemaphoreType.DMA((nw, 2))] * 2 + [pltpu.SemaphoreType.DMA((NDEV,))] * 4
        scratch += [pltpu.SemaphoreType.DMA((4,))] * 2 + [pltpu.SemaphoreType.DMA((3,))] * 2
        keep = nw + 4
    else:
        out_shape = own + [jax.ShapeDtypeStruct((3 if k == whole else 2, rws[k], cols[k]), BF16) for k in range(nw)] + parts
        out_specs = [_VM] * nw + [_HBM] * (nw + len(parts))
        scratch += [pltpu.SemaphoreType.DMA((nw,))]
        keep = 2 * nw
    outs = pl.pallas_call(
        body,
        name=name,
        out_shape=out_shape,
        in_specs=[_HBM] * (nw + nb) + [_VM] * (3 if last else 0),
        out_specs=out_specs,
        scratch_shapes=scratch,
        compiler_params=pltpu.CompilerParams(vmem_limit_bytes=60 * 1024 * 1024),
    )(*a3s, *bs, *(small or ()))
    return outs[:keep]


def _adam_update(w, g, m, v):
    m = ADAM_B1 * m + (1.0 - ADAM_B1) * g
    v = ADAM_B2 * v + (1.0 - ADAM_B2) * (g * g)
    m_hat = m / (1.0 - ADAM_B1 ** ADAM_STEP)
    v_hat = v / (1.0 - ADAM_B2 ** ADAM_STEP)
    return -ADAM_LR * (m_hat / (jnp.sqrt(v_hat) + ADAM_EPS) + ADAM_WD * w), m, v


def _adam_ada_call(w, m, v, dmy, cact_t, gx):
    def body(w_ref, m_ref, v_ref, dmy_ref, ct_ref, gx_ref, g_ref, d_ref, mo_ref, vo_ref, gxo_ref):
        g = jnp.zeros((DM // 4, MODW), F32)
        for s in range(NDEV):
            g = g + ct_ref[:, s : s + 1] * dmy_ref[s : s + 1, :]
        g_ref[...] = g
        d_ref[...], mo_ref[...], vo_ref[...] = _adam_update(w_ref[...], g, m_ref[...], v_ref[...])
        gxo_ref[...] = gx_ref[...]

    blk = pl.BlockSpec((DM // 4, MODW), lambda i: (i, 0))
    xblk = pl.BlockSpec((SEQ // 4, DM), lambda i: (i, 0))
    return pl.pallas_call(
        body,
        name="adam_ada",
        grid=(4,),
        out_shape=[jax.ShapeDtypeStruct(w.shape, F32)] * 4 + [jax.ShapeDtypeStruct(gx.shape, F32)],
        in_specs=[blk] * 3 + [_const((8, MODW)), pl.BlockSpec((DM // 4, NDEV), lambda i: (i, 0)), xblk],
        out_specs=[blk] * 4 + [xblk],
        compiler_params=pltpu.CompilerParams(dimension_semantics=("arbitrary",), vmem_limit_bytes=VMEM_LIMIT),
    )(w, m, v, dmy, cact_t, gx)


def _adam_sum_call(name, ws, owns, fars, ms, vs):
    n = len(ws)

    def body(*refs):
        for i in range(n):
            w, own, far, m, v = (refs[j * n + i] for j in range(5))
            g = own[...]
            for q in range(fars[i].shape[0]):
                g = g + far[q].astype(F32)
            refs[5 * n + i][...] = g
            refs[6 * n + i][...], refs[7 * n + i][...], refs[8 * n + i][...] = _adam_update(w[...], g, m[...], v[...])

    shapes = [jax.ShapeDtypeStruct(w.shape, F32) for w in ws]
    blks = [pl.BlockSpec((w.shape[0] // 2, w.shape[1]), lambda i: (i, 0)) for w in ws]
    fblks = [pl.BlockSpec((f.shape[0], w.shape[0] // 2, w.shape[1]), lambda i: (0, i, 0)) for w, f in zip(ws, fars)]
    outs = pl.pallas_call(
        body,
        name=name,
        grid=(2,),
        out_shape=shapes * 4,
        in_specs=blks * 2 + fblks + blks * 2,
        out_specs=blks * 4,
        compiler_params=pltpu.CompilerParams(dimension_semantics=("arbitrary",), vmem_limit_bytes=VMEM_LIMIT),
    )(*ws, *owns, *fars, *ms, *vs)
    return outs[:n], outs[n : 2 * n], outs[2 * n : 3 * n], outs[3 * n :]


def _adam_call(name, ws, gs, ms, vs):
    n = len(ws)

    def body(*refs):
        for i in range(n):
            w, g, m, v = (refs[j * n + i][...] for j in range(4))
            refs[4 * n + i][...], refs[5 * n + i][...], refs[6 * n + i][...] = _adam_update(w, g, m, v)
            refs[7 * n + i][...] = g

    shapes = [jax.ShapeDtypeStruct(w.shape, F32) for w in ws]
    outs = pl.pallas_call(
        body,
        name=name,
        out_shape=shapes * 4,
        in_specs=[_VM] * (4 * n),
        out_specs=[_VM] * (4 * n),
        compiler_params=pltpu.CompilerParams(vmem_limit_bytes=VMEM_LIMIT),
    )(*ws, *gs, *ms, *vs)
    return outs[:n], outs[n : 2 * n], outs[2 * n : 3 * n], outs[3 * n :]


def kernel(x, c, w_ada, b_ada, g_norm1, w_in, dw_w, dw_b, conv_ln_g, conv_ln_b, w_conv_pw, w_pool_group, pool_scale, w_out, g_norm2, w_ffn_gate, w_ffn_up, w_ffn_down, g_final, loss_target, m_w_ada, m_b_ada, m_g_norm1, m_w_in, m_dw_w, m_dw_b, m_conv_ln_g, m_conv_ln_b, m_w_conv_pw, m_w_pool_group, m_pool_scale, m_w_out, m_g_norm2, m_w_ffn_gate, m_w_ffn_up, m_w_ffn_down, m_g_final, v_w_ada, v_b_ada, v_g_norm1, v_w_in, v_dw_w, v_dw_b, v_conv_ln_g, v_conv_ln_b, v_w_conv_pw, v_w_pool_group, v_pool_scale, v_w_out, v_g_norm2, v_w_ffn_gate, v_w_ffn_up, v_w_ffn_down, v_g_final):
    me = 4 * lax.axis_index("x") + 2 * lax.axis_index("y") + lax.axis_index("c")
    xs, tgt = x[0], loss_target[0]
    b_my = lax.dynamic_slice(b_ada, (0, me * MODW), (1, MODW))
    win_t, wout, wpw, mod8, cact, dww8 = _gather_call(c, w_ada[0], b_my, dw_w[0], [w_in[0].T, w_out[0], w_conv_pw[0]])
    dww = jnp.pad(jnp.transpose(dww8, (1, 0, 2)).reshape(KCONV, CONVW), ((0, HALO_C - KCONV), (0, 0)))
    wgp = w_pool_group[0]

    h1b, uag, hc, hd, hsb3, pbv, ycb3, y, x1, wg_t, wu_t, wd = _fwd_mix_call(
        xs, mod8, g_norm1, win_t, dww, dw_b, conv_ln_g, conv_ln_b, wpw, wgp, pool_scale, wout,
        [w_ffn_gate[0].T, w_ffn_up[0].T, w_ffn_down[0]])
    h2b, a3, dfb, dg3, du3, dx1, facc = _ffn_call(x1, tgt, mod8, g_norm2, g_final.reshape(1, DM), wg_t, wu_t, wd)
    own_gate, own_up, own_down, s_gate, s_up, s_down = _wgrad_rs_call("wgrad_ffn", [dg3, du3, a3], [0, 0, 1], [h2b, dfb], (2, 0, 1))
    gx, dub3, macc, ddw, dwg, p_out, p_pw, f_gate, f_up, f_down = _bwd_mix_call(
        dx1, xs, y, uag, hc, hd, pbv, mod8, g_norm1, win_t, dww, conv_ln_g, conv_ln_b, wpw, wgp, pool_scale, wout, ycb3, hsb3,
        [s_gate, s_up, s_down])
    spack = jnp.concatenate([macc[3:4], facc[3:4], facc[4:5], macc[4:6], facc[5:6], jnp.zeros((2, DM), F32)], axis=0)
    bulk = jnp.concatenate([ddw.reshape(HALO_C // 2, DM), dwg.reshape(-1, DM)], axis=0).astype(BF16)
    dmodp = jnp.concatenate([macc[0:3], facc[0:3]], axis=0).reshape(8, MODW)
    g_in_t, g_out, g_pw, ssum, bsum, gbada, dmy = _wgrad_rs_call(
        "wgrad_rs", [dub3, p_out, p_pw], [0, None, None], [h1b], (1, 2, 0), small=(spack, bulk, dmodp))
    g_wada, d_wada, m_wada, v_wada, gx = _adam_ada_call(w_ada[0], m_w_ada[0], v_w_ada[0], dmy, cact.T, gx)
    turned = ("w_in", "w_ffn_gate", "w_ffn_up")
    ffn = ("w_ffn_gate", "w_ffn_up", "w_ffn_down")
    ffn_given = dict(w_ffn_gate=(w_ffn_gate, m_w_ffn_gate, v_w_ffn_gate), w_ffn_up=(w_ffn_up, m_w_ffn_up, v_w_ffn_up),
                     w_ffn_down=(w_ffn_down, m_w_ffn_down, v_w_ffn_down))

    def ffn_work(j):
        return [ffn_given[n][j][0].T if n in turned else ffn_given[n][j][0] for n in ffn]

    g_ffn, d_ffn, m_ffn, v_ffn = _adam_sum_call(
        "adam_ffn", ffn_work(0), [own_gate, own_up, own_down], [f_gate, f_up, f_down], ffn_work(1), ffn_work(2))

    loss = ssum[5, 0]
    ddw_all = bsum[: HALO_C // 2].reshape(HALO_C, CONVW)[:KCONV]
    grads = {
        "w_ada": g_wada,
        "b_ada": gbada.reshape(1, 6 * DM),
        "g_norm1": ssum[0:1],
        "w_in": g_in_t,
        "dw_w": lax.dynamic_slice(ddw_all, (0, me * (CONVW // NDEV)), (KCONV, CONVW // NDEV)),
        "dw_b": ssum[3:4, 0:CONVW],
        "conv_ln_g": ssum[3:4, CONVW:DM],
        "conv_ln_b": ssum[4:5, 0:CONVW],
        "w_conv_pw": g_pw,
        "w_pool_group": bsum[HALO_C // 2 :].reshape(len(WINS) * PGD, PGD),
        "pool_scale": ssum[4:5, CONVW:DM],
        "w_out": g_out,
        "g_norm2": ssum[1:2],
        "w_ffn_gate": g_ffn[0],
        "w_ffn_up": g_ffn[1],
        "w_ffn_down": g_ffn[2],
        "g_final": ssum[2:3],
    }
    given = dict(w_ada=(w_ada, m_w_ada, v_w_ada), b_ada=(b_ada, m_b_ada, v_b_ada), g_norm1=(g_norm1, m_g_norm1, v_g_norm1),
                 w_in=(w_in, m_w_in, v_w_in), dw_w=(dw_w, m_dw_w, v_dw_w), dw_b=(dw_b, m_dw_b, v_dw_b),
                 conv_ln_g=(conv_ln_g, m_conv_ln_g, v_conv_ln_g), conv_ln_b=(conv_ln_b, m_conv_ln_b, v_conv_ln_b),
                 w_conv_pw=(w_conv_pw, m_w_conv_pw, v_w_conv_pw), w_pool_group=(w_pool_group, m_w_pool_group, v_w_pool_group),
                 pool_scale=(pool_scale, m_pool_scale, v_pool_scale), w_out=(w_out, m_w_out, v_w_out), g_norm2=(g_norm2, m_g_norm2, v_g_norm2),
                 w_ffn_gate=(w_ffn_gate, m_w_ffn_gate, v_w_ffn_gate), w_ffn_up=(w_ffn_up, m_w_ffn_up, v_w_ffn_up),
                 w_ffn_down=(w_ffn_down, m_w_ffn_down, v_w_ffn_down), g_final=(g_final, m_g_final, v_g_final))
    names = list(given)
    groups = [["w_in", "w_out", "w_conv_pw"],
              ["b_ada", "g_norm1", "dw_w", "dw_b", "conv_ln_g", "conv_ln_b", "w_pool_group", "pool_scale", "g_norm2", "g_final"]]

    def work(n, a):
        return a[0].T if n in turned else a.reshape(grads[n].shape)

    def full(n, a):
        return a.T[None] if n in turned else a.reshape(given[n][0].shape)

    delta, new_m, new_v = {"w_ada": d_wada}, {"w_ada": m_wada}, {"w_ada": v_wada}
    for i, n in enumerate(ffn):
        delta[n], new_m[n], new_v[n] = d_ffn[i], m_ffn[i], v_ffn[i]
    for gi, grp in enumerate(groups):
        ds, ms, vs, gs = _adam_call(f"adam{gi}", [work(n, given[n][0]) for n in grp], [grads[n] for n in grp],
                                    [work(n, given[n][1]) for n in grp], [work(n, given[n][2]) for n in grp])
        for n, d_, m_, v_, g_ in zip(grp, ds, ms, vs, gs):
            delta[n], new_m[n], new_v[n], grads[n] = d_, m_, v_, g_

    return (loss, gx.reshape(x.shape), *[full(n, grads[n]) for n in names], *[full(n, delta[n]) for n in names],
            *[full(n, new_m[n]) for n in names], *[full(n, new_v[n]) for n in names])
```
